```python
import math
import jax, jax.numpy as jnp
from jax import lax
import numpy as np

D_MODEL = 1024
BATCH = 2
SEQ = 8192
DEPTH = 2

CHUNK = 64
Q_BLOCK = 128
D_FF = 2816
EPS = 1e-6
GLA_HEADS = 4
GLA_DK = 48
GLA_DV = 96
GLA_RANK = 16
GLA_GATE_NORM = 16.0
DIFF_HEADS = 4
DIFF_DQK = 48
DIFF_DV = 96
CONV_DIM = 256
CONV_WIDTH = 3
NUM_BUCKETS = 32
MAX_DISTANCE = 128

GLA_WIDTH = GLA_HEADS * GLA_DV
DIFF_WIDTH = DIFF_HEADS * DIFF_DV
MIX_WIDTH = GLA_WIDTH + DIFF_WIDTH + CONV_DIM
IN_SPLITS = (GLA_HEADS * GLA_DK, GLA_HEADS * GLA_DK, GLA_WIDTH, GLA_RANK, GLA_WIDTH,
             DIFF_HEADS * 2 * DIFF_DQK, DIFF_HEADS * 2 * DIFF_DQK, DIFF_WIDTH,
             CONV_DIM, CONV_DIM, CONV_DIM)
N_IN = 3088

kernel_name = "hybrid_gla_diffattn_shortconv_macaron"


def rms_norm(x, g):
    xf = x.astype(jnp.float32)
    y = xf * lax.rsqrt(jnp.mean(xf * xf, axis=-1, keepdims=True) + EPS)
    return (y * g.astype(jnp.float32)).astype(x.dtype)


def swiglu(h, w_gate, w_up, w_down):
    return (jax.nn.silu(h @ w_gate) * (h @ w_up)) @ w_down


def t5_bucket(rel):
    nb = NUM_BUCKETS // 2
    max_exact = nb // 2
    ret = (rel > 0).astype(jnp.int32) * nb
    n = jnp.abs(rel)
    nf = jnp.maximum(n, 1).astype(jnp.float32)
    large = max_exact + (jnp.log(nf / max_exact) / math.log(MAX_DISTANCE / max_exact)
                         * (nb - max_exact)).astype(jnp.int32)
    large = jnp.minimum(large, nb - 1)
    return ret + jnp.where(n < max_exact, n, large)


def gla_chunked(q, k, v, g):
    Bsz, S, H, dk = q.shape
    dv = v.shape[-1]
    nc = S // CHUNK

    def to_chunks(t):
        return t.astype(jnp.float32).reshape(Bsz, nc, CHUNK, H, t.shape[-1]).transpose(1, 0, 3, 2, 4)

    qc = to_chunks(q) * (dk ** -0.5)
    kc, vc = to_chunks(k), to_chunks(v)
    bc = jnp.cumsum(to_chunks(g), axis=3)
    causal = jnp.tril(jnp.ones((CHUNK, CHUNK), bool))[:, :, None]

    def step(state, inp):
        qi, ki, vi, bi = inp
        diff = bi[:, :, :, None, :] - bi[:, :, None, :, :]
        decay = jnp.where(causal, jnp.exp(jnp.where(causal, diff, 0.0)), 0.0)
        scores = jnp.einsum('bhtk,bhtsk,bhsk->bhts', qi, decay, ki)
        b_last = bi[:, :, -1:, :]
        o = scores @ vi + jnp.einsum('bhtk,bhkv->bhtv', qi * jnp.exp(bi), state)
        state = (state * jnp.exp(b_last)[:, :, 0, :, None]
                 + jnp.einsum('bhsk,bhsv->bhkv', ki * jnp.exp(b_last - bi), vi))
        return state, o

    state0 = jnp.zeros((Bsz, H, dk, dv), jnp.float32)
    _, o = lax.scan(step, state0, (qc, kc, vc, bc))
    return o.transpose(1, 0, 3, 2, 4).reshape(Bsz, S, H, dv)


def diff_attention(q1, q2, k1, k2, v, lam, rel_bias):
    Bsz, H, S, d = q1.shape
    nq = S // Q_BLOCK
    scale = d ** -0.5
    kpos = jnp.arange(S)
    bias_table = rel_bias.astype(jnp.float32)

    def to_blocks(t):
        return t.reshape(Bsz, H, nq, Q_BLOCK, d).transpose(2, 0, 1, 3, 4)

    def one_block(args):
        i, q1b, q2b = args
        qpos = i * Q_BLOCK + jnp.arange(Q_BLOCK)
        bias = bias_table[t5_bucket(kpos[None, :] - qpos[:, None])].transpose(2, 0, 1)
        allowed = (kpos[None, :] // CHUNK) <= (qpos[:, None] // CHUNK)

        def attn_map(qb, kk):
            s = jnp.einsum('bhqd,bhkd->bhqk', qb, kk) * scale + bias
            return jax.nn.softmax(jnp.where(allowed, s, -jnp.inf), axis=-1)

        p = attn_map(q1b, k1) - lam * attn_map(q2b, k2)
        return jnp.einsum('bhqk,bhkv->bhqv', p, v)

    o = lax.map(one_block, (jnp.arange(nq), to_blocks(q1), to_blocks(q2)))
    return o.transpose(1, 2, 0, 3, 4).reshape(Bsz, H, S, v.shape[-1])


def short_conv(b_gate, c_gate, u, w):
    z = (c_gate * u).astype(jnp.float32)
    y = lax.conv_general_dilated(z, w.astype(jnp.float32)[:, None, :], window_strides=(1,),
                                 padding=[(CONV_WIDTH - 1, 0)],
                                 dimension_numbers=('NWC', 'WIO', 'NWC'),
                                 feature_group_count=CONV_DIM)
    return b_gate.astype(jnp.float32) * y


def hybrid_mixer(h, w_in, gk_up, gk_bias, gla_norm, lq1, lk1, lq2, lk2, diff_subln,
                 rel_bias, conv_w, w_out, lam_init):
    Bsz, S, _ = h.shape
    offsets = [int(o) for o in np.cumsum(IN_SPLITS)[:-1]]
    p = h @ w_in
    (gq, gk, gv, g_lr, g_gate, dq, dk, dv, cb, cc, ch) = jnp.split(p, offsets, axis=-1)

    g = jax.nn.log_sigmoid((g_lr @ gk_up + gk_bias).astype(jnp.float32)) / GLA_GATE_NORM
    o = gla_chunked(gq.reshape(Bsz, S, GLA_HEADS, GLA_DK), gk.reshape(Bsz, S, GLA_HEADS, GLA_DK),
                    gv.reshape(Bsz, S, GLA_HEADS, GLA_DV), g.reshape(Bsz, S, GLA_HEADS, GLA_DK))
    gla_out = (rms_norm(o, gla_norm)
               * jax.nn.silu(g_gate.astype(jnp.float32).reshape(Bsz, S, GLA_HEADS, GLA_DV))
               ).reshape(Bsz, S, GLA_WIDTH)

    def heads(t, d):
        return t.astype(jnp.float32).reshape(Bsz, S, DIFF_HEADS, d).transpose(0, 2, 1, 3)
    q = heads(dq, 2 * DIFF_DQK)
    k = heads(dk, 2 * DIFF_DQK)
    v = heads(dv, DIFF_DV)
    f32 = jnp.float32
    lam = (jnp.exp(jnp.sum(lq1.astype(f32) * lk1.astype(f32)))
           - jnp.exp(jnp.sum(lq2.astype(f32) * lk2.astype(f32))) + lam_init)
    o = diff_attention(q[..., :DIFF_DQK], q[..., DIFF_DQK:], k[..., :DIFF_DQK], k[..., DIFF_DQK:],
                       v, lam, rel_bias)
    diff_out = (rms_norm(o, diff_subln) * (1.0 - lam_init)).transpose(0, 2, 1, 3).reshape(Bsz, S, DIFF_WIDTH)

    conv_out = short_conv(cb, cc, ch, conv_w)

    mixed = jnp.concatenate([gla_out, diff_out, conv_out], axis=-1).astype(h.dtype)
    return mixed @ w_out


def setup_inputs(seed: int = 0) -> dict:
    key = jax.random.key(seed)
    ks = jax.random.split(key, 32)
    f32 = jnp.float32

    def nrm(i, shape, scale):
        return jax.random.normal(ks[i], shape, f32) * scale

    def gain(i, shape):
        return 1.0 + 0.02 * jax.random.normal(ks[i], shape, f32)

    L = DEPTH
    return {
        "x": nrm(0, (BATCH, SEQ, D_MODEL), 1.0),
        "ffn1_norm": gain(1, (L, D_MODEL)),
        "ffn1_gate": nrm(2, (L, D_MODEL, D_FF), D_MODEL ** -0.5),
        "ffn1_up": nrm(3, (L, D_MODEL, D_FF), D_MODEL ** -0.5),
        "ffn1_down": nrm(4, (L, D_FF, D_MODEL), D_FF ** -0.5),
        "mix_norm": gain(5, (L, D_MODEL)),
        "w_in": nrm(6, (L, D_MODEL, N_IN), D_MODEL ** -0.5),
        "gla_gk_up": nrm(7, (L, GLA_RANK, GLA_HEADS * GLA_DK), GLA_RANK ** -0.5),
        "gla_gk_bias": nrm(8, (L, GLA_HEADS * GLA_DK), 0.1),
        "gla_norm": gain(9, (L, GLA_DV)),
        "diff_lambda_q1": nrm(10, (L, DIFF_DQK), 0.1),
        "diff_lambda_k1": nrm(11, (L, DIFF_DQK), 0.1),
        "diff_lambda_q2": nrm(12, (L, DIFF_DQK), 0.1),
        "diff_lambda_k2": nrm(13, (L, DIFF_DQK), 0.1),
        "diff_subln": gain(14, (L, DIFF_DV)),
        "rel_bias": nrm(15, (NUM_BUCKETS, DIFF_HEADS), 0.5),
        "conv_w": nrm(16, (L, CONV_WIDTH, CONV_DIM), CONV_WIDTH ** -0.5),
        "w_out": nrm(17, (L, MIX_WIDTH, D_MODEL), MIX_WIDTH ** -0.5),
        "ffn2_norm": gain(18, (L, D_MODEL)),
        "ffn2_gate": nrm(19, (L, D_MODEL, D_FF), D_MODEL ** -0.5),
        "ffn2_up": nrm(20, (L, D_MODEL, D_FF), D_MODEL ** -0.5),
        "ffn2_down": nrm(21, (L, D_FF, D_MODEL), D_FF ** -0.5),
        "final_norm": gain(22, (D_MODEL,)),
    }


def reference(x, ffn1_norm, ffn1_gate, ffn1_up, ffn1_down, mix_norm, w_in, gla_gk_up,
              gla_gk_bias, gla_norm, diff_lambda_q1, diff_lambda_k1, diff_lambda_q2,
              diff_lambda_k2, diff_subln, rel_bias, conv_w, w_out, ffn2_norm, ffn2_gate,
              ffn2_up, ffn2_down, final_norm):
    for l in range(DEPTH):
        lam_init = 0.8 - 0.6 * math.exp(-0.3 * l)
        x = x + 0.5 * swiglu(rms_norm(x, ffn1_norm[l]), ffn1_gate[l], ffn1_up[l], ffn1_down[l])
        x = x + hybrid_mixer(rms_norm(x, mix_norm[l]), w_in[l], gla_gk_up[l], gla_gk_bias[l],
                             gla_norm[l], diff_lambda_q1[l], diff_lambda_k1[l], diff_lambda_q2[l],
                             diff_lambda_k2[l], diff_subln[l], rel_bias, conv_w[l], w_out[l],
                             lam_init).astype(x.dtype)
        x = x + 0.5 * swiglu(rms_norm(x, ffn2_norm[l]), ffn2_gate[l], ffn2_up[l], ffn2_down[l])
    return rms_norm(x, final_norm)
```

```python
import functools
import math

import numpy as np
import jax
import jax.numpy as jnp
from jax import lax
from jax.experimental import pallas as pl
from jax.experimental.pallas import tpu as pltpu

F32 = jnp.float32
BF16 = jnp.bfloat16

D_MODEL = 1024
D_FF = 2816
EPS = 1e-6
CHUNK = 64
GLA_HEADS = 4
GLA_DK = 48
GLA_DV = 96
GLA_RANK = 16
GLA_GATE_NORM = 16.0
DIFF_HEADS = 4
DIFF_DQK = 48
DIFF_DV = 96
CONV_DIM = 256
CONV_WIDTH = 3
NUM_BUCKETS = 32
MAX_DISTANCE = 128
GLA_WIDTH = GLA_HEADS * GLA_DV
DIFF_WIDTH = DIFF_HEADS * DIFF_DV

LANES = 128
VMEM_LIMIT_BYTES = 56 * 1024 * 1024

HEAD_SLOT = 64
GLA_QK_LANES = GLA_HEADS * HEAD_SLOT
SUB = 16
N_SUB = CHUNK // SUB
KST_ROWS = SUB * (N_SUB * (N_SUB - 1) // 2)
DIFF_LANES = DIFF_HEADS * LANES
MASK_VALUE = -1e30

OFF_GQ = 0
OFF_GK = OFF_GQ + GLA_QK_LANES
OFF_GLR = OFF_GK + GLA_QK_LANES
OFF_GV = OFF_GLR + LANES
OFF_GATE = OFF_GV + GLA_WIDTH
OFF_DQ = OFF_GATE + GLA_WIDTH
OFF_DK = OFF_DQ + DIFF_LANES
OFF_DV = OFF_DK + DIFF_LANES
OFF_CB = OFF_DV + DIFF_LANES
OFF_CC = OFF_CB + CONV_DIM
OFF_CH = OFF_CC + CONV_DIM
N_PROJ = OFF_CH + CONV_DIM

FFN_ROWS = 512
FFN_F_CHUNK = 1408
PROJ_ROWS = 512
GLA_ROWS = 512
ATT_TILE = 256
OUT_ROWS = 512


def _dot(a, b):
    return jnp.dot(a, b, preferred_element_type=F32)


def _dot_nt(a, b):
    return lax.dot_general(a, b, (((1,), (1,)), ((), ())), preferred_element_type=F32)


def _dot_tn(a, b):
    return lax.dot_general(a, b, (((0,), (0,)), ((), ())), preferred_element_type=F32)


def _split_bf16(x):
    hi = x.astype(BF16)
    lo = (x - hi.astype(F32)).astype(BF16)
    return hi, lo


def _rms(x, g):
    return x * lax.rsqrt(jnp.mean(x * x, axis=-1, keepdims=True) + EPS) * g


def _sigmoid(x):
    return 1.0 / (1.0 + jnp.exp(-x))


def _ffn_kernel(x_ref, g_ref, wg_ref, wu_ref, wd_ref, fg_ref, o_ref, *, final):
    x = x_ref[...]
    h = _rms(x, g_ref[...]).astype(BF16)
    acc = None
    for c in range(D_FF // FFN_F_CHUNK):
        sl = slice(c * FFN_F_CHUNK, (c + 1) * FFN_F_CHUNK)
        gate = _dot(h, wg_ref[:, sl])
        up = _dot(h, wu_ref[:, sl])
        a = (gate * _sigmoid(gate) * up).astype(BF16)
        part = _dot(a, wd_ref[sl, :])
        acc = part if acc is None else acc + part
    y = x + 0.5 * acc
    if final:
        y = _rms(y, fg_ref[...])
    o_ref[...] = y


def _const_spec(shape):
    nd = len(shape)
    return pl.BlockSpec(shape, lambda *_: (0,) * nd, pipeline_mode=pl.Buffered(1))


def _ffn(x, g, wg, wu, wd, final_g, final):
    t = x.shape[0]
    tm = min(FFN_ROWS, t)
    return pl.pallas_call(
        functools.partial(_ffn_kernel, final=final),
        out_shape=jax.ShapeDtypeStruct((t, D_MODEL), F32),
        grid=(t // tm,),
        in_specs=[
            pl.BlockSpec((tm, D_MODEL), lambda i: (i, 0)),
            _const_spec((1, D_MODEL)),
            _const_spec((D_MODEL, D_FF)),
            _const_spec((D_MODEL, D_FF)),
            _const_spec((D_FF, D_MODEL)),
            _const_spec((1, D_MODEL)),
        ],
        out_specs=pl.BlockSpec((tm, D_MODEL), lambda i: (i, 0)),
        compiler_params=pltpu.CompilerParams(
            dimension_semantics=("parallel",), vmem_limit_bytes=VMEM_LIMIT_BYTES),
        name="ffn",
    )(x, g, wg, wu, wd, final_g)


def _proj_kernel(x_ref, g_ref, w_ref, gkup_ref, gkb_ref, cw_ref,
                 gq_ref, gk_ref, gg_ref, gv_ref, gate_ref, dq_ref, dk_ref, dv_ref, conv_ref,
                 zbuf_ref, *, tiles_per_seq):
    tm = x_ref.shape[0]
    h = _rms(x_ref[...], g_ref[...]).astype(BF16)

    def proj(off, width):
        return _dot(h, w_ref[:, off:off + width])

    gq_ref[...] = proj(OFF_GQ, GLA_QK_LANES)
    gk_ref[...] = proj(OFF_GK, GLA_QK_LANES)
    gv_ref[...] = proj(OFF_GV, GLA_WIDTH).astype(BF16)
    gate_ref[...] = proj(OFF_GATE, GLA_WIDTH)

    glr_hi, glr_lo = _split_bf16(proj(OFF_GLR, LANES))
    up_hi = gkup_ref[0]
    up_lo = gkup_ref[1]
    z = _dot(glr_hi, up_hi) + _dot(glr_lo, up_hi) + _dot(glr_hi, up_lo) + gkb_ref[...]
    logsig = jnp.minimum(z, 0.0) - jnp.log1p(jnp.exp(-jnp.abs(z)))
    lane = lax.broadcasted_iota(jnp.int32, (tm, GLA_QK_LANES), 1)
    gg_ref[...] = jnp.where(lane % HEAD_SLOT < GLA_DK, logsig * (1.0 / GLA_GATE_NORM), 0.0)

    dq_ref[...] = (proj(OFF_DQ, DIFF_LANES) * (DIFF_DQK ** -0.5)).astype(BF16)
    dk_ref[...] = proj(OFF_DK, DIFF_LANES).astype(BF16)
    lane_v = lax.broadcasted_iota(jnp.int32, (tm, DIFF_LANES), 1)
    dv_ref[...] = jnp.where(lane_v % LANES == DIFF_DV, 1.0, proj(OFF_DV, DIFF_LANES)).astype(BF16)

    @pl.when(pl.program_id(0) % tiles_per_seq == 0)
    def _():
        zbuf_ref[0:8, :] = jnp.zeros((8, CONV_DIM), F32)

    zc = proj(OFF_CC, CONV_DIM) * proj(OFF_CH, CONV_DIM)
    zbuf_ref[8:8 + tm, :] = zc
    cw = cw_ref[...]
    y = (cw[2:3, :] * zc + cw[1:2, :] * zbuf_ref[7:7 + tm, :] + cw[0:1, :] * zbuf_ref[6:6 + tm, :])
    conv_ref[...] = (proj(OFF_CB, CONV_DIM) * y).astype(BF16)
    zbuf_ref[0:8, :] = zbuf_ref[tm:tm + 8, :]


def _proj(x, g, w, gkup, gkb, cw, seq):
    t = x.shape[0]
    tm = min(PROJ_ROWS, seq)
    row = lambda width: pl.BlockSpec((tm, width), lambda i: (i, 0))
    out_shapes = (
        jax.ShapeDtypeStruct((t, GLA_QK_LANES), F32),
        jax.ShapeDtypeStruct((t, GLA_QK_LANES), F32),
        jax.ShapeDtypeStruct((t, GLA_QK_LANES), F32),
        jax.ShapeDtypeStruct((t, GLA_WIDTH), BF16),
        jax.ShapeDtypeStruct((t, GLA_WIDTH), F32),
        jax.ShapeDtypeStruct((t, DIFF_LANES), BF16),
        jax.ShapeDtypeStruct((t, DIFF_LANES), BF16),
        jax.ShapeDtypeStruct((t, DIFF_LANES), BF16),
        jax.ShapeDtypeStruct((t, CONV_DIM), BF16),
    )
    return pl.pallas_call(
        functools.partial(_proj_kernel, tiles_per_seq=seq // tm),
        out_shape=out_shapes,
        grid=(t // tm,),
        in_specs=[
            row(D_MODEL),
            _const_spec((1, D_MODEL)),
            _const_spec((D_MODEL, N_PROJ)),
            _const_spec((2, LANES, GLA_QK_LANES)),
            _const_spec((1, GLA_QK_LANES)),
            _const_spec((8, CONV_DIM)),
        ],
        out_specs=tuple(row(s.shape[1]) for s in out_shapes),
        scratch_shapes=[pltpu.VMEM((tm + 8, CONV_DIM), F32)],
        compiler_params=pltpu.CompilerParams(
            dimension_semantics=("arbitrary",), vmem_limit_bytes=VMEM_LIMIT_BYTES),
        name="mixer_proj",
    )(x, g, w, gkup, gkb, cw)


def _gla_cum_matrix():
    t = np.arange(CHUNK)[:, None]
    u = np.arange(CHUNK)[None, :]
    tri = (u <= t)
    ref = (u <= (t // SUB) * SUB - 1)
    ones = np.ones((CHUNK, CHUNK), bool)
    parts = [tri, ref, ones]
    for j in range(1, N_SUB):
        s = np.arange(SUB * j)[:, None]
        parts.append((u > s) & (u <= SUB * j - 1))
    return np.concatenate(parts, axis=0).astype(np.float32)


def _gla_score_mask():
    t = np.arange(CHUNK)[:, None]
    m = np.arange(KST_ROWS)
    group = np.concatenate([np.full(SUB * j, j) for j in range(1, N_SUB)])
    keep = (group[None, :] == (t // SUB))
    return np.tile(keep, (1, GLA_HEADS)).astype(np.float32)


def _gla_kernel(q_ref, k_ref, g_ref, v_ref, gate_ref, cum_ref, amask_ref, gn_ref,
                o_ref, s_ref, raw_ref):
    rows = q_ref.shape[0]
    n_chunks = rows // CHUNK

    @pl.when(pl.program_id(1) == 0)
    def _():
        s_ref[...] = jnp.zeros_like(s_ref)

    cum = cum_ref[...]
    amask = amask_ref[...] > 0.5
    qk_head = lax.broadcasted_iota(jnp.int32, (1, GLA_QK_LANES), 1) // HEAD_SLOT
    v_head = lax.broadcasted_iota(jnp.int32, (1, GLA_WIDTH), 1) // GLA_DV
    row_head_q = lax.broadcasted_iota(jnp.int32, (GLA_QK_LANES, 1), 0) // HEAD_SLOT
    state_mask = row_head_q == v_head
    ind_r = lax.broadcasted_iota(jnp.int32, (GLA_QK_LANES, LANES), 0) // HEAD_SLOT
    ind_c = lax.broadcasted_iota(jnp.int32, (GLA_QK_LANES, LANES), 1)
    ind_sum = jnp.where(ind_r == ind_c, 1.0, 0.0).astype(BF16)
    bc_r = lax.broadcasted_iota(jnp.int32, (LANES, GLA_WIDTH), 0)
    bc_c = lax.broadcasted_iota(jnp.int32, (LANES, GLA_WIDTH), 1) // GLA_DV
    ind_bcast = jnp.where(bc_r == bc_c, 1.0, 0.0).astype(BF16)
    row_in_sub = lax.broadcasted_iota(jnp.int32, (CHUNK, 1), 0) % SUB
    scale = GLA_DK ** -0.5

    def chunk_body(c, carry):
        r0 = pl.multiple_of(c * CHUNK, CHUNK)
        q = q_ref[pl.ds(r0, CHUNK), :] * scale
        k = k_ref[pl.ds(r0, CHUNK), :]
        g = g_ref[pl.ds(r0, CHUNK), :]
        v = v_ref[pl.ds(r0, CHUNK), :]
        g_hi, g_lo = _split_bf16(g)
        cums = _dot(cum, g_hi) + _dot(cum, g_lo)
        b = cums[0:CHUNK]
        b_ref = cums[CHUNK:2 * CHUNK]
        b_last = cums[2 * CHUNK:3 * CHUNK]
        e_kst = cums[3 * CHUNK:3 * CHUNK + KST_ROWS]

        state = s_ref[...]
        q_in = (q * jnp.exp(b)).astype(BF16)
        o = _dot(q_in, state.astype(BF16))

        kw = (k * jnp.exp(b_last - b)).astype(BF16)
        kv = _dot_tn(kw, v)
        decay_rows = jnp.exp(jnp.concatenate([b_last, b_last], axis=0))
        decay_cols = decay_rows.T[:, 0:1]
        s_ref[...] = state * decay_cols + jnp.where(state_mask, kv, 0.0)

        q_sub = (q * jnp.exp(jnp.minimum(b - b_ref, 0.0))).astype(BF16)
        k_st = jnp.concatenate([k[0:SUB * j] for j in range(1, N_SUB)], axis=0) * jnp.exp(e_kst)
        k_bd = jnp.concatenate(
            [jnp.where(qk_head == hd, k_st, 0.0) for hd in range(GLA_HEADS)], axis=0).astype(BF16)
        a = _dot_nt(q_sub, k_bd)
        a = jnp.where(amask, a, 0.0).astype(BF16)
        v_st = jnp.concatenate([v[0:SUB * j] for j in range(1, N_SUB)], axis=0)
        v_bd = jnp.concatenate(
            [jnp.where(v_head == hd, v_st, jnp.zeros_like(v_st)) for hd in range(GLA_HEADS)], axis=0)
        o = o + _dot(a, v_bd)

        vf = v.astype(F32)
        prods = []
        for d in range(SUB):
            ks = k if d == 0 else pltpu.roll(k, d, 0)
            bs = b if d == 0 else pltpu.roll(b, d, 0)
            prods.append((q * ks * jnp.exp(jnp.minimum(b - bs, 0.0))).astype(BF16))
        dsum = _dot(jnp.concatenate(prods, axis=0), ind_sum)
        valid = jnp.concatenate([row_in_sub >= d for d in range(SUB)], axis=0)
        dsum = jnp.where(valid, dsum, 0.0).astype(BF16)
        dbc = _dot(dsum, ind_bcast)
        for d in range(SUB):
            vs = vf if d == 0 else pltpu.roll(vf, d, 0)
            o = o + dbc[d * CHUNK:(d + 1) * CHUNK] * vs
        raw_ref[pl.ds(r0, CHUNK), :] = o
        return carry

    lax.fori_loop(0, n_chunks, chunk_body, 0)

    o = raw_ref[...]
    hr = lax.broadcasted_iota(jnp.int32, (GLA_WIDTH, GLA_WIDTH), 0) // GLA_DV
    hc = lax.broadcasted_iota(jnp.int32, (GLA_WIDTH, GLA_WIDTH), 1) // GLA_DV
    head_ones = jnp.where(hr == hc, 1.0, 0.0).astype(BF16)
    sq_hi, sq_lo = _split_bf16(o * o)
    ms = (_dot(sq_hi, head_ones) + _dot(sq_lo, head_ones)) * (1.0 / GLA_DV)
    gate = gate_ref[...]
    y = o * lax.rsqrt(ms + EPS) * gn_ref[...] * (gate * _sigmoid(gate))
    o_ref[...] = y.astype(BF16)


def _gla(gq, gk, gg, gv, gate, gn_tiled, batch, seq):
    rows = min(GLA_ROWS, seq)
    steps = seq // rows
    row = lambda width: pl.BlockSpec((rows, width), lambda b, i: (b * steps + i, 0))
    cum = jnp.asarray(_gla_cum_matrix(), BF16)
    amask = jnp.asarray(_gla_score_mask(), F32)
    return pl.pallas_call(
        _gla_kernel,
        out_shape=jax.ShapeDtypeStruct((batch * seq, GLA_WIDTH), BF16),
        grid=(batch, steps),
        in_specs=[
            row(GLA_QK_LANES), row(GLA_QK_LANES), row(GLA_QK_LANES), row(GLA_WIDTH), row(GLA_WIDTH),
            pl.BlockSpec(cum.shape, lambda b, i: (0, 0)),
            pl.BlockSpec(amask.shape, lambda b, i: (0, 0)),
            pl.BlockSpec((1, GLA_WIDTH), lambda b, i: (0, 0)),
        ],
        out_specs=row(GLA_WIDTH),
        scratch_shapes=[pltpu.VMEM((GLA_QK_LANES, GLA_WIDTH), F32),
                        pltpu.VMEM((rows, GLA_WIDTH), F32)],
        compiler_params=pltpu.CompilerParams(
            dimension_semantics=("arbitrary", "arbitrary"), vmem_limit_bytes=VMEM_LIMIT_BYTES),
        name="gla",
    )(gq, gk, gg, gv, gate, cum, amask, gn_tiled)


def _t5_bucket(rel):
    nb = NUM_BUCKETS // 2
    max_exact = nb // 2
    ret = (rel > 0).astype(jnp.int32) * nb
    n = jnp.abs(rel)
    nf = jnp.maximum(n, 1).astype(jnp.float32)
    large = max_exact + (jnp.log(nf / max_exact) / math.log(MAX_DISTANCE / max_exact)
                         * (nb - max_exact)).astype(jnp.int32)
    large = jnp.minimum(large, nb - 1)
    return ret + jnp.where(n < max_exact, n, large)


def _bias_tiles(rel_bias, tile):
    assert tile >= MAX_DISTANCE and tile % CHUNK == 0
    table = rel_bias.astype(F32)
    far = table[NUM_BUCKETS // 2 - 1]
    r = jnp.arange(tile)[:, None]
    c = jnp.arange(tile)[None, :]
    diag = table[_t5_bucket(c - r)] - far
    diag = jnp.where(((c // CHUNK) <= (r // CHUNK))[:, :, None], diag, MASK_VALUE)
    left = table[_t5_bucket(c - r - tile)] - far
    return jnp.stack([diag, left], axis=0).transpose(3, 0, 1, 2)


def _attn_kernel(q_ref, k_ref, v_ref, bias_ref, lamv_ref, subln_ref, o_ref, m_ref, acc_ref,
                 *, lam_init):
    tq = q_ref.shape[0]
    tk = tq
    i = pl.program_id(2)

    q = q_ref[...]
    lane = lax.broadcasted_iota(jnp.int32, (tq, LANES), 1)
    zero = jnp.zeros_like(q)
    qs = jnp.concatenate([jnp.where(lane < LANES // 2, q, zero),
                          jnp.where(lane >= LANES // 2, q, zero)], axis=0)

    m_ref[...] = jnp.full(m_ref.shape, MASK_VALUE, F32)
    acc_ref[...] = jnp.zeros(acc_ref.shape, F32)

    def tile(j, bias):
        k0 = pl.multiple_of(j * tk, tk)
        kj = k_ref[pl.ds(k0, tk), :]
        vj = v_ref[pl.ds(k0, tk), :]
        s = _dot_nt(qs, kj)
        if bias is not None:
            s = s + jnp.concatenate([bias, bias], axis=0)
        m_prev = m_ref[...]
        m_next = jnp.maximum(m_prev, jnp.max(s, axis=-1, keepdims=True))
        p = jnp.exp(s - jnp.concatenate([m_next] * (tk // LANES), axis=1))
        alpha = jnp.exp(m_prev - m_next)
        acc_ref[...] = alpha * acc_ref[...] + _dot(p.astype(BF16), vj)
        m_ref[...] = m_next

    def far_body(j, carry):
        tile(j, None)
        return carry

    lax.fori_loop(0, jnp.maximum(i - 1, 0), far_body, 0)

    @pl.when(i >= 1)
    def _():
        tile(i - 1, bias_ref[0, 1])

    tile(i, bias_ref[0, 0])

    acc = acc_ref[...]
    a1 = acc[0:tq]
    a2 = acc[tq:2 * tq]
    l1 = a1[:, DIFF_DV:DIFF_DV + 1]
    l2 = a2[:, DIFF_DV:DIFF_DV + 1]
    lv = lamv_ref[...]
    lam = (jnp.exp(jnp.sum(lv[0:1] * lv[1:2], axis=-1, keepdims=True))
           - jnp.exp(jnp.sum(lv[2:3] * lv[3:4], axis=-1, keepdims=True)) + lam_init)
    o = a1 / l1 - lam * (a2 / l2)
    o = jnp.where(lane < DIFF_DV, o, 0.0)
    ms = jnp.sum(o * o, axis=-1, keepdims=True) * (1.0 / DIFF_DV)
    y = o * lax.rsqrt(ms + EPS) * subln_ref[...] * (1.0 - lam_init)
    o_ref[...] = y.astype(BF16)


def _attn(dq, dk, dv, bias_tiles, lamv, subln, batch, seq, lam_init):
    tq = min(ATT_TILE, seq)
    nq = seq // tq
    return pl.pallas_call(
        functools.partial(_attn_kernel, lam_init=lam_init),
        out_shape=jax.ShapeDtypeStruct((batch * seq, DIFF_LANES), BF16),
        grid=(batch, DIFF_HEADS, nq),
        in_specs=[
            pl.BlockSpec((tq, LANES), lambda b, h, i: (b * nq + i, h)),
            pl.BlockSpec((seq, LANES), lambda b, h, i: (b, h)),
            pl.BlockSpec((seq, LANES), lambda b, h, i: (b, h)),
            pl.BlockSpec((1, 2, tq, tq), lambda b, h, i: (h, 0, 0, 0)),
            pl.BlockSpec((4, LANES), lambda b, h, i: (0, 0)),
            pl.BlockSpec((1, LANES), lambda b, h, i: (0, 0)),
        ],
        out_specs=pl.BlockSpec((tq, LANES), lambda b, h, i: (b * nq + i, h)),
        scratch_shapes=[pltpu.VMEM((2 * tq, LANES), F32), pltpu.VMEM((2 * tq, LANES), F32)],
        compiler_params=pltpu.CompilerParams(
            dimension_semantics=("parallel", "parallel", "arbitrary"),
            vmem_limit_bytes=VMEM_LIMIT_BYTES),
        name="diff_attn",
    )(dq, dk, dv, bias_tiles, lamv, subln)


def _out_kernel(x_ref, gla_ref, diff_ref, conv_ref, w1_ref, w2_ref, w3_ref, o_ref):
    o_ref[...] = (x_ref[...] + _dot(gla_ref[...], w1_ref[...]) + _dot(diff_ref[...], w2_ref[...])
                  + _dot(conv_ref[...], w3_ref[...]))


def _out(x, gla_o, diff_o, conv_o, w1, w2, w3):
    t = x.shape[0]
    tm = min(OUT_ROWS, t)
    row = lambda width: pl.BlockSpec((tm, width), lambda i: (i, 0))
    return pl.pallas_call(
        _out_kernel,
        out_shape=jax.ShapeDtypeStruct((t, D_MODEL), F32),
        grid=(t // tm,),
        in_specs=[row(D_MODEL), row(GLA_WIDTH), row(DIFF_LANES), row(CONV_DIM),
                  _const_spec((GLA_WIDTH, D_MODEL)), _const_spec((DIFF_LANES, D_MODEL)),
                  _const_spec((CONV_DIM, D_MODEL))],
        out_specs=row(D_MODEL),
        compiler_params=pltpu.CompilerParams(
            dimension_semantics=("parallel",), vmem_limit_bytes=VMEM_LIMIT_BYTES),
        name="mixer_out",
    )(x, gla_o, diff_o, conv_o, w1, w2, w3)


def _pad_cols(w, width):
    return jnp.pad(w, ((0, 0), (0, width - w.shape[1])))


def _slot_cols(w, heads, used, slot):
    lead = w.shape[0]
    w = w.reshape(lead, heads, used)
    return jnp.pad(w, ((0, 0), (0, 0), (0, slot - used))).reshape(lead, heads * slot)


def _layout_w_in(w_in):
    offs = np.cumsum([0, GLA_HEADS * GLA_DK, GLA_HEADS * GLA_DK, GLA_WIDTH, GLA_RANK, GLA_WIDTH,
                      DIFF_HEADS * 2 * DIFF_DQK, DIFF_HEADS * 2 * DIFF_DQK, DIFF_WIDTH,
                      CONV_DIM, CONV_DIM, CONV_DIM])
    gq, gk, gv, glr, gate, dq, dk, dv, cb, cc, ch = [w_in[:, offs[n]:offs[n + 1]] for n in range(11)]
    half = LANES // 2
    cols = [
        _slot_cols(gq, GLA_HEADS, GLA_DK, HEAD_SLOT),
        _slot_cols(gk, GLA_HEADS, GLA_DK, HEAD_SLOT),
        _pad_cols(glr, LANES),
        gv,
        gate,
        _slot_cols(dq, 2 * DIFF_HEADS, DIFF_DQK, half),
        _slot_cols(dk, 2 * DIFF_HEADS, DIFF_DQK, half),
        _slot_cols(dv, DIFF_HEADS, DIFF_DV, LANES),
        cb, cc, ch,
    ]
    w = jnp.concatenate(cols, axis=1)
    assert w.shape[1] == N_PROJ
    return w.astype(BF16)


def kernel(x, ffn1_norm, ffn1_gate, ffn1_up, ffn1_down, mix_norm, w_in, gla_gk_up, gla_gk_bias,
           gla_norm, diff_lambda_q1, diff_lambda_k1, diff_lambda_q2, diff_lambda_k2, diff_subln,
           rel_bias, conv_w, w_out, ffn2_norm, ffn2_gate, ffn2_up, ffn2_down, final_norm):
    batch, seq, _ = x.shape
    depth = w_in.shape[0]
    t = batch * seq
    xf = x.reshape(t, D_MODEL)
    bias_tiles = _bias_tiles(rel_bias, min(ATT_TILE, seq))
    final_g = final_norm.reshape(1, D_MODEL)

    for l in range(depth):
        lam_init = 0.8 - 0.6 * math.exp(-0.3 * l)
        xf = _ffn(xf, ffn1_norm[l].reshape(1, D_MODEL), ffn1_gate[l].astype(BF16),
                  ffn1_up[l].astype(BF16), ffn1_down[l].astype(BF16), final_g, False)

        w_pad = _layout_w_in(w_in[l])
        up = _slot_cols(gla_gk_up[l], GLA_HEADS, GLA_DK, HEAD_SLOT)
        up = jnp.pad(up, ((0, LANES - GLA_RANK), (0, 0)))
        up_hi = up.astype(BF16)
        up_lo = (up - up_hi.astype(F32)).astype(BF16)
        gkup = jnp.stack([up_hi, up_lo], axis=0)
        gkb = _slot_cols(gla_gk_bias[l].reshape(1, -1), GLA_HEADS, GLA_DK, HEAD_SLOT)
        cw = jnp.pad(conv_w[l], ((0, 8 - CONV_WIDTH), (0, 0)))
        gq, gk, gg, gv, gate, dq, dk, dv, conv_o = _proj(
            xf, mix_norm[l].reshape(1, D_MODEL), w_pad, gkup, gkb, cw, seq)

        gn_tiled = jnp.tile(gla_norm[l], GLA_HEADS).reshape(1, GLA_WIDTH)
        gla_o = _gla(gq, gk, gg, gv, gate, gn_tiled, batch, seq)

        lamv = jnp.stack([diff_lambda_q1[l], diff_lambda_k1[l], diff_lambda_q2[l], diff_lambda_k2[l]])
        lamv = _pad_cols(lamv.astype(F32), LANES)
        subln = _pad_cols(diff_subln[l].reshape(1, DIFF_DV), LANES)
        diff_o = _attn(dq, dk, dv, bias_tiles, lamv, subln, batch, seq, lam_init)

        wo = w_out[l]
        w1 = wo[0:GLA_WIDTH].astype(BF16)
        w2 = wo[GLA_WIDTH:GLA_WIDTH + DIFF_WIDTH].reshape(DIFF_HEADS, DIFF_DV, D_MODEL)
        w2 = jnp.pad(w2, ((0, 0), (0, LANES - DIFF_DV), (0, 0))).reshape(DIFF_LANES, D_MODEL).astype(BF16)
        w3 = wo[GLA_WIDTH + DIFF_WIDTH:].astype(BF16)
        xf = _out(xf, gla_o, diff_o, conv_o, w1, w2, w3)

        xf = _ffn(xf, ffn2_norm[l].reshape(1, D_MODEL), ffn2_gate[l].astype(BF16),
                  ffn2_up[l].astype(BF16), ffn2_down[l].astype(BF16), final_g, l == depth - 1)

    return xf.reshape(batch, seq, D_MODEL)
```

```python
import functools
import math

import numpy as np
import jax
import jax.numpy as jnp
from jax import lax
from jax.experimental import pallas as pl
from jax.experimental.pallas import tpu as pltpu

F32 = jnp.float32
BF16 = jnp.bfloat16

D_MODEL = 1024
D_FF = 2816
EPS = 1e-6
CHUNK = 64
GLA_HEADS = 4
GLA_DK = 48
GLA_DV = 96
GLA_RANK = 16
GLA_GATE_NORM = 16.0
DIFF_HEADS = 4
DIFF_DQK = 48
DIFF_DV = 96
CONV_DIM = 256
CONV_WIDTH = 3
NUM_BUCKETS = 32
MAX_DISTANCE = 128
GLA_WIDTH = GLA_HEADS * GLA_DV
DIFF_WIDTH = DIFF_HEADS * DIFF_DV

LANES = 128
VMEM_LIMIT_BYTES = 56 * 1024 * 1024

HEAD_SLOT = 64
GLA_QK_LANES = GLA_HEADS * HEAD_SLOT
SUB = 16
N_SUB = CHUNK // SUB
KST_ROWS = SUB * (N_SUB * (N_SUB - 1) // 2)
DIFF_LANES = DIFF_HEADS * LANES
MASK_VALUE = -1e30
LOG2E = math.log2(math.e)

OFF_GQ = 0
OFF_GK = OFF_GQ + GLA_QK_LANES
OFF_GLR = OFF_GK + GLA_QK_LANES
OFF_GV = OFF_GLR + LANES
OFF_GATE = OFF_GV + GLA_WIDTH
OFF_DQ = OFF_GATE + GLA_WIDTH
OFF_DK = OFF_DQ + DIFF_LANES
OFF_DV = OFF_DK + DIFF_LANES
OFF_CB = OFF_DV + DIFF_LANES
OFF_CC = OFF_CB + CONV_DIM
OFF_CH = OFF_CC + CONV_DIM
N_PROJ = OFF_CH + CONV_DIM

FFN_ROWS = 512
FFN_F_CHUNK = 1408
PROJ_ROWS = 512
GLA_ROWS = 512
ATT_TILE = 256
OUT_ROWS = 512


def _dot(a, b):
    return jnp.dot(a, b, preferred_element_type=F32)


def _dot_nt(a, b):
    return lax.dot_general(a, b, (((1,), (1,)), ((), ())), preferred_element_type=F32)


def _dot_tn(a, b):
    return lax.dot_general(a, b, (((0,), (0,)), ((), ())), preferred_element_type=F32)


def _split_bf16(x):
    hi = x.astype(BF16)
    lo = (x - hi.astype(F32)).astype(BF16)
    return hi, lo


def _rms(x, g):
    return x * lax.rsqrt(jnp.mean(x * x, axis=-1, keepdims=True) + EPS) * g


def _sigmoid(x):
    return 1.0 / (1.0 + jnp.exp(-x))


def _ffn_kernel(x_ref, g_ref, wg_ref, wu_ref, wd_ref, fg_ref, o_ref, *, final):
    x = x_ref[...]
    h = _rms(x, g_ref[...]).astype(BF16)
    acc = None
    for c in range(D_FF // FFN_F_CHUNK):
        sl = slice(c * FFN_F_CHUNK, (c + 1) * FFN_F_CHUNK)
        gate = _dot(h, wg_ref[:, sl])
        up = _dot(h, wu_ref[:, sl])
        a = (gate * _sigmoid(gate) * up).astype(BF16)
        part = _dot(a, wd_ref[sl, :])
        acc = part if acc is None else acc + part
    y = x + 0.5 * acc
    if final:
        y = _rms(y, fg_ref[...])
    o_ref[...] = y


def _const_spec(shape):
    nd = len(shape)
    return pl.BlockSpec(shape, lambda *_: (0,) * nd, pipeline_mode=pl.Buffered(1))


def _ffn(x, g, wg, wu, wd, final_g, final):
    t = x.shape[0]
    tm = min(FFN_ROWS, t)
    return pl.pallas_call(
        functools.partial(_ffn_kernel, final=final),
        out_shape=jax.ShapeDtypeStruct((t, D_MODEL), F32),
        grid=(t // tm,),
        in_specs=[
            pl.BlockSpec((tm, D_MODEL), lambda i: (i, 0)),
            _const_spec((1, D_MODEL)),
            _const_spec((D_MODEL, D_FF)),
            _const_spec((D_MODEL, D_FF)),
            _const_spec((D_FF, D_MODEL)),
            _const_spec((1, D_MODEL)),
        ],
        out_specs=pl.BlockSpec((tm, D_MODEL), lambda i: (i, 0)),
        compiler_params=pltpu.CompilerParams(
            dimension_semantics=("parallel",), vmem_limit_bytes=VMEM_LIMIT_BYTES),
        name="ffn",
    )(x, g, wg, wu, wd, final_g)


def _proj_kernel(x_ref, g_ref, w_ref, gkup_ref, gkb_ref, cw_ref,
                 gq_ref, gk_ref, gg_ref, gv_ref, gate_ref, dq_ref, dk_ref, dv_ref, conv_ref,
                 zbuf_ref, *, tiles_per_seq):
    tm = x_ref.shape[0]
    h = _rms(x_ref[...], g_ref[...]).astype(BF16)

    def proj(off, width):
        return _dot(h, w_ref[:, off:off + width])

    gq_ref[...] = proj(OFF_GQ, GLA_QK_LANES)
    gk_ref[...] = proj(OFF_GK, GLA_QK_LANES)
    gv_ref[...] = proj(OFF_GV, GLA_WIDTH).astype(BF16)
    gate_ref[...] = proj(OFF_GATE, GLA_WIDTH)

    glr_hi, glr_lo = _split_bf16(proj(OFF_GLR, LANES))
    up_hi = gkup_ref[0]
    up_lo = gkup_ref[1]
    z = _dot(glr_hi, up_hi) + _dot(glr_lo, up_hi) + _dot(glr_hi, up_lo) + gkb_ref[...]
    logsig = jnp.minimum(z, 0.0) - jnp.log1p(jnp.exp(-jnp.abs(z)))
    lane = lax.broadcasted_iota(jnp.int32, (tm, GLA_QK_LANES), 1)
    gg_ref[...] = jnp.where(lane % HEAD_SLOT < GLA_DK, logsig * (1.0 / GLA_GATE_NORM), 0.0)

    dq_ref[...] = (proj(OFF_DQ, DIFF_LANES) * (DIFF_DQK ** -0.5 * LOG2E)).astype(BF16)
    dk_ref[...] = proj(OFF_DK, DIFF_LANES).astype(BF16)
    lane_v = lax.broadcasted_iota(jnp.int32, (tm, DIFF_LANES), 1)
    dv_ref[...] = jnp.where(lane_v % LANES == DIFF_DV, 1.0, proj(OFF_DV, DIFF_LANES)).astype(BF16)

    @pl.when(pl.program_id(0) % tiles_per_seq == 0)
    def _():
        zbuf_ref[0:8, :] = jnp.zeros((8, CONV_DIM), F32)

    zc = proj(OFF_CC, CONV_DIM) * proj(OFF_CH, CONV_DIM)
    zbuf_ref[8:8 + tm, :] = zc
    cw = cw_ref[...]
    y = (cw[2:3, :] * zc + cw[1:2, :] * zbuf_ref[7:7 + tm, :] + cw[0:1, :] * zbuf_ref[6:6 + tm, :])
    conv_ref[...] = (proj(OFF_CB, CONV_DIM) * y).astype(BF16)
    zbuf_ref[0:8, :] = zbuf_ref[tm:tm + 8, :]


def _proj(x, g, w, gkup, gkb, cw, seq):
    t = x.shape[0]
    tm = min(PROJ_ROWS, seq)
    row = lambda width: pl.BlockSpec((tm, width), lambda i: (i, 0))
    out_shapes = (
        jax.ShapeDtypeStruct((t, GLA_QK_LANES), F32),
        jax.ShapeDtypeStruct((t, GLA_QK_LANES), F32),
        jax.ShapeDtypeStruct((t, GLA_QK_LANES), F32),
        jax.ShapeDtypeStruct((t, GLA_WIDTH), BF16),
        jax.ShapeDtypeStruct((t, GLA_WIDTH), F32),
        jax.ShapeDtypeStruct((t, DIFF_LANES), BF16),
        jax.ShapeDtypeStruct((t, DIFF_LANES), BF16),
        jax.ShapeDtypeStruct((t, DIFF_LANES), BF16),
        jax.ShapeDtypeStruct((t, CONV_DIM), BF16),
    )
    return pl.pallas_call(
        functools.partial(_proj_kernel, tiles_per_seq=seq // tm),
        out_shape=out_shapes,
        grid=(t // tm,),
        in_specs=[
            row(D_MODEL),
            _const_spec((1, D_MODEL)),
            _const_spec((D_MODEL, N_PROJ)),
            _const_spec((2, LANES, GLA_QK_LANES)),
            _const_spec((1, GLA_QK_LANES)),
            _const_spec((8, CONV_DIM)),
        ],
        out_specs=tuple(row(s.shape[1]) for s in out_shapes),
        scratch_shapes=[pltpu.VMEM((tm + 8, CONV_DIM), F32)],
        compiler_params=pltpu.CompilerParams(
            dimension_semantics=("arbitrary",), vmem_limit_bytes=VMEM_LIMIT_BYTES),
        name="mixer_proj",
    )(x, g, w, gkup, gkb, cw)


def _gla_cum_matrix():
    t = np.arange(CHUNK)[:, None]
    u = np.arange(CHUNK)[None, :]
    tri = (u <= t)
    ref = (u <= (t // SUB) * SUB - 1)
    ones = np.ones((CHUNK, CHUNK), bool)
    parts = [tri, ref, ones]
    for j in range(1, N_SUB):
        s = np.arange(SUB * j)[:, None]
        parts.append((u > s) & (u <= SUB * j - 1))
    return np.concatenate(parts, axis=0).astype(np.float32)


def _gla_score_mask():
    t = np.arange(CHUNK)[:, None]
    m = np.arange(KST_ROWS)
    group = np.concatenate([np.full(SUB * j, j) for j in range(1, N_SUB)])
    keep = (group[None, :] == (t // SUB))
    return np.tile(keep, (1, GLA_HEADS)).astype(np.float32)


def _gla_kernel(q_ref, k_ref, g_ref, v_ref, gate_ref, cum_ref, amask_ref, gn_ref,
                o_ref, s_ref, raw_ref):
    rows = q_ref.shape[0]
    n_chunks = rows // CHUNK

    @pl.when(pl.program_id(1) == 0)
    def _():
        s_ref[...] = jnp.zeros_like(s_ref)

    cum = cum_ref[...]
    amask = amask_ref[...] > 0.5
    qk_head = lax.broadcasted_iota(jnp.int32, (1, GLA_QK_LANES), 1) // HEAD_SLOT
    v_head = lax.broadcasted_iota(jnp.int32, (1, GLA_WIDTH), 1) // GLA_DV
    row_head_q = lax.broadcasted_iota(jnp.int32, (GLA_QK_LANES, 1), 0) // HEAD_SLOT
    state_mask = row_head_q == v_head
    ind_r = lax.broadcasted_iota(jnp.int32, (GLA_QK_LANES, LANES), 0) // HEAD_SLOT
    ind_c = lax.broadcasted_iota(jnp.int32, (GLA_QK_LANES, LANES), 1)
    ind_sum = jnp.where(ind_r == ind_c, 1.0, 0.0).astype(BF16)
    bc_r = lax.broadcasted_iota(jnp.int32, (LANES, GLA_WIDTH), 0)
    bc_c = lax.broadcasted_iota(jnp.int32, (LANES, GLA_WIDTH), 1) // GLA_DV
    ind_bcast = jnp.where(bc_r == bc_c, 1.0, 0.0).astype(BF16)
    row_in_sub = lax.broadcasted_iota(jnp.int32, (CHUNK, 1), 0) % SUB
    scale = GLA_DK ** -0.5

    def chunk_body(c, carry):
        r0 = pl.multiple_of(c * CHUNK, CHUNK)
        q = q_ref[pl.ds(r0, CHUNK), :] * scale
        k = k_ref[pl.ds(r0, CHUNK), :]
        g = g_ref[pl.ds(r0, CHUNK), :]
        v = v_ref[pl.ds(r0, CHUNK), :]
        g_hi, g_lo = _split_bf16(g)
        cums = _dot(cum, g_hi) + _dot(cum, g_lo)
        b = cums[0:CHUNK]
        b_ref = cums[CHUNK:2 * CHUNK]
        b_last = cums[2 * CHUNK:3 * CHUNK]
        e_kst = cums[3 * CHUNK:3 * CHUNK + KST_ROWS]

        state = s_ref[...]
        q_in = (q * jnp.exp(b)).astype(BF16)
        o = _dot(q_in, state.astype(BF16))

        kw = (k * jnp.exp(b_last - b)).astype(BF16)
        kv = _dot_tn(kw, v)
        decay_rows = jnp.exp(jnp.concatenate([b_last, b_last], axis=0))
        decay_cols = decay_rows.T[:, 0:1]
        s_ref[...] = state * decay_cols + jnp.where(state_mask, kv, 0.0)

        q_sub = (q * jnp.exp(jnp.minimum(b - b_ref, 0.0))).astype(BF16)
        k_st = jnp.concatenate([k[0:SUB * j] for j in range(1, N_SUB)], axis=0) * jnp.exp(e_kst)
        k_bd = jnp.concatenate(
            [jnp.where(qk_head == hd, k_st, 0.0) for hd in range(GLA_HEADS)], axis=0).astype(BF16)
        a = _dot_nt(q_sub, k_bd)
        a = jnp.where(amask, a, 0.0).astype(BF16)
        v_st = jnp.concatenate([v[0:SUB * j] for j in range(1, N_SUB)], axis=0)
        v_bd = jnp.concatenate(
            [jnp.where(v_head == hd, v_st, jnp.zeros_like(v_st)) for hd in range(GLA_HEADS)], axis=0)
        o = o + _dot(a, v_bd)

        vf = v.astype(F32)
        prods = []
        for d in range(SUB):
            ks = k if d == 0 else pltpu.roll(k, d, 0)
            bs = b if d == 0 else pltpu.roll(b, d, 0)
            prods.append((q * ks * jnp.exp(jnp.minimum(b - bs, 0.0))).astype(BF16))
        dsum = _dot(jnp.concatenate(prods, axis=0), ind_sum)
        valid = jnp.concatenate([row_in_sub >= d for d in range(SUB)], axis=0)
        dsum = jnp.where(valid, dsum, 0.0).astype(BF16)
        dbc = _dot(dsum, ind_bcast)
        for d in range(SUB):
            vs = vf if d == 0 else pltpu.roll(vf, d, 0)
            o = o + dbc[d * CHUNK:(d + 1) * CHUNK] * vs
        raw_ref[pl.ds(r0, CHUNK), :] = o
        return carry

    lax.fori_loop(0, n_chunks, chunk_body, 0)

    o = raw_ref[...]
    hr = lax.broadcasted_iota(jnp.int32, (GLA_WIDTH, GLA_WIDTH), 0) // GLA_DV
    hc = lax.broadcasted_iota(jnp.int32, (GLA_WIDTH, GLA_WIDTH), 1) // GLA_DV
    head_ones = jnp.where(hr == hc, 1.0, 0.0).astype(BF16)
    sq_hi, sq_lo = _split_bf16(o * o)
    ms = (_dot(sq_hi, head_ones) + _dot(sq_lo, head_ones)) * (1.0 / GLA_DV)
    gate = gate_ref[...]
    y = o * lax.rsqrt(ms + EPS) * gn_ref[...] * (gate * _sigmoid(gate))
    o_ref[...] = y.astype(BF16)


def _gla(gq, gk, gg, gv, gate, gn_tiled, batch, seq):
    rows = min(GLA_ROWS, seq)
    steps = seq // rows
    row = lambda width: pl.BlockSpec((rows, width), lambda b, i: (b * steps + i, 0))
    cum = jnp.asarray(_gla_cum_matrix(), BF16)
    amask = jnp.asarray(_gla_score_mask(), F32)
    return pl.pallas_call(
        _gla_kernel,
        out_shape=jax.ShapeDtypeStruct((batch * seq, GLA_WIDTH), BF16),
        grid=(batch, steps),
        in_specs=[
            row(GLA_QK_LANES), row(GLA_QK_LANES), row(GLA_QK_LANES), row(GLA_WIDTH), row(GLA_WIDTH),
            pl.BlockSpec(cum.shape, lambda b, i: (0, 0)),
            pl.BlockSpec(amask.shape, lambda b, i: (0, 0)),
            pl.BlockSpec((1, GLA_WIDTH), lambda b, i: (0, 0)),
        ],
        out_specs=row(GLA_WIDTH),
        scratch_shapes=[pltpu.VMEM((GLA_QK_LANES, GLA_WIDTH), F32),
                        pltpu.VMEM((rows, GLA_WIDTH), F32)],
        compiler_params=pltpu.CompilerParams(
            dimension_semantics=("arbitrary", "arbitrary"), vmem_limit_bytes=VMEM_LIMIT_BYTES),
        name="gla",
    )(gq, gk, gg, gv, gate, cum, amask, gn_tiled)


def _t5_bucket(rel):
    nb = NUM_BUCKETS // 2
    max_exact = nb // 2
    ret = (rel > 0).astype(jnp.int32) * nb
    n = jnp.abs(rel)
    nf = jnp.maximum(n, 1).astype(jnp.float32)
    large = max_exact + (jnp.log(nf / max_exact) / math.log(MAX_DISTANCE / max_exact)
                         * (nb - max_exact)).astype(jnp.int32)
    large = jnp.minimum(large, nb - 1)
    return ret + jnp.where(n < max_exact, n, large)


def _bias_tiles(rel_bias, tile):
    assert tile >= MAX_DISTANCE and tile % CHUNK == 0
    table = rel_bias.astype(F32)
    far = table[NUM_BUCKETS // 2 - 1]
    r = jnp.arange(tile)[:, None]
    c = jnp.arange(tile)[None, :]

    def lookup(rel):
        onehot = _t5_bucket(rel)[None, :, :, None] == jnp.arange(NUM_BUCKETS)[None, None, None, :]
        vals = jnp.where(onehot, table.T[:, None, None, :], 0.0)
        return (jnp.sum(vals, axis=-1) - far[:, None, None]) * LOG2E

    diag = jnp.where(((c // CHUNK) <= (r // CHUNK))[None], lookup(c - r), MASK_VALUE)
    left = lookup(c - r - tile)
    return jnp.stack([left, diag], axis=1)


def _attn_kernel(q_ref, k_ref, v_ref, bias_ref, lamv_ref, subln_ref, o_ref,
                 s_buf, p_buf, alpha_buf, m_all, acc_all, *, tile, lam_init):
    seq = q_ref.shape[0]
    nq = seq // tile

    s_buf[...] = jnp.zeros(s_buf.shape, F32)
    p_buf[...] = jnp.zeros(p_buf.shape, BF16)
    alpha_buf[...] = jnp.zeros(alpha_buf.shape, F32)
    m_all[...] = jnp.full(m_all.shape, MASK_VALUE, F32)
    acc_all[...] = jnp.zeros(acc_all.shape, F32)

    lane = lax.broadcasted_iota(jnp.int32, (tile, LANES), 1)

    def stage1(i, j, slot):
        q = q_ref[pl.ds(pl.multiple_of(i * tile, tile), tile), :]
        zero = jnp.zeros_like(q)
        qs = jnp.concatenate([jnp.where(lane < LANES // 2, q, zero),
                              jnp.where(lane >= LANES // 2, q, zero)], axis=0)
        kj = k_ref[pl.ds(pl.multiple_of(j * tile, tile), tile), :]
        s_buf[slot] = _dot_nt(qs, kj)

    def stage2(i, bias, slot):
        for half in range(2):
            rows = slice(half * tile, (half + 1) * tile)
            s = s_buf[slot, rows, :]
            if bias is not None:
                s = s + bias
            m_prev = m_all[i, rows, :]
            m_next = jnp.maximum(m_prev, jnp.max(s, axis=-1, keepdims=True))
            p = jnp.exp2(s - jnp.concatenate([m_next] * (tile // LANES), axis=1))
            p_buf[slot, rows, :] = p.astype(BF16)
            alpha_buf[slot, rows, :] = jnp.exp2(m_prev - m_next)
            m_all[i, rows, :] = m_next

    def stage3(i, j, slot):
        vj = v_ref[pl.ds(pl.multiple_of(j * tile, tile), tile), :]
        acc_all[i] = alpha_buf[slot] * acc_all[i] + _dot(p_buf[slot], vj)

    def run_pipeline(n_pairs, first_pair, advance, with_bias):
        n_steps = n_pairs + 2
        n_steps += n_steps % 2

        def step(t, pairs, slot_a, slot_b):
            (i0, j0), (i1, j1), (i2, j2) = pairs
            valid1 = jnp.logical_and(t >= 1, t <= n_pairs)
            valid2 = jnp.logical_and(t >= 2, t <= n_pairs + 1)
            stage3(jnp.where(valid2, i2, nq), j2, slot_a)
            stage2(jnp.where(valid1, i1, nq), bias_ref[0, j1 - i1 + 1] if with_bias else None, slot_b)
            stage1(i0, j0, slot_a)
            return (advance(i0, j0), (i0, j0), (i1, j1))

        def body(u, pairs):
            pairs = step(2 * u, pairs, 0, 1)
            return step(2 * u + 1, pairs, 1, 0)

        lax.fori_loop(0, n_steps // 2, body, (first_pair,) * 3)

    def next_far(i, j):
        wrap = j == i - 2
        done = jnp.logical_and(wrap, i == nq - 1)
        step_i = jnp.logical_and(wrap, jnp.logical_not(done))
        return (jnp.where(step_i, i + 1, i), jnp.where(done, j, jnp.where(wrap, 0, j + 1)))

    def next_near(i, j):
        wrap = j == i
        done = jnp.logical_and(wrap, i == nq - 1)
        step_i = jnp.logical_and(wrap, jnp.logical_not(done))
        return (jnp.where(step_i, i + 1, i), jnp.where(jnp.logical_or(done, wrap), j, j + 1))

    zero = jnp.int32(0)
    if nq > 2:
        run_pipeline((nq - 1) * (nq - 2) // 2, (jnp.int32(2), zero), next_far, False)
    run_pipeline(2 * nq - 1, (zero, zero), next_near, True)

    def finish(i, carry):
        acc = acc_all[i]
        a1 = acc[0:tile]
        a2 = acc[tile:2 * tile]
        l1 = a1[:, DIFF_DV:DIFF_DV + 1]
        l2 = a2[:, DIFF_DV:DIFF_DV + 1]
        lv = lamv_ref[...]
        lam = (jnp.exp(jnp.sum(lv[0:1] * lv[1:2], axis=-1, keepdims=True))
               - jnp.exp(jnp.sum(lv[2:3] * lv[3:4], axis=-1, keepdims=True)) + lam_init)
        o = a1 / l1 - lam * (a2 / l2)
        o = jnp.where(lane < DIFF_DV, o, 0.0)
        ms = jnp.sum(o * o, axis=-1, keepdims=True) * (1.0 / DIFF_DV)
        y = o * lax.rsqrt(ms + EPS) * subln_ref[...] * (1.0 - lam_init)
        o_ref[pl.ds(pl.multiple_of(i * tile, tile), tile), :] = y.astype(BF16)
        return carry

    lax.fori_loop(0, nq, finish, 0)


def _attn(dq, dk, dv, bias_tiles, lamv, subln, batch, seq, lam_init):
    tile = min(ATT_TILE, seq)
    nq = seq // tile
    head_block = pl.BlockSpec((seq, LANES), lambda b, h: (b, h))
    return pl.pallas_call(
        functools.partial(_attn_kernel, tile=tile, lam_init=lam_init),
        out_shape=jax.ShapeDtypeStruct((batch * seq, DIFF_LANES), BF16),
        grid=(batch, DIFF_HEADS),
        in_specs=[
            head_block, head_block, head_block,
            pl.BlockSpec((1, 2, tile, tile), lambda b, h: (h, 0, 0, 0)),
            pl.BlockSpec((4, LANES), lambda b, h: (0, 0)),
            pl.BlockSpec((1, LANES), lambda b, h: (0, 0)),
        ],
        out_specs=head_block,
        scratch_shapes=[pltpu.VMEM((2, 2 * tile, tile), F32),
                        pltpu.VMEM((2, 2 * tile, tile), BF16),
                        pltpu.VMEM((2, 2 * tile, LANES), F32),
                        pltpu.VMEM((nq + 1, 2 * tile, LANES), F32),
                        pltpu.VMEM((nq + 1, 2 * tile, LANES), F32)],
        compiler_params=pltpu.CompilerParams(
            dimension_semantics=("parallel", "parallel"), vmem_limit_bytes=VMEM_LIMIT_BYTES),
        name="diff_attn",
    )(dq, dk, dv, bias_tiles, lamv, subln)


def _out_kernel(x_ref, gla_ref, diff_ref, conv_ref, w1_ref, w2_ref, w3_ref, o_ref):
    o_ref[...] = (x_ref[...] + _dot(gla_ref[...], w1_ref[...]) + _dot(diff_ref[...], w2_ref[...])
                  + _dot(conv_ref[...], w3_ref[...]))


def _out(x, gla_o, diff_o, conv_o, w1, w2, w3):
    t = x.shape[0]
    tm = min(OUT_ROWS, t)
    row = lambda width: pl.BlockSpec((tm, width), lambda i: (i, 0))
    return pl.pallas_call(
        _out_kernel,
        out_shape=jax.ShapeDtypeStruct((t, D_MODEL), F32),
        grid=(t // tm,),
        in_specs=[row(D_MODEL), row(GLA_WIDTH), row(DIFF_LANES), row(CONV_DIM),
                  _const_spec((GLA_WIDTH, D_MODEL)), _const_spec((DIFF_LANES, D_MODEL)),
                  _const_spec((CONV_DIM, D_MODEL))],
        out_specs=row(D_MODEL),
        compiler_params=pltpu.CompilerParams(
            dimension_semantics=("parallel",), vmem_limit_bytes=VMEM_LIMIT_BYTES),
        name="mixer_out",
    )(x, gla_o, diff_o, conv_o, w1, w2, w3)


def _pad_cols(w, width):
    return jnp.pad(w, ((0, 0), (0, width - w.shape[1])))


def _slot_cols(w, heads, used, slot):
    lead = w.shape[0]
    w = w.reshape(lead, heads, used)
    return jnp.pad(w, ((0, 0), (0, 0), (0, slot - used))).reshape(lead, heads * slot)


def _layout_w_in(w_in):
    offs = np.cumsum([0, GLA_HEADS * GLA_DK, GLA_HEADS * GLA_DK, GLA_WIDTH, GLA_RANK, GLA_WIDTH,
                      DIFF_HEADS * 2 * DIFF_DQK, DIFF_HEADS * 2 * DIFF_DQK, DIFF_WIDTH,
                      CONV_DIM, CONV_DIM, CONV_DIM])
    gq, gk, gv, glr, gate, dq, dk, dv, cb, cc, ch = [w_in[:, offs[n]:offs[n + 1]] for n in range(11)]
    half = LANES // 2
    cols = [
        _slot_cols(gq, GLA_HEADS, GLA_DK, HEAD_SLOT),
        _slot_cols(gk, GLA_HEADS, GLA_DK, HEAD_SLOT),
        _pad_cols(glr, LANES),
        gv,
        gate,
        _slot_cols(dq, 2 * DIFF_HEADS, DIFF_DQK, half),
        _slot_cols(dk, 2 * DIFF_HEADS, DIFF_DQK, half),
        _slot_cols(dv, DIFF_HEADS, DIFF_DV, LANES),
        cb, cc, ch,
    ]
    w = jnp.concatenate(cols, axis=1)
    assert w.shape[1] == N_PROJ
    return w.astype(BF16)


def kernel(x, ffn1_norm, ffn1_gate, ffn1_up, ffn1_down, mix_norm, w_in, gla_gk_up, gla_gk_bias,
           gla_norm, diff_lambda_q1, diff_lambda_k1, diff_lambda_q2, diff_lambda_k2, diff_subln,
           rel_bias, conv_w, w_out, ffn2_norm, ffn2_gate, ffn2_up, ffn2_down, final_norm):
    batch, seq, _ = x.shape
    depth = w_in.shape[0]
    t = batch * seq
    xf = x.reshape(t, D_MODEL)
    bias_tiles = _bias_tiles(rel_bias, min(ATT_TILE, seq))
    final_g = final_norm.reshape(1, D_MODEL)

    for l in range(depth):
        lam_init = 0.8 - 0.6 * math.exp(-0.3 * l)
        xf = _ffn(xf, ffn1_norm[l].reshape(1, D_MODEL), ffn1_gate[l].astype(BF16),
                  ffn1_up[l].astype(BF16), ffn1_down[l].astype(BF16), final_g, False)

        w_pad = _layout_w_in(w_in[l])
        up = _slot_cols(gla_gk_up[l], GLA_HEADS, GLA_DK, HEAD_SLOT)
        up = jnp.pad(up, ((0, LANES - GLA_RANK), (0, 0)))
        up_hi = up.astype(BF16)
        up_lo = (up - up_hi.astype(F32)).astype(BF16)
        gkup = jnp.stack([up_hi, up_lo], axis=0)
        gkb = _slot_cols(gla_gk_bias[l].reshape(1, -1), GLA_HEADS, GLA_DK, HEAD_SLOT)
        cw = jnp.pad(conv_w[l], ((0, 8 - CONV_WIDTH), (0, 0)))
        gq, gk, gg, gv, gate, dq, dk, dv, conv_o = _proj(
            xf, mix_norm[l].reshape(1, D_MODEL), w_pad, gkup, gkb, cw, seq)

        gn_tiled = jnp.tile(gla_norm[l], GLA_HEADS).reshape(1, GLA_WIDTH)
        gla_o = _gla(gq, gk, gg, gv, gate, gn_tiled, batch, seq)

        lamv = jnp.stack([diff_lambda_q1[l], diff_lambda_k1[l], diff_lambda_q2[l], diff_lambda_k2[l]])
        lamv = _pad_cols(lamv.astype(F32), LANES)
        subln = _pad_cols(diff_subln[l].reshape(1, DIFF_DV), LANES)
        diff_o = _attn(dq, dk, dv, bias_tiles, lamv, subln, batch, seq, lam_init)

        wo = w_out[l]
        w1 = wo[0:GLA_WIDTH].astype(BF16)
        w2 = wo[GLA_WIDTH:GLA_WIDTH + DIFF_WIDTH].reshape(DIFF_HEADS, DIFF_DV, D_MODEL)
        w2 = jnp.pad(w2, ((0, 0), (0, LANES - DIFF_DV), (0, 0))).reshape(DIFF_LANES, D_MODEL).astype(BF16)
        w3 = wo[GLA_WIDTH + DIFF_WIDTH:].astype(BF16)
        xf = _out(xf, gla_o, diff_o, conv_o, w1, w2, w3)

        xf = _ffn(xf, ffn2_norm[l].reshape(1, D_MODEL), ffn2_gate[l].astype(BF16),
                  ffn2_up[l].astype(BF16), ffn2_down[l].astype(BF16), final_g, l == depth - 1)

    return xf.reshape(batch, seq, D_MODEL)
```

```python
import functools
import math

import numpy as np
import jax
import jax.numpy as jnp
from jax import lax
from jax.experimental import pallas as pl
from jax.experimental.pallas import tpu as pltpu

F32 = jnp.float32
BF16 = jnp.bfloat16

D_MODEL = 1024
D_FF = 2816
EPS = 1e-6
CHUNK = 64
GLA_HEADS = 4
GLA_DK = 48
GLA_DV = 96
GLA_RANK = 16
GLA_GATE_NORM = 16.0
DIFF_HEADS = 4
DIFF_DQK = 48
DIFF_DV = 96
CONV_DIM = 256
CONV_WIDTH = 3
NUM_BUCKETS = 32
MAX_DISTANCE = 128
GLA_WIDTH = GLA_HEADS * GLA_DV
DIFF_WIDTH = DIFF_HEADS * DIFF_DV

LANES = 128
VMEM_LIMIT_BYTES = 56 * 1024 * 1024

HEAD_SLOT = 64
GLA_QK_LANES = GLA_HEADS * HEAD_SLOT
SUB = 16
N_SUB = CHUNK // SUB
KST_ROWS = SUB * (N_SUB * (N_SUB - 1) // 2)
DIFF_LANES = DIFF_HEADS * LANES
MASK_VALUE = -1e30
LOG2E = math.log2(math.e)

OFF_GQ = 0
OFF_GK = OFF_GQ + GLA_QK_LANES
OFF_GLR = OFF_GK + GLA_QK_LANES
OFF_GV = OFF_GLR + LANES
OFF_GATE = OFF_GV + GLA_WIDTH
OFF_DQ = OFF_GATE + GLA_WIDTH
OFF_DK = OFF_DQ + DIFF_LANES
OFF_DV = OFF_DK + DIFF_LANES
OFF_CB = OFF_DV + DIFF_LANES
OFF_CC = OFF_CB + CONV_DIM
OFF_CH = OFF_CC + CONV_DIM
N_PROJ = OFF_CH + CONV_DIM

FFN_ROWS = 512
FFN_F_CHUNK = 1408
PROJ_ROWS = 512
GLA_ROWS = 512
ATT_TILE = 256
ATT_UNROLL = 8
OUT_ROWS = 512


def _dot(a, b):
    return jnp.dot(a, b, preferred_element_type=F32)


def _dot_nt(a, b):
    return lax.dot_general(a, b, (((1,), (1,)), ((), ())), preferred_element_type=F32)


def _dot_tn(a, b):
    return lax.dot_general(a, b, (((0,), (0,)), ((), ())), preferred_element_type=F32)


def _split_bf16(x):
    hi = x.astype(BF16)
    lo = (x - hi.astype(F32)).astype(BF16)
    return hi, lo


def _rms(x, g):
    return x * lax.rsqrt(jnp.mean(x * x, axis=-1, keepdims=True) + EPS) * g


def _sigmoid(x):
    return 1.0 / (1.0 + jnp.exp(-x))


def _ffn_kernel(x_ref, g_ref, wg_ref, wu_ref, wd_ref, fg_ref, o_ref, *, final):
    x = x_ref[...]
    h = _rms(x, g_ref[...]).astype(BF16)
    acc = None
    for c in range(D_FF // FFN_F_CHUNK):
        sl = slice(c * FFN_F_CHUNK, (c + 1) * FFN_F_CHUNK)
        gate = _dot(h, wg_ref[:, sl])
        up = _dot(h, wu_ref[:, sl])
        a = (gate * _sigmoid(gate) * up).astype(BF16)
        part = _dot(a, wd_ref[sl, :])
        acc = part if acc is None else acc + part
    y = x + 0.5 * acc
    if final:
        y = _rms(y, fg_ref[...])
    o_ref[...] = y


def _const_spec(shape):
    nd = len(shape)
    return pl.BlockSpec(shape, lambda *_: (0,) * nd, pipeline_mode=pl.Buffered(1))


def _ffn(x, g, wg, wu, wd, final_g, final):
    t = x.shape[0]
    tm = min(FFN_ROWS, t)
    return pl.pallas_call(
        functools.partial(_ffn_kernel, final=final),
        out_shape=jax.ShapeDtypeStruct((t, D_MODEL), F32),
        grid=(t // tm,),
        in_specs=[
            pl.BlockSpec((tm, D_MODEL), lambda i: (i, 0)),
            _const_spec((1, D_MODEL)),
            _const_spec((D_MODEL, D_FF)),
            _const_spec((D_MODEL, D_FF)),
            _const_spec((D_FF, D_MODEL)),
            _const_spec((1, D_MODEL)),
        ],
        out_specs=pl.BlockSpec((tm, D_MODEL), lambda i: (i, 0)),
        compiler_params=pltpu.CompilerParams(
            dimension_semantics=("parallel",), vmem_limit_bytes=VMEM_LIMIT_BYTES),
        name="ffn",
    )(x, g, wg, wu, wd, final_g)


def _proj_kernel(x_ref, g_ref, w_ref, gkup_ref, gkb_ref, cw_ref,
                 gq_ref, gk_ref, gg_ref, gv_ref, gate_ref, dq_ref, dk_ref, dv_ref, conv_ref,
                 zbuf_ref, *, tiles_per_seq):
    tm = x_ref.shape[0]
    h = _rms(x_ref[...], g_ref[...]).astype(BF16)

    def proj(off, width):
        return _dot(h, w_ref[:, off:off + width])

    gq_ref[...] = proj(OFF_GQ, GLA_QK_LANES)
    gk_ref[...] = proj(OFF_GK, GLA_QK_LANES)
    gv_ref[...] = proj(OFF_GV, GLA_WIDTH).astype(BF16)
    gate_ref[...] = proj(OFF_GATE, GLA_WIDTH)

    glr_hi, glr_lo = _split_bf16(proj(OFF_GLR, LANES))
    up_hi = gkup_ref[0]
    up_lo = gkup_ref[1]
    z = _dot(glr_hi, up_hi) + _dot(glr_lo, up_hi) + _dot(glr_hi, up_lo) + gkb_ref[...]
    logsig = jnp.minimum(z, 0.0) - jnp.log1p(jnp.exp(-jnp.abs(z)))
    lane = lax.broadcasted_iota(jnp.int32, (tm, GLA_QK_LANES), 1)
    gg_ref[...] = jnp.where(lane % HEAD_SLOT < GLA_DK, logsig * (1.0 / GLA_GATE_NORM), 0.0)

    dq_ref[...] = (proj(OFF_DQ, DIFF_LANES) * (DIFF_DQK ** -0.5 * LOG2E)).astype(BF16)
    dk_ref[...] = proj(OFF_DK, DIFF_LANES).astype(BF16)
    lane_v = lax.broadcasted_iota(jnp.int32, (tm, DIFF_LANES), 1)
    dv_ref[...] = jnp.where(lane_v % LANES == DIFF_DV, 1.0, proj(OFF_DV, DIFF_LANES)).astype(BF16)

    @pl.when(pl.program_id(0) % tiles_per_seq == 0)
    def _():
        zbuf_ref[0:8, :] = jnp.zeros((8, CONV_DIM), F32)

    zc = proj(OFF_CC, CONV_DIM) * proj(OFF_CH, CONV_DIM)
    zbuf_ref[8:8 + tm, :] = zc
    cw = cw_ref[...]
    y = (cw[2:3, :] * zc + cw[1:2, :] * zbuf_ref[7:7 + tm, :] + cw[0:1, :] * zbuf_ref[6:6 + tm, :])
    conv_ref[...] = (proj(OFF_CB, CONV_DIM) * y).astype(BF16)
    zbuf_ref[0:8, :] = zbuf_ref[tm:tm + 8, :]


def _proj(x, g, w, gkup, gkb, cw, seq):
    t = x.shape[0]
    tm = min(PROJ_ROWS, seq)
    row = lambda width: pl.BlockSpec((tm, width), lambda i: (i, 0))
    out_shapes = (
        jax.ShapeDtypeStruct((t, GLA_QK_LANES), F32),
        jax.ShapeDtypeStruct((t, GLA_QK_LANES), F32),
        jax.ShapeDtypeStruct((t, GLA_QK_LANES), F32),
        jax.ShapeDtypeStruct((t, GLA_WIDTH), BF16),
        jax.ShapeDtypeStruct((t, GLA_WIDTH), F32),
        jax.ShapeDtypeStruct((t, DIFF_LANES), BF16),
        jax.ShapeDtypeStruct((t, DIFF_LANES), BF16),
        jax.ShapeDtypeStruct((t, DIFF_LANES), BF16),
        jax.ShapeDtypeStruct((t, CONV_DIM), BF16),
    )
    return pl.pallas_call(
        functools.partial(_proj_kernel, tiles_per_seq=seq // tm),
        out_shape=out_shapes,
        grid=(t // tm,),
        in_specs=[
            row(D_MODEL),
            _const_spec((1, D_MODEL)),
            _const_spec((D_MODEL, N_PROJ)),
            _const_spec((2, LANES, GLA_QK_LANES)),
            _const_spec((1, GLA_QK_LANES)),
            _const_spec((8, CONV_DIM)),
        ],
        out_specs=tuple(row(s.shape[1]) for s in out_shapes),
        scratch_shapes=[pltpu.VMEM((tm + 8, CONV_DIM), F32)],
        compiler_params=pltpu.CompilerParams(
            dimension_semantics=("arbitrary",), vmem_limit_bytes=VMEM_LIMIT_BYTES),
        name="mixer_proj",
    )(x, g, w, gkup, gkb, cw)


def _gla_cum_matrix():
    t = np.arange(CHUNK)[:, None]
    u = np.arange(CHUNK)[None, :]
    tri = (u <= t)
    ref = (u <= (t // SUB) * SUB - 1)
    ones = np.ones((CHUNK, CHUNK), bool)
    parts = [tri, ref, ones]
    for j in range(1, N_SUB):
        s = np.arange(SUB * j)[:, None]
        parts.append((u > s) & (u <= SUB * j - 1))
    return np.concatenate(parts, axis=0).astype(np.float32)


def _gla_score_mask():
    t = np.arange(CHUNK)[:, None]
    m = np.arange(KST_ROWS)
    group = np.concatenate([np.full(SUB * j, j) for j in range(1, N_SUB)])
    keep = (group[None, :] == (t // SUB))
    return np.tile(keep, (1, GLA_HEADS)).astype(np.float32)


def _gla_kernel(q_ref, k_ref, g_ref, v_ref, gate_ref, cum_ref, amask_ref, gn_ref,
                o_ref, s_ref, raw_ref):
    rows = q_ref.shape[0]
    n_chunks = rows // CHUNK

    @pl.when(pl.program_id(1) == 0)
    def _():
        s_ref[...] = jnp.zeros_like(s_ref)

    cum = cum_ref[...]
    amask = amask_ref[...] > 0.5
    qk_head = lax.broadcasted_iota(jnp.int32, (1, GLA_QK_LANES), 1) // HEAD_SLOT
    v_head = lax.broadcasted_iota(jnp.int32, (1, GLA_WIDTH), 1) // GLA_DV
    row_head_q = lax.broadcasted_iota(jnp.int32, (GLA_QK_LANES, 1), 0) // HEAD_SLOT
    state_mask = row_head_q == v_head
    ind_r = lax.broadcasted_iota(jnp.int32, (GLA_QK_LANES, LANES), 0) // HEAD_SLOT
    ind_c = lax.broadcasted_iota(jnp.int32, (GLA_QK_LANES, LANES), 1)
    ind_sum = jnp.where(ind_r == ind_c, 1.0, 0.0).astype(BF16)
    bc_r = lax.broadcasted_iota(jnp.int32, (LANES, GLA_WIDTH), 0)
    bc_c = lax.broadcasted_iota(jnp.int32, (LANES, GLA_WIDTH), 1) // GLA_DV
    ind_bcast = jnp.where(bc_r == bc_c, 1.0, 0.0).astype(BF16)
    row_in_sub = lax.broadcasted_iota(jnp.int32, (CHUNK, 1), 0) % SUB
    scale = GLA_DK ** -0.5

    def chunk_body(c, carry):
        r0 = pl.multiple_of(c * CHUNK, CHUNK)
        q = q_ref[pl.ds(r0, CHUNK), :] * scale
        k = k_ref[pl.ds(r0, CHUNK), :]
        g = g_ref[pl.ds(r0, CHUNK), :]
        v = v_ref[pl.ds(r0, CHUNK), :]
        g_hi, g_lo = _split_bf16(g)
        cums = _dot(cum, g_hi) + _dot(cum, g_lo)
        b = cums[0:CHUNK]
        b_ref = cums[CHUNK:2 * CHUNK]
        b_last = cums[2 * CHUNK:3 * CHUNK]
        e_kst = cums[3 * CHUNK:3 * CHUNK + KST_ROWS]

        state = s_ref[...]
        q_in = (q * jnp.exp(b)).astype(BF16)
        o = _dot(q_in, state.astype(BF16))

        kw = (k * jnp.exp(b_last - b)).astype(BF16)
        kv = _dot_tn(kw, v)
        decay_rows = jnp.exp(jnp.concatenate([b_last, b_last], axis=0))
        decay_cols = decay_rows.T[:, 0:1]
        s_ref[...] = state * decay_cols + jnp.where(state_mask, kv, 0.0)

        q_sub = (q * jnp.exp(jnp.minimum(b - b_ref, 0.0))).astype(BF16)
        k_st = jnp.concatenate([k[0:SUB * j] for j in range(1, N_SUB)], axis=0) * jnp.exp(e_kst)
        k_bd = jnp.concatenate(
            [jnp.where(qk_head == hd, k_st, 0.0) for hd in range(GLA_HEADS)], axis=0).astype(BF16)
        a = _dot_nt(q_sub, k_bd)
        a = jnp.where(amask, a, 0.0).astype(BF16)
        v_st = jnp.concatenate([v[0:SUB * j] for j in range(1, N_SUB)], axis=0)
        v_bd = jnp.concatenate(
            [jnp.where(v_head == hd, v_st, jnp.zeros_like(v_st)) for hd in range(GLA_HEADS)], axis=0)
        o = o + _dot(a, v_bd)

        vf = v.astype(F32)
        prods = []
        for d in range(SUB):
            ks = k if d == 0 else pltpu.roll(k, d, 0)
            bs = b if d == 0 else pltpu.roll(b, d, 0)
            prods.append((q * ks * jnp.exp(jnp.minimum(b - bs, 0.0))).astype(BF16))
        dsum = _dot(jnp.concatenate(prods, axis=0), ind_sum)
        valid = jnp.concatenate([row_in_sub >= d for d in range(SUB)], axis=0)
        dsum = jnp.where(valid, dsum, 0.0).astype(BF16)
        dbc = _dot(dsum, ind_bcast)
        for d in range(SUB):
            vs = vf if d == 0 else pltpu.roll(vf, d, 0)
            o = o + dbc[d * CHUNK:(d + 1) * CHUNK] * vs
        raw_ref[pl.ds(r0, CHUNK), :] = o
        return carry

    lax.fori_loop(0, n_chunks, chunk_body, 0)

    o = raw_ref[...]
    hr = lax.broadcasted_iota(jnp.int32, (GLA_WIDTH, GLA_WIDTH), 0) // GLA_DV
    hc = lax.broadcasted_iota(jnp.int32, (GLA_WIDTH, GLA_WIDTH), 1) // GLA_DV
    head_ones = jnp.where(hr == hc, 1.0, 0.0).astype(BF16)
    sq_hi, sq_lo = _split_bf16(o * o)
    ms = (_dot(sq_hi, head_ones) + _dot(sq_lo, head_ones)) * (1.0 / GLA_DV)
    gate = gate_ref[...]
    y = o * lax.rsqrt(ms + EPS) * gn_ref[...] * (gate * _sigmoid(gate))
    o_ref[...] = y.astype(BF16)


def _gla(gq, gk, gg, gv, gate, gn_tiled, batch, seq):
    rows = min(GLA_ROWS, seq)
    steps = seq // rows
    row = lambda width: pl.BlockSpec((rows, width), lambda b, i: (b * steps + i, 0))
    cum = jnp.asarray(_gla_cum_matrix(), BF16)
    amask = jnp.asarray(_gla_score_mask(), F32)
    return pl.pallas_call(
        _gla_kernel,
        out_shape=jax.ShapeDtypeStruct((batch * seq, GLA_WIDTH), BF16),
        grid=(batch, steps),
        in_specs=[
            row(GLA_QK_LANES), row(GLA_QK_LANES), row(GLA_QK_LANES), row(GLA_WIDTH), row(GLA_WIDTH),
            pl.BlockSpec(cum.shape, lambda b, i: (0, 0)),
            pl.BlockSpec(amask.shape, lambda b, i: (0, 0)),
            pl.BlockSpec((1, GLA_WIDTH), lambda b, i: (0, 0)),
        ],
        out_specs=row(GLA_WIDTH),
        scratch_shapes=[pltpu.VMEM((GLA_QK_LANES, GLA_WIDTH), F32),
                        pltpu.VMEM((rows, GLA_WIDTH), F32)],
        compiler_params=pltpu.CompilerParams(
            dimension_semantics=("arbitrary", "arbitrary"), vmem_limit_bytes=VMEM_LIMIT_BYTES),
        name="gla",
    )(gq, gk, gg, gv, gate, cum, amask, gn_tiled)


def _t5_bucket(rel):
    nb = NUM_BUCKETS // 2
    max_exact = nb // 2
    ret = (rel > 0).astype(jnp.int32) * nb
    n = jnp.abs(rel)
    nf = jnp.maximum(n, 1).astype(jnp.float32)
    large = max_exact + (jnp.log(nf / max_exact) / math.log(MAX_DISTANCE / max_exact)
                         * (nb - max_exact)).astype(jnp.int32)
    large = jnp.minimum(large, nb - 1)
    return ret + jnp.where(n < max_exact, n, large)


def _bias_tiles(rel_bias, tile):
    assert tile >= MAX_DISTANCE and tile % CHUNK == 0
    table = rel_bias.astype(F32)
    far = table[NUM_BUCKETS // 2 - 1]
    r = jnp.arange(tile)[:, None]
    c = jnp.arange(tile)[None, :]

    def lookup(rel):
        onehot = _t5_bucket(rel)[None, :, :, None] == jnp.arange(NUM_BUCKETS)[None, None, None, :]
        vals = jnp.where(onehot, table.T[:, None, None, :], 0.0)
        return (jnp.sum(vals, axis=-1) - far[:, None, None]) * LOG2E

    diag = jnp.where(((c // CHUNK) <= (r // CHUNK))[None], lookup(c - r), MASK_VALUE)
    left = lookup(c - r - tile)
    return jnp.stack([left, diag], axis=1)


def _attn_kernel(q_ref, k_ref, v_ref, bias_ref, lamv_ref, subln_ref, o_ref,
                 s_buf, p_buf, alpha_buf, m_all, acc_all, *, tile, lam_init):
    seq = q_ref.shape[0]
    nq = seq // tile

    s_buf[...] = jnp.zeros(s_buf.shape, F32)
    p_buf[...] = jnp.zeros(p_buf.shape, BF16)
    alpha_buf[...] = jnp.zeros(alpha_buf.shape, F32)
    m_all[...] = jnp.full(m_all.shape, MASK_VALUE, F32)
    acc_all[...] = jnp.zeros(acc_all.shape, F32)

    lane = lax.broadcasted_iota(jnp.int32, (tile, LANES), 1)

    def stage1(i, j, slot):
        q = q_ref[pl.ds(pl.multiple_of(i * tile, tile), tile), :]
        zero = jnp.zeros_like(q)
        qs = jnp.concatenate([jnp.where(lane < LANES // 2, q, zero),
                              jnp.where(lane >= LANES // 2, q, zero)], axis=0)
        kj = k_ref[pl.ds(pl.multiple_of(j * tile, tile), tile), :]
        s_buf[slot] = _dot_nt(qs, kj)

    def stage2(i, bias, slot):
        for half in range(2):
            rows = slice(half * tile, (half + 1) * tile)
            s = s_buf[slot, rows, :]
            if bias is not None:
                s = s + bias
            m_prev = m_all[i, rows, :]
            m_next = jnp.maximum(m_prev, jnp.max(s, axis=-1, keepdims=True))
            p = jnp.exp2(s - jnp.concatenate([m_next] * (tile // LANES), axis=1))
            p_buf[slot, rows, :] = p.astype(BF16)
            alpha_buf[slot, rows, :] = jnp.exp2(m_prev - m_next)
            m_all[i, rows, :] = m_next

    def stage3(i, j, slot):
        vj = v_ref[pl.ds(pl.multiple_of(j * tile, tile), tile), :]
        acc_all[i] = alpha_buf[slot] * acc_all[i] + _dot(p_buf[slot], vj)

    def run_pipeline(n_pairs, first_pair, advance, with_bias):
        n_steps = pl.cdiv(n_pairs + 2, ATT_UNROLL) * ATT_UNROLL

        def step(t, pairs, slot_a, slot_b):
            (i0, j0), (i1, j1), (i2, j2) = pairs
            valid1 = jnp.logical_and(t >= 1, t <= n_pairs)
            valid2 = jnp.logical_and(t >= 2, t <= n_pairs + 1)
            stage3(jnp.where(valid2, i2, nq), j2, slot_a)
            stage2(jnp.where(valid1, i1, nq), bias_ref[0, j1 - i1 + 1] if with_bias else None, slot_b)
            stage1(i0, j0, slot_a)
            return (advance(i0, j0), (i0, j0), (i1, j1))

        def body(u, pairs):
            for r in range(ATT_UNROLL):
                pairs = step(ATT_UNROLL * u + r, pairs, r % 2, (r + 1) % 2)
            return pairs

        lax.fori_loop(0, n_steps // ATT_UNROLL, body, (first_pair,) * 3)

    def next_far(i, j):
        wrap = j == i - 2
        done = jnp.logical_and(wrap, i == nq - 1)
        step_i = jnp.logical_and(wrap, jnp.logical_not(done))
        return (jnp.where(step_i, i + 1, i), jnp.where(done, j, jnp.where(wrap, 0, j + 1)))

    def next_near(i, j):
        wrap = j == i
        done = jnp.logical_and(wrap, i == nq - 1)
        step_i = jnp.logical_and(wrap, jnp.logical_not(done))
        return (jnp.where(step_i, i + 1, i), jnp.where(jnp.logical_or(done, wrap), j, j + 1))

    zero = jnp.int32(0)
    if nq > 2:
        run_pipeline((nq - 1) * (nq - 2) // 2, (jnp.int32(2), zero), next_far, False)
    run_pipeline(2 * nq - 1, (zero, zero), next_near, True)

    def finish(i, carry):
        acc = acc_all[i]
        a1 = acc[0:tile]
        a2 = acc[tile:2 * tile]
        l1 = a1[:, DIFF_DV:DIFF_DV + 1]
        l2 = a2[:, DIFF_DV:DIFF_DV + 1]
        lv = lamv_ref[...]
        lam = (jnp.exp(jnp.sum(lv[0:1] * lv[1:2], axis=-1, keepdims=True))
               - jnp.exp(jnp.sum(lv[2:3] * lv[3:4], axis=-1, keepdims=True)) + lam_init)
        o = a1 / l1 - lam * (a2 / l2)
        o = jnp.where(lane < DIFF_DV, o, 0.0)
        ms = jnp.sum(o * o, axis=-1, keepdims=True) * (1.0 / DIFF_DV)
        y = o * lax.rsqrt(ms + EPS) * subln_ref[...] * (1.0 - lam_init)
        o_ref[pl.ds(pl.multiple_of(i * tile, tile), tile), :] = y.astype(BF16)
        return carry

    lax.fori_loop(0, nq, finish, 0)


def _attn(dq, dk, dv, bias_tiles, lamv, subln, batch, seq, lam_init):
    tile = min(ATT_TILE, seq)
    nq = seq // tile
    head_block = pl.BlockSpec((seq, LANES), lambda b, h: (b, h))
    return pl.pallas_call(
        functools.partial(_attn_kernel, tile=tile, lam_init=lam_init),
        out_shape=jax.ShapeDtypeStruct((batch * seq, DIFF_LANES), BF16),
        grid=(batch, DIFF_HEADS),
        in_specs=[
            head_block, head_block, head_block,
            pl.BlockSpec((1, 2, tile, tile), lambda b, h: (h, 0, 0, 0)),
            pl.BlockSpec((4, LANES), lambda b, h: (0, 0)),
            pl.BlockSpec((1, LANES), lambda b, h: (0, 0)),
        ],
        out_specs=head_block,
        scratch_shapes=[pltpu.VMEM((2, 2 * tile, tile), F32),
                        pltpu.VMEM((2, 2 * tile, tile), BF16),
                        pltpu.VMEM((2, 2 * tile, LANES), F32),
                        pltpu.VMEM((nq + 1, 2 * tile, LANES), F32),
                        pltpu.VMEM((nq + 1, 2 * tile, LANES), F32)],
        compiler_params=pltpu.CompilerParams(
            dimension_semantics=("parallel", "parallel"), vmem_limit_bytes=VMEM_LIMIT_BYTES),
        name="diff_attn",
    )(dq, dk, dv, bias_tiles, lamv, subln)


def _out_kernel(x_ref, gla_ref, diff_ref, conv_ref, w1_ref, w2_ref, w3_ref, o_ref):
    o_ref[...] = (x_ref[...] + _dot(gla_ref[...], w1_ref[...]) + _dot(diff_ref[...], w2_ref[...])
                  + _dot(conv_ref[...], w3_ref[...]))


def _out(x, gla_o, diff_o, conv_o, w1, w2, w3):
    t = x.shape[0]
    tm = min(OUT_ROWS, t)
    row = lambda width: pl.BlockSpec((tm, width), lambda i: (i, 0))
    return pl.pallas_call(
        _out_kernel,
        out_shape=jax.ShapeDtypeStruct((t, D_MODEL), F32),
        grid=(t // tm,),
        in_specs=[row(D_MODEL), row(GLA_WIDTH), row(DIFF_LANES), row(CONV_DIM),
                  _const_spec((GLA_WIDTH, D_MODEL)), _const_spec((DIFF_LANES, D_MODEL)),
                  _const_spec((CONV_DIM, D_MODEL))],
        out_specs=row(D_MODEL),
        compiler_params=pltpu.CompilerParams(
            dimension_semantics=("parallel",), vmem_limit_bytes=VMEM_LIMIT_BYTES),
        name="mixer_out",
    )(x, gla_o, diff_o, conv_o, w1, w2, w3)


def _pad_cols(w, width):
    return jnp.pad(w, ((0, 0), (0, width - w.shape[1])))


def _slot_cols(w, heads, used, slot):
    lead = w.shape[0]
    w = w.reshape(lead, heads, used)
    return jnp.pad(w, ((0, 0), (0, 0), (0, slot - used))).reshape(lead, heads * slot)


def _layout_w_in(w_in):
    offs = np.cumsum([0, GLA_HEADS * GLA_DK, GLA_HEADS * GLA_DK, GLA_WIDTH, GLA_RANK, GLA_WIDTH,
                      DIFF_HEADS * 2 * DIFF_DQK, DIFF_HEADS * 2 * DIFF_DQK, DIFF_WIDTH,
                      CONV_DIM, CONV_DIM, CONV_DIM])
    gq, gk, gv, glr, gate, dq, dk, dv, cb, cc, ch = [w_in[:, offs[n]:offs[n + 1]] for n in range(11)]
    half = LANES // 2
    cols = [
        _slot_cols(gq, GLA_HEADS, GLA_DK, HEAD_SLOT),
        _slot_cols(gk, GLA_HEADS, GLA_DK, HEAD_SLOT),
        _pad_cols(glr, LANES),
        gv,
        gate,
        _slot_cols(dq, 2 * DIFF_HEADS, DIFF_DQK, half),
        _slot_cols(dk, 2 * DIFF_HEADS, DIFF_DQK, half),
        _slot_cols(dv, DIFF_HEADS, DIFF_DV, LANES),
        cb, cc, ch,
    ]
    w = jnp.concatenate(cols, axis=1)
    assert w.shape[1] == N_PROJ
    return w.astype(BF16)


def kernel(x, ffn1_norm, ffn1_gate, ffn1_up, ffn1_down, mix_norm, w_in, gla_gk_up, gla_gk_bias,
           gla_norm, diff_lambda_q1, diff_lambda_k1, diff_lambda_q2, diff_lambda_k2, diff_subln,
           rel_bias, conv_w, w_out, ffn2_norm, ffn2_gate, ffn2_up, ffn2_down, final_norm):
    batch, seq, _ = x.shape
    depth = w_in.shape[0]
    t = batch * seq
    xf = x.reshape(t, D_MODEL)
    bias_tiles = _bias_tiles(rel_bias, min(ATT_TILE, seq))
    final_g = final_norm.reshape(1, D_MODEL)

    for l in range(depth):
        lam_init = 0.8 - 0.6 * math.exp(-0.3 * l)
        xf = _ffn(xf, ffn1_norm[l].reshape(1, D_MODEL), ffn1_gate[l].astype(BF16),
                  ffn1_up[l].astype(BF16), ffn1_down[l].astype(BF16), final_g, False)

        w_pad = _layout_w_in(w_in[l])
        up = _slot_cols(gla_gk_up[l], GLA_HEADS, GLA_DK, HEAD_SLOT)
        up = jnp.pad(up, ((0, LANES - GLA_RANK), (0, 0)))
        up_hi = up.astype(BF16)
        up_lo = (up - up_hi.astype(F32)).astype(BF16)
        gkup = jnp.stack([up_hi, up_lo], axis=0)
        gkb = _slot_cols(gla_gk_bias[l].reshape(1, -1), GLA_HEADS, GLA_DK, HEAD_SLOT)
        cw = jnp.pad(conv_w[l], ((0, 8 - CONV_WIDTH), (0, 0)))
        gq, gk, gg, gv, gate, dq, dk, dv, conv_o = _proj(
            xf, mix_norm[l].reshape(1, D_MODEL), w_pad, gkup, gkb, cw, seq)

        gn_tiled = jnp.tile(gla_norm[l], GLA_HEADS).reshape(1, GLA_WIDTH)
        gla_o = _gla(gq, gk, gg, gv, gate, gn_tiled, batch, seq)

        lamv = jnp.stack([diff_lambda_q1[l], diff_lambda_k1[l], diff_lambda_q2[l], diff_lambda_k2[l]])
        lamv = _pad_cols(lamv.astype(F32), LANES)
        subln = _pad_cols(diff_subln[l].reshape(1, DIFF_DV), LANES)
        diff_o = _attn(dq, dk, dv, bias_tiles, lamv, subln, batch, seq, lam_init)

        wo = w_out[l]
        w1 = wo[0:GLA_WIDTH].astype(BF16)
        w2 = wo[GLA_WIDTH:GLA_WIDTH + DIFF_WIDTH].reshape(DIFF_HEADS, DIFF_DV, D_MODEL)
        w2 = jnp.pad(w2, ((0, 0), (0, LANES - DIFF_DV), (0, 0))).reshape(DIFF_LANES, D_MODEL).astype(BF16)
        w3 = wo[GLA_WIDTH + DIFF_WIDTH:].astype(BF16)
        xf = _out(xf, gla_o, diff_o, conv_o, w1, w2, w3)

        xf = _ffn(xf, ffn2_norm[l].reshape(1, D_MODEL), ffn2_gate[l].astype(BF16),
                  ffn2_up[l].astype(BF16), ffn2_down[l].astype(BF16), final_g, l == depth - 1)

    return xf.reshape(batch, seq, D_MODEL)
```

```python
import functools
import math

import numpy as np
import jax
import jax.numpy as jnp
from jax import lax
from jax.experimental import pallas as pl
from jax.experimental.pallas import tpu as pltpu

F32 = jnp.float32
BF16 = jnp.bfloat16

D_MODEL = 1024
D_FF = 2816
EPS = 1e-6
CHUNK = 64
GLA_HEADS = 4
GLA_DK = 48
GLA_DV = 96
GLA_RANK = 16
GLA_GATE_NORM = 16.0
DIFF_HEADS = 4
DIFF_DQK = 48
DIFF_DV = 96
CONV_DIM = 256
CONV_WIDTH = 3
NUM_BUCKETS = 32
MAX_DISTANCE = 128
GLA_WIDTH = GLA_HEADS * GLA_DV
DIFF_WIDTH = DIFF_HEADS * DIFF_DV

LANES = 128
VMEM_LIMIT_BYTES = 56 * 1024 * 1024

HEAD_SLOT = 64
GLA_QK_LANES = GLA_HEADS * HEAD_SLOT
SUB = 16
N_SUB = CHUNK // SUB
KST_ROWS = SUB * (N_SUB * (N_SUB - 1) // 2)
DIFF_LANES = DIFF_HEADS * LANES
MASK_VALUE = -1e30
GLA_SAFE_DECAY = 60.0
LOG2E = math.log2(math.e)

OFF_GQ = 0
OFF_GK = OFF_GQ + GLA_QK_LANES
OFF_GLR = OFF_GK + GLA_QK_LANES
OFF_GV = OFF_GLR + LANES
OFF_GATE = OFF_GV + GLA_WIDTH
OFF_DQ = OFF_GATE + GLA_WIDTH
OFF_DK = OFF_DQ + DIFF_LANES
OFF_DV = OFF_DK + DIFF_LANES
OFF_CB = OFF_DV + DIFF_LANES
OFF_CC = OFF_CB + CONV_DIM
OFF_CH = OFF_CC + CONV_DIM
N_PROJ = OFF_CH + CONV_DIM

FFN_ROWS = 512
FFN_F_CHUNK = 1408
PROJ_ROWS = 512
GLA_ROWS = 512
ATT_TILE = 256
ATT_UNROLL = 8
OUT_ROWS = 512


def _dot(a, b):
    return jnp.dot(a, b, preferred_element_type=F32)


def _dot_nt(a, b):
    return lax.dot_general(a, b, (((1,), (1,)), ((), ())), preferred_element_type=F32)


def _dot_tn(a, b):
    return lax.dot_general(a, b, (((0,), (0,)), ((), ())), preferred_element_type=F32)


def _split_bf16(x):
    hi = x.astype(BF16)
    lo = (x - hi.astype(F32)).astype(BF16)
    return hi, lo


def _rms(x, g):
    return x * lax.rsqrt(jnp.mean(x * x, axis=-1, keepdims=True) + EPS) * g


def _sigmoid(x):
    return 1.0 / (1.0 + jnp.exp(-x))


def _ffn_kernel(x_ref, g_ref, wg_ref, wu_ref, wd_ref, fg_ref, o_ref, *, final):
    x = x_ref[...]
    h = _rms(x, g_ref[...]).astype(BF16)
    acc = None
    for c in range(D_FF // FFN_F_CHUNK):
        sl = slice(c * FFN_F_CHUNK, (c + 1) * FFN_F_CHUNK)
        gate = _dot(h, wg_ref[:, sl])
        up = _dot(h, wu_ref[:, sl])
        a = (gate * _sigmoid(gate) * up).astype(BF16)
        part = _dot(a, wd_ref[sl, :])
        acc = part if acc is None else acc + part
    y = x + 0.5 * acc
    if final:
        y = _rms(y, fg_ref[...])
    o_ref[...] = y


def _const_spec(shape):
    nd = len(shape)
    return pl.BlockSpec(shape, lambda *_: (0,) * nd, pipeline_mode=pl.Buffered(1))


def _ffn(x, g, wg, wu, wd, final_g, final):
    t = x.shape[0]
    tm = min(FFN_ROWS, t)
    return pl.pallas_call(
        functools.partial(_ffn_kernel, final=final),
        out_shape=jax.ShapeDtypeStruct((t, D_MODEL), F32),
        grid=(t // tm,),
        in_specs=[
            pl.BlockSpec((tm, D_MODEL), lambda i: (i, 0)),
            _const_spec((1, D_MODEL)),
            _const_spec((D_MODEL, D_FF)),
            _const_spec((D_MODEL, D_FF)),
            _const_spec((D_FF, D_MODEL)),
            _const_spec((1, D_MODEL)),
        ],
        out_specs=pl.BlockSpec((tm, D_MODEL), lambda i: (i, 0)),
        compiler_params=pltpu.CompilerParams(
            dimension_semantics=("parallel",), vmem_limit_bytes=VMEM_LIMIT_BYTES),
        name="ffn",
    )(x, g, wg, wu, wd, final_g)


def _proj_kernel(x_ref, g_ref, w_ref, gkup_ref, gkb_ref, cw_ref,
                 gq_ref, gk_ref, gg_ref, gv_ref, gate_ref, dq_ref, dk_ref, dv_ref, conv_ref,
                 zbuf_ref, *, tiles_per_seq):
    tm = x_ref.shape[0]
    h = _rms(x_ref[...], g_ref[...]).astype(BF16)

    def proj(off, width):
        return _dot(h, w_ref[:, off:off + width])

    gq_ref[...] = proj(OFF_GQ, GLA_QK_LANES)
    gk_ref[...] = proj(OFF_GK, GLA_QK_LANES)
    gv_ref[...] = proj(OFF_GV, GLA_WIDTH).astype(BF16)
    gate_ref[...] = proj(OFF_GATE, GLA_WIDTH)

    glr_hi, glr_lo = _split_bf16(proj(OFF_GLR, LANES))
    up_hi = gkup_ref[0]
    up_lo = gkup_ref[1]
    z = _dot(glr_hi, up_hi) + _dot(glr_lo, up_hi) + _dot(glr_hi, up_lo) + gkb_ref[...]
    logsig = jnp.minimum(z, 0.0) - jnp.log1p(jnp.exp(-jnp.abs(z)))
    lane = lax.broadcasted_iota(jnp.int32, (tm, GLA_QK_LANES), 1)
    gg_ref[...] = jnp.where(lane % HEAD_SLOT < GLA_DK, logsig * (1.0 / GLA_GATE_NORM), 0.0)

    dq_ref[...] = (proj(OFF_DQ, DIFF_LANES) * (DIFF_DQK ** -0.5 * LOG2E)).astype(BF16)
    dk_ref[...] = proj(OFF_DK, DIFF_LANES).astype(BF16)
    lane_v = lax.broadcasted_iota(jnp.int32, (tm, DIFF_LANES), 1)
    dv_ref[...] = jnp.where(lane_v % LANES == DIFF_DV, 1.0, proj(OFF_DV, DIFF_LANES)).astype(BF16)

    @pl.when(pl.program_id(0) % tiles_per_seq == 0)
    def _():
        zbuf_ref[0:8, :] = jnp.zeros((8, CONV_DIM), F32)

    zc = proj(OFF_CC, CONV_DIM) * proj(OFF_CH, CONV_DIM)
    zbuf_ref[8:8 + tm, :] = zc
    cw = cw_ref[...]
    y = (cw[2:3, :] * zc + cw[1:2, :] * zbuf_ref[7:7 + tm, :] + cw[0:1, :] * zbuf_ref[6:6 + tm, :])
    conv_ref[...] = (proj(OFF_CB, CONV_DIM) * y).astype(BF16)
    zbuf_ref[0:8, :] = zbuf_ref[tm:tm + 8, :]


def _proj(x, g, w, gkup, gkb, cw, seq):
    t = x.shape[0]
    tm = min(PROJ_ROWS, seq)
    row = lambda width: pl.BlockSpec((tm, width), lambda i: (i, 0))
    out_shapes = (
        jax.ShapeDtypeStruct((t, GLA_QK_LANES), F32),
        jax.ShapeDtypeStruct((t, GLA_QK_LANES), F32),
        jax.ShapeDtypeStruct((t, GLA_QK_LANES), F32),
        jax.ShapeDtypeStruct((t, GLA_WIDTH), BF16),
        jax.ShapeDtypeStruct((t, GLA_WIDTH), F32),
        jax.ShapeDtypeStruct((t, DIFF_LANES), BF16),
        jax.ShapeDtypeStruct((t, DIFF_LANES), BF16),
        jax.ShapeDtypeStruct((t, DIFF_LANES), BF16),
        jax.ShapeDtypeStruct((t, CONV_DIM), BF16),
    )
    return pl.pallas_call(
        functools.partial(_proj_kernel, tiles_per_seq=seq // tm),
        out_shape=out_shapes,
        grid=(t // tm,),
        in_specs=[
            row(D_MODEL),
            _const_spec((1, D_MODEL)),
            _const_spec((D_MODEL, N_PROJ)),
            _const_spec((2, LANES, GLA_QK_LANES)),
            _const_spec((1, GLA_QK_LANES)),
            _const_spec((8, CONV_DIM)),
        ],
        out_specs=tuple(row(s.shape[1]) for s in out_shapes),
        scratch_shapes=[pltpu.VMEM((tm + 8, CONV_DIM), F32)],
        compiler_params=pltpu.CompilerParams(
            dimension_semantics=("arbitrary",), vmem_limit_bytes=VMEM_LIMIT_BYTES),
        name="mixer_proj",
    )(x, g, w, gkup, gkb, cw)


def _gla_cum_matrix():
    t = np.arange(CHUNK)[:, None]
    u = np.arange(CHUNK)[None, :]
    tri = (u <= t)
    ref = (u <= (t // SUB) * SUB - 1)
    ones = np.ones((CHUNK, CHUNK), bool)
    parts = [tri, ref, ones]
    for j in range(1, N_SUB):
        s = np.arange(SUB * j)[:, None]
        parts.append((u > s) & (u <= SUB * j - 1))
    return np.concatenate(parts, axis=0).astype(np.float32)


def _gla_score_mask():
    t = np.arange(CHUNK)[:, None]
    m = np.arange(KST_ROWS)
    group = np.concatenate([np.full(SUB * j, j) for j in range(1, N_SUB)])
    keep = (group[None, :] == (t // SUB))
    return np.tile(keep, (1, GLA_HEADS)).astype(np.float32)


def _gla_kernel(q_ref, k_ref, g_ref, v_ref, gate_ref, cum_ref, cumf_ref, amask_ref, gn_ref,
                o_ref, s_ref, raw_ref):
    rows = q_ref.shape[0]
    n_chunks = rows // CHUNK

    @pl.when(pl.program_id(1) == 0)
    def _():
        s_ref[...] = jnp.zeros_like(s_ref)

    qk_head = lax.broadcasted_iota(jnp.int32, (1, GLA_QK_LANES), 1) // HEAD_SLOT
    v_head = lax.broadcasted_iota(jnp.int32, (1, GLA_WIDTH), 1) // GLA_DV
    row_head_v = lax.broadcasted_iota(jnp.int32, (GLA_WIDTH, 1), 0) // GLA_DV
    state_mask = row_head_v == qk_head
    scale = GLA_DK ** -0.5

    sel_r = lax.broadcasted_iota(jnp.int32, (8, rows), 0)
    sel_c = lax.broadcasted_iota(jnp.int32, (8, rows), 1) // CHUNK
    chunk_sel = jnp.where(sel_r == sel_c, 1.0, 0.0).astype(BF16)
    gs_hi, gs_lo = _split_bf16(g_ref[...])
    totals = _dot(chunk_sel, gs_hi) + _dot(chunk_sel, gs_lo)
    safe = jnp.min(totals) >= -GLA_SAFE_DECAY

    def fast_chunk(c):
        r0 = c * CHUNK
        q = q_ref[r0:r0 + CHUNK, :] * scale
        k = k_ref[r0:r0 + CHUNK, :]
        v = v_ref[r0:r0 + CHUNK, :]
        g_hi, g_lo = _split_bf16(g_ref[r0:r0 + CHUNK, :])
        cumf = cumf_ref[...]
        cums = _dot(cumf, g_hi) + _dot(cumf, g_lo)
        b = cums[0:CHUNK]
        b_last = cums[CHUNK:2 * CHUNK]
        qd = (q * jnp.exp(b)).astype(BF16)
        kd = (k * jnp.exp(-b)).astype(BF16)
        kw = (k * jnp.exp(b_last - b)).astype(BF16)
        kd_bd = jnp.concatenate(
            [jnp.where(qk_head == hd, kd, jnp.zeros_like(kd)) for hd in range(GLA_HEADS)], axis=0)
        a = _dot_nt(qd, kd_bd)
        t_idx = lax.broadcasted_iota(jnp.int32, (CHUNK, GLA_QK_LANES), 0)
        s_idx = lax.broadcasted_iota(jnp.int32, (CHUNK, GLA_QK_LANES), 1) % HEAD_SLOT
        a = jnp.where(s_idx <= t_idx, a, 0.0).astype(BF16)
        v_bd = jnp.concatenate(
            [jnp.where(v_head == hd, v, jnp.zeros_like(v)) for hd in range(GLA_HEADS)], axis=0)
        state = s_ref[...]
        raw_ref[r0:r0 + CHUNK, :] = _dot_nt(qd, state.astype(BF16)) + _dot(a, v_bd)
        decay = jnp.exp(jnp.concatenate([b_last] * (GLA_WIDTH // CHUNK), axis=0))
        s_ref[...] = state * decay + jnp.where(state_mask, _dot_tn(v, kw), 0.0)

    @pl.when(safe)
    def _():
        for c in range(n_chunks):
            fast_chunk(c)

    def general_path():
        cum = cum_ref[...]
        amask = amask_ref[...] > 0.5
        ind_r = lax.broadcasted_iota(jnp.int32, (GLA_QK_LANES, LANES), 0) // HEAD_SLOT
        ind_c = lax.broadcasted_iota(jnp.int32, (GLA_QK_LANES, LANES), 1)
        ind_sum = jnp.where(ind_r == ind_c, 1.0, 0.0).astype(BF16)
        bc_r = lax.broadcasted_iota(jnp.int32, (LANES, GLA_WIDTH), 0)
        bc_c = lax.broadcasted_iota(jnp.int32, (LANES, GLA_WIDTH), 1) // GLA_DV
        ind_bcast = jnp.where(bc_r == bc_c, 1.0, 0.0).astype(BF16)
        row_in_sub = lax.broadcasted_iota(jnp.int32, (CHUNK, 1), 0) % SUB

        def chunk_body(c, carry):
            r0 = pl.multiple_of(c * CHUNK, CHUNK)
            q = q_ref[pl.ds(r0, CHUNK), :] * scale
            k = k_ref[pl.ds(r0, CHUNK), :]
            g = g_ref[pl.ds(r0, CHUNK), :]
            v = v_ref[pl.ds(r0, CHUNK), :]
            g_hi, g_lo = _split_bf16(g)
            cums = _dot(cum, g_hi) + _dot(cum, g_lo)
            b = cums[0:CHUNK]
            b_ref = cums[CHUNK:2 * CHUNK]
            b_last = cums[2 * CHUNK:3 * CHUNK]
            e_kst = cums[3 * CHUNK:3 * CHUNK + KST_ROWS]

            state = s_ref[...]
            q_in = (q * jnp.exp(b)).astype(BF16)
            o = _dot_nt(q_in, state.astype(BF16))

            kw = (k * jnp.exp(b_last - b)).astype(BF16)
            decay = jnp.exp(jnp.concatenate([b_last] * (GLA_WIDTH // CHUNK), axis=0))
            s_ref[...] = state * decay + jnp.where(state_mask, _dot_tn(v, kw), 0.0)

            q_sub = (q * jnp.exp(jnp.minimum(b - b_ref, 0.0))).astype(BF16)
            k_st = jnp.concatenate([k[0:SUB * j] for j in range(1, N_SUB)], axis=0) * jnp.exp(e_kst)
            k_bd = jnp.concatenate(
                [jnp.where(qk_head == hd, k_st, 0.0) for hd in range(GLA_HEADS)], axis=0).astype(BF16)
            a = _dot_nt(q_sub, k_bd)
            a = jnp.where(amask, a, 0.0).astype(BF16)
            v_st = jnp.concatenate([v[0:SUB * j] for j in range(1, N_SUB)], axis=0)
            v_bd = jnp.concatenate(
                [jnp.where(v_head == hd, v_st, jnp.zeros_like(v_st)) for hd in range(GLA_HEADS)], axis=0)
            o = o + _dot(a, v_bd)

            vf = v.astype(F32)
            prods = []
            for d in range(SUB):
                ks = k if d == 0 else pltpu.roll(k, d, 0)
                bs = b if d == 0 else pltpu.roll(b, d, 0)
                prods.append((q * ks * jnp.exp(jnp.minimum(b - bs, 0.0))).astype(BF16))
            dsum = _dot(jnp.concatenate(prods, axis=0), ind_sum)
            valid = jnp.concatenate([row_in_sub >= d for d in range(SUB)], axis=0)
            dsum = jnp.where(valid, dsum, 0.0).astype(BF16)
            dbc = _dot(dsum, ind_bcast)
            for d in range(SUB):
                vs = vf if d == 0 else pltpu.roll(vf, d, 0)
                o = o + dbc[d * CHUNK:(d + 1) * CHUNK] * vs
            raw_ref[pl.ds(r0, CHUNK), :] = o
            return carry

        lax.fori_loop(0, n_chunks, chunk_body, 0)

    @pl.when(jnp.logical_not(safe))
    def _():
        general_path()

    o = raw_ref[...]
    hr = lax.broadcasted_iota(jnp.int32, (GLA_WIDTH, GLA_WIDTH), 0) // GLA_DV
    hc = lax.broadcasted_iota(jnp.int32, (GLA_WIDTH, GLA_WIDTH), 1) // GLA_DV
    head_ones = jnp.where(hr == hc, 1.0, 0.0).astype(BF16)
    sq_hi, sq_lo = _split_bf16(o * o)
    ms = (_dot(sq_hi, head_ones) + _dot(sq_lo, head_ones)) * (1.0 / GLA_DV)
    gate = gate_ref[...]
    y = o * lax.rsqrt(ms + EPS) * gn_ref[...] * (gate * _sigmoid(gate))
    o_ref[...] = y.astype(BF16)


def _gla(gq, gk, gg, gv, gate, gn_tiled, batch, seq):
    rows = min(GLA_ROWS, seq)
    steps = seq // rows
    row = lambda width: pl.BlockSpec((rows, width), lambda b, i: (b * steps + i, 0))
    assert rows // CHUNK <= 8
    cum_np = _gla_cum_matrix()
    cum = jnp.asarray(cum_np, BF16)
    cumf = jnp.asarray(np.concatenate([cum_np[0:CHUNK], cum_np[2 * CHUNK:3 * CHUNK]], axis=0), BF16)
    amask = jnp.asarray(_gla_score_mask(), F32)
    return pl.pallas_call(
        _gla_kernel,
        out_shape=jax.ShapeDtypeStruct((batch * seq, GLA_WIDTH), BF16),
        grid=(batch, steps),
        in_specs=[
            row(GLA_QK_LANES), row(GLA_QK_LANES), row(GLA_QK_LANES), row(GLA_WIDTH), row(GLA_WIDTH),
            pl.BlockSpec(cum.shape, lambda b, i: (0, 0)),
            pl.BlockSpec(cumf.shape, lambda b, i: (0, 0)),
            pl.BlockSpec(amask.shape, lambda b, i: (0, 0)),
            pl.BlockSpec((1, GLA_WIDTH), lambda b, i: (0, 0)),
        ],
        out_specs=row(GLA_WIDTH),
        scratch_shapes=[pltpu.VMEM((GLA_WIDTH, GLA_QK_LANES), F32),
                        pltpu.VMEM((rows, GLA_WIDTH), F32)],
        compiler_params=pltpu.CompilerParams(
            dimension_semantics=("arbitrary", "arbitrary"), vmem_limit_bytes=VMEM_LIMIT_BYTES),
        name="gla",
    )(gq, gk, gg, gv, gate, cum, cumf, amask, gn_tiled)


def _t5_bucket(rel):
    nb = NUM_BUCKETS // 2
    max_exact = nb // 2
    ret = (rel > 0).astype(jnp.int32) * nb
    n = jnp.abs(rel)
    nf = jnp.maximum(n, 1).astype(jnp.float32)
    large = max_exact + (jnp.log(nf / max_exact) / math.log(MAX_DISTANCE / max_exact)
                         * (nb - max_exact)).astype(jnp.int32)
    large = jnp.minimum(large, nb - 1)
    return ret + jnp.where(n < max_exact, n, large)


def _bias_tiles(rel_bias, tile):
    assert tile >= MAX_DISTANCE and tile % CHUNK == 0
    table = rel_bias.astype(F32)
    far = table[NUM_BUCKETS // 2 - 1]
    r = jnp.arange(tile)[:, None]
    c = jnp.arange(tile)[None, :]

    def lookup(rel):
        onehot = _t5_bucket(rel)[None, :, :, None] == jnp.arange(NUM_BUCKETS)[None, None, None, :]
        vals = jnp.where(onehot, table.T[:, None, None, :], 0.0)
        return (jnp.sum(vals, axis=-1) - far[:, None, None]) * LOG2E

    diag = jnp.where(((c // CHUNK) <= (r // CHUNK))[None], lookup(c - r), MASK_VALUE)
    left = lookup(c - r - tile)
    return jnp.stack([left, diag], axis=1)


def _attn_kernel(q_ref, k_ref, v_ref, bias_ref, lamv_ref, subln_ref, o_ref,
                 s_buf, p_buf, alpha_buf, m_all, acc_all, *, tile, lam_init):
    seq = q_ref.shape[0]
    nq = seq // tile

    s_buf[...] = jnp.zeros(s_buf.shape, F32)
    p_buf[...] = jnp.zeros(p_buf.shape, BF16)
    alpha_buf[...] = jnp.zeros(alpha_buf.shape, F32)
    m_all[...] = jnp.full(m_all.shape, MASK_VALUE, F32)
    acc_all[...] = jnp.zeros(acc_all.shape, F32)

    lane = lax.broadcasted_iota(jnp.int32, (tile, LANES), 1)

    def stage1(i, j, slot):
        q = q_ref[pl.ds(pl.multiple_of(i * tile, tile), tile), :]
        zero = jnp.zeros_like(q)
        qs = jnp.concatenate([jnp.where(lane < LANES // 2, q, zero),
                              jnp.where(lane >= LANES // 2, q, zero)], axis=0)
        kj = k_ref[pl.ds(pl.multiple_of(j * tile, tile), tile), :]
        s_buf[slot] = _dot_nt(qs, kj)

    def stage2(i, bias, slot):
        for half in range(2):
            rows = slice(half * tile, (half + 1) * tile)
            s = s_buf[slot, rows, :]
            if bias is not None:
                s = s + bias
            m_prev = m_all[i, rows, :]
            m_next = jnp.maximum(m_prev, jnp.max(s, axis=-1, keepdims=True))
            p = jnp.exp2(s - jnp.concatenate([m_next] * (tile // LANES), axis=1))
            p_buf[slot, rows, :] = p.astype(BF16)
            alpha_buf[slot, rows, :] = jnp.exp2(m_prev - m_next)
            m_all[i, rows, :] = m_next

    def stage3(i, j, slot):
        vj = v_ref[pl.ds(pl.multiple_of(j * tile, tile), tile), :]
        acc_all[i] = alpha_buf[slot] * acc_all[i] + _dot(p_buf[slot], vj)

    def run_pipeline(n_pairs, first_pair, advance, with_bias):
        n_steps = pl.cdiv(n_pairs + 2, ATT_UNROLL) * ATT_UNROLL

        def step(t, pairs, slot_a, slot_b):
            (i0, j0), (i1, j1), (i2, j2) = pairs
            valid1 = jnp.logical_and(t >= 1, t <= n_pairs)
            valid2 = jnp.logical_and(t >= 2, t <= n_pairs + 1)
            stage3(jnp.where(valid2, i2, nq), j2, slot_a)
            stage2(jnp.where(valid1, i1, nq), bias_ref[0, j1 - i1 + 1] if with_bias else None, slot_b)
            stage1(i0, j0, slot_a)
            return (advance(i0, j0), (i0, j0), (i1, j1))

        def body(u, pairs):
            for r in range(ATT_UNROLL):
                pairs = step(ATT_UNROLL * u + r, pairs, r % 2, (r + 1) % 2)
            return pairs

        lax.fori_loop(0, n_steps // ATT_UNROLL, body, (first_pair,) * 3)

    def next_far(i, j):
        wrap = j == i - 2
        done = jnp.logical_and(wrap, i == nq - 1)
        step_i = jnp.logical_and(wrap, jnp.logical_not(done))
        return (jnp.where(step_i, i + 1, i), jnp.where(done, j, jnp.where(wrap, 0, j + 1)))

    def next_near(i, j):
        wrap = j == i
        done = jnp.logical_and(wrap, i == nq - 1)
        step_i = jnp.logical_and(wrap, jnp.logical_not(done))
        return (jnp.where(step_i, i + 1, i), jnp.where(jnp.logical_or(done, wrap), j, j + 1))

    zero = jnp.int32(0)
    if nq > 2:
        run_pipeline((nq - 1) * (nq - 2) // 2, (jnp.int32(2), zero), next_far, False)
    run_pipeline(2 * nq - 1, (zero, zero), next_near, True)

    def finish(i, carry):
        acc = acc_all[i]
        a1 = acc[0:tile]
        a2 = acc[tile:2 * tile]
        l1 = a1[:, DIFF_DV:DIFF_DV + 1]
        l2 = a2[:, DIFF_DV:DIFF_DV + 1]
        lv = lamv_ref[...]
        lam = (jnp.exp(jnp.sum(lv[0:1] * lv[1:2], axis=-1, keepdims=True))
               - jnp.exp(jnp.sum(lv[2:3] * lv[3:4], axis=-1, keepdims=True)) + lam_init)
        o = a1 / l1 - lam * (a2 / l2)
        o = jnp.where(lane < DIFF_DV, o, 0.0)
        ms = jnp.sum(o * o, axis=-1, keepdims=True) * (1.0 / DIFF_DV)
        y = o * lax.rsqrt(ms + EPS) * subln_ref[...] * (1.0 - lam_init)
        o_ref[pl.ds(pl.multiple_of(i * tile, tile), tile), :] = y.astype(BF16)
        return carry

    lax.fori_loop(0, nq, finish, 0)


def _attn(dq, dk, dv, bias_tiles, lamv, subln, batch, seq, lam_init):
    tile = min(ATT_TILE, seq)
    nq = seq // tile
    head_block = pl.BlockSpec((seq, LANES), lambda b, h: (b, h))
    return pl.pallas_call(
        functools.partial(_attn_kernel, tile=tile, lam_init=lam_init),
        out_shape=jax.ShapeDtypeStruct((batch * seq, DIFF_LANES), BF16),
        grid=(batch, DIFF_HEADS),
        in_specs=[
            head_block, head_block, head_block,
            pl.BlockSpec((1, 2, tile, tile), lambda b, h: (h, 0, 0, 0)),
            pl.BlockSpec((4, LANES), lambda b, h: (0, 0)),
            pl.BlockSpec((1, LANES), lambda b, h: (0, 0)),
        ],
        out_specs=head_block,
        scratch_shapes=[pltpu.VMEM((2, 2 * tile, tile), F32),
                        pltpu.VMEM((2, 2 * tile, tile), BF16),
                        pltpu.VMEM((2, 2 * tile, LANES), F32),
                        pltpu.VMEM((nq + 1, 2 * tile, LANES), F32),
                        pltpu.VMEM((nq + 1, 2 * tile, LANES), F32)],
        compiler_params=pltpu.CompilerParams(
            dimension_semantics=("parallel", "parallel"), vmem_limit_bytes=VMEM_LIMIT_BYTES),
        name="diff_attn",
    )(dq, dk, dv, bias_tiles, lamv, subln)


def _out_kernel(x_ref, gla_ref, diff_ref, conv_ref, w1_ref, w2_ref, w3_ref, o_ref):
    o_ref[...] = (x_ref[...] + _dot(gla_ref[...], w1_ref[...]) + _dot(diff_ref[...], w2_ref[...])
                  + _dot(conv_ref[...], w3_ref[...]))


def _out(x, gla_o, diff_o, conv_o, w1, w2, w3):
    t = x.shape[0]
    tm = min(OUT_ROWS, t)
    row = lambda width: pl.BlockSpec((tm, width), lambda i: (i, 0))
    return pl.pallas_call(
        _out_kernel,
        out_shape=jax.ShapeDtypeStruct((t, D_MODEL), F32),
        grid=(t // tm,),
        in_specs=[row(D_MODEL), row(GLA_WIDTH), row(DIFF_LANES), row(CONV_DIM),
                  _const_spec((GLA_WIDTH, D_MODEL)), _const_spec((DIFF_LANES, D_MODEL)),
                  _const_spec((CONV_DIM, D_MODEL))],
        out_specs=row(D_MODEL),
        compiler_params=pltpu.CompilerParams(
            dimension_semantics=("parallel",), vmem_limit_bytes=VMEM_LIMIT_BYTES),
        name="mixer_out",
    )(x, gla_o, diff_o, conv_o, w1, w2, w3)


def _pad_cols(w, width):
    return jnp.pad(w, ((0, 0), (0, width - w.shape[1])))


def _slot_cols(w, heads, used, slot):
    lead = w.shape[0]
    w = w.reshape(lead, heads, used)
    return jnp.pad(w, ((0, 0), (0, 0), (0, slot - used))).reshape(lead, heads * slot)


def _layout_w_in(w_in):
    offs = np.cumsum([0, GLA_HEADS * GLA_DK, GLA_HEADS * GLA_DK, GLA_WIDTH, GLA_RANK, GLA_WIDTH,
                      DIFF_HEADS * 2 * DIFF_DQK, DIFF_HEADS * 2 * DIFF_DQK, DIFF_WIDTH,
                      CONV_DIM, CONV_DIM, CONV_DIM])
    gq, gk, gv, glr, gate, dq, dk, dv, cb, cc, ch = [w_in[:, offs[n]:offs[n + 1]] for n in range(11)]
    half = LANES // 2
    cols = [
        _slot_cols(gq, GLA_HEADS, GLA_DK, HEAD_SLOT),
        _slot_cols(gk, GLA_HEADS, GLA_DK, HEAD_SLOT),
        _pad_cols(glr, LANES),
        gv,
        gate,
        _slot_cols(dq, 2 * DIFF_HEADS, DIFF_DQK, half),
        _slot_cols(dk, 2 * DIFF_HEADS, DIFF_DQK, half),
        _slot_cols(dv, DIFF_HEADS, DIFF_DV, LANES),
        cb, cc, ch,
    ]
    w = jnp.concatenate(cols, axis=1)
    assert w.shape[1] == N_PROJ
    return w.astype(BF16)


def kernel(x, ffn1_norm, ffn1_gate, ffn1_up, ffn1_down, mix_norm, w_in, gla_gk_up, gla_gk_bias,
           gla_norm, diff_lambda_q1, diff_lambda_k1, diff_lambda_q2, diff_lambda_k2, diff_subln,
           rel_bias, conv_w, w_out, ffn2_norm, ffn2_gate, ffn2_up, ffn2_down, final_norm):
    batch, seq, _ = x.shape
    depth = w_in.shape[0]
    t = batch * seq
    xf = x.reshape(t, D_MODEL)
    bias_tiles = _bias_tiles(rel_bias, min(ATT_TILE, seq))
    final_g = final_norm.reshape(1, D_MODEL)

    for l in range(depth):
        lam_init = 0.8 - 0.6 * math.exp(-0.3 * l)
        xf = _ffn(xf, ffn1_norm[l].reshape(1, D_MODEL), ffn1_gate[l].astype(BF16),
                  ffn1_up[l].astype(BF16), ffn1_down[l].astype(BF16), final_g, False)

        w_pad = _layout_w_in(w_in[l])
        up = _slot_cols(gla_gk_up[l], GLA_HEADS, GLA_DK, HEAD_SLOT)
        up = jnp.pad(up, ((0, LANES - GLA_RANK), (0, 0)))
        up_hi = up.astype(BF16)
        up_lo = (up - up_hi.astype(F32)).astype(BF16)
        gkup = jnp.stack([up_hi, up_lo], axis=0)
        gkb = _slot_cols(gla_gk_bias[l].reshape(1, -1), GLA_HEADS, GLA_DK, HEAD_SLOT)
        cw = jnp.pad(conv_w[l], ((0, 8 - CONV_WIDTH), (0, 0)))
        gq, gk, gg, gv, gate, dq, dk, dv, conv_o = _proj(
            xf, mix_norm[l].reshape(1, D_MODEL), w_pad, gkup, gkb, cw, seq)

        gn_tiled = jnp.tile(gla_norm[l], GLA_HEADS).reshape(1, GLA_WIDTH)
        gla_o = _gla(gq, gk, gg, gv, gate, gn_tiled, batch, seq)

        lamv = jnp.stack([diff_lambda_q1[l], diff_lambda_k1[l], diff_lambda_q2[l], diff_lambda_k2[l]])
        lamv = _pad_cols(lamv.astype(F32), LANES)
        subln = _pad_cols(diff_subln[l].reshape(1, DIFF_DV), LANES)
        diff_o = _attn(dq, dk, dv, bias_tiles, lamv, subln, batch, seq, lam_init)

        wo = w_out[l]
        w1 = wo[0:GLA_WIDTH].astype(BF16)
        w2 = wo[GLA_WIDTH:GLA_WIDTH + DIFF_WIDTH].reshape(DIFF_HEADS, DIFF_DV, D_MODEL)
        w2 = jnp.pad(w2, ((0, 0), (0, LANES - DIFF_DV), (0, 0))).reshape(DIFF_LANES, D_MODEL).astype(BF16)
        w3 = wo[GLA_WIDTH + DIFF_WIDTH:].astype(BF16)
        xf = _out(xf, gla_o, diff_o, conv_o, w1, w2, w3)

        xf = _ffn(xf, ffn2_norm[l].reshape(1, D_MODEL), ffn2_gate[l].astype(BF16),
                  ffn2_up[l].astype(BF16), ffn2_down[l].astype(BF16), final_g, l == depth - 1)

    return xf.reshape(batch, seq, D_MODEL)
```

```python
import functools
import math

import numpy as np
import jax
import jax.numpy as jnp
from jax import lax
from jax.experimental import pallas as pl
from jax.experimental.pallas import tpu as pltpu

F32 = jnp.float32
BF16 = jnp.bfloat16

D_MODEL = 1024
D_FF = 2816
EPS = 1e-6
CHUNK = 64
GLA_HEADS = 4
GLA_DK = 48
GLA_DV = 96
GLA_RANK = 16
GLA_GATE_NORM = 16.0
DIFF_HEADS = 4
DIFF_DQK = 48
DIFF_DV = 96
CONV_DIM = 256
CONV_WIDTH = 3
NUM_BUCKETS = 32
MAX_DISTANCE = 128
GLA_WIDTH = GLA_HEADS * GLA_DV
DIFF_WIDTH = DIFF_HEADS * DIFF_DV

LANES = 128
VMEM_LIMIT_BYTES = 56 * 1024 * 1024

HEAD_SLOT = 64
GLA_QK_LANES = GLA_HEADS * HEAD_SLOT
SUB = 16
N_SUB = CHUNK // SUB
KST_ROWS = SUB * (N_SUB * (N_SUB - 1) // 2)
DIFF_LANES = DIFF_HEADS * LANES
MASK_VALUE = -1e30
GLA_SAFE_DECAY = 60.0
LOG2E = math.log2(math.e)

OFF_GQ = 0
OFF_GK = OFF_GQ + GLA_QK_LANES
OFF_GLR = OFF_GK + GLA_QK_LANES
OFF_GV = OFF_GLR + LANES
OFF_GATE = OFF_GV + GLA_WIDTH
OFF_DQ = OFF_GATE + GLA_WIDTH
OFF_DK = OFF_DQ + DIFF_LANES
OFF_DV = OFF_DK + DIFF_LANES
OFF_CB = OFF_DV + DIFF_LANES
OFF_CC = OFF_CB + CONV_DIM
OFF_CH = OFF_CC + CONV_DIM
N_PROJ = OFF_CH + CONV_DIM

FFN_ROWS = 512
FFN_F_CHUNK = 1408
PROJ_ROWS = 512
GLA_ROWS = 512
ATT_TILE = 256
ATT_UNROLL = 8
OUT_ROWS = 512


def _dot(a, b):
    return jnp.dot(a, b, preferred_element_type=F32)


def _dot_nt(a, b):
    return lax.dot_general(a, b, (((1,), (1,)), ((), ())), preferred_element_type=F32)


def _dot_tn(a, b):
    return lax.dot_general(a, b, (((0,), (0,)), ((), ())), preferred_element_type=F32)


def _split_bf16(x):
    hi = x.astype(BF16)
    lo = (x - hi.astype(F32)).astype(BF16)
    return hi, lo


def _rms(x, g):
    return x * lax.rsqrt(jnp.mean(x * x, axis=-1, keepdims=True) + EPS) * g


def _sigmoid(x):
    return 1.0 / (1.0 + jnp.exp(-x))


def _ffn_kernel(x_ref, g_ref, wg_ref, wu_ref, wd_ref, fg_ref, o_ref, *, final):
    x = x_ref[...]
    h = _rms(x, g_ref[...]).astype(BF16)
    acc = None
    for c in range(D_FF // FFN_F_CHUNK):
        sl = slice(c * FFN_F_CHUNK, (c + 1) * FFN_F_CHUNK)
        gate = _dot(h, wg_ref[:, sl])
        up = _dot(h, wu_ref[:, sl])
        a = (gate * _sigmoid(gate) * up).astype(BF16)
        part = _dot(a, wd_ref[sl, :])
        acc = part if acc is None else acc + part
    y = x + 0.5 * acc
    if final:
        y = _rms(y, fg_ref[...])
    o_ref[...] = y


def _const_spec(shape):
    nd = len(shape)
    return pl.BlockSpec(shape, lambda *_: (0,) * nd, pipeline_mode=pl.Buffered(1))


def _ffn(x, g, wg, wu, wd, final_g, final):
    t = x.shape[0]
    tm = min(FFN_ROWS, t)
    return pl.pallas_call(
        functools.partial(_ffn_kernel, final=final),
        out_shape=jax.ShapeDtypeStruct((t, D_MODEL), F32),
        grid=(t // tm,),
        in_specs=[
            pl.BlockSpec((tm, D_MODEL), lambda i: (i, 0)),
            _const_spec((1, D_MODEL)),
            _const_spec((D_MODEL, D_FF)),
            _const_spec((D_MODEL, D_FF)),
            _const_spec((D_FF, D_MODEL)),
            _const_spec((1, D_MODEL)),
        ],
        out_specs=pl.BlockSpec((tm, D_MODEL), lambda i: (i, 0)),
        compiler_params=pltpu.CompilerParams(
            dimension_semantics=("parallel",), vmem_limit_bytes=VMEM_LIMIT_BYTES),
        name="ffn",
    )(x, g, wg, wu, wd, final_g)


def _proj_kernel(x_ref, g_ref, w_ref, gkup_ref, gkb_ref, cw_ref,
                 gq_ref, gk_ref, gg_ref, gv_ref, gate_ref, dq_ref, dk_ref, dv_ref, conv_ref,
                 zbuf_ref, *, tiles_per_seq):
    tm = x_ref.shape[0]
    h = _rms(x_ref[...], g_ref[...]).astype(BF16)

    def proj(off, width):
        return _dot(h, w_ref[:, off:off + width])

    gq_ref[...] = proj(OFF_GQ, GLA_QK_LANES)
    gk_ref[...] = proj(OFF_GK, GLA_QK_LANES)
    gv_ref[...] = proj(OFF_GV, GLA_WIDTH).astype(BF16)
    gate_ref[...] = proj(OFF_GATE, GLA_WIDTH)

    glr_hi, glr_lo = _split_bf16(proj(OFF_GLR, LANES))
    up_hi = gkup_ref[0]
    up_lo = gkup_ref[1]
    z = _dot(glr_hi, up_hi) + _dot(glr_lo, up_hi) + _dot(glr_hi, up_lo) + gkb_ref[...]
    logsig = jnp.minimum(z, 0.0) - jnp.log1p(jnp.exp(-jnp.abs(z)))
    lane = lax.broadcasted_iota(jnp.int32, (tm, GLA_QK_LANES), 1)
    gg_ref[...] = jnp.where(lane % HEAD_SLOT < GLA_DK, logsig * (1.0 / GLA_GATE_NORM), 0.0)

    dq_ref[...] = (proj(OFF_DQ, DIFF_LANES) * (DIFF_DQK ** -0.5 * LOG2E)).astype(BF16)
    dk_ref[...] = proj(OFF_DK, DIFF_LANES).astype(BF16)
    lane_v = lax.broadcasted_iota(jnp.int32, (tm, DIFF_LANES), 1)
    dv_ref[...] = jnp.where(lane_v % LANES == DIFF_DV, 1.0, proj(OFF_DV, DIFF_LANES)).astype(BF16)

    @pl.when(pl.program_id(0) % tiles_per_seq == 0)
    def _():
        zbuf_ref[0:8, :] = jnp.zeros((8, CONV_DIM), F32)

    zc = proj(OFF_CC, CONV_DIM) * proj(OFF_CH, CONV_DIM)
    zbuf_ref[8:8 + tm, :] = zc
    cw = cw_ref[...]
    y = (cw[2:3, :] * zc + cw[1:2, :] * zbuf_ref[7:7 + tm, :] + cw[0:1, :] * zbuf_ref[6:6 + tm, :])
    conv_ref[...] = (proj(OFF_CB, CONV_DIM) * y).astype(BF16)
    zbuf_ref[0:8, :] = zbuf_ref[tm:tm + 8, :]


def _proj(x, g, w, gkup, gkb, cw, seq):
    t = x.shape[0]
    tm = min(PROJ_ROWS, seq)
    row = lambda width: pl.BlockSpec((tm, width), lambda i: (i, 0))
    out_shapes = (
        jax.ShapeDtypeStruct((t, GLA_QK_LANES), F32),
        jax.ShapeDtypeStruct((t, GLA_QK_LANES), F32),
        jax.ShapeDtypeStruct((t, GLA_QK_LANES), F32),
        jax.ShapeDtypeStruct((t, GLA_WIDTH), BF16),
        jax.ShapeDtypeStruct((t, GLA_WIDTH), F32),
        jax.ShapeDtypeStruct((t, DIFF_LANES), BF16),
        jax.ShapeDtypeStruct((t, DIFF_LANES), BF16),
        jax.ShapeDtypeStruct((t, DIFF_LANES), BF16),
        jax.ShapeDtypeStruct((t, CONV_DIM), BF16),
    )
    return pl.pallas_call(
        functools.partial(_proj_kernel, tiles_per_seq=seq // tm),
        out_shape=out_shapes,
        grid=(t // tm,),
        in_specs=[
            row(D_MODEL),
            _const_spec((1, D_MODEL)),
            _const_spec((D_MODEL, N_PROJ)),
            _const_spec((2, LANES, GLA_QK_LANES)),
            _const_spec((1, GLA_QK_LANES)),
            _const_spec((8, CONV_DIM)),
        ],
        out_specs=tuple(row(s.shape[1]) for s in out_shapes),
        scratch_shapes=[pltpu.VMEM((tm + 8, CONV_DIM), F32)],
        compiler_params=pltpu.CompilerParams(
            dimension_semantics=("arbitrary",), vmem_limit_bytes=VMEM_LIMIT_BYTES),
        name="mixer_proj",
    )(x, g, w, gkup, gkb, cw)


def _gla_cum_matrix():
    t = np.arange(CHUNK)[:, None]
    u = np.arange(CHUNK)[None, :]
    tri = (u <= t)
    ref = (u <= (t // SUB) * SUB - 1)
    ones = np.ones((CHUNK, CHUNK), bool)
    parts = [tri, ref, ones]
    for j in range(1, N_SUB):
        s = np.arange(SUB * j)[:, None]
        parts.append((u > s) & (u <= SUB * j - 1))
    return np.concatenate(parts, axis=0).astype(np.float32)


def _gla_score_mask():
    t = np.arange(CHUNK)[:, None]
    m = np.arange(KST_ROWS)
    group = np.concatenate([np.full(SUB * j, j) for j in range(1, N_SUB)])
    keep = (group[None, :] == (t // SUB))
    return np.tile(keep, (1, GLA_HEADS)).astype(np.float32)


def _gla_kernel(q_ref, k_ref, g_ref, v_ref, gate_ref, cum_ref, cumf_ref, amask_ref, gn_ref,
                o_ref, s_ref, raw_ref):
    rows = q_ref.shape[0]
    n_chunks = rows // CHUNK

    @pl.when(pl.program_id(1) == 0)
    def _():
        s_ref[...] = jnp.zeros_like(s_ref)

    qk_head = lax.broadcasted_iota(jnp.int32, (1, GLA_QK_LANES), 1) // HEAD_SLOT
    v_head = lax.broadcasted_iota(jnp.int32, (1, GLA_WIDTH), 1) // GLA_DV
    row_head_v = lax.broadcasted_iota(jnp.int32, (GLA_WIDTH, 1), 0) // GLA_DV
    state_mask = row_head_v == qk_head
    scale = GLA_DK ** -0.5

    safe = jnp.min(g_ref[...]) * CHUNK >= -GLA_SAFE_DECAY

    def fast_path():
        chunk = lambda x, c: x[c * CHUNK:(c + 1) * CHUNK]
        sel_r = lax.broadcasted_iota(jnp.int32, (8, rows), 0)
        sel_c = lax.broadcasted_iota(jnp.int32, (8, rows), 1) // CHUNK
        chunk_sel = jnp.where(sel_r == sel_c, 1.0, 0.0).astype(BF16)
        gs_hi, gs_lo = _split_bf16(g_ref[...])
        totals = _dot(chunk_sel, gs_hi) + _dot(chunk_sel, gs_lo)
        tri = cumf_ref[...]
        b = jnp.concatenate([_dot(tri, chunk(gs_hi, c)) + _dot(tri, chunk(gs_lo, c))
                             for c in range(n_chunks)], axis=0)
        b_last = jnp.concatenate([jnp.broadcast_to(totals[c:c + 1], (CHUNK, GLA_QK_LANES))
                                  for c in range(n_chunks)], axis=0)
        q = q_ref[...] * scale
        k = k_ref[...]
        v = v_ref[...]
        qd = (q * jnp.exp(b)).astype(BF16)
        kd = (k * jnp.exp(-b)).astype(BF16)
        kw = (k * jnp.exp(b_last - b)).astype(BF16)
        t_idx = lax.broadcasted_iota(jnp.int32, (CHUNK, GLA_QK_LANES), 0)
        s_idx = lax.broadcasted_iota(jnp.int32, (CHUNK, GLA_QK_LANES), 1) % HEAD_SLOT
        causal = s_idx <= t_idx
        intra, update = [], []
        for c in range(n_chunks):
            kd_c, v_c = chunk(kd, c), chunk(v, c)
            kd_bd = jnp.concatenate(
                [jnp.where(qk_head == hd, kd_c, jnp.zeros_like(kd_c)) for hd in range(GLA_HEADS)],
                axis=0)
            a = _dot_nt(chunk(qd, c), kd_bd)
            a = jnp.where(causal, a, 0.0).astype(BF16)
            v_bd = jnp.concatenate(
                [jnp.where(v_head == hd, v_c, jnp.zeros_like(v_c)) for hd in range(GLA_HEADS)],
                axis=0)
            intra.append(_dot(a, v_bd))
            update.append(_dot_tn(v_c, chunk(kw, c)))
        state = s_ref[...]
        for c in range(n_chunks):
            raw_ref[c * CHUNK:(c + 1) * CHUNK, :] = (
                intra[c] + _dot_nt(chunk(qd, c), state.astype(BF16)))
            state = state * jnp.exp(totals[c:c + 1]) + jnp.where(state_mask, update[c], 0.0)
        s_ref[...] = state

    @pl.when(safe)
    def _():
        fast_path()

    def general_path():
        cum = cum_ref[...]
        amask = amask_ref[...] > 0.5
        ind_r = lax.broadcasted_iota(jnp.int32, (GLA_QK_LANES, LANES), 0) // HEAD_SLOT
        ind_c = lax.broadcasted_iota(jnp.int32, (GLA_QK_LANES, LANES), 1)
        ind_sum = jnp.where(ind_r == ind_c, 1.0, 0.0).astype(BF16)
        bc_r = lax.broadcasted_iota(jnp.int32, (LANES, GLA_WIDTH), 0)
        bc_c = lax.broadcasted_iota(jnp.int32, (LANES, GLA_WIDTH), 1) // GLA_DV
        ind_bcast = jnp.where(bc_r == bc_c, 1.0, 0.0).astype(BF16)
        row_in_sub = lax.broadcasted_iota(jnp.int32, (CHUNK, 1), 0) % SUB

        def chunk_body(c, carry):
            r0 = pl.multiple_of(c * CHUNK, CHUNK)
            q = q_ref[pl.ds(r0, CHUNK), :] * scale
            k = k_ref[pl.ds(r0, CHUNK), :]
            g = g_ref[pl.ds(r0, CHUNK), :]
            v = v_ref[pl.ds(r0, CHUNK), :]
            g_hi, g_lo = _split_bf16(g)
            cums = _dot(cum, g_hi) + _dot(cum, g_lo)
            b = cums[0:CHUNK]
            b_ref = cums[CHUNK:2 * CHUNK]
            b_last = cums[2 * CHUNK:3 * CHUNK]
            e_kst = cums[3 * CHUNK:3 * CHUNK + KST_ROWS]

            state = s_ref[...]
            q_in = (q * jnp.exp(b)).astype(BF16)
            o = _dot_nt(q_in, state.astype(BF16))

            kw = (k * jnp.exp(b_last - b)).astype(BF16)
            decay = jnp.exp(jnp.concatenate([b_last] * (GLA_WIDTH // CHUNK), axis=0))
            s_ref[...] = state * decay + jnp.where(state_mask, _dot_tn(v, kw), 0.0)

            q_sub = (q * jnp.exp(jnp.minimum(b - b_ref, 0.0))).astype(BF16)
            k_st = jnp.concatenate([k[0:SUB * j] for j in range(1, N_SUB)], axis=0) * jnp.exp(e_kst)
            k_bd = jnp.concatenate(
                [jnp.where(qk_head == hd, k_st, 0.0) for hd in range(GLA_HEADS)], axis=0).astype(BF16)
            a = _dot_nt(q_sub, k_bd)
            a = jnp.where(amask, a, 0.0).astype(BF16)
            v_st = jnp.concatenate([v[0:SUB * j] for j in range(1, N_SUB)], axis=0)
            v_bd = jnp.concatenate(
                [jnp.where(v_head == hd, v_st, jnp.zeros_like(v_st)) for hd in range(GLA_HEADS)], axis=0)
            o = o + _dot(a, v_bd)

            vf = v.astype(F32)
            prods = []
            for d in range(SUB):
                ks = k if d == 0 else pltpu.roll(k, d, 0)
                bs = b if d == 0 else pltpu.roll(b, d, 0)
                prods.append((q * ks * jnp.exp(jnp.minimum(b - bs, 0.0))).astype(BF16))
            dsum = _dot(jnp.concatenate(prods, axis=0), ind_sum)
            valid = jnp.concatenate([row_in_sub >= d for d in range(SUB)], axis=0)
            dsum = jnp.where(valid, dsum, 0.0).astype(BF16)
            dbc = _dot(dsum, ind_bcast)
            for d in range(SUB):
                vs = vf if d == 0 else pltpu.roll(vf, d, 0)
                o = o + dbc[d * CHUNK:(d + 1) * CHUNK] * vs
            raw_ref[pl.ds(r0, CHUNK), :] = o
            return carry

        lax.fori_loop(0, n_chunks, chunk_body, 0)

    @pl.when(jnp.logical_not(safe))
    def _():
        general_path()

    o = raw_ref[...]
    hr = lax.broadcasted_iota(jnp.int32, (GLA_WIDTH, GLA_WIDTH), 0) // GLA_DV
    hc = lax.broadcasted_iota(jnp.int32, (GLA_WIDTH, GLA_WIDTH), 1) // GLA_DV
    head_ones = jnp.where(hr == hc, 1.0, 0.0).astype(BF16)
    ms = _dot((o * o).astype(BF16), head_ones) * (1.0 / GLA_DV)
    gate = gate_ref[...]
    y = o * lax.rsqrt(ms + EPS) * gn_ref[...] * (gate * _sigmoid(gate))
    o_ref[...] = y.astype(BF16)


def _gla(gq, gk, gg, gv, gate, gn_tiled, batch, seq):
    rows = min(GLA_ROWS, seq)
    steps = seq // rows
    row = lambda width: pl.BlockSpec((rows, width), lambda b, i: (b * steps + i, 0))
    assert rows // CHUNK <= 8
    cum_np = _gla_cum_matrix()
    cum = jnp.asarray(cum_np, BF16)
    cumf = jnp.asarray(cum_np[0:CHUNK], BF16)
    amask = jnp.asarray(_gla_score_mask(), F32)
    return pl.pallas_call(
        _gla_kernel,
        out_shape=jax.ShapeDtypeStruct((batch * seq, GLA_WIDTH), BF16),
        grid=(batch, steps),
        in_specs=[
            row(GLA_QK_LANES), row(GLA_QK_LANES), row(GLA_QK_LANES), row(GLA_WIDTH), row(GLA_WIDTH),
            pl.BlockSpec(cum.shape, lambda b, i: (0, 0)),
            pl.BlockSpec(cumf.shape, lambda b, i: (0, 0)),
            pl.BlockSpec(amask.shape, lambda b, i: (0, 0)),
            pl.BlockSpec((1, GLA_WIDTH), lambda b, i: (0, 0)),
        ],
        out_specs=row(GLA_WIDTH),
        scratch_shapes=[pltpu.VMEM((GLA_WIDTH, GLA_QK_LANES), F32),
                        pltpu.VMEM((rows, GLA_WIDTH), F32)],
        compiler_params=pltpu.CompilerParams(
            dimension_semantics=("arbitrary", "arbitrary"), vmem_limit_bytes=VMEM_LIMIT_BYTES),
        name="gla",
    )(gq, gk, gg, gv, gate, cum, cumf, amask, gn_tiled)


def _t5_bucket(rel):
    nb = NUM_BUCKETS // 2
    max_exact = nb // 2
    ret = (rel > 0).astype(jnp.int32) * nb
    n = jnp.abs(rel)
    nf = jnp.maximum(n, 1).astype(jnp.float32)
    large = max_exact + (jnp.log(nf / max_exact) / math.log(MAX_DISTANCE / max_exact)
                         * (nb - max_exact)).astype(jnp.int32)
    large = jnp.minimum(large, nb - 1)
    return ret + jnp.where(n < max_exact, n, large)


def _bias_tiles(rel_bias, tile):
    assert tile >= MAX_DISTANCE and tile % CHUNK == 0
    table = rel_bias.astype(F32)
    far = table[NUM_BUCKETS // 2 - 1]
    r = jnp.arange(tile)[:, None]
    c = jnp.arange(tile)[None, :]

    def lookup(rel):
        onehot = _t5_bucket(rel)[None, :, :, None] == jnp.arange(NUM_BUCKETS)[None, None, None, :]
        vals = jnp.where(onehot, table.T[:, None, None, :], 0.0)
        return (jnp.sum(vals, axis=-1) - far[:, None, None]) * LOG2E

    diag = jnp.where(((c // CHUNK) <= (r // CHUNK))[None], lookup(c - r), MASK_VALUE)
    left = lookup(c - r - tile)
    return jnp.stack([left, diag], axis=1)


def _attn_kernel(q_ref, k_ref, v_ref, bias_ref, lamv_ref, subln_ref, o_ref,
                 s_buf, p_buf, alpha_buf, m_all, acc_all, *, tile, lam_init):
    seq = q_ref.shape[0]
    nq = seq // tile

    s_buf[...] = jnp.zeros(s_buf.shape, F32)
    p_buf[...] = jnp.zeros(p_buf.shape, BF16)
    alpha_buf[...] = jnp.zeros(alpha_buf.shape, F32)
    m_all[...] = jnp.full(m_all.shape, MASK_VALUE, F32)
    acc_all[...] = jnp.zeros(acc_all.shape, F32)

    lane = lax.broadcasted_iota(jnp.int32, (tile, LANES), 1)

    def stage1(i, j, slot):
        q = q_ref[pl.ds(pl.multiple_of(i * tile, tile), tile), :]
        zero = jnp.zeros_like(q)
        qs = jnp.concatenate([jnp.where(lane < LANES // 2, q, zero),
                              jnp.where(lane >= LANES // 2, q, zero)], axis=0)
        kj = k_ref[pl.ds(pl.multiple_of(j * tile, tile), tile), :]
        s_buf[slot] = _dot_nt(qs, kj)

    def stage2(i, bias, slot):
        for half in range(2):
            rows = slice(half * tile, (half + 1) * tile)
            s = s_buf[slot, rows, :]
            if bias is not None:
                s = s + bias
            m_prev = m_all[i, rows, :]
            m_next = jnp.maximum(m_prev, jnp.max(s, axis=-1, keepdims=True))
            p = jnp.exp2(s - jnp.concatenate([m_next] * (tile // LANES), axis=1))
            p_buf[slot, rows, :] = p.astype(BF16)
            alpha_buf[slot, rows, :] = jnp.exp2(m_prev - m_next)
            m_all[i, rows, :] = m_next

    def stage3(i, j, slot):
        vj = v_ref[pl.ds(pl.multiple_of(j * tile, tile), tile), :]
        acc_all[i] = alpha_buf[slot] * acc_all[i] + _dot(p_buf[slot], vj)

    def run_pipeline(n_pairs, first_pair, advance, with_bias):
        n_steps = pl.cdiv(n_pairs + 2, ATT_UNROLL) * ATT_UNROLL

        def step(t, pairs, slot_a, slot_b):
            (i0, j0), (i1, j1), (i2, j2) = pairs
            valid1 = jnp.logical_and(t >= 1, t <= n_pairs)
            valid2 = jnp.logical_and(t >= 2, t <= n_pairs + 1)
            stage3(jnp.where(valid2, i2, nq), j2, slot_a)
            stage2(jnp.where(valid1, i1, nq), bias_ref[0, j1 - i1 + 1] if with_bias else None, slot_b)
            stage1(i0, j0, slot_a)
            return (advance(i0, j0), (i0, j0), (i1, j1))

        def body(u, pairs):
            for r in range(ATT_UNROLL):
                pairs = step(ATT_UNROLL * u + r, pairs, r % 2, (r + 1) % 2)
            return pairs

        lax.fori_loop(0, n_steps // ATT_UNROLL, body, (first_pair,) * 3)

    def next_far(i, j):
        wrap = j == i - 2
        done = jnp.logical_and(wrap, i == nq - 1)
        step_i = jnp.logical_and(wrap, jnp.logical_not(done))
        return (jnp.where(step_i, i + 1, i), jnp.where(done, j, jnp.where(wrap, 0, j + 1)))

    def next_near(i, j):
        wrap = j == i
        done = jnp.logical_and(wrap, i == nq - 1)
        step_i = jnp.logical_and(wrap, jnp.logical_not(done))
        return (jnp.where(step_i, i + 1, i), jnp.where(jnp.logical_or(done, wrap), j, j + 1))

    zero = jnp.int32(0)
    if nq > 2:
        run_pipeline((nq - 1) * (nq - 2) // 2, (jnp.int32(2), zero), next_far, False)
    run_pipeline(2 * nq - 1, (zero, zero), next_near, True)

    def finish(i, carry):
        acc = acc_all[i]
        a1 = acc[0:tile]
        a2 = acc[tile:2 * tile]
        l1 = a1[:, DIFF_DV:DIFF_DV + 1]
        l2 = a2[:, DIFF_DV:DIFF_DV + 1]
        lv = lamv_ref[...]
        lam = (jnp.exp(jnp.sum(lv[0:1] * lv[1:2], axis=-1, keepdims=True))
               - jnp.exp(jnp.sum(lv[2:3] * lv[3:4], axis=-1, keepdims=True)) + lam_init)
        o = a1 / l1 - lam * (a2 / l2)
        o = jnp.where(lane < DIFF_DV, o, 0.0)
        ms = jnp.sum(o * o, axis=-1, keepdims=True) * (1.0 / DIFF_DV)
        y = o * lax.rsqrt(ms + EPS) * subln_ref[...] * (1.0 - lam_init)
        o_ref[pl.ds(pl.multiple_of(i * tile, tile), tile), :] = y.astype(BF16)
        return carry

    lax.fori_loop(0, nq, finish, 0)


def _attn(dq, dk, dv, bias_tiles, lamv, subln, batch, seq, lam_init):
    tile = min(ATT_TILE, seq)
    nq = seq // tile
    head_block = pl.BlockSpec((seq, LANES), lambda b, h: (b, h))
    return pl.pallas_call(
        functools.partial(_attn_kernel, tile=tile, lam_init=lam_init),
        out_shape=jax.ShapeDtypeStruct((batch * seq, DIFF_LANES), BF16),
        grid=(batch, DIFF_HEADS),
        in_specs=[
            head_block, head_block, head_block,
            pl.BlockSpec((1, 2, tile, tile), lambda b, h: (h, 0, 0, 0)),
            pl.BlockSpec((4, LANES), lambda b, h: (0, 0)),
            pl.BlockSpec((1, LANES), lambda b, h: (0, 0)),
        ],
        out_specs=head_block,
        scratch_shapes=[pltpu.VMEM((2, 2 * tile, tile), F32),
                        pltpu.VMEM((2, 2 * tile, tile), BF16),
                        pltpu.VMEM((2, 2 * tile, LANES), F32),
                        pltpu.VMEM((nq + 1, 2 * tile, LANES), F32),
                        pltpu.VMEM((nq + 1, 2 * tile, LANES), F32)],
        compiler_params=pltpu.CompilerParams(
            dimension_semantics=("parallel", "parallel"), vmem_limit_bytes=VMEM_LIMIT_BYTES),
        name="diff_attn",
    )(dq, dk, dv, bias_tiles, lamv, subln)


def _out_kernel(x_ref, gla_ref, diff_ref, conv_ref, w1_ref, w2_ref, w3_ref, o_ref):
    o_ref[...] = (x_ref[...] + _dot(gla_ref[...], w1_ref[...]) + _dot(diff_ref[...], w2_ref[...])
                  + _dot(conv_ref[...], w3_ref[...]))


def _out(x, gla_o, diff_o, conv_o, w1, w2, w3):
    t = x.shape[0]
    tm = min(OUT_ROWS, t)
    row = lambda width: pl.BlockSpec((tm, width), lambda i: (i, 0))
    return pl.pallas_call(
        _out_kernel,
        out_shape=jax.ShapeDtypeStruct((t, D_MODEL), F32),
        grid=(t // tm,),
        in_specs=[row(D_MODEL), row(GLA_WIDTH), row(DIFF_LANES), row(CONV_DIM),
                  _const_spec((GLA_WIDTH, D_MODEL)), _const_spec((DIFF_LANES, D_MODEL)),
                  _const_spec((CONV_DIM, D_MODEL))],
        out_specs=row(D_MODEL),
        compiler_params=pltpu.CompilerParams(
            dimension_semantics=("parallel",), vmem_limit_bytes=VMEM_LIMIT_BYTES),
        name="mixer_out",
    )(x, gla_o, diff_o, conv_o, w1, w2, w3)


def _pad_cols(w, width):
    return jnp.pad(w, ((0, 0), (0, width - w.shape[1])))


def _slot_cols(w, heads, used, slot):
    lead = w.shape[0]
    w = w.reshape(lead, heads, used)
    return jnp.pad(w, ((0, 0), (0, 0), (0, slot - used))).reshape(lead, heads * slot)


def _layout_w_in(w_in):
    offs = np.cumsum([0, GLA_HEADS * GLA_DK, GLA_HEADS * GLA_DK, GLA_WIDTH, GLA_RANK, GLA_WIDTH,
                      DIFF_HEADS * 2 * DIFF_DQK, DIFF_HEADS * 2 * DIFF_DQK, DIFF_WIDTH,
                      CONV_DIM, CONV_DIM, CONV_DIM])
    gq, gk, gv, glr, gate, dq, dk, dv, cb, cc, ch = [w_in[:, offs[n]:offs[n + 1]] for n in range(11)]
    half = LANES // 2
    cols = [
        _slot_cols(gq, GLA_HEADS, GLA_DK, HEAD_SLOT),
        _slot_cols(gk, GLA_HEADS, GLA_DK, HEAD_SLOT),
        _pad_cols(glr, LANES),
        gv,
        gate,
        _slot_cols(dq, 2 * DIFF_HEADS, DIFF_DQK, half),
        _slot_cols(dk, 2 * DIFF_HEADS, DIFF_DQK, half),
        _slot_cols(dv, DIFF_HEADS, DIFF_DV, LANES),
        cb, cc, ch,
    ]
    w = jnp.concatenate(cols, axis=1)
    assert w.shape[1] == N_PROJ
    return w.astype(BF16)


def kernel(x, ffn1_norm, ffn1_gate, ffn1_up, ffn1_down, mix_norm, w_in, gla_gk_up, gla_gk_bias,
           gla_norm, diff_lambda_q1, diff_lambda_k1, diff_lambda_q2, diff_lambda_k2, diff_subln,
           rel_bias, conv_w, w_out, ffn2_norm, ffn2_gate, ffn2_up, ffn2_down, final_norm):
    batch, seq, _ = x.shape
    depth = w_in.shape[0]
    t = batch * seq
    xf = x.reshape(t, D_MODEL)
    bias_tiles = _bias_tiles(rel_bias, min(ATT_TILE, seq))
    final_g = final_norm.reshape(1, D_MODEL)

    for l in range(depth):
        lam_init = 0.8 - 0.6 * math.exp(-0.3 * l)
        xf = _ffn(xf, ffn1_norm[l].reshape(1, D_MODEL), ffn1_gate[l].astype(BF16),
                  ffn1_up[l].astype(BF16), ffn1_down[l].astype(BF16), final_g, False)

        w_pad = _layout_w_in(w_in[l])
        up = _slot_cols(gla_gk_up[l], GLA_HEADS, GLA_DK, HEAD_SLOT)
        up = jnp.pad(up, ((0, LANES - GLA_RANK), (0, 0)))
        up_hi = up.astype(BF16)
        up_lo = (up - up_hi.astype(F32)).astype(BF16)
        gkup = jnp.stack([up_hi, up_lo], axis=0)
        gkb = _slot_cols(gla_gk_bias[l].reshape(1, -1), GLA_HEADS, GLA_DK, HEAD_SLOT)
        cw = jnp.pad(conv_w[l], ((0, 8 - CONV_WIDTH), (0, 0)))
        gq, gk, gg, gv, gate, dq, dk, dv, conv_o = _proj(
            xf, mix_norm[l].reshape(1, D_MODEL), w_pad, gkup, gkb, cw, seq)

        gn_tiled = jnp.tile(gla_norm[l], GLA_HEADS).reshape(1, GLA_WIDTH)
        gla_o = _gla(gq, gk, gg, gv, gate, gn_tiled, batch, seq)

        lamv = jnp.stack([diff_lambda_q1[l], diff_lambda_k1[l], diff_lambda_q2[l], diff_lambda_k2[l]])
        lamv = _pad_cols(lamv.astype(F32), LANES)
        subln = _pad_cols(diff_subln[l].reshape(1, DIFF_DV), LANES)
        diff_o = _attn(dq, dk, dv, bias_tiles, lamv, subln, batch, seq, lam_init)

        wo = w_out[l]
        w1 = wo[0:GLA_WIDTH].astype(BF16)
        w2 = wo[GLA_WIDTH:GLA_WIDTH + DIFF_WIDTH].reshape(DIFF_HEADS, DIFF_DV, D_MODEL)
        w2 = jnp.pad(w2, ((0, 0), (0, LANES - DIFF_DV), (0, 0))).reshape(DIFF_LANES, D_MODEL).astype(BF16)
        w3 = wo[GLA_WIDTH + DIFF_WIDTH:].astype(BF16)
        xf = _out(xf, gla_o, diff_o, conv_o, w1, w2, w3)

        xf = _ffn(xf, ffn2_norm[l].reshape(1, D_MODEL), ffn2_gate[l].astype(BF16),
                  ffn2_up[l].astype(BF16), ffn2_down[l].astype(BF16), final_g, l == depth - 1)

    return xf.reshape(batch, seq, D_MODEL)
```

```python
import functools
import math

import numpy as np
import jax
import jax.numpy as jnp
from jax import lax
from jax.experimental import pallas as pl
from jax.experimental.pallas import tpu as pltpu

F32 = jnp.float32
BF16 = jnp.bfloat16

D_MODEL = 1024
D_FF = 2816
EPS = 1e-6
CHUNK = 64
GLA_HEADS = 4
GLA_DK = 48
GLA_DV = 96
GLA_RANK = 16
GLA_GATE_NORM = 16.0
DIFF_HEADS = 4
DIFF_DQK = 48
DIFF_DV = 96
CONV_DIM = 256
CONV_WIDTH = 3
NUM_BUCKETS = 32
MAX_DISTANCE = 128
GLA_WIDTH = GLA_HEADS * GLA_DV
DIFF_WIDTH = DIFF_HEADS * DIFF_DV

LANES = 128
VMEM_LIMIT_BYTES = 56 * 1024 * 1024

HEAD_SLOT = 64
GLA_QK_LANES = GLA_HEADS * HEAD_SLOT
SUB = 16
N_SUB = CHUNK // SUB
KST_ROWS = SUB * (N_SUB * (N_SUB - 1) // 2)
DIFF_LANES = DIFF_HEADS * LANES
MASK_VALUE = -1e30
GLA_SAFE_DECAY = 60.0
LOG2E = math.log2(math.e)

OFF_GQ = 0
OFF_GK = OFF_GQ + GLA_QK_LANES
OFF_GV = OFF_GK + GLA_QK_LANES
OFF_GATE = OFF_GV + GLA_WIDTH
OFF_DQ = OFF_GATE + GLA_WIDTH
OFF_DK = OFF_DQ + DIFF_LANES
OFF_DV = OFF_DK + DIFF_LANES
OFF_CB = OFF_DV + DIFF_LANES
OFF_CC = OFF_CB + CONV_DIM
OFF_CH = OFF_CC + CONV_DIM
N_PROJ = OFF_CH + CONV_DIM
GLR_LANE = GLA_DK
assert GLR_LANE + GLA_RANK <= HEAD_SLOT

FFN_ROWS = 512
FFN_F_CHUNK = 1408
PROJ_ROWS = 512
GLA_ROWS = 512
ATT_TILE = 256
ATT_UNROLL_FAR = 16
ATT_UNROLL_NEAR = 8
OUT_ROWS = 512


def _dot(a, b):
    return jnp.dot(a, b, preferred_element_type=F32)


def _dot_nt(a, b):
    return lax.dot_general(a, b, (((1,), (1,)), ((), ())), preferred_element_type=F32)


def _dot_tn(a, b):
    return lax.dot_general(a, b, (((0,), (0,)), ((), ())), preferred_element_type=F32)


def _split_bf16(x):
    hi = x.astype(BF16)
    lo = (x - hi.astype(F32)).astype(BF16)
    return hi, lo


def _rms(x, g):
    return x * lax.rsqrt(jnp.mean(x * x, axis=-1, keepdims=True) + EPS) * g


def _sigmoid(x):
    return 1.0 / (1.0 + jnp.exp(-x))


def _ffn_kernel(x_ref, g_ref, wg_ref, wu_ref, wd_ref, fg_ref, o_ref, *, final):
    x = x_ref[...]
    h = _rms(x, g_ref[...]).astype(BF16)
    acc = None
    for c in range(D_FF // FFN_F_CHUNK):
        sl = slice(c * FFN_F_CHUNK, (c + 1) * FFN_F_CHUNK)
        gate = _dot(h, wg_ref[:, sl])
        up = _dot(h, wu_ref[:, sl])
        a = (gate * _sigmoid(gate) * up).astype(BF16)
        part = _dot(a, wd_ref[sl, :])
        acc = part if acc is None else acc + part
    y = x + 0.5 * acc
    if final:
        y = _rms(y, fg_ref[...])
    o_ref[...] = y


def _const_spec(shape):
    nd = len(shape)
    return pl.BlockSpec(shape, lambda *_: (0,) * nd, pipeline_mode=pl.Buffered(1))


def _ffn(x, g, wg, wu, wd, final_g, final):
    t = x.shape[0]
    tm = min(FFN_ROWS, t)
    return pl.pallas_call(
        functools.partial(_ffn_kernel, final=final),
        out_shape=jax.ShapeDtypeStruct((t, D_MODEL), F32),
        grid=(t // tm,),
        in_specs=[
            pl.BlockSpec((tm, D_MODEL), lambda i: (i, 0)),
            _const_spec((1, D_MODEL)),
            _const_spec((D_MODEL, D_FF)),
            _const_spec((D_MODEL, D_FF)),
            _const_spec((D_FF, D_MODEL)),
            _const_spec((1, D_MODEL)),
        ],
        out_specs=pl.BlockSpec((tm, D_MODEL), lambda i: (i, 0)),
        compiler_params=pltpu.CompilerParams(
            dimension_semantics=("parallel",), vmem_limit_bytes=VMEM_LIMIT_BYTES),
        name="ffn",
    )(x, g, wg, wu, wd, final_g)


def _proj_kernel(x_ref, g_ref, w_ref, gkup_ref, gkb_ref, cw_ref,
                 gq_ref, gk_ref, gg_ref, gv_ref, gate_ref, dq_ref, dk_ref, dv_ref, conv_ref,
                 zbuf_ref, *, tiles_per_seq):
    tm = x_ref.shape[0]
    h = _rms(x_ref[...], g_ref[...]).astype(BF16)

    def proj(off, width):
        return _dot(h, w_ref[:, off:off + width])

    qk = proj(OFF_GQ, 2 * GLA_QK_LANES)
    gq = qk[:, 0:GLA_QK_LANES]
    gq_ref[...] = gq
    gk_ref[...] = qk[:, GLA_QK_LANES:]
    vg = proj(OFF_GV, 2 * GLA_WIDTH)
    gv_ref[...] = vg[:, 0:GLA_WIDTH].astype(BF16)
    gate_ref[...] = vg[:, GLA_WIDTH:]

    glr_hi, glr_lo = _split_bf16(gq)
    up_hi = gkup_ref[0]
    up_lo = gkup_ref[1]
    z = _dot(glr_hi, up_hi) + _dot(glr_lo, up_hi) + _dot(glr_hi, up_lo) + gkb_ref[...]
    logsig = jnp.minimum(z, 0.0) - jnp.log1p(jnp.exp(-jnp.abs(z)))
    lane = lax.broadcasted_iota(jnp.int32, (tm, GLA_QK_LANES), 1)
    gg_ref[...] = jnp.where(lane % HEAD_SLOT < GLA_DK, logsig * (1.0 / GLA_GATE_NORM), 0.0)

    dqk = proj(OFF_DQ, 2 * DIFF_LANES)
    dq_ref[...] = (dqk[:, 0:DIFF_LANES] * (DIFF_DQK ** -0.5 * LOG2E)).astype(BF16)
    dk_ref[...] = dqk[:, DIFF_LANES:].astype(BF16)
    lane_v = lax.broadcasted_iota(jnp.int32, (tm, DIFF_LANES), 1)
    dv_ref[...] = jnp.where(lane_v % LANES == DIFF_DV, 1.0, proj(OFF_DV, DIFF_LANES)).astype(BF16)

    @pl.when(pl.program_id(0) % tiles_per_seq == 0)
    def _():
        zbuf_ref[0:8, :] = jnp.zeros((8, CONV_DIM), F32)

    conv = proj(OFF_CB, 3 * CONV_DIM)
    zc = conv[:, CONV_DIM:2 * CONV_DIM] * conv[:, 2 * CONV_DIM:]
    zbuf_ref[8:8 + tm, :] = zc
    cw = cw_ref[...]
    y = (cw[2:3, :] * zc + cw[1:2, :] * zbuf_ref[7:7 + tm, :] + cw[0:1, :] * zbuf_ref[6:6 + tm, :])
    conv_ref[...] = (conv[:, 0:CONV_DIM] * y).astype(BF16)
    zbuf_ref[0:8, :] = zbuf_ref[tm:tm + 8, :]


def _proj(x, g, w, gkup, gkb, cw, seq):
    t = x.shape[0]
    tm = min(PROJ_ROWS, seq)
    row = lambda width: pl.BlockSpec((tm, width), lambda i: (i, 0))
    out_shapes = (
        jax.ShapeDtypeStruct((t, GLA_QK_LANES), F32),
        jax.ShapeDtypeStruct((t, GLA_QK_LANES), F32),
        jax.ShapeDtypeStruct((t, GLA_QK_LANES), F32),
        jax.ShapeDtypeStruct((t, GLA_WIDTH), BF16),
        jax.ShapeDtypeStruct((t, GLA_WIDTH), F32),
        jax.ShapeDtypeStruct((t, DIFF_LANES), BF16),
        jax.ShapeDtypeStruct((t, DIFF_LANES), BF16),
        jax.ShapeDtypeStruct((t, DIFF_LANES), BF16),
        jax.ShapeDtypeStruct((t, CONV_DIM), BF16),
    )
    return pl.pallas_call(
        functools.partial(_proj_kernel, tiles_per_seq=seq // tm),
        out_shape=out_shapes,
        grid=(t // tm,),
        in_specs=[
            row(D_MODEL),
            _const_spec((1, D_MODEL)),
            _const_spec((D_MODEL, N_PROJ)),
            _const_spec((2, GLA_QK_LANES, GLA_QK_LANES)),
            _const_spec((1, GLA_QK_LANES)),
            _const_spec((8, CONV_DIM)),
        ],
        out_specs=tuple(row(s.shape[1]) for s in out_shapes),
        scratch_shapes=[pltpu.VMEM((tm + 8, CONV_DIM), F32)],
        compiler_params=pltpu.CompilerParams(
            dimension_semantics=("arbitrary",), vmem_limit_bytes=VMEM_LIMIT_BYTES),
        name="mixer_proj",
    )(x, g, w, gkup, gkb, cw)


def _gla_cum_matrix():
    t = np.arange(CHUNK)[:, None]
    u = np.arange(CHUNK)[None, :]
    tri = (u <= t)
    ref = (u <= (t // SUB) * SUB - 1)
    ones = np.ones((CHUNK, CHUNK), bool)
    parts = [tri, ref, ones]
    for j in range(1, N_SUB):
        s = np.arange(SUB * j)[:, None]
        parts.append((u > s) & (u <= SUB * j - 1))
    return np.concatenate(parts, axis=0).astype(np.float32)


def _gla_score_mask():
    t = np.arange(CHUNK)[:, None]
    m = np.arange(KST_ROWS)
    group = np.concatenate([np.full(SUB * j, j) for j in range(1, N_SUB)])
    keep = (group[None, :] == (t // SUB))
    return np.tile(keep, (1, GLA_HEADS)).astype(np.float32)


def _gla_kernel(q_ref, k_ref, g_ref, v_ref, gate_ref, cum_ref, cumf_ref, amask_ref, gn_ref,
                o_ref, s_ref, raw_ref):
    rows = q_ref.shape[0]
    n_chunks = rows // CHUNK

    @pl.when(pl.program_id(1) == 0)
    def _():
        s_ref[...] = jnp.zeros_like(s_ref)

    qk_head = lax.broadcasted_iota(jnp.int32, (1, GLA_QK_LANES), 1) // HEAD_SLOT
    v_head = lax.broadcasted_iota(jnp.int32, (1, GLA_WIDTH), 1) // GLA_DV
    row_head_v = lax.broadcasted_iota(jnp.int32, (GLA_WIDTH, 1), 0) // GLA_DV
    state_mask = row_head_v == qk_head
    scale = GLA_DK ** -0.5

    safe = jnp.min(g_ref[...]) * CHUNK >= -GLA_SAFE_DECAY

    def fast_path():
        chunk = lambda x, c: x[c * CHUNK:(c + 1) * CHUNK]
        sel_r = lax.broadcasted_iota(jnp.int32, (8, rows), 0)
        sel_c = lax.broadcasted_iota(jnp.int32, (8, rows), 1) // CHUNK
        chunk_sel = jnp.where(sel_r == sel_c, 1.0, 0.0).astype(BF16)
        gs_hi, gs_lo = _split_bf16(g_ref[...])
        totals = _dot(chunk_sel, gs_hi) + _dot(chunk_sel, gs_lo)
        tri = cumf_ref[...]
        b = jnp.concatenate([_dot(tri, chunk(gs_hi, c)) + _dot(tri, chunk(gs_lo, c))
                             for c in range(n_chunks)], axis=0)
        b_last = jnp.concatenate([jnp.broadcast_to(totals[c:c + 1], (CHUNK, GLA_QK_LANES))
                                  for c in range(n_chunks)], axis=0)
        q = q_ref[...] * scale
        k = k_ref[...]
        v = v_ref[...]
        qd = (q * jnp.exp(b)).astype(BF16)
        kd = (k * jnp.exp(-b)).astype(BF16)
        kw = (k * jnp.exp(b_last - b)).astype(BF16)
        t_idx = lax.broadcasted_iota(jnp.int32, (CHUNK, GLA_QK_LANES), 0)
        s_idx = lax.broadcasted_iota(jnp.int32, (CHUNK, GLA_QK_LANES), 1) % HEAD_SLOT
        causal = s_idx <= t_idx
        intra, update = [], []
        for c in range(n_chunks):
            kd_c, v_c = chunk(kd, c), chunk(v, c)
            kd_bd = jnp.concatenate(
                [jnp.where(qk_head == hd, kd_c, jnp.zeros_like(kd_c)) for hd in range(GLA_HEADS)],
                axis=0)
            a = _dot_nt(chunk(qd, c), kd_bd)
            a = jnp.where(causal, a, 0.0).astype(BF16)
            v_bd = jnp.concatenate(
                [jnp.where(v_head == hd, v_c, jnp.zeros_like(v_c)) for hd in range(GLA_HEADS)],
                axis=0)
            intra.append(_dot(a, v_bd))
            update.append(_dot_tn(v_c, chunk(kw, c)))
        state = s_ref[...]
        for c in range(n_chunks):
            raw_ref[c * CHUNK:(c + 1) * CHUNK, :] = (
                intra[c] + _dot_nt(chunk(qd, c), state.astype(BF16)))
            state = state * jnp.exp(totals[c:c + 1]) + jnp.where(state_mask, update[c], 0.0)
        s_ref[...] = state

    @pl.when(safe)
    def _():
        fast_path()

    def general_path():
        cum = cum_ref[...]
        amask = amask_ref[...] > 0.5
        ind_r = lax.broadcasted_iota(jnp.int32, (GLA_QK_LANES, LANES), 0) // HEAD_SLOT
        ind_c = lax.broadcasted_iota(jnp.int32, (GLA_QK_LANES, LANES), 1)
        ind_sum = jnp.where(ind_r == ind_c, 1.0, 0.0).astype(BF16)
        bc_r = lax.broadcasted_iota(jnp.int32, (LANES, GLA_WIDTH), 0)
        bc_c = lax.broadcasted_iota(jnp.int32, (LANES, GLA_WIDTH), 1) // GLA_DV
        ind_bcast = jnp.where(bc_r == bc_c, 1.0, 0.0).astype(BF16)
        row_in_sub = lax.broadcasted_iota(jnp.int32, (CHUNK, 1), 0) % SUB

        def chunk_body(c, carry):
            r0 = pl.multiple_of(c * CHUNK, CHUNK)
            q = q_ref[pl.ds(r0, CHUNK), :] * scale
            k = k_ref[pl.ds(r0, CHUNK), :]
            g = g_ref[pl.ds(r0, CHUNK), :]
            v = v_ref[pl.ds(r0, CHUNK), :]
            g_hi, g_lo = _split_bf16(g)
            cums = _dot(cum, g_hi) + _dot(cum, g_lo)
            b = cums[0:CHUNK]
            b_ref = cums[CHUNK:2 * CHUNK]
            b_last = cums[2 * CHUNK:3 * CHUNK]
            e_kst = cums[3 * CHUNK:3 * CHUNK + KST_ROWS]

            state = s_ref[...]
            q_in = (q * jnp.exp(b)).astype(BF16)
            o = _dot_nt(q_in, state.astype(BF16))

            kw = (k * jnp.exp(b_last - b)).astype(BF16)
            decay = jnp.exp(jnp.concatenate([b_last] * (GLA_WIDTH // CHUNK), axis=0))
            s_ref[...] = state * decay + jnp.where(state_mask, _dot_tn(v, kw), 0.0)

            q_sub = (q * jnp.exp(jnp.minimum(b - b_ref, 0.0))).astype(BF16)
            k_st = jnp.concatenate([k[0:SUB * j] for j in range(1, N_SUB)], axis=0) * jnp.exp(e_kst)
            k_bd = jnp.concatenate(
                [jnp.where(qk_head == hd, k_st, 0.0) for hd in range(GLA_HEADS)], axis=0).astype(BF16)
            a = _dot_nt(q_sub, k_bd)
            a = jnp.where(amask, a, 0.0).astype(BF16)
            v_st = jnp.concatenate([v[0:SUB * j] for j in range(1, N_SUB)], axis=0)
            v_bd = jnp.concatenate(
                [jnp.where(v_head == hd, v_st, jnp.zeros_like(v_st)) for hd in range(GLA_HEADS)], axis=0)
            o = o + _dot(a, v_bd)

            vf = v.astype(F32)
            prods = []
            for d in range(SUB):
                ks = k if d == 0 else pltpu.roll(k, d, 0)
                bs = b if d == 0 else pltpu.roll(b, d, 0)
                prods.append((q * ks * jnp.exp(jnp.minimum(b - bs, 0.0))).astype(BF16))
            dsum = _dot(jnp.concatenate(prods, axis=0), ind_sum)
            valid = jnp.concatenate([row_in_sub >= d for d in range(SUB)], axis=0)
            dsum = jnp.where(valid, dsum, 0.0).astype(BF16)
            dbc = _dot(dsum, ind_bcast)
            for d in range(SUB):
                vs = vf if d == 0 else pltpu.roll(vf, d, 0)
                o = o + dbc[d * CHUNK:(d + 1) * CHUNK] * vs
            raw_ref[pl.ds(r0, CHUNK), :] = o
            return carry

        lax.fori_loop(0, n_chunks, chunk_body, 0)

    @pl.when(jnp.logical_not(safe))
    def _():
        general_path()

    o = raw_ref[...]
    hr = lax.broadcasted_iota(jnp.int32, (GLA_WIDTH, GLA_WIDTH), 0) // GLA_DV
    hc = lax.broadcasted_iota(jnp.int32, (GLA_WIDTH, GLA_WIDTH), 1) // GLA_DV
    head_ones = jnp.where(hr == hc, 1.0, 0.0).astype(BF16)
    ms = _dot((o * o).astype(BF16), head_ones) * (1.0 / GLA_DV)
    gate = gate_ref[...]
    y = o * lax.rsqrt(ms + EPS) * gn_ref[...] * (gate * _sigmoid(gate))
    o_ref[...] = y.astype(BF16)


def _gla(gq, gk, gg, gv, gate, gn_tiled, batch, seq):
    rows = min(GLA_ROWS, seq)
    steps = seq // rows
    row = lambda width: pl.BlockSpec((rows, width), lambda b, i: (b * steps + i, 0))
    assert rows // CHUNK <= 8
    cum_np = _gla_cum_matrix()
    cum = jnp.asarray(cum_np, BF16)
    cumf = jnp.asarray(cum_np[0:CHUNK], BF16)
    amask = jnp.asarray(_gla_score_mask(), F32)
    return pl.pallas_call(
        _gla_kernel,
        out_shape=jax.ShapeDtypeStruct((batch * seq, GLA_WIDTH), BF16),
        grid=(batch, steps),
        in_specs=[
            row(GLA_QK_LANES), row(GLA_QK_LANES), row(GLA_QK_LANES), row(GLA_WIDTH), row(GLA_WIDTH),
            pl.BlockSpec(cum.shape, lambda b, i: (0, 0)),
            pl.BlockSpec(cumf.shape, lambda b, i: (0, 0)),
            pl.BlockSpec(amask.shape, lambda b, i: (0, 0)),
            pl.BlockSpec((1, GLA_WIDTH), lambda b, i: (0, 0)),
        ],
        out_specs=row(GLA_WIDTH),
        scratch_shapes=[pltpu.VMEM((GLA_WIDTH, GLA_QK_LANES), F32),
                        pltpu.VMEM((rows, GLA_WIDTH), F32)],
        compiler_params=pltpu.CompilerParams(
            dimension_semantics=("arbitrary", "arbitrary"), vmem_limit_bytes=VMEM_LIMIT_BYTES),
        name="gla",
    )(gq, gk, gg, gv, gate, cum, cumf, amask, gn_tiled)


def _t5_bucket(rel):
    nb = NUM_BUCKETS // 2
    max_exact = nb // 2
    ret = (rel > 0).astype(jnp.int32) * nb
    n = jnp.abs(rel)
    nf = jnp.maximum(n, 1).astype(jnp.float32)
    large = max_exact + (jnp.log(nf / max_exact) / math.log(MAX_DISTANCE / max_exact)
                         * (nb - max_exact)).astype(jnp.int32)
    large = jnp.minimum(large, nb - 1)
    return ret + jnp.where(n < max_exact, n, large)


def _bias_tiles(rel_bias, tile):
    assert tile >= MAX_DISTANCE and tile % CHUNK == 0
    table = rel_bias.astype(F32)
    far = table[NUM_BUCKETS // 2 - 1]
    heads = table.shape[1]
    rel = jnp.concatenate([jnp.arange(0, tile), jnp.arange(-2 * tile, 0)])
    onehot = _t5_bucket(rel)[None, :, None] == jnp.arange(NUM_BUCKETS)[None, None, :]
    per_rel = jnp.sum(jnp.where(onehot, table.T[:, None, :], 0.0), axis=-1)
    per_rel = (per_rel - far[:, None]) * LOG2E
    span = 3 * tile
    flat = jnp.tile(per_rel, (1, 2 * tile))[:, :2 * tile * (span - 1)]
    toep = flat.reshape(heads, 2 * tile, span - 1)[:, :, :tile]
    r = jnp.arange(tile)[:, None]
    c = jnp.arange(tile)[None, :]
    diag = jnp.where(((c // CHUNK) <= (r // CHUNK))[None], toep[:, :tile], MASK_VALUE)
    left = toep[:, tile:]
    return jnp.stack([left, diag], axis=1)


def _attn_kernel(q_ref, k_ref, v_ref, bias_ref, lamv_ref, subln_ref, o_ref,
                 s_buf, p_buf, alpha_buf, m_all, acc_all, *, tile, lam_init):
    seq = q_ref.shape[0]
    nq = seq // tile

    s_buf[...] = jnp.zeros(s_buf.shape, F32)
    p_buf[...] = jnp.zeros(p_buf.shape, BF16)
    alpha_buf[...] = jnp.zeros(alpha_buf.shape, F32)
    m_all[...] = jnp.full(m_all.shape, MASK_VALUE, F32)
    acc_all[...] = jnp.zeros(acc_all.shape, F32)

    lane = lax.broadcasted_iota(jnp.int32, (tile, LANES), 1)

    def stage1(i, j, slot):
        q = q_ref[pl.ds(pl.multiple_of(i * tile, tile), tile), :]
        zero = jnp.zeros_like(q)
        qs = jnp.concatenate([jnp.where(lane < LANES // 2, q, zero),
                              jnp.where(lane >= LANES // 2, q, zero)], axis=0)
        kj = k_ref[pl.ds(pl.multiple_of(j * tile, tile), tile), :]
        s_buf[slot] = _dot_nt(qs, kj)

    def stage2(i, bias, slot):
        for half in range(2):
            rows = slice(half * tile, (half + 1) * tile)
            s = s_buf[slot, rows, :]
            if bias is not None:
                s = s + bias
            m_prev = m_all[i, rows, :]
            m_next = jnp.maximum(m_prev, jnp.max(s, axis=-1, keepdims=True))
            p = jnp.exp2(s - jnp.concatenate([m_next] * (tile // LANES), axis=1))
            p_buf[slot, rows, :] = p.astype(BF16)
            alpha_buf[slot, rows, :] = jnp.exp2(m_prev - m_next)
            m_all[i, rows, :] = m_next

    def stage3(i, j, slot):
        vj = v_ref[pl.ds(pl.multiple_of(j * tile, tile), tile), :]
        acc_all[i] = alpha_buf[slot] * acc_all[i] + _dot(p_buf[slot], vj)

    def run_pipeline(n_pairs, first_pair, advance, with_bias, unroll):
        n_steps = pl.cdiv(n_pairs + 2, unroll) * unroll

        def step(t, pairs, slot_a, slot_b):
            (i0, j0), (i1, j1), (i2, j2) = pairs
            valid1 = jnp.logical_and(t >= 1, t <= n_pairs)
            valid2 = jnp.logical_and(t >= 2, t <= n_pairs + 1)
            stage3(jnp.where(valid2, i2, nq), j2, slot_a)
            stage2(jnp.where(valid1, i1, nq), bias_ref[0, j1 - i1 + 1] if with_bias else None, slot_b)
            stage1(i0, j0, slot_a)
            return (advance(i0, j0), (i0, j0), (i1, j1))

        def body(u, pairs):
            for r in range(unroll):
                pairs = step(unroll * u + r, pairs, r % 2, (r + 1) % 2)
            return pairs

        lax.fori_loop(0, n_steps // unroll, body, (first_pair,) * 3)

    def next_far(i, j):
        wrap = j == i - 2
        done = jnp.logical_and(wrap, i == nq - 1)
        step_i = jnp.logical_and(wrap, jnp.logical_not(done))
        return (jnp.where(step_i, i + 1, i), jnp.where(done, j, jnp.where(wrap, 0, j + 1)))

    def next_near(i, j):
        wrap = j == i
        done = jnp.logical_and(wrap, i == nq - 1)
        step_i = jnp.logical_and(wrap, jnp.logical_not(done))
        return (jnp.where(step_i, i + 1, i), jnp.where(jnp.logical_or(done, wrap), j, j + 1))

    zero = jnp.int32(0)
    if nq > 2:
        run_pipeline((nq - 1) * (nq - 2) // 2, (jnp.int32(2), zero), next_far, False, ATT_UNROLL_FAR)
    run_pipeline(2 * nq - 1, (zero, zero), next_near, True, ATT_UNROLL_NEAR)

    def finish(i, carry):
        acc = acc_all[i]
        a1 = acc[0:tile]
        a2 = acc[tile:2 * tile]
        l1 = a1[:, DIFF_DV:DIFF_DV + 1]
        l2 = a2[:, DIFF_DV:DIFF_DV + 1]
        lv = lamv_ref[...]
        lam = (jnp.exp(jnp.sum(lv[0:1] * lv[1:2], axis=-1, keepdims=True))
               - jnp.exp(jnp.sum(lv[2:3] * lv[3:4], axis=-1, keepdims=True)) + lam_init)
        o = a1 / l1 - lam * (a2 / l2)
        o = jnp.where(lane < DIFF_DV, o, 0.0)
        ms = jnp.sum(o * o, axis=-1, keepdims=True) * (1.0 / DIFF_DV)
        y = o * lax.rsqrt(ms + EPS) * subln_ref[...] * (1.0 - lam_init)
        o_ref[pl.ds(pl.multiple_of(i * tile, tile), tile), :] = y.astype(BF16)
        return carry

    lax.fori_loop(0, nq, finish, 0)


def _attn(dq, dk, dv, bias_tiles, lamv, subln, batch, seq, lam_init):
    tile = min(ATT_TILE, seq)
    nq = seq // tile
    head_block = pl.BlockSpec((seq, LANES), lambda b, h: (b, h))
    return pl.pallas_call(
        functools.partial(_attn_kernel, tile=tile, lam_init=lam_init),
        out_shape=jax.ShapeDtypeStruct((batch * seq, DIFF_LANES), BF16),
        grid=(batch, DIFF_HEADS),
        in_specs=[
            head_block, head_block, head_block,
            pl.BlockSpec((1, 2, tile, tile), lambda b, h: (h, 0, 0, 0)),
            pl.BlockSpec((4, LANES), lambda b, h: (0, 0)),
            pl.BlockSpec((1, LANES), lambda b, h: (0, 0)),
        ],
        out_specs=head_block,
        scratch_shapes=[pltpu.VMEM((2, 2 * tile, tile), F32),
                        pltpu.VMEM((2, 2 * tile, tile), BF16),
                        pltpu.VMEM((2, 2 * tile, LANES), F32),
                        pltpu.VMEM((nq + 1, 2 * tile, LANES), F32),
                        pltpu.VMEM((nq + 1, 2 * tile, LANES), F32)],
        compiler_params=pltpu.CompilerParams(
            dimension_semantics=("parallel", "parallel"), vmem_limit_bytes=VMEM_LIMIT_BYTES),
        name="diff_attn",
    )(dq, dk, dv, bias_tiles, lamv, subln)


def _out_kernel(x_ref, gla_ref, diff_ref, conv_ref, w1_ref, w2_ref, w3_ref, o_ref):
    o_ref[...] = (x_ref[...] + _dot(gla_ref[...], w1_ref[...]) + _dot(diff_ref[...], w2_ref[...])
                  + _dot(conv_ref[...], w3_ref[...]))


def _out(x, gla_o, diff_o, conv_o, w1, w2, w3):
    t = x.shape[0]
    tm = min(OUT_ROWS, t)
    row = lambda width: pl.BlockSpec((tm, width), lambda i: (i, 0))
    return pl.pallas_call(
        _out_kernel,
        out_shape=jax.ShapeDtypeStruct((t, D_MODEL), F32),
        grid=(t // tm,),
        in_specs=[row(D_MODEL), row(GLA_WIDTH), row(DIFF_LANES), row(CONV_DIM),
                  _const_spec((GLA_WIDTH, D_MODEL)), _const_spec((DIFF_LANES, D_MODEL)),
                  _const_spec((CONV_DIM, D_MODEL))],
        out_specs=row(D_MODEL),
        compiler_params=pltpu.CompilerParams(
            dimension_semantics=("parallel",), vmem_limit_bytes=VMEM_LIMIT_BYTES),
        name="mixer_out",
    )(x, gla_o, diff_o, conv_o, w1, w2, w3)


def _pad_cols(w, width):
    return jnp.pad(w, ((0, 0), (0, width - w.shape[1])))


def _slot_cols(w, heads, used, slot):
    lead = w.shape[0]
    w = w.reshape(lead, heads, used)
    return jnp.pad(w, ((0, 0), (0, 0), (0, slot - used))).reshape(lead, heads * slot)


def _layout_w_in(w_in):
    offs = np.cumsum([0, GLA_HEADS * GLA_DK, GLA_HEADS * GLA_DK, GLA_WIDTH, GLA_RANK, GLA_WIDTH,
                      DIFF_HEADS * 2 * DIFF_DQK, DIFF_HEADS * 2 * DIFF_DQK, DIFF_WIDTH,
                      CONV_DIM, CONV_DIM, CONV_DIM])
    gq, gk, gv, glr, gate, dq, dk, dv, cb, cc, ch = [w_in[:, offs[n]:offs[n + 1]] for n in range(11)]
    half = LANES // 2
    gq_slots = _slot_cols(gq, GLA_HEADS, GLA_DK, HEAD_SLOT)
    gq_slots = gq_slots.at[:, GLR_LANE:GLR_LANE + GLA_RANK].set(glr)
    cols = [
        gq_slots,
        _slot_cols(gk, GLA_HEADS, GLA_DK, HEAD_SLOT),
        gv,
        gate,
        _slot_cols(dq, 2 * DIFF_HEADS, DIFF_DQK, half),
        _slot_cols(dk, 2 * DIFF_HEADS, DIFF_DQK, half),
        _slot_cols(dv, DIFF_HEADS, DIFF_DV, LANES),
        cb, cc, ch,
    ]
    w = jnp.concatenate(cols, axis=1)
    assert w.shape[1] == N_PROJ
    return w.astype(BF16)


def kernel(x, ffn1_norm, ffn1_gate, ffn1_up, ffn1_down, mix_norm, w_in, gla_gk_up, gla_gk_bias,
           gla_norm, diff_lambda_q1, diff_lambda_k1, diff_lambda_q2, diff_lambda_k2, diff_subln,
           rel_bias, conv_w, w_out, ffn2_norm, ffn2_gate, ffn2_up, ffn2_down, final_norm):
    batch, seq, _ = x.shape
    depth = w_in.shape[0]
    t = batch * seq
    xf = x.reshape(t, D_MODEL)
    bias_tiles = _bias_tiles(rel_bias, min(ATT_TILE, seq))
    final_g = final_norm.reshape(1, D_MODEL)

    for l in range(depth):
        lam_init = 0.8 - 0.6 * math.exp(-0.3 * l)
        xf = _ffn(xf, ffn1_norm[l].reshape(1, D_MODEL), ffn1_gate[l].astype(BF16),
                  ffn1_up[l].astype(BF16), ffn1_down[l].astype(BF16), final_g, False)

        w_pad = _layout_w_in(w_in[l])
        up = _slot_cols(gla_gk_up[l], GLA_HEADS, GLA_DK, HEAD_SLOT)
        up = jnp.pad(up, ((GLR_LANE, GLA_QK_LANES - GLR_LANE - GLA_RANK), (0, 0)))
        up_hi = up.astype(BF16)
        up_lo = (up - up_hi.astype(F32)).astype(BF16)
        gkup = jnp.stack([up_hi, up_lo], axis=0)
        gkb = _slot_cols(gla_gk_bias[l].reshape(1, -1), GLA_HEADS, GLA_DK, HEAD_SLOT)
        cw = jnp.pad(conv_w[l], ((0, 8 - CONV_WIDTH), (0, 0)))
        gq, gk, gg, gv, gate, dq, dk, dv, conv_o = _proj(
            xf, mix_norm[l].reshape(1, D_MODEL), w_pad, gkup, gkb, cw, seq)

        gn_tiled = jnp.tile(gla_norm[l], GLA_HEADS).reshape(1, GLA_WIDTH)
        gla_o = _gla(gq, gk, gg, gv, gate, gn_tiled, batch, seq)

        lamv = jnp.stack([diff_lambda_q1[l], diff_lambda_k1[l], diff_lambda_q2[l], diff_lambda_k2[l]])
        lamv = _pad_cols(lamv.astype(F32), LANES)
        subln = _pad_cols(diff_subln[l].reshape(1, DIFF_DV), LANES)
        diff_o = _attn(dq, dk, dv, bias_tiles, lamv, subln, batch, seq, lam_init)

        wo = w_out[l]
        w1 = wo[0:GLA_WIDTH].astype(BF16)
        w2 = wo[GLA_WIDTH:GLA_WIDTH + DIFF_WIDTH].reshape(DIFF_HEADS, DIFF_DV, D_MODEL)
        w2 = jnp.pad(w2, ((0, 0), (0, LANES - DIFF_DV), (0, 0))).reshape(DIFF_LANES, D_MODEL).astype(BF16)
        w3 = wo[GLA_WIDTH + DIFF_WIDTH:].astype(BF16)
        xf = _out(xf, gla_o, diff_o, conv_o, w1, w2, w3)

        xf = _ffn(xf, ffn2_norm[l].reshape(1, D_MODEL), ffn2_gate[l].astype(BF16),
                  ffn2_up[l].astype(BF16), ffn2_down[l].astype(BF16), final_g, l == depth - 1)

    return xf.reshape(batch, seq, D_MODEL)
```

```python
import functools
import math

import numpy as np
import jax
import jax.numpy as jnp
from jax import lax
from jax.experimental import pallas as pl
from jax.experimental.pallas import tpu as pltpu

F32 = jnp.float32
BF16 = jnp.bfloat16

D_MODEL = 1024
D_FF = 2816
EPS = 1e-6
CHUNK = 64
GLA_HEADS = 4
GLA_DK = 48
GLA_DV = 96
GLA_RANK = 16
GLA_GATE_NORM = 16.0
DIFF_HEADS = 4
DIFF_DQK = 48
DIFF_DV = 96
CONV_DIM = 256
CONV_WIDTH = 3
NUM_BUCKETS = 32
MAX_DISTANCE = 128
GLA_WIDTH = GLA_HEADS * GLA_DV
DIFF_WIDTH = DIFF_HEADS * DIFF_DV

LANES = 128
BF16_SUBLANES = 16
VMEM_LIMIT_BYTES = 56 * 1024 * 1024

HEAD_SLOT = 64
GLA_QK_LANES = GLA_HEADS * HEAD_SLOT
SUB = 16
N_SUB = CHUNK // SUB
KST_ROWS = SUB * (N_SUB * (N_SUB - 1) // 2)
DIFF_LANES = DIFF_HEADS * LANES
MASK_VALUE = -1e30
GLA_SAFE_DECAY = 60.0
LOG2E = math.log2(math.e)

OFF_GQ = 0
OFF_GK = OFF_GQ + GLA_QK_LANES
OFF_GV = OFF_GK + GLA_QK_LANES
OFF_GATE = OFF_GV + GLA_WIDTH
OFF_DQ = OFF_GATE + GLA_WIDTH
OFF_DK = OFF_DQ + DIFF_LANES
OFF_DV = OFF_DK + DIFF_LANES
OFF_CB = OFF_DV + DIFF_LANES
OFF_CC = OFF_CB + CONV_DIM
OFF_CH = OFF_CC + CONV_DIM
N_PROJ = OFF_CH + CONV_DIM
GLR_LANE = GLA_DK
assert GLR_LANE + GLA_RANK <= HEAD_SLOT

FFN_ROWS = 512
MXU_DEPTH = 256
FFN_F_SPLITS = (0, 6 * MXU_DEPTH, D_FF)
assert all((hi - lo) % MXU_DEPTH == 0 for lo, hi in zip(FFN_F_SPLITS[:-1], FFN_F_SPLITS[1:]))
PROJ_ROWS = 512
GLA_ROWS = 512
ATT_TILE = 256
ATT_UNROLL_FAR = 16
ATT_UNROLL_NEAR = 8


def _dot(a, b):
    return jnp.dot(a, b, preferred_element_type=F32)


def _dot_nt(a, b):
    return lax.dot_general(a, b, (((1,), (1,)), ((), ())), preferred_element_type=F32)


def _dot_tn(a, b):
    return lax.dot_general(a, b, (((0,), (0,)), ((), ())), preferred_element_type=F32)


def _split_bf16(x):
    hi = x.astype(BF16)
    lo = (x - hi.astype(F32)).astype(BF16)
    return hi, lo


def _rms(x, g):
    return x * lax.rsqrt(jnp.mean(x * x, axis=-1, keepdims=True) + EPS) * g


def _sigmoid(x):
    return 1.0 / (1.0 + jnp.exp(-x))


def _ffn_kernel(*refs, final, mix, cast):
    refs = list(refs)
    x_ref, g_ref, wg_ref, wu_ref, wd_ref, fg_ref = refs[:6]
    mix_refs = refs[6:12] if mix else ()
    n_in = 6 + len(mix_refs)
    src_refs = refs[n_in:n_in + 3] if cast else ()
    o_ref = refs[n_in + len(src_refs)]
    dst_refs = refs[n_in + len(src_refs) + 1:]
    for src, dst in zip(src_refs, dst_refs):
        dst[...] = src[...].astype(BF16)

    x = x_ref[...]
    if mix:
        gla_ref, diff_ref, conv_ref, w1_ref, w2_ref, w3_ref = mix_refs
        x = (x + _dot(gla_ref[...], w1_ref[...]) + _dot(diff_ref[...], w2_ref[...])
             + _dot(conv_ref[...], w3_ref[...]))
    h = _rms(x, g_ref[...]).astype(BF16)
    acc = None
    for lo, hi in zip(FFN_F_SPLITS[:-1], FFN_F_SPLITS[1:]):
        sl = slice(lo, hi)
        gate = _dot(h, wg_ref[:, sl])
        up = _dot(h, wu_ref[:, sl])
        a = (gate * _sigmoid(gate) * up).astype(BF16)
        part = _dot(a, wd_ref[sl, :])
        acc = part if acc is None else acc + part
    y = x + 0.5 * acc
    if final:
        y = _rms(y, fg_ref[...])
    o_ref[...] = y


def _const_spec(shape):
    nd = len(shape)
    return pl.BlockSpec(shape, lambda *_: (0,) * nd, pipeline_mode=pl.Buffered(1))


def _cast_block(n_rows, steps):
    share = 1
    while (n_rows * share) % (steps * BF16_SUBLANES) or steps % share:
        share *= 2
        assert share <= steps
    return n_rows * share // steps, share


def _ffn(x, g, wg, wu, wd, final_g, final, mix=None, cast=None):
    t = x.shape[0]
    tm = min(FFN_ROWS, t)
    steps = t // tm
    row = lambda width: pl.BlockSpec((tm, width), lambda i: (i, 0))
    operands = [x, g, wg, wu, wd, final_g]
    in_specs = [row(D_MODEL), _const_spec((1, D_MODEL)), _const_spec((D_MODEL, D_FF)),
                _const_spec((D_MODEL, D_FF)), _const_spec((D_FF, D_MODEL)), _const_spec((1, D_MODEL))]
    out_shapes = [jax.ShapeDtypeStruct((t, D_MODEL), F32)]
    out_specs = [row(D_MODEL)]
    if mix is not None:
        operands += list(mix)
        in_specs += [row(a.shape[1]) for a in mix[:3]] + [_const_spec(w.shape) for w in mix[3:]]
    if cast is not None:
        stacked, layer = cast
        for w in stacked:
            rows, share = _cast_block(w.shape[1], steps)
            operands.append(w)
            in_specs.append(pl.BlockSpec((None, rows, w.shape[2]),
                                         lambda i, share=share: (layer, i // share, 0)))
            out_shapes.append(jax.ShapeDtypeStruct(w.shape[1:], BF16))
            out_specs.append(pl.BlockSpec((rows, w.shape[2]), lambda i, share=share: (i // share, 0)))
    outs = pl.pallas_call(
        functools.partial(_ffn_kernel, final=final, mix=mix is not None, cast=cast is not None),
        out_shape=tuple(out_shapes),
        grid=(steps,),
        in_specs=in_specs,
        out_specs=tuple(out_specs),
        compiler_params=pltpu.CompilerParams(
            dimension_semantics=("arbitrary",), vmem_limit_bytes=VMEM_LIMIT_BYTES),
        name="ffn",
    )(*operands)
    return outs[0], tuple(outs[1:])


def _proj_kernel(x_ref, g_ref, w_ref, gkup_ref, gkb_ref, cw_ref,
                 gq_ref, gk_ref, gg_ref, gv_ref, gate_ref, dq_ref, dk_ref, dv_ref, conv_ref,
                 zbuf_ref, *, tiles_per_seq):
    tm = x_ref.shape[0]
    h = _rms(x_ref[...], g_ref[...]).astype(BF16)

    def proj(off, width):
        return _dot(h, w_ref[:, off:off + width])

    qk = proj(OFF_GQ, 2 * GLA_QK_LANES)
    gq = qk[:, 0:GLA_QK_LANES]
    gq_ref[...] = gq
    gk_ref[...] = qk[:, GLA_QK_LANES:]
    vg = proj(OFF_GV, 2 * GLA_WIDTH)
    gv_ref[...] = vg[:, 0:GLA_WIDTH].astype(BF16)
    gate_ref[...] = vg[:, GLA_WIDTH:]

    glr_hi, glr_lo = _split_bf16(gq)
    up_hi = gkup_ref[0]
    up_lo = gkup_ref[1]
    z = _dot(glr_hi, up_hi) + _dot(glr_lo, up_hi) + _dot(glr_hi, up_lo) + gkb_ref[...]
    logsig = jnp.minimum(z, 0.0) - jnp.log1p(jnp.exp(-jnp.abs(z)))
    lane = lax.broadcasted_iota(jnp.int32, (tm, GLA_QK_LANES), 1)
    gg_ref[...] = jnp.where(lane % HEAD_SLOT < GLA_DK, logsig * (1.0 / GLA_GATE_NORM), 0.0)

    dqk = proj(OFF_DQ, 2 * DIFF_LANES)
    dq_ref[...] = (dqk[:, 0:DIFF_LANES] * (DIFF_DQK ** -0.5 * LOG2E)).astype(BF16)
    dk_ref[...] = dqk[:, DIFF_LANES:].astype(BF16)
    lane_v = lax.broadcasted_iota(jnp.int32, (tm, DIFF_LANES), 1)
    dv_ref[...] = jnp.where(lane_v % LANES == DIFF_DV, 1.0, proj(OFF_DV, DIFF_LANES)).astype(BF16)

    @pl.when(pl.program_id(0) % tiles_per_seq == 0)
    def _():
        zbuf_ref[0:8, :] = jnp.zeros((8, CONV_DIM), F32)

    conv = proj(OFF_CB, 3 * CONV_DIM)
    zc = conv[:, CONV_DIM:2 * CONV_DIM] * conv[:, 2 * CONV_DIM:]
    zbuf_ref[8:8 + tm, :] = zc
    cw = cw_ref[...]
    y = (cw[2:3, :] * zc + cw[1:2, :] * zbuf_ref[7:7 + tm, :] + cw[0:1, :] * zbuf_ref[6:6 + tm, :])
    conv_ref[...] = (conv[:, 0:CONV_DIM] * y).astype(BF16)
    zbuf_ref[0:8, :] = zbuf_ref[tm:tm + 8, :]


def _proj(x, g, w, gkup, gkb, cw, seq):
    t = x.shape[0]
    tm = min(PROJ_ROWS, seq)
    row = lambda width: pl.BlockSpec((tm, width), lambda i: (i, 0))
    out_shapes = (
        jax.ShapeDtypeStruct((t, GLA_QK_LANES), F32),
        jax.ShapeDtypeStruct((t, GLA_QK_LANES), F32),
        jax.ShapeDtypeStruct((t, GLA_QK_LANES), F32),
        jax.ShapeDtypeStruct((t, GLA_WIDTH), BF16),
        jax.ShapeDtypeStruct((t, GLA_WIDTH), F32),
        jax.ShapeDtypeStruct((t, DIFF_LANES), BF16),
        jax.ShapeDtypeStruct((t, DIFF_LANES), BF16),
        jax.ShapeDtypeStruct((t, DIFF_LANES), BF16),
        jax.ShapeDtypeStruct((t, CONV_DIM), BF16),
    )
    return pl.pallas_call(
        functools.partial(_proj_kernel, tiles_per_seq=seq // tm),
        out_shape=out_shapes,
        grid=(t // tm,),
        in_specs=[
            row(D_MODEL),
            _const_spec((1, D_MODEL)),
            _const_spec((D_MODEL, N_PROJ)),
            _const_spec((2, GLA_QK_LANES, GLA_QK_LANES)),
            _const_spec((1, GLA_QK_LANES)),
            _const_spec((8, CONV_DIM)),
        ],
        out_specs=tuple(row(s.shape[1]) for s in out_shapes),
        scratch_shapes=[pltpu.VMEM((tm + 8, CONV_DIM), F32)],
        compiler_params=pltpu.CompilerParams(
            dimension_semantics=("arbitrary",), vmem_limit_bytes=VMEM_LIMIT_BYTES),
        name="mixer_proj",
    )(x, g, w, gkup, gkb, cw)


def _gla_cum_matrix():
    t = np.arange(CHUNK)[:, None]
    u = np.arange(CHUNK)[None, :]
    tri = (u <= t)
    ref = (u <= (t // SUB) * SUB - 1)
    ones = np.ones((CHUNK, CHUNK), bool)
    parts = [tri, ref, ones]
    for j in range(1, N_SUB):
        s = np.arange(SUB * j)[:, None]
        parts.append((u > s) & (u <= SUB * j - 1))
    return np.concatenate(parts, axis=0).astype(np.float32)


def _gla_score_mask():
    t = np.arange(CHUNK)[:, None]
    m = np.arange(KST_ROWS)
    group = np.concatenate([np.full(SUB * j, j) for j in range(1, N_SUB)])
    keep = (group[None, :] == (t // SUB))
    return np.tile(keep, (1, GLA_HEADS)).astype(np.float32)


def _gla_kernel(q_ref, k_ref, g_ref, v_ref, gate_ref, cum_ref, cumf_ref, amask_ref, gn_ref,
                o_ref, s_ref, raw_ref):
    rows = q_ref.shape[0]
    n_chunks = rows // CHUNK

    @pl.when(pl.program_id(1) == 0)
    def _():
        s_ref[...] = jnp.zeros_like(s_ref)

    qk_head = lax.broadcasted_iota(jnp.int32, (1, GLA_QK_LANES), 1) // HEAD_SLOT
    v_head = lax.broadcasted_iota(jnp.int32, (1, GLA_WIDTH), 1) // GLA_DV
    row_head_v = lax.broadcasted_iota(jnp.int32, (GLA_WIDTH, 1), 0) // GLA_DV
    state_mask = row_head_v == qk_head
    scale = GLA_DK ** -0.5

    safe = jnp.min(g_ref[...]) * CHUNK >= -GLA_SAFE_DECAY

    def fast_path():
        chunk = lambda x, c: x[c * CHUNK:(c + 1) * CHUNK]
        sel_r = lax.broadcasted_iota(jnp.int32, (8, rows), 0)
        sel_c = lax.broadcasted_iota(jnp.int32, (8, rows), 1) // CHUNK
        chunk_sel = jnp.where(sel_r == sel_c, 1.0, 0.0).astype(BF16)
        gs_hi, gs_lo = _split_bf16(g_ref[...])
        totals = _dot(chunk_sel, gs_hi) + _dot(chunk_sel, gs_lo)
        tri = cumf_ref[...]
        b = jnp.concatenate([_dot(tri, chunk(gs_hi, c)) + _dot(tri, chunk(gs_lo, c))
                             for c in range(n_chunks)], axis=0)
        b_last = jnp.concatenate([jnp.broadcast_to(totals[c:c + 1], (CHUNK, GLA_QK_LANES))
                                  for c in range(n_chunks)], axis=0)
        q = q_ref[...] * scale
        k = k_ref[...]
        v = v_ref[...]
        qd = (q * jnp.exp(b)).astype(BF16)
        kd = (k * jnp.exp(-b)).astype(BF16)
        kw = (k * jnp.exp(b_last - b)).astype(BF16)
        t_idx = lax.broadcasted_iota(jnp.int32, (CHUNK, GLA_QK_LANES), 0)
        s_idx = lax.broadcasted_iota(jnp.int32, (CHUNK, GLA_QK_LANES), 1) % HEAD_SLOT
        causal = s_idx <= t_idx
        intra, update = [], []
        for c in range(n_chunks):
            kd_c, v_c = chunk(kd, c), chunk(v, c)
            kd_bd = jnp.concatenate(
                [jnp.where(qk_head == hd, kd_c, jnp.zeros_like(kd_c)) for hd in range(GLA_HEADS)],
                axis=0)
            a = _dot_nt(chunk(qd, c), kd_bd)
            a = jnp.where(causal, a, 0.0).astype(BF16)
            v_bd = jnp.concatenate(
                [jnp.where(v_head == hd, v_c, jnp.zeros_like(v_c)) for hd in range(GLA_HEADS)],
                axis=0)
            intra.append(_dot(a, v_bd))
            update.append(_dot_tn(v_c, chunk(kw, c)))
        state = s_ref[...]
        for c in range(n_chunks):
            raw_ref[c * CHUNK:(c + 1) * CHUNK, :] = (
                intra[c] + _dot_nt(chunk(qd, c), state.astype(BF16)))
            state = state * jnp.exp(totals[c:c + 1]) + jnp.where(state_mask, update[c], 0.0)
        s_ref[...] = state

    @pl.when(safe)
    def _():
        fast_path()

    def general_path():
        cum = cum_ref[...]
        amask = amask_ref[...] > 0.5
        ind_r = lax.broadcasted_iota(jnp.int32, (GLA_QK_LANES, LANES), 0) // HEAD_SLOT
        ind_c = lax.broadcasted_iota(jnp.int32, (GLA_QK_LANES, LANES), 1)
        ind_sum = jnp.where(ind_r == ind_c, 1.0, 0.0).astype(BF16)
        bc_r = lax.broadcasted_iota(jnp.int32, (LANES, GLA_WIDTH), 0)
        bc_c = lax.broadcasted_iota(jnp.int32, (LANES, GLA_WIDTH), 1) // GLA_DV
        ind_bcast = jnp.where(bc_r == bc_c, 1.0, 0.0).astype(BF16)
        row_in_sub = lax.broadcasted_iota(jnp.int32, (CHUNK, 1), 0) % SUB

        def chunk_body(c, carry):
            r0 = pl.multiple_of(c * CHUNK, CHUNK)
            q = q_ref[pl.ds(r0, CHUNK), :] * scale
            k = k_ref[pl.ds(r0, CHUNK), :]
            g = g_ref[pl.ds(r0, CHUNK), :]
            v = v_ref[pl.ds(r0, CHUNK), :]
            g_hi, g_lo = _split_bf16(g)
            cums = _dot(cum, g_hi) + _dot(cum, g_lo)
            b = cums[0:CHUNK]
            b_ref = cums[CHUNK:2 * CHUNK]
            b_last = cums[2 * CHUNK:3 * CHUNK]
            e_kst = cums[3 * CHUNK:3 * CHUNK + KST_ROWS]

            state = s_ref[...]
            q_in = (q * jnp.exp(b)).astype(BF16)
            o = _dot_nt(q_in, state.astype(BF16))

            kw = (k * jnp.exp(b_last - b)).astype(BF16)
            decay = jnp.exp(jnp.concatenate([b_last] * (GLA_WIDTH // CHUNK), axis=0))
            s_ref[...] = state * decay + jnp.where(state_mask, _dot_tn(v, kw), 0.0)

            q_sub = (q * jnp.exp(jnp.minimum(b - b_ref, 0.0))).astype(BF16)
            k_st = jnp.concatenate([k[0:SUB * j] for j in range(1, N_SUB)], axis=0) * jnp.exp(e_kst)
            k_bd = jnp.concatenate(
                [jnp.where(qk_head == hd, k_st, 0.0) for hd in range(GLA_HEADS)], axis=0).astype(BF16)
            a = _dot_nt(q_sub, k_bd)
            a = jnp.where(amask, a, 0.0).astype(BF16)
            v_st = jnp.concatenate([v[0:SUB * j] for j in range(1, N_SUB)], axis=0)
            v_bd = jnp.concatenate(
                [jnp.where(v_head == hd, v_st, jnp.zeros_like(v_st)) for hd in range(GLA_HEADS)], axis=0)
            o = o + _dot(a, v_bd)

            vf = v.astype(F32)
            prods = []
            for d in range(SUB):
                ks = k if d == 0 else pltpu.roll(k, d, 0)
                bs = b if d == 0 else pltpu.roll(b, d, 0)
                prods.append((q * ks * jnp.exp(jnp.minimum(b - bs, 0.0))).astype(BF16))
            dsum = _dot(jnp.concatenate(prods, axis=0), ind_sum)
            valid = jnp.concatenate([row_in_sub >= d for d in range(SUB)], axis=0)
            dsum = jnp.where(valid, dsum, 0.0).astype(BF16)
            dbc = _dot(dsum, ind_bcast)
            for d in range(SUB):
                vs = vf if d == 0 else pltpu.roll(vf, d, 0)
                o = o + dbc[d * CHUNK:(d + 1) * CHUNK] * vs
            raw_ref[pl.ds(r0, CHUNK), :] = o
            return carry

        lax.fori_loop(0, n_chunks, chunk_body, 0)

    @pl.when(jnp.logical_not(safe))
    def _():
        general_path()

    o = raw_ref[...]
    hr = lax.broadcasted_iota(jnp.int32, (GLA_WIDTH, GLA_WIDTH), 0) // GLA_DV
    hc = lax.broadcasted_iota(jnp.int32, (GLA_WIDTH, GLA_WIDTH), 1) // GLA_DV
    head_ones = jnp.where(hr == hc, 1.0, 0.0).astype(BF16)
    ms = _dot((o * o).astype(BF16), head_ones) * (1.0 / GLA_DV)
    gate = gate_ref[...]
    y = o * lax.rsqrt(ms + EPS) * gn_ref[...] * (gate * _sigmoid(gate))
    o_ref[...] = y.astype(BF16)


def _gla(gq, gk, gg, gv, gate, gn_tiled, batch, seq):
    rows = min(GLA_ROWS, seq)
    steps = seq // rows
    row = lambda width: pl.BlockSpec((rows, width), lambda b, i: (b * steps + i, 0))
    assert rows // CHUNK <= 8
    cum_np = _gla_cum_matrix()
    cum = jnp.asarray(cum_np, BF16)
    cumf = jnp.asarray(cum_np[0:CHUNK], BF16)
    amask = jnp.asarray(_gla_score_mask(), F32)
    return pl.pallas_call(
        _gla_kernel,
        out_shape=jax.ShapeDtypeStruct((batch * seq, GLA_WIDTH), BF16),
        grid=(batch, steps),
        in_specs=[
            row(GLA_QK_LANES), row(GLA_QK_LANES), row(GLA_QK_LANES), row(GLA_WIDTH), row(GLA_WIDTH),
            pl.BlockSpec(cum.shape, lambda b, i: (0, 0)),
            pl.BlockSpec(cumf.shape, lambda b, i: (0, 0)),
            pl.BlockSpec(amask.shape, lambda b, i: (0, 0)),
            pl.BlockSpec((1, GLA_WIDTH), lambda b, i: (0, 0)),
        ],
        out_specs=row(GLA_WIDTH),
        scratch_shapes=[pltpu.VMEM((GLA_WIDTH, GLA_QK_LANES), F32),
                        pltpu.VMEM((rows, GLA_WIDTH), F32)],
        compiler_params=pltpu.CompilerParams(
            dimension_semantics=("arbitrary", "arbitrary"), vmem_limit_bytes=VMEM_LIMIT_BYTES),
        name="gla",
    )(gq, gk, gg, gv, gate, cum, cumf, amask, gn_tiled)


def _t5_bucket(rel):
    nb = NUM_BUCKETS // 2
    max_exact = nb // 2
    ret = (rel > 0).astype(jnp.int32) * nb
    n = jnp.abs(rel)
    nf = jnp.maximum(n, 1).astype(jnp.float32)
    large = max_exact + (jnp.log(nf / max_exact) / math.log(MAX_DISTANCE / max_exact)
                         * (nb - max_exact)).astype(jnp.int32)
    large = jnp.minimum(large, nb - 1)
    return ret + jnp.where(n < max_exact, n, large)


def _bias_tiles(rel_bias, tile):
    assert tile >= MAX_DISTANCE and tile % CHUNK == 0
    table = rel_bias.astype(F32)
    far = table[NUM_BUCKETS // 2 - 1]
    heads = table.shape[1]
    rel = jnp.concatenate([jnp.arange(0, tile), jnp.arange(-2 * tile, 0)])
    onehot = _t5_bucket(rel)[None, :, None] == jnp.arange(NUM_BUCKETS)[None, None, :]
    per_rel = jnp.sum(jnp.where(onehot, table.T[:, None, :], 0.0), axis=-1)
    per_rel = (per_rel - far[:, None]) * LOG2E
    span = 3 * tile
    flat = jnp.tile(per_rel, (1, 2 * tile))[:, :2 * tile * (span - 1)]
    toep = flat.reshape(heads, 2 * tile, span - 1)[:, :, :tile]
    r = jnp.arange(tile)[:, None]
    c = jnp.arange(tile)[None, :]
    diag = jnp.where(((c // CHUNK) <= (r // CHUNK))[None], toep[:, :tile], MASK_VALUE)
    left = toep[:, tile:]
    return jnp.stack([left, diag], axis=1)


def _attn_kernel(q_ref, k_ref, v_ref, bias_ref, lamv_ref, subln_ref, o_ref,
                 s_buf, p_buf, alpha_buf, m_all, acc_all, *, tile, lam_init):
    seq = q_ref.shape[0]
    nq = seq // tile

    s_buf[...] = jnp.zeros(s_buf.shape, F32)
    p_buf[...] = jnp.zeros(p_buf.shape, BF16)
    alpha_buf[...] = jnp.zeros(alpha_buf.shape, F32)
    m_all[...] = jnp.full(m_all.shape, MASK_VALUE, F32)
    acc_all[...] = jnp.zeros(acc_all.shape, F32)

    lane = lax.broadcasted_iota(jnp.int32, (tile, LANES), 1)

    def stage1(i, j, slot):
        q = q_ref[pl.ds(pl.multiple_of(i * tile, tile), tile), :]
        zero = jnp.zeros_like(q)
        qs = jnp.concatenate([jnp.where(lane < LANES // 2, q, zero),
                              jnp.where(lane >= LANES // 2, q, zero)], axis=0)
        kj = k_ref[pl.ds(pl.multiple_of(j * tile, tile), tile), :]
        s_buf[slot] = _dot_nt(qs, kj)

    def stage2(i, bias, slot):
        for half in range(2):
            rows = slice(half * tile, (half + 1) * tile)
            s = s_buf[slot, rows, :]
            if bias is not None:
                s = s + bias
            m_prev = m_all[i, rows, :]
            m_next = jnp.maximum(m_prev, jnp.max(s, axis=-1, keepdims=True))
            p = jnp.exp2(s - jnp.concatenate([m_next] * (tile // LANES), axis=1))
            p_buf[slot, rows, :] = p.astype(BF16)
            alpha_buf[slot, rows, :] = jnp.exp2(m_prev - m_next)
            m_all[i, rows, :] = m_next

    def stage3(i, j, slot):
        vj = v_ref[pl.ds(pl.multiple_of(j * tile, tile), tile), :]
        acc_all[i] = alpha_buf[slot] * acc_all[i] + _dot(p_buf[slot], vj)

    def run_pipeline(n_pairs, first_pair, advance, with_bias, unroll):
        n_steps = pl.cdiv(n_pairs + 2, unroll) * unroll

        def step(t, pairs, slot_a, slot_b):
            (i0, j0), (i1, j1), (i2, j2) = pairs
            valid1 = jnp.logical_and(t >= 1, t <= n_pairs)
            valid2 = jnp.logical_and(t >= 2, t <= n_pairs + 1)
            stage3(jnp.where(valid2, i2, nq), j2, slot_a)
            stage2(jnp.where(valid1, i1, nq), bias_ref[0, j1 - i1 + 1] if with_bias else None, slot_b)
            stage1(i0, j0, slot_a)
            return (advance(i0, j0), (i0, j0), (i1, j1))

        def body(u, pairs):
            for r in range(unroll):
                pairs = step(unroll * u + r, pairs, r % 2, (r + 1) % 2)
            return pairs

        lax.fori_loop(0, n_steps // unroll, body, (first_pair,) * 3)

    def next_far(i, j):
        wrap = j == i - 2
        done = jnp.logical_and(wrap, i == nq - 1)
        step_i = jnp.logical_and(wrap, jnp.logical_not(done))
        return (jnp.where(step_i, i + 1, i), jnp.where(done, j, jnp.where(wrap, 0, j + 1)))

    def next_near(i, j):
        wrap = j == i
        done = jnp.logical_and(wrap, i == nq - 1)
        step_i = jnp.logical_and(wrap, jnp.logical_not(done))
        return (jnp.where(step_i, i + 1, i), jnp.where(jnp.logical_or(done, wrap), j, j + 1))

    zero = jnp.int32(0)
    if nq > 2:
        run_pipeline((nq - 1) * (nq - 2) // 2, (jnp.int32(2), zero), next_far, False, ATT_UNROLL_FAR)
    run_pipeline(2 * nq - 1, (zero, zero), next_near, True, ATT_UNROLL_NEAR)

    def finish(i, carry):
        acc = acc_all[i]
        a1 = acc[0:tile]
        a2 = acc[tile:2 * tile]
        l1 = a1[:, DIFF_DV:DIFF_DV + 1]
        l2 = a2[:, DIFF_DV:DIFF_DV + 1]
        lv = lamv_ref[...]
        lam = (jnp.exp(jnp.sum(lv[0:1] * lv[1:2], axis=-1, keepdims=True))
               - jnp.exp(jnp.sum(lv[2:3] * lv[3:4], axis=-1, keepdims=True)) + lam_init)
        o = a1 / l1 - lam * (a2 / l2)
        o = jnp.where(lane < DIFF_DV, o, 0.0)
        ms = jnp.sum(o * o, axis=-1, keepdims=True) * (1.0 / DIFF_DV)
        y = o * lax.rsqrt(ms + EPS) * subln_ref[...] * (1.0 - lam_init)
        o_ref[pl.ds(pl.multiple_of(i * tile, tile), tile), :] = y.astype(BF16)
        return carry

    lax.fori_loop(0, nq, finish, 0)


def _attn(dq, dk, dv, bias_tiles, lamv, subln, batch, seq, lam_init):
    tile = min(ATT_TILE, seq)
    nq = seq // tile
    head_block = pl.BlockSpec((seq, LANES), lambda b, h: (b, h))
    return pl.pallas_call(
        functools.partial(_attn_kernel, tile=tile, lam_init=lam_init),
        out_shape=jax.ShapeDtypeStruct((batch * seq, DIFF_LANES), BF16),
        grid=(batch, DIFF_HEADS),
        in_specs=[
            head_block, head_block, head_block,
            pl.BlockSpec((1, 2, tile, tile), lambda b, h: (h, 0, 0, 0)),
            pl.BlockSpec((4, LANES), lambda b, h: (0, 0)),
            pl.BlockSpec((1, LANES), lambda b, h: (0, 0)),
        ],
        out_specs=head_block,
        scratch_shapes=[pltpu.VMEM((2, 2 * tile, tile), F32),
                        pltpu.VMEM((2, 2 * tile, tile), BF16),
                        pltpu.VMEM((2, 2 * tile, LANES), F32),
                        pltpu.VMEM((nq + 1, 2 * tile, LANES), F32),
                        pltpu.VMEM((nq + 1, 2 * tile, LANES), F32)],
        compiler_params=pltpu.CompilerParams(
            dimension_semantics=("parallel", "parallel"), vmem_limit_bytes=VMEM_LIMIT_BYTES),
        name="diff_attn",
    )(dq, dk, dv, bias_tiles, lamv, subln)


def _pad_cols(w, width):
    return jnp.pad(w, ((0, 0), (0, width - w.shape[1])))


def _slot_cols(w, heads, used, slot):
    lead = w.shape[0]
    w = w.reshape(lead, heads, used)
    return jnp.pad(w, ((0, 0), (0, 0), (0, slot - used))).reshape(lead, heads * slot)


def _layout_w_in(w_in):
    offs = np.cumsum([0, GLA_HEADS * GLA_DK, GLA_HEADS * GLA_DK, GLA_WIDTH, GLA_RANK, GLA_WIDTH,
                      DIFF_HEADS * 2 * DIFF_DQK, DIFF_HEADS * 2 * DIFF_DQK, DIFF_WIDTH,
                      CONV_DIM, CONV_DIM, CONV_DIM])
    gq, gk, gv, glr, gate, dq, dk, dv, cb, cc, ch = [w_in[:, offs[n]:offs[n + 1]] for n in range(11)]
    half = LANES // 2
    gq_slots = _slot_cols(gq, GLA_HEADS, GLA_DK, HEAD_SLOT)
    gq_slots = gq_slots.at[:, GLR_LANE:GLR_LANE + GLA_RANK].set(glr)
    cols = [
        gq_slots,
        _slot_cols(gk, GLA_HEADS, GLA_DK, HEAD_SLOT),
        gv,
        gate,
        _slot_cols(dq, 2 * DIFF_HEADS, DIFF_DQK, half),
        _slot_cols(dk, 2 * DIFF_HEADS, DIFF_DQK, half),
        _slot_cols(dv, DIFF_HEADS, DIFF_DV, LANES),
        cb, cc, ch,
    ]
    w = jnp.concatenate(cols, axis=1)
    assert w.shape[1] == N_PROJ
    return w.astype(BF16)


def kernel(x, ffn1_norm, ffn1_gate, ffn1_up, ffn1_down, mix_norm, w_in, gla_gk_up, gla_gk_bias,
           gla_norm, diff_lambda_q1, diff_lambda_k1, diff_lambda_q2, diff_lambda_k2, diff_subln,
           rel_bias, conv_w, w_out, ffn2_norm, ffn2_gate, ffn2_up, ffn2_down, final_norm):
    batch, seq, _ = x.shape
    depth = w_in.shape[0]
    t = batch * seq
    xf = x.reshape(t, D_MODEL)
    bias_tiles = _bias_tiles(rel_bias, min(ATT_TILE, seq))
    final_g = final_norm.reshape(1, D_MODEL)
    ffn1_w = (ffn1_gate[0].astype(BF16), ffn1_up[0].astype(BF16), ffn1_down[0].astype(BF16))

    for l in range(depth):
        lam_init = 0.8 - 0.6 * math.exp(-0.3 * l)
        xf, ffn2_w = _ffn(xf, ffn1_norm[l].reshape(1, D_MODEL), *ffn1_w, final_g, False,
                          cast=((ffn2_gate, ffn2_up, ffn2_down), l))

        w_pad = _layout_w_in(w_in[l])
        up = _slot_cols(gla_gk_up[l], GLA_HEADS, GLA_DK, HEAD_SLOT)
        up = jnp.pad(up, ((GLR_LANE, GLA_QK_LANES - GLR_LANE - GLA_RANK), (0, 0)))
        up_hi = up.astype(BF16)
        up_lo = (up - up_hi.astype(F32)).astype(BF16)
        gkup = jnp.stack([up_hi, up_lo], axis=0)
        gkb = _slot_cols(gla_gk_bias[l].reshape(1, -1), GLA_HEADS, GLA_DK, HEAD_SLOT)
        cw = jnp.pad(conv_w[l], ((0, 8 - CONV_WIDTH), (0, 0)))
        gq, gk, gg, gv, gate, dq, dk, dv, conv_o = _proj(
            xf, mix_norm[l].reshape(1, D_MODEL), w_pad, gkup, gkb, cw, seq)

        gn_tiled = jnp.tile(gla_norm[l], GLA_HEADS).reshape(1, GLA_WIDTH)
        gla_o = _gla(gq, gk, gg, gv, gate, gn_tiled, batch, seq)

        lamv = jnp.stack([diff_lambda_q1[l], diff_lambda_k1[l], diff_lambda_q2[l], diff_lambda_k2[l]])
        lamv = _pad_cols(lamv.astype(F32), LANES)
        subln = _pad_cols(diff_subln[l].reshape(1, DIFF_DV), LANES)
        diff_o = _attn(dq, dk, dv, bias_tiles, lamv, subln, batch, seq, lam_init)

        wo = w_out[l]
        w1 = wo[0:GLA_WIDTH].astype(BF16)
        w2 = wo[GLA_WIDTH:GLA_WIDTH + DIFF_WIDTH].reshape(DIFF_HEADS, DIFF_DV, D_MODEL)
        w2 = jnp.pad(w2, ((0, 0), (0, LANES - DIFF_DV), (0, 0))).reshape(DIFF_LANES, D_MODEL).astype(BF16)
        w3 = wo[GLA_WIDTH + DIFF_WIDTH:].astype(BF16)
        last = l == depth - 1
        xf, ffn1_w = _ffn(xf, ffn2_norm[l].reshape(1, D_MODEL), *ffn2_w, final_g, last,
                          mix=(gla_o, diff_o, conv_o, w1, w2, w3),
                          cast=None if last else ((ffn1_gate, ffn1_up, ffn1_down), l + 1))

    return xf.reshape(batch, seq, D_MODEL)
```

```python
import functools
import math

import numpy as np
import jax
import jax.numpy as jnp
from jax import lax
from jax.experimental import pallas as pl
from jax.experimental.pallas import tpu as pltpu

F32 = jnp.float32
BF16 = jnp.bfloat16

D_MODEL = 1024
D_FF = 2816
EPS = 1e-6
CHUNK = 64
GLA_HEADS = 4
GLA_DK = 48
GLA_DV = 96
GLA_RANK = 16
GLA_GATE_NORM = 16.0
DIFF_HEADS = 4
DIFF_DQK = 48
DIFF_DV = 96
CONV_DIM = 256
CONV_WIDTH = 3
NUM_BUCKETS = 32
MAX_DISTANCE = 128
GLA_WIDTH = GLA_HEADS * GLA_DV
DIFF_WIDTH = DIFF_HEADS * DIFF_DV

LANES = 128
BF16_SUBLANES = 16
VMEM_LIMIT_BYTES = 56 * 1024 * 1024

HEAD_SLOT = 64
GLA_QK_LANES = GLA_HEADS * HEAD_SLOT
SUB = 16
N_SUB = CHUNK // SUB
KST_ROWS = SUB * (N_SUB * (N_SUB - 1) // 2)
DIFF_LANES = DIFF_HEADS * LANES
MASK_VALUE = -1e30
GLA_SAFE_DECAY = 60.0
LOG2E = math.log2(math.e)

OFF_GQ = 0
OFF_GK = OFF_GQ + GLA_QK_LANES
OFF_GV = OFF_GK + GLA_QK_LANES
OFF_GATE = OFF_GV + GLA_WIDTH
OFF_DQ = OFF_GATE + GLA_WIDTH
OFF_DK = OFF_DQ + DIFF_LANES
OFF_DV = OFF_DK + DIFF_LANES
OFF_CB = OFF_DV + DIFF_LANES
OFF_CC = OFF_CB + CONV_DIM
OFF_CH = OFF_CC + CONV_DIM
N_PROJ = OFF_CH + CONV_DIM
GLR_LANE = GLA_DK
assert GLR_LANE + GLA_RANK <= HEAD_SLOT

FFN_ROWS = 512
MXU_DEPTH = 256
FFN_F_SPLITS = (0, 6 * MXU_DEPTH, D_FF)
assert all((hi - lo) % MXU_DEPTH == 0 for lo, hi in zip(FFN_F_SPLITS[:-1], FFN_F_SPLITS[1:]))
PROJ_ROWS = 512
GLA_ROWS = 512
ATT_TILE = 256
ATT_UNROLL_FAR = 16
ATT_UNROLL_NEAR = 8
ATT_FINISH_GROUP = 4


def _dot(a, b):
    return jnp.dot(a, b, preferred_element_type=F32)


def _dot_nt(a, b):
    return lax.dot_general(a, b, (((1,), (1,)), ((), ())), preferred_element_type=F32)


def _dot_tn(a, b):
    return lax.dot_general(a, b, (((0,), (0,)), ((), ())), preferred_element_type=F32)


def _split_bf16(x):
    hi = x.astype(BF16)
    lo = (x - hi.astype(F32)).astype(BF16)
    return hi, lo


def _rms(x, g):
    return x * lax.rsqrt(jnp.mean(x * x, axis=-1, keepdims=True) + EPS) * g


def _sigmoid(x):
    return 1.0 / (1.0 + jnp.exp(-x))


def _ffn_kernel(*refs, final, mix, cast):
    refs = list(refs)
    x_ref, g_ref, wg_ref, wu_ref, wd_ref, fg_ref = refs[:6]
    mix_refs = refs[6:12] if mix else ()
    n_in = 6 + len(mix_refs)
    src_refs = refs[n_in:n_in + 3] if cast else ()
    o_ref = refs[n_in + len(src_refs)]
    dst_refs = refs[n_in + len(src_refs) + 1:]
    for src, dst in zip(src_refs, dst_refs):
        dst[...] = src[...].astype(BF16)

    x = x_ref[...]
    if mix:
        gla_ref, diff_ref, conv_ref, w1_ref, w2_ref, w3_ref = mix_refs
        x = (x + _dot(gla_ref[...], w1_ref[...]) + _dot(diff_ref[...], w2_ref[...])
             + _dot(conv_ref[...], w3_ref[...]))
    h = _rms(x, g_ref[...]).astype(BF16)
    acc = None
    for lo, hi in zip(FFN_F_SPLITS[:-1], FFN_F_SPLITS[1:]):
        sl = slice(lo, hi)
        gate = _dot(h, wg_ref[:, sl])
        up = _dot(h, wu_ref[:, sl])
        a = (gate * _sigmoid(gate) * up).astype(BF16)
        part = _dot(a, wd_ref[sl, :])
        acc = part if acc is None else acc + part
    y = x + 0.5 * acc
    if final:
        y = _rms(y, fg_ref[...])
    o_ref[...] = y


def _const_spec(shape):
    nd = len(shape)
    return pl.BlockSpec(shape, lambda *_: (0,) * nd, pipeline_mode=pl.Buffered(1))


def _cast_block(n_rows, steps):
    share = 1
    while (n_rows * share) % (steps * BF16_SUBLANES) or steps % share:
        share *= 2
        assert share <= steps
    return n_rows * share // steps, share


def _ffn(x, g, wg, wu, wd, final_g, final, mix=None, cast=None):
    t = x.shape[0]
    tm = min(FFN_ROWS, t)
    steps = t // tm
    row = lambda width: pl.BlockSpec((tm, width), lambda i: (i, 0))
    operands = [x, g, wg, wu, wd, final_g]
    in_specs = [row(D_MODEL), _const_spec((1, D_MODEL)), _const_spec((D_MODEL, D_FF)),
                _const_spec((D_MODEL, D_FF)), _const_spec((D_FF, D_MODEL)), _const_spec((1, D_MODEL))]
    out_shapes = [jax.ShapeDtypeStruct((t, D_MODEL), F32)]
    out_specs = [row(D_MODEL)]
    if mix is not None:
        operands += list(mix)
        in_specs += [row(a.shape[1]) for a in mix[:3]] + [_const_spec(w.shape) for w in mix[3:]]
    if cast is not None:
        stacked, layer = cast
        for w in stacked:
            rows, share = _cast_block(w.shape[1], steps)
            operands.append(w)
            in_specs.append(pl.BlockSpec((None, rows, w.shape[2]),
                                         lambda i, share=share: (layer, i // share, 0)))
            out_shapes.append(jax.ShapeDtypeStruct(w.shape[1:], BF16))
            out_specs.append(pl.BlockSpec((rows, w.shape[2]), lambda i, share=share: (i // share, 0)))
    outs = pl.pallas_call(
        functools.partial(_ffn_kernel, final=final, mix=mix is not None, cast=cast is not None),
        out_shape=tuple(out_shapes),
        grid=(steps,),
        in_specs=in_specs,
        out_specs=tuple(out_specs),
        compiler_params=pltpu.CompilerParams(
            dimension_semantics=("arbitrary",), vmem_limit_bytes=VMEM_LIMIT_BYTES),
        name="ffn",
    )(*operands)
    return outs[0], tuple(outs[1:])


def _proj_kernel(x_ref, g_ref, w_ref, gkup_ref, gkb_ref, cw_ref,
                 gq_ref, gk_ref, gg_ref, gv_ref, gate_ref, dq_ref, dk_ref, dv_ref, conv_ref,
                 zbuf_ref, *, tiles_per_seq):
    tm = x_ref.shape[0]
    h = _rms(x_ref[...], g_ref[...]).astype(BF16)

    def proj(off, width):
        return _dot(h, w_ref[:, off:off + width])

    qk = proj(OFF_GQ, 2 * GLA_QK_LANES)
    gq = qk[:, 0:GLA_QK_LANES]
    gq_ref[...] = gq
    gk_ref[...] = qk[:, GLA_QK_LANES:]
    vg = proj(OFF_GV, 2 * GLA_WIDTH)
    gv_ref[...] = vg[:, 0:GLA_WIDTH].astype(BF16)
    gate_ref[...] = vg[:, GLA_WIDTH:]

    glr_hi, glr_lo = _split_bf16(gq)
    up_hi = gkup_ref[0]
    up_lo = gkup_ref[1]
    z = _dot(glr_hi, up_hi) + _dot(glr_lo, up_hi) + _dot(glr_hi, up_lo) + gkb_ref[...]
    logsig = jnp.minimum(z, 0.0) - jnp.log1p(jnp.exp(-jnp.abs(z)))
    lane = lax.broadcasted_iota(jnp.int32, (tm, GLA_QK_LANES), 1)
    gg_ref[...] = jnp.where(lane % HEAD_SLOT < GLA_DK, logsig * (1.0 / GLA_GATE_NORM), 0.0)

    dqk = proj(OFF_DQ, 2 * DIFF_LANES)
    dq_ref[...] = (dqk[:, 0:DIFF_LANES] * (DIFF_DQK ** -0.5 * LOG2E)).astype(BF16)
    dk_ref[...] = dqk[:, DIFF_LANES:].astype(BF16)
    lane_v = lax.broadcasted_iota(jnp.int32, (tm, DIFF_LANES), 1)
    dv_ref[...] = jnp.where(lane_v % LANES == DIFF_DV, 1.0, proj(OFF_DV, DIFF_LANES)).astype(BF16)

    @pl.when(pl.program_id(0) % tiles_per_seq == 0)
    def _():
        zbuf_ref[0:8, :] = jnp.zeros((8, CONV_DIM), F32)

    conv = proj(OFF_CB, 3 * CONV_DIM)
    zc = conv[:, CONV_DIM:2 * CONV_DIM] * conv[:, 2 * CONV_DIM:]
    zbuf_ref[8:8 + tm, :] = zc
    cw = cw_ref[...]
    y = (cw[2:3, :] * zc + cw[1:2, :] * zbuf_ref[7:7 + tm, :] + cw[0:1, :] * zbuf_ref[6:6 + tm, :])
    conv_ref[...] = (conv[:, 0:CONV_DIM] * y).astype(BF16)
    zbuf_ref[0:8, :] = zbuf_ref[tm:tm + 8, :]


def _proj(x, g, w, gkup, gkb, cw, seq):
    t = x.shape[0]
    tm = min(PROJ_ROWS, seq)
    row = lambda width: pl.BlockSpec((tm, width), lambda i: (i, 0))
    out_shapes = (
        jax.ShapeDtypeStruct((t, GLA_QK_LANES), F32),
        jax.ShapeDtypeStruct((t, GLA_QK_LANES), F32),
        jax.ShapeDtypeStruct((t, GLA_QK_LANES), F32),
        jax.ShapeDtypeStruct((t, GLA_WIDTH), BF16),
        jax.ShapeDtypeStruct((t, GLA_WIDTH), F32),
        jax.ShapeDtypeStruct((t, DIFF_LANES), BF16),
        jax.ShapeDtypeStruct((t, DIFF_LANES), BF16),
        jax.ShapeDtypeStruct((t, DIFF_LANES), BF16),
        jax.ShapeDtypeStruct((t, CONV_DIM), BF16),
    )
    return pl.pallas_call(
        functools.partial(_proj_kernel, tiles_per_seq=seq // tm),
        out_shape=out_shapes,
        grid=(t // tm,),
        in_specs=[
            row(D_MODEL),
            _const_spec((1, D_MODEL)),
            _const_spec((D_MODEL, N_PROJ)),
            _const_spec((2, GLA_QK_LANES, GLA_QK_LANES)),
            _const_spec((1, GLA_QK_LANES)),
            _const_spec((8, CONV_DIM)),
        ],
        out_specs=tuple(row(s.shape[1]) for s in out_shapes),
        scratch_shapes=[pltpu.VMEM((tm + 8, CONV_DIM), F32)],
        compiler_params=pltpu.CompilerParams(
            dimension_semantics=("arbitrary",), vmem_limit_bytes=VMEM_LIMIT_BYTES),
        name="mixer_proj",
    )(x, g, w, gkup, gkb, cw)


def _gla_cum_matrix():
    t = np.arange(CHUNK)[:, None]
    u = np.arange(CHUNK)[None, :]
    tri = (u <= t)
    ref = (u <= (t // SUB) * SUB - 1)
    ones = np.ones((CHUNK, CHUNK), bool)
    parts = [tri, ref, ones]
    for j in range(1, N_SUB):
        s = np.arange(SUB * j)[:, None]
        parts.append((u > s) & (u <= SUB * j - 1))
    return np.concatenate(parts, axis=0).astype(np.float32)


def _gla_score_mask():
    t = np.arange(CHUNK)[:, None]
    m = np.arange(KST_ROWS)
    group = np.concatenate([np.full(SUB * j, j) for j in range(1, N_SUB)])
    keep = (group[None, :] == (t // SUB))
    return np.tile(keep, (1, GLA_HEADS)).astype(np.float32)


def _gla_kernel(q_ref, k_ref, g_ref, v_ref, gate_ref, cum_ref, cumf_ref, amask_ref, gn_ref,
                o_ref, s_ref, raw_ref):
    rows = q_ref.shape[0]
    n_chunks = rows // CHUNK

    @pl.when(pl.program_id(1) == 0)
    def _():
        s_ref[...] = jnp.zeros_like(s_ref)

    qk_head = lax.broadcasted_iota(jnp.int32, (1, GLA_QK_LANES), 1) // HEAD_SLOT
    v_head = lax.broadcasted_iota(jnp.int32, (1, GLA_WIDTH), 1) // GLA_DV
    row_head_v = lax.broadcasted_iota(jnp.int32, (GLA_WIDTH, 1), 0) // GLA_DV
    state_mask = row_head_v == qk_head
    scale = GLA_DK ** -0.5

    safe = jnp.min(g_ref[...]) * CHUNK >= -GLA_SAFE_DECAY

    def fast_path():
        chunk = lambda x, c: x[c * CHUNK:(c + 1) * CHUNK]
        sel_r = lax.broadcasted_iota(jnp.int32, (8, rows), 0)
        sel_c = lax.broadcasted_iota(jnp.int32, (8, rows), 1) // CHUNK
        chunk_sel = jnp.where(sel_r == sel_c, 1.0, 0.0).astype(BF16)
        gs_hi, gs_lo = _split_bf16(g_ref[...])
        totals = _dot(chunk_sel, gs_hi) + _dot(chunk_sel, gs_lo)
        tri = cumf_ref[...]
        b = jnp.concatenate([_dot(tri, chunk(gs_hi, c)) + _dot(tri, chunk(gs_lo, c))
                             for c in range(n_chunks)], axis=0)
        b_last = jnp.concatenate([jnp.broadcast_to(totals[c:c + 1], (CHUNK, GLA_QK_LANES))
                                  for c in range(n_chunks)], axis=0)
        q = q_ref[...] * scale
        k = k_ref[...]
        v = v_ref[...]
        qd = (q * jnp.exp(b)).astype(BF16)
        kd = (k * jnp.exp(-b)).astype(BF16)
        kw = (k * jnp.exp(b_last - b)).astype(BF16)
        t_idx = lax.broadcasted_iota(jnp.int32, (CHUNK, GLA_QK_LANES), 0)
        s_idx = lax.broadcasted_iota(jnp.int32, (CHUNK, GLA_QK_LANES), 1) % HEAD_SLOT
        causal = s_idx <= t_idx
        intra, update = [], []
        for c in range(n_chunks):
            kd_c, v_c = chunk(kd, c), chunk(v, c)
            kd_bd = jnp.concatenate(
                [jnp.where(qk_head == hd, kd_c, jnp.zeros_like(kd_c)) for hd in range(GLA_HEADS)],
                axis=0)
            a = _dot_nt(chunk(qd, c), kd_bd)
            a = jnp.where(causal, a, 0.0).astype(BF16)
            v_bd = jnp.concatenate(
                [jnp.where(v_head == hd, v_c, jnp.zeros_like(v_c)) for hd in range(GLA_HEADS)],
                axis=0)
            intra.append(_dot(a, v_bd))
            update.append(_dot_tn(v_c, chunk(kw, c)))
        state = s_ref[...]
        for c in range(n_chunks):
            raw_ref[c * CHUNK:(c + 1) * CHUNK, :] = (
                intra[c] + _dot_nt(chunk(qd, c), state.astype(BF16)))
            state = state * jnp.exp(totals[c:c + 1]) + jnp.where(state_mask, update[c], 0.0)
        s_ref[...] = state

    @pl.when(safe)
    def _():
        fast_path()

    def general_path():
        cum = cum_ref[...]
        amask = amask_ref[...] > 0.5
        ind_r = lax.broadcasted_iota(jnp.int32, (GLA_QK_LANES, LANES), 0) // HEAD_SLOT
        ind_c = lax.broadcasted_iota(jnp.int32, (GLA_QK_LANES, LANES), 1)
        ind_sum = jnp.where(ind_r == ind_c, 1.0, 0.0).astype(BF16)
        bc_r = lax.broadcasted_iota(jnp.int32, (LANES, GLA_WIDTH), 0)
        bc_c = lax.broadcasted_iota(jnp.int32, (LANES, GLA_WIDTH), 1) // GLA_DV
        ind_bcast = jnp.where(bc_r == bc_c, 1.0, 0.0).astype(BF16)
        row_in_sub = lax.broadcasted_iota(jnp.int32, (CHUNK, 1), 0) % SUB

        def chunk_body(c, carry):
            r0 = pl.multiple_of(c * CHUNK, CHUNK)
            q = q_ref[pl.ds(r0, CHUNK), :] * scale
            k = k_ref[pl.ds(r0, CHUNK), :]
            g = g_ref[pl.ds(r0, CHUNK), :]
            v = v_ref[pl.ds(r0, CHUNK), :]
            g_hi, g_lo = _split_bf16(g)
            cums = _dot(cum, g_hi) + _dot(cum, g_lo)
            b = cums[0:CHUNK]
            b_ref = cums[CHUNK:2 * CHUNK]
            b_last = cums[2 * CHUNK:3 * CHUNK]
            e_kst = cums[3 * CHUNK:3 * CHUNK + KST_ROWS]

            state = s_ref[...]
            q_in = (q * jnp.exp(b)).astype(BF16)
            o = _dot_nt(q_in, state.astype(BF16))

            kw = (k * jnp.exp(b_last - b)).astype(BF16)
            decay = jnp.exp(jnp.concatenate([b_last] * (GLA_WIDTH // CHUNK), axis=0))
            s_ref[...] = state * decay + jnp.where(state_mask, _dot_tn(v, kw), 0.0)

            q_sub = (q * jnp.exp(jnp.minimum(b - b_ref, 0.0))).astype(BF16)
            k_st = jnp.concatenate([k[0:SUB * j] for j in range(1, N_SUB)], axis=0) * jnp.exp(e_kst)
            k_bd = jnp.concatenate(
                [jnp.where(qk_head == hd, k_st, 0.0) for hd in range(GLA_HEADS)], axis=0).astype(BF16)
            a = _dot_nt(q_sub, k_bd)
            a = jnp.where(amask, a, 0.0).astype(BF16)
            v_st = jnp.concatenate([v[0:SUB * j] for j in range(1, N_SUB)], axis=0)
            v_bd = jnp.concatenate(
                [jnp.where(v_head == hd, v_st, jnp.zeros_like(v_st)) for hd in range(GLA_HEADS)], axis=0)
            o = o + _dot(a, v_bd)

            vf = v.astype(F32)
            prods = []
            for d in range(SUB):
                ks = k if d == 0 else pltpu.roll(k, d, 0)
                bs = b if d == 0 else pltpu.roll(b, d, 0)
                prods.append((q * ks * jnp.exp(jnp.minimum(b - bs, 0.0))).astype(BF16))
            dsum = _dot(jnp.concatenate(prods, axis=0), ind_sum)
            valid = jnp.concatenate([row_in_sub >= d for d in range(SUB)], axis=0)
            dsum = jnp.where(valid, dsum, 0.0).astype(BF16)
            dbc = _dot(dsum, ind_bcast)
            for d in range(SUB):
                vs = vf if d == 0 else pltpu.roll(vf, d, 0)
                o = o + dbc[d * CHUNK:(d + 1) * CHUNK] * vs
            raw_ref[pl.ds(r0, CHUNK), :] = o
            return carry

        lax.fori_loop(0, n_chunks, chunk_body, 0)

    @pl.when(jnp.logical_not(safe))
    def _():
        general_path()

    o = raw_ref[...]
    hr = lax.broadcasted_iota(jnp.int32, (GLA_WIDTH, GLA_WIDTH), 0) // GLA_DV
    hc = lax.broadcasted_iota(jnp.int32, (GLA_WIDTH, GLA_WIDTH), 1) // GLA_DV
    head_ones = jnp.where(hr == hc, 1.0, 0.0).astype(BF16)
    ms = _dot((o * o).astype(BF16), head_ones) * (1.0 / GLA_DV)
    gate = gate_ref[...]
    y = o * lax.rsqrt(ms + EPS) * gn_ref[...] * (gate * _sigmoid(gate))
    o_ref[...] = y.astype(BF16)


def _gla(gq, gk, gg, gv, gate, gn_tiled, batch, seq):
    rows = min(GLA_ROWS, seq)
    steps = seq // rows
    row = lambda width: pl.BlockSpec((rows, width), lambda b, i: (b * steps + i, 0))
    assert rows // CHUNK <= 8
    cum_np = _gla_cum_matrix()
    cum = jnp.asarray(cum_np, BF16)
    cumf = jnp.asarray(cum_np[0:CHUNK], BF16)
    amask = jnp.asarray(_gla_score_mask(), F32)
    return pl.pallas_call(
        _gla_kernel,
        out_shape=jax.ShapeDtypeStruct((batch * seq, GLA_WIDTH), BF16),
        grid=(batch, steps),
        in_specs=[
            row(GLA_QK_LANES), row(GLA_QK_LANES), row(GLA_QK_LANES), row(GLA_WIDTH), row(GLA_WIDTH),
            pl.BlockSpec(cum.shape, lambda b, i: (0, 0)),
            pl.BlockSpec(cumf.shape, lambda b, i: (0, 0)),
            pl.BlockSpec(amask.shape, lambda b, i: (0, 0)),
            pl.BlockSpec((1, GLA_WIDTH), lambda b, i: (0, 0)),
        ],
        out_specs=row(GLA_WIDTH),
        scratch_shapes=[pltpu.VMEM((GLA_WIDTH, GLA_QK_LANES), F32),
                        pltpu.VMEM((rows, GLA_WIDTH), F32)],
        compiler_params=pltpu.CompilerParams(
            dimension_semantics=("arbitrary", "arbitrary"), vmem_limit_bytes=VMEM_LIMIT_BYTES),
        name="gla",
    )(gq, gk, gg, gv, gate, cum, cumf, amask, gn_tiled)


def _t5_bucket(rel):
    nb = NUM_BUCKETS // 2
    max_exact = nb // 2
    ret = (rel > 0).astype(jnp.int32) * nb
    n = jnp.abs(rel)
    nf = jnp.maximum(n, 1).astype(jnp.float32)
    large = max_exact + (jnp.log(nf / max_exact) / math.log(MAX_DISTANCE / max_exact)
                         * (nb - max_exact)).astype(jnp.int32)
    large = jnp.minimum(large, nb - 1)
    return ret + jnp.where(n < max_exact, n, large)


def _bias_tiles(rel_bias, tile):
    assert tile >= MAX_DISTANCE and tile % CHUNK == 0
    table = rel_bias.astype(F32)
    far = table[NUM_BUCKETS // 2 - 1]
    heads = table.shape[1]
    rel = jnp.concatenate([jnp.arange(0, tile), jnp.arange(-2 * tile, 0)])
    onehot = _t5_bucket(rel)[None, :, None] == jnp.arange(NUM_BUCKETS)[None, None, :]
    per_rel = jnp.sum(jnp.where(onehot, table.T[:, None, :], 0.0), axis=-1)
    per_rel = (per_rel - far[:, None]) * LOG2E
    span = 3 * tile
    flat = jnp.tile(per_rel, (1, 2 * tile))[:, :2 * tile * (span - 1)]
    toep = flat.reshape(heads, 2 * tile, span - 1)[:, :, :tile]
    r = jnp.arange(tile)[:, None]
    c = jnp.arange(tile)[None, :]
    diag = jnp.where(((c // CHUNK) <= (r // CHUNK))[None], toep[:, :tile], MASK_VALUE)
    left = toep[:, tile:]
    return jnp.stack([left, diag], axis=1)


def _attn_kernel(q_ref, k_ref, v_ref, bias_ref, lamv_ref, subln_ref, o_ref,
                 s_buf, p_buf, alpha_buf, m_all, acc_all, *, tile, lam_init):
    seq = q_ref.shape[0]
    nq = seq // tile

    s_buf[...] = jnp.zeros(s_buf.shape, F32)
    p_buf[...] = jnp.zeros(p_buf.shape, BF16)
    alpha_buf[...] = jnp.zeros(alpha_buf.shape, F32)
    m_all[...] = jnp.full(m_all.shape, MASK_VALUE, F32)
    acc_all[...] = jnp.zeros(acc_all.shape, F32)

    lane = lax.broadcasted_iota(jnp.int32, (tile, LANES), 1)

    def stage1(i, j, slot):
        q = q_ref[pl.ds(pl.multiple_of(i * tile, tile), tile), :]
        zero = jnp.zeros_like(q)
        qs = jnp.concatenate([jnp.where(lane < LANES // 2, q, zero),
                              jnp.where(lane >= LANES // 2, q, zero)], axis=0)
        kj = k_ref[pl.ds(pl.multiple_of(j * tile, tile), tile), :]
        s_buf[slot] = _dot_nt(qs, kj)

    def stage2(i, bias, slot):
        for half in range(2):
            rows = slice(half * tile, (half + 1) * tile)
            s = s_buf[slot, rows, :]
            if bias is not None:
                s = s + bias
            m_prev = m_all[i, rows, :]
            m_next = jnp.maximum(m_prev, jnp.max(s, axis=-1, keepdims=True))
            p = jnp.exp2(s - jnp.concatenate([m_next] * (tile // LANES), axis=1))
            p_buf[slot, rows, :] = p.astype(BF16)
            alpha_buf[slot, rows, :] = jnp.exp2(m_prev - m_next)
            m_all[i, rows, :] = m_next

    def stage3(i, j, slot):
        vj = v_ref[pl.ds(pl.multiple_of(j * tile, tile), tile), :]
        acc_all[i] = alpha_buf[slot] * acc_all[i] + _dot(p_buf[slot], vj)

    def run_pipeline(n_pairs, first_pair, advance, with_bias, unroll):
        n_steps = pl.cdiv(n_pairs + 2, unroll) * unroll

        def step(t, pairs, slot_a, slot_b):
            (i0, j0), (i1, j1), (i2, j2) = pairs
            valid1 = jnp.logical_and(t >= 1, t <= n_pairs)
            valid2 = jnp.logical_and(t >= 2, t <= n_pairs + 1)
            stage3(jnp.where(valid2, i2, nq), j2, slot_a)
            stage2(jnp.where(valid1, i1, nq), bias_ref[0, j1 - i1 + 1] if with_bias else None, slot_b)
            stage1(i0, j0, slot_a)
            return (advance(i0, j0), (i0, j0), (i1, j1))

        def body(u, pairs):
            for r in range(unroll):
                pairs = step(unroll * u + r, pairs, r % 2, (r + 1) % 2)
            return pairs

        lax.fori_loop(0, n_steps // unroll, body, (first_pair,) * 3)

    def next_far(i, j):
        wrap = j == i - 2
        done = jnp.logical_and(wrap, i == nq - 1)
        step_i = jnp.logical_and(wrap, jnp.logical_not(done))
        return (jnp.where(step_i, i + 1, i), jnp.where(done, j, jnp.where(wrap, 0, j + 1)))

    def next_near(i, j):
        wrap = j == i
        done = jnp.logical_and(wrap, i == nq - 1)
        step_i = jnp.logical_and(wrap, jnp.logical_not(done))
        return (jnp.where(step_i, i + 1, i), jnp.where(jnp.logical_or(done, wrap), j, j + 1))

    zero = jnp.int32(0)
    if nq > 2:
        run_pipeline((nq - 1) * (nq - 2) // 2, (jnp.int32(2), zero), next_far, False, ATT_UNROLL_FAR)
    run_pipeline(2 * nq - 1, (zero, zero), next_near, True, ATT_UNROLL_NEAR)

    lv = lamv_ref[...]
    lam = (jnp.exp(jnp.sum(lv[0:1] * lv[1:2], axis=-1, keepdims=True))
           - jnp.exp(jnp.sum(lv[2:3] * lv[3:4], axis=-1, keepdims=True)) + lam_init)

    def finish(i):
        acc = acc_all[i]
        a1 = acc[0:tile]
        a2 = acc[tile:2 * tile]
        ones_col = lane == DIFF_DV
        ratio = jnp.sum(jnp.where(ones_col, a1 / jnp.where(ones_col, a2, 1.0), 0.0),
                        axis=-1, keepdims=True)
        u = a1 - lam * (ratio * a2)
        sq = jnp.where(ones_col, a1 * math.sqrt(DIFF_DV * EPS), u)
        ms = jnp.sum(sq * sq, axis=-1, keepdims=True) * (1.0 / DIFF_DV)
        y = u * lax.rsqrt(ms) * subln_ref[...] * (1.0 - lam_init)
        o_ref[pl.ds(pl.multiple_of(i * tile, tile), tile), :] = y.astype(BF16)

    group = math.gcd(nq, ATT_FINISH_GROUP)

    def finish_group(u, carry):
        for r in range(group):
            finish(u * group + r)
        return carry

    lax.fori_loop(0, nq // group, finish_group, 0)


def _attn(dq, dk, dv, bias_tiles, lamv, subln, batch, seq, lam_init):
    tile = min(ATT_TILE, seq)
    nq = seq // tile
    head_block = pl.BlockSpec((seq, LANES), lambda b, h: (b, h))
    return pl.pallas_call(
        functools.partial(_attn_kernel, tile=tile, lam_init=lam_init),
        out_shape=jax.ShapeDtypeStruct((batch * seq, DIFF_LANES), BF16),
        grid=(batch, DIFF_HEADS),
        in_specs=[
            head_block, head_block, head_block,
            pl.BlockSpec((1, 2, tile, tile), lambda b, h: (h, 0, 0, 0)),
            pl.BlockSpec((4, LANES), lambda b, h: (0, 0)),
            pl.BlockSpec((1, LANES), lambda b, h: (0, 0)),
        ],
        out_specs=head_block,
        scratch_shapes=[pltpu.VMEM((2, 2 * tile, tile), F32),
                        pltpu.VMEM((2, 2 * tile, tile), BF16),
                        pltpu.VMEM((2, 2 * tile, LANES), F32),
                        pltpu.VMEM((nq + 1, 2 * tile, LANES), F32),
                        pltpu.VMEM((nq + 1, 2 * tile, LANES), F32)],
        compiler_params=pltpu.CompilerParams(
            dimension_semantics=("parallel", "parallel"), vmem_limit_bytes=VMEM_LIMIT_BYTES),
        name="diff_attn",
    )(dq, dk, dv, bias_tiles, lamv, subln)


def _pad_cols(w, width):
    return jnp.pad(w, ((0, 0), (0, width - w.shape[1])))


def _slot_cols(w, heads, used, slot):
    lead = w.shape[0]
    w = w.reshape(lead, heads, used)
    return jnp.pad(w, ((0, 0), (0, 0), (0, slot - used))).reshape(lead, heads * slot)


def _layout_w_in(w_in):
    offs = np.cumsum([0, GLA_HEADS * GLA_DK, GLA_HEADS * GLA_DK, GLA_WIDTH, GLA_RANK, GLA_WIDTH,
                      DIFF_HEADS * 2 * DIFF_DQK, DIFF_HEADS * 2 * DIFF_DQK, DIFF_WIDTH,
                      CONV_DIM, CONV_DIM, CONV_DIM])
    gq, gk, gv, glr, gate, dq, dk, dv, cb, cc, ch = [w_in[:, offs[n]:offs[n + 1]] for n in range(11)]
    half = LANES // 2
    gq_slots = _slot_cols(gq, GLA_HEADS, GLA_DK, HEAD_SLOT)
    gq_slots = gq_slots.at[:, GLR_LANE:GLR_LANE + GLA_RANK].set(glr)
    cols = [
        gq_slots,
        _slot_cols(gk, GLA_HEADS, GLA_DK, HEAD_SLOT),
        gv,
        gate,
        _slot_cols(dq, 2 * DIFF_HEADS, DIFF_DQK, half),
        _slot_cols(dk, 2 * DIFF_HEADS, DIFF_DQK, half),
        _slot_cols(dv, DIFF_HEADS, DIFF_DV, LANES),
        cb, cc, ch,
    ]
    w = jnp.concatenate(cols, axis=1)
    assert w.shape[1] == N_PROJ
    return w.astype(BF16)


def kernel(x, ffn1_norm, ffn1_gate, ffn1_up, ffn1_down, mix_norm, w_in, gla_gk_up, gla_gk_bias,
           gla_norm, diff_lambda_q1, diff_lambda_k1, diff_lambda_q2, diff_lambda_k2, diff_subln,
           rel_bias, conv_w, w_out, ffn2_norm, ffn2_gate, ffn2_up, ffn2_down, final_norm):
    batch, seq, _ = x.shape
    depth = w_in.shape[0]
    t = batch * seq
    xf = x.reshape(t, D_MODEL)
    bias_tiles = _bias_tiles(rel_bias, min(ATT_TILE, seq))
    final_g = final_norm.reshape(1, D_MODEL)
    ffn1_w = (ffn1_gate[0].astype(BF16), ffn1_up[0].astype(BF16), ffn1_down[0].astype(BF16))

    for l in range(depth):
        lam_init = 0.8 - 0.6 * math.exp(-0.3 * l)
        xf, ffn2_w = _ffn(xf, ffn1_norm[l].reshape(1, D_MODEL), *ffn1_w, final_g, False,
                          cast=((ffn2_gate, ffn2_up, ffn2_down), l))

        w_pad = _layout_w_in(w_in[l])
        up = _slot_cols(gla_gk_up[l], GLA_HEADS, GLA_DK, HEAD_SLOT)
        up = jnp.pad(up, ((GLR_LANE, GLA_QK_LANES - GLR_LANE - GLA_RANK), (0, 0)))
        up_hi = up.astype(BF16)
        up_lo = (up - up_hi.astype(F32)).astype(BF16)
        gkup = jnp.stack([up_hi, up_lo], axis=0)
        gkb = _slot_cols(gla_gk_bias[l].reshape(1, -1), GLA_HEADS, GLA_DK, HEAD_SLOT)
        cw = jnp.pad(conv_w[l], ((0, 8 - CONV_WIDTH), (0, 0)))
        gq, gk, gg, gv, gate, dq, dk, dv, conv_o = _proj(
            xf, mix_norm[l].reshape(1, D_MODEL), w_pad, gkup, gkb, cw, seq)

        gn_tiled = jnp.tile(gla_norm[l], GLA_HEADS).reshape(1, GLA_WIDTH)
        gla_o = _gla(gq, gk, gg, gv, gate, gn_tiled, batch, seq)

        lamv = jnp.stack([diff_lambda_q1[l], diff_lambda_k1[l], diff_lambda_q2[l], diff_lambda_k2[l]])
        lamv = _pad_cols(lamv.astype(F32), LANES)
        subln = _pad_cols(diff_subln[l].reshape(1, DIFF_DV), LANES)
        diff_o = _attn(dq, dk, dv, bias_tiles, lamv, subln, batch, seq, lam_init)

        wo = w_out[l]
        w1 = wo[0:GLA_WIDTH].astype(BF16)
        w2 = wo[GLA_WIDTH:GLA_WIDTH + DIFF_WIDTH].reshape(DIFF_HEADS, DIFF_DV, D_MODEL)
        w2 = jnp.pad(w2, ((0, 0), (0, LANES - DIFF_DV), (0, 0))).reshape(DIFF_LANES, D_MODEL).astype(BF16)
        w3 = wo[GLA_WIDTH + DIFF_WIDTH:].astype(BF16)
        last = l == depth - 1
        xf, ffn1_w = _ffn(xf, ffn2_norm[l].reshape(1, D_MODEL), *ffn2_w, final_g, last,
                          mix=(gla_o, diff_o, conv_o, w1, w2, w3),
                          cast=None if last else ((ffn1_gate, ffn1_up, ffn1_down), l + 1))

    return xf.reshape(batch, seq, D_MODEL)
```

```python
import functools
import math

import numpy as np
import jax
import jax.numpy as jnp
from jax import lax
from jax.experimental import pallas as pl
from jax.experimental.pallas import tpu as pltpu

F32 = jnp.float32
BF16 = jnp.bfloat16

D_MODEL = 1024
D_FF = 2816
EPS = 1e-6
CHUNK = 64
GLA_HEADS = 4
GLA_DK = 48
GLA_DV = 96
GLA_RANK = 16
GLA_GATE_NORM = 16.0
DIFF_HEADS = 4
DIFF_DQK = 48
DIFF_DV = 96
CONV_DIM = 256
CONV_WIDTH = 3
NUM_BUCKETS = 32
MAX_DISTANCE = 128
GLA_WIDTH = GLA_HEADS * GLA_DV
DIFF_WIDTH = DIFF_HEADS * DIFF_DV

LANES = 128
BF16_SUBLANES = 16
VMEM_LIMIT_BYTES = 56 * 1024 * 1024

HEAD_SLOT = 64
GLA_QK_LANES = GLA_HEADS * HEAD_SLOT
SUB = 16
N_SUB = CHUNK // SUB
KST_ROWS = SUB * (N_SUB * (N_SUB - 1) // 2)
DIFF_LANES = DIFF_HEADS * LANES
MASK_VALUE = -1e30
GLA_SAFE_DECAY = 60.0
LOG2E = math.log2(math.e)

OFF_GQ = 0
OFF_GK = OFF_GQ + GLA_QK_LANES
OFF_GV = OFF_GK + GLA_QK_LANES
OFF_GATE = OFF_GV + GLA_WIDTH
OFF_DQ = OFF_GATE + GLA_WIDTH
OFF_DK = OFF_DQ + DIFF_LANES
OFF_DV = OFF_DK + DIFF_LANES
OFF_CB = OFF_DV + DIFF_LANES
OFF_CC = OFF_CB + CONV_DIM
OFF_CH = OFF_CC + CONV_DIM
N_PROJ = OFF_CH + CONV_DIM
GLR_LANE = GLA_DK
assert GLR_LANE + GLA_RANK <= HEAD_SLOT

FFN_ROWS = 512
MXU_DEPTH = 256
FFN_F_SPLITS = (0, 6 * MXU_DEPTH, D_FF)
assert all((hi - lo) % MXU_DEPTH == 0 for lo, hi in zip(FFN_F_SPLITS[:-1], FFN_F_SPLITS[1:]))
PROJ_ROWS = 512
GLA_ROWS = 512
ATT_TILE = 256
ATT_UNROLL_FAR = 32
ATT_UNROLL_NEAR = 8
ATT_FINISH_GROUP = 4


def _dot(a, b):
    return jnp.dot(a, b, preferred_element_type=F32)


def _dot_nt(a, b):
    return lax.dot_general(a, b, (((1,), (1,)), ((), ())), preferred_element_type=F32)


def _dot_tn(a, b):
    return lax.dot_general(a, b, (((0,), (0,)), ((), ())), preferred_element_type=F32)


def _split_bf16(x):
    hi = x.astype(BF16)
    lo = (x - hi.astype(F32)).astype(BF16)
    return hi, lo


def _rms(x, g):
    return x * lax.rsqrt(jnp.mean(x * x, axis=-1, keepdims=True) + EPS) * g


def _sigmoid(x):
    return 1.0 / (1.0 + jnp.exp(-x))


def _ffn_kernel(*refs, final, mix, cast):
    refs = list(refs)
    x_ref, g_ref, wg_ref, wu_ref, wd_ref, fg_ref = refs[:6]
    mix_refs = refs[6:12] if mix else ()
    n_in = 6 + len(mix_refs)
    src_refs = refs[n_in:n_in + 3] if cast else ()
    o_ref = refs[n_in + len(src_refs)]
    dst_refs = refs[n_in + len(src_refs) + 1:]
    for src, dst in zip(src_refs, dst_refs):
        dst[...] = src[...].astype(BF16)

    x = x_ref[...]
    if mix:
        gla_ref, diff_ref, conv_ref, w1_ref, w2_ref, w3_ref = mix_refs
        x = (x + _dot(gla_ref[...], w1_ref[...]) + _dot(diff_ref[...], w2_ref[...])
             + _dot(conv_ref[...], w3_ref[...]))
    h = _rms(x, g_ref[...]).astype(BF16)
    acc = None
    for lo, hi in zip(FFN_F_SPLITS[:-1], FFN_F_SPLITS[1:]):
        sl = slice(lo, hi)
        gate = _dot(h, wg_ref[:, sl])
        up = _dot(h, wu_ref[:, sl])
        a = (gate * _sigmoid(gate) * up).astype(BF16)
        part = _dot(a, wd_ref[sl, :])
        acc = part if acc is None else acc + part
    y = x + 0.5 * acc
    if final:
        y = _rms(y, fg_ref[...])
    o_ref[...] = y


def _const_spec(shape):
    nd = len(shape)
    return pl.BlockSpec(shape, lambda *_: (0,) * nd, pipeline_mode=pl.Buffered(1))


def _cast_block(n_rows, steps):
    share = 1
    while (n_rows * share) % (steps * BF16_SUBLANES) or steps % share:
        share *= 2
        assert share <= steps
    return n_rows * share // steps, share


def _ffn(x, g, wg, wu, wd, final_g, final, mix=None, cast=None):
    t = x.shape[0]
    tm = min(FFN_ROWS, t)
    steps = t // tm
    row = lambda width: pl.BlockSpec((tm, width), lambda i: (i, 0))
    operands = [x, g, wg, wu, wd, final_g]
    in_specs = [row(D_MODEL), _const_spec((1, D_MODEL)), _const_spec((D_MODEL, D_FF)),
                _const_spec((D_MODEL, D_FF)), _const_spec((D_FF, D_MODEL)), _const_spec((1, D_MODEL))]
    out_shapes = [jax.ShapeDtypeStruct((t, D_MODEL), F32)]
    out_specs = [row(D_MODEL)]
    if mix is not None:
        operands += list(mix)
        in_specs += [row(a.shape[1]) for a in mix[:3]] + [_const_spec(w.shape) for w in mix[3:]]
    if cast is not None:
        stacked, layer = cast
        for w in stacked:
            rows, share = _cast_block(w.shape[1], steps)
            operands.append(w)
            in_specs.append(pl.BlockSpec((None, rows, w.shape[2]),
                                         lambda i, share=share: (layer, i // share, 0)))
            out_shapes.append(jax.ShapeDtypeStruct(w.shape[1:], BF16))
            out_specs.append(pl.BlockSpec((rows, w.shape[2]), lambda i, share=share: (i // share, 0)))
    outs = pl.pallas_call(
        functools.partial(_ffn_kernel, final=final, mix=mix is not None, cast=cast is not None),
        out_shape=tuple(out_shapes),
        grid=(steps,),
        in_specs=in_specs,
        out_specs=tuple(out_specs),
        compiler_params=pltpu.CompilerParams(
            dimension_semantics=("arbitrary",), vmem_limit_bytes=VMEM_LIMIT_BYTES),
        name="ffn",
    )(*operands)
    return outs[0], tuple(outs[1:])


def _proj_kernel(x_ref, g_ref, w_ref, gkup_ref, gkb_ref, cw_ref,
                 gq_ref, gk_ref, gg_ref, gv_ref, gate_ref, dq_ref, dk_ref, dv_ref, conv_ref,
                 zbuf_ref, *, tiles_per_seq):
    tm = x_ref.shape[0]
    h = _rms(x_ref[...], g_ref[...]).astype(BF16)

    def proj(off, width):
        return _dot(h, w_ref[:, off:off + width])

    qk = proj(OFF_GQ, 2 * GLA_QK_LANES)
    gq = qk[:, 0:GLA_QK_LANES]
    gq_ref[...] = gq
    gk_ref[...] = qk[:, GLA_QK_LANES:]
    vg = proj(OFF_GV, 2 * GLA_WIDTH)
    gv_ref[...] = vg[:, 0:GLA_WIDTH].astype(BF16)
    gate_ref[...] = vg[:, GLA_WIDTH:]

    glr_hi, glr_lo = _split_bf16(gq)
    up_hi = gkup_ref[0]
    up_lo = gkup_ref[1]
    z = _dot(glr_hi, up_hi) + _dot(glr_lo, up_hi) + _dot(glr_hi, up_lo) + gkb_ref[...]
    logsig = jnp.minimum(z, 0.0) - jnp.log1p(jnp.exp(-jnp.abs(z)))
    lane = lax.broadcasted_iota(jnp.int32, (tm, GLA_QK_LANES), 1)
    gg_ref[...] = jnp.where(lane % HEAD_SLOT < GLA_DK, logsig * (1.0 / GLA_GATE_NORM), 0.0)

    dqk = proj(OFF_DQ, 2 * DIFF_LANES)
    dq_ref[...] = (dqk[:, 0:DIFF_LANES] * (DIFF_DQK ** -0.5 * LOG2E)).astype(BF16)
    dk_ref[...] = dqk[:, DIFF_LANES:].astype(BF16)
    lane_v = lax.broadcasted_iota(jnp.int32, (tm, DIFF_LANES), 1)
    dv_ref[...] = jnp.where(lane_v % LANES == DIFF_DV, 1.0, proj(OFF_DV, DIFF_LANES)).astype(BF16)

    @pl.when(pl.program_id(0) % tiles_per_seq == 0)
    def _():
        zbuf_ref[0:8, :] = jnp.zeros((8, CONV_DIM), F32)

    conv = proj(OFF_CB, 3 * CONV_DIM)
    zc = conv[:, CONV_DIM:2 * CONV_DIM] * conv[:, 2 * CONV_DIM:]
    zbuf_ref[8:8 + tm, :] = zc
    cw = cw_ref[...]
    y = (cw[2:3, :] * zc + cw[1:2, :] * zbuf_ref[7:7 + tm, :] + cw[0:1, :] * zbuf_ref[6:6 + tm, :])
    conv_ref[...] = (conv[:, 0:CONV_DIM] * y).astype(BF16)
    zbuf_ref[0:8, :] = zbuf_ref[tm:tm + 8, :]


def _proj(x, g, w, gkup, gkb, cw, seq):
    t = x.shape[0]
    tm = min(PROJ_ROWS, seq)
    row = lambda width: pl.BlockSpec((tm, width), lambda i: (i, 0))
    out_shapes = (
        jax.ShapeDtypeStruct((t, GLA_QK_LANES), F32),
        jax.ShapeDtypeStruct((t, GLA_QK_LANES), F32),
        jax.ShapeDtypeStruct((t, GLA_QK_LANES), F32),
        jax.ShapeDtypeStruct((t, GLA_WIDTH), BF16),
        jax.ShapeDtypeStruct((t, GLA_WIDTH), F32),
        jax.ShapeDtypeStruct((t, DIFF_LANES), BF16),
        jax.ShapeDtypeStruct((t, DIFF_LANES), BF16),
        jax.ShapeDtypeStruct((t, DIFF_LANES), BF16),
        jax.ShapeDtypeStruct((t, CONV_DIM), BF16),
    )
    return pl.pallas_call(
        functools.partial(_proj_kernel, tiles_per_seq=seq // tm),
        out_shape=out_shapes,
        grid=(t // tm,),
        in_specs=[
            row(D_MODEL),
            _const_spec((1, D_MODEL)),
            _const_spec((D_MODEL, N_PROJ)),
            _const_spec((2, GLA_QK_LANES, GLA_QK_LANES)),
            _const_spec((1, GLA_QK_LANES)),
            _const_spec((8, CONV_DIM)),
        ],
        out_specs=tuple(row(s.shape[1]) for s in out_shapes),
        scratch_shapes=[pltpu.VMEM((tm + 8, CONV_DIM), F32)],
        compiler_params=pltpu.CompilerParams(
            dimension_semantics=("arbitrary",), vmem_limit_bytes=VMEM_LIMIT_BYTES),
        name="mixer_proj",
    )(x, g, w, gkup, gkb, cw)


def _gla_cum_matrix():
    t = np.arange(CHUNK)[:, None]
    u = np.arange(CHUNK)[None, :]
    tri = (u <= t)
    ref = (u <= (t // SUB) * SUB - 1)
    ones = np.ones((CHUNK, CHUNK), bool)
    parts = [tri, ref, ones]
    for j in range(1, N_SUB):
        s = np.arange(SUB * j)[:, None]
        parts.append((u > s) & (u <= SUB * j - 1))
    return np.concatenate(parts, axis=0).astype(np.float32)


def _gla_score_mask():
    t = np.arange(CHUNK)[:, None]
    m = np.arange(KST_ROWS)
    group = np.concatenate([np.full(SUB * j, j) for j in range(1, N_SUB)])
    keep = (group[None, :] == (t // SUB))
    return np.tile(keep, (1, GLA_HEADS)).astype(np.float32)


def _gla_kernel(q_ref, k_ref, g_ref, v_ref, gate_ref, cum_ref, cumf_ref, amask_ref, gn_ref,
                o_ref, s_ref, raw_ref):
    rows = q_ref.shape[0]
    n_chunks = rows // CHUNK

    @pl.when(pl.program_id(1) == 0)
    def _():
        s_ref[...] = jnp.zeros_like(s_ref)

    qk_head = lax.broadcasted_iota(jnp.int32, (1, GLA_QK_LANES), 1) // HEAD_SLOT
    v_head = lax.broadcasted_iota(jnp.int32, (1, GLA_WIDTH), 1) // GLA_DV
    row_head_v = lax.broadcasted_iota(jnp.int32, (GLA_WIDTH, 1), 0) // GLA_DV
    state_mask = row_head_v == qk_head
    scale = GLA_DK ** -0.5

    safe = jnp.min(g_ref[...]) * CHUNK >= -GLA_SAFE_DECAY

    def fast_path():
        chunk = lambda x, c: x[c * CHUNK:(c + 1) * CHUNK]
        sel_r = lax.broadcasted_iota(jnp.int32, (8, rows), 0)
        sel_c = lax.broadcasted_iota(jnp.int32, (8, rows), 1) // CHUNK
        chunk_sel = jnp.where(sel_r == sel_c, 1.0, 0.0).astype(BF16)
        gs_hi, gs_lo = _split_bf16(g_ref[...])
        totals = _dot(chunk_sel, gs_hi) + _dot(chunk_sel, gs_lo)
        tri = cumf_ref[...]
        b = jnp.concatenate([_dot(tri, chunk(gs_hi, c)) + _dot(tri, chunk(gs_lo, c))
                             for c in range(n_chunks)], axis=0)
        b_last = jnp.concatenate([jnp.broadcast_to(totals[c:c + 1], (CHUNK, GLA_QK_LANES))
                                  for c in range(n_chunks)], axis=0)
        q = q_ref[...] * scale
        k = k_ref[...]
        v = v_ref[...]
        qd = (q * jnp.exp(b)).astype(BF16)
        kd = (k * jnp.exp(-b)).astype(BF16)
        kw = (k * jnp.exp(b_last - b)).astype(BF16)
        t_idx = lax.broadcasted_iota(jnp.int32, (CHUNK, GLA_QK_LANES), 0)
        s_idx = lax.broadcasted_iota(jnp.int32, (CHUNK, GLA_QK_LANES), 1) % HEAD_SLOT
        causal = s_idx <= t_idx
        intra, update = [], []
        for c in range(n_chunks):
            kd_c, v_c = chunk(kd, c), chunk(v, c)
            kd_bd = jnp.concatenate(
                [jnp.where(qk_head == hd, kd_c, jnp.zeros_like(kd_c)) for hd in range(GLA_HEADS)],
                axis=0)
            a = _dot_nt(chunk(qd, c), kd_bd)
            a = jnp.where(causal, a, 0.0).astype(BF16)
            v_bd = jnp.concatenate(
                [jnp.where(v_head == hd, v_c, jnp.zeros_like(v_c)) for hd in range(GLA_HEADS)],
                axis=0)
            intra.append(_dot(a, v_bd))
            update.append(_dot_tn(v_c, chunk(kw, c)))
        state = s_ref[...]
        for c in range(n_chunks):
            raw_ref[c * CHUNK:(c + 1) * CHUNK, :] = (
                intra[c] + _dot_nt(chunk(qd, c), state.astype(BF16)))
            state = state * jnp.exp(totals[c:c + 1]) + jnp.where(state_mask, update[c], 0.0)
        s_ref[...] = state

    @pl.when(safe)
    def _():
        fast_path()

    def general_path():
        cum = cum_ref[...]
        amask = amask_ref[...] > 0.5
        ind_r = lax.broadcasted_iota(jnp.int32, (GLA_QK_LANES, LANES), 0) // HEAD_SLOT
        ind_c = lax.broadcasted_iota(jnp.int32, (GLA_QK_LANES, LANES), 1)
        ind_sum = jnp.where(ind_r == ind_c, 1.0, 0.0).astype(BF16)
        bc_r = lax.broadcasted_iota(jnp.int32, (LANES, GLA_WIDTH), 0)
        bc_c = lax.broadcasted_iota(jnp.int32, (LANES, GLA_WIDTH), 1) // GLA_DV
        ind_bcast = jnp.where(bc_r == bc_c, 1.0, 0.0).astype(BF16)
        row_in_sub = lax.broadcasted_iota(jnp.int32, (CHUNK, 1), 0) % SUB

        def chunk_body(c, carry):
            r0 = pl.multiple_of(c * CHUNK, CHUNK)
            q = q_ref[pl.ds(r0, CHUNK), :] * scale
            k = k_ref[pl.ds(r0, CHUNK), :]
            g = g_ref[pl.ds(r0, CHUNK), :]
            v = v_ref[pl.ds(r0, CHUNK), :]
            g_hi, g_lo = _split_bf16(g)
            cums = _dot(cum, g_hi) + _dot(cum, g_lo)
            b = cums[0:CHUNK]
            b_ref = cums[CHUNK:2 * CHUNK]
            b_last = cums[2 * CHUNK:3 * CHUNK]
            e_kst = cums[3 * CHUNK:3 * CHUNK + KST_ROWS]

            state = s_ref[...]
            q_in = (q * jnp.exp(b)).astype(BF16)
            o = _dot_nt(q_in, state.astype(BF16))

            kw = (k * jnp.exp(b_last - b)).astype(BF16)
            decay = jnp.exp(jnp.concatenate([b_last] * (GLA_WIDTH // CHUNK), axis=0))
            s_ref[...] = state * decay + jnp.where(state_mask, _dot_tn(v, kw), 0.0)

            q_sub = (q * jnp.exp(jnp.minimum(b - b_ref, 0.0))).astype(BF16)
            k_st = jnp.concatenate([k[0:SUB * j] for j in range(1, N_SUB)], axis=0) * jnp.exp(e_kst)
            k_bd = jnp.concatenate(
                [jnp.where(qk_head == hd, k_st, 0.0) for hd in range(GLA_HEADS)], axis=0).astype(BF16)
            a = _dot_nt(q_sub, k_bd)
            a = jnp.where(amask, a, 0.0).astype(BF16)
            v_st = jnp.concatenate([v[0:SUB * j] for j in range(1, N_SUB)], axis=0)
            v_bd = jnp.concatenate(
                [jnp.where(v_head == hd, v_st, jnp.zeros_like(v_st)) for hd in range(GLA_HEADS)], axis=0)
            o = o + _dot(a, v_bd)

            vf = v.astype(F32)
            prods = []
            for d in range(SUB):
                ks = k if d == 0 else pltpu.roll(k, d, 0)
                bs = b if d == 0 else pltpu.roll(b, d, 0)
                prods.append((q * ks * jnp.exp(jnp.minimum(b - bs, 0.0))).astype(BF16))
            dsum = _dot(jnp.concatenate(prods, axis=0), ind_sum)
            valid = jnp.concatenate([row_in_sub >= d for d in range(SUB)], axis=0)
            dsum = jnp.where(valid, dsum, 0.0).astype(BF16)
            dbc = _dot(dsum, ind_bcast)
            for d in range(SUB):
                vs = vf if d == 0 else pltpu.roll(vf, d, 0)
                o = o + dbc[d * CHUNK:(d + 1) * CHUNK] * vs
            raw_ref[pl.ds(r0, CHUNK), :] = o
            return carry

        lax.fori_loop(0, n_chunks, chunk_body, 0)

    @pl.when(jnp.logical_not(safe))
    def _():
        general_path()

    o = raw_ref[...]
    hr = lax.broadcasted_iota(jnp.int32, (GLA_WIDTH, GLA_WIDTH), 0) // GLA_DV
    hc = lax.broadcasted_iota(jnp.int32, (GLA_WIDTH, GLA_WIDTH), 1) // GLA_DV
    head_ones = jnp.where(hr == hc, 1.0, 0.0).astype(BF16)
    ms = _dot((o * o).astype(BF16), head_ones) * (1.0 / GLA_DV)
    gate = gate_ref[...]
    y = o * lax.rsqrt(ms + EPS) * gn_ref[...] * (gate * _sigmoid(gate))
    o_ref[...] = y.astype(BF16)


def _gla(gq, gk, gg, gv, gate, gn_tiled, batch, seq):
    rows = min(GLA_ROWS, seq)
    steps = seq // rows
    row = lambda width: pl.BlockSpec((rows, width), lambda b, i: (b * steps + i, 0))
    assert rows // CHUNK <= 8
    cum_np = _gla_cum_matrix()
    cum = jnp.asarray(cum_np, BF16)
    cumf = jnp.asarray(cum_np[0:CHUNK], BF16)
    amask = jnp.asarray(_gla_score_mask(), F32)
    return pl.pallas_call(
        _gla_kernel,
        out_shape=jax.ShapeDtypeStruct((batch * seq, GLA_WIDTH), BF16),
        grid=(batch, steps),
        in_specs=[
            row(GLA_QK_LANES), row(GLA_QK_LANES), row(GLA_QK_LANES), row(GLA_WIDTH), row(GLA_WIDTH),
            pl.BlockSpec(cum.shape, lambda b, i: (0, 0)),
            pl.BlockSpec(cumf.shape, lambda b, i: (0, 0)),
            pl.BlockSpec(amask.shape, lambda b, i: (0, 0)),
            pl.BlockSpec((1, GLA_WIDTH), lambda b, i: (0, 0)),
        ],
        out_specs=row(GLA_WIDTH),
        scratch_shapes=[pltpu.VMEM((GLA_WIDTH, GLA_QK_LANES), F32),
                        pltpu.VMEM((rows, GLA_WIDTH), F32)],
        compiler_params=pltpu.CompilerParams(
            dimension_semantics=("arbitrary", "arbitrary"), vmem_limit_bytes=VMEM_LIMIT_BYTES),
        name="gla",
    )(gq, gk, gg, gv, gate, cum, cumf, amask, gn_tiled)


def _t5_bucket(rel):
    nb = NUM_BUCKETS // 2
    max_exact = nb // 2
    ret = (rel > 0).astype(jnp.int32) * nb
    n = jnp.abs(rel)
    nf = jnp.maximum(n, 1).astype(jnp.float32)
    large = max_exact + (jnp.log(nf / max_exact) / math.log(MAX_DISTANCE / max_exact)
                         * (nb - max_exact)).astype(jnp.int32)
    large = jnp.minimum(large, nb - 1)
    return ret + jnp.where(n < max_exact, n, large)


def _bias_tiles(rel_bias, tile):
    assert tile >= MAX_DISTANCE and tile % CHUNK == 0
    table = rel_bias.astype(F32)
    far = table[NUM_BUCKETS // 2 - 1]
    heads = table.shape[1]
    rel = jnp.concatenate([jnp.arange(0, tile), jnp.arange(-2 * tile, 0)])
    onehot = _t5_bucket(rel)[None, :, None] == jnp.arange(NUM_BUCKETS)[None, None, :]
    per_rel = jnp.sum(jnp.where(onehot, table.T[:, None, :], 0.0), axis=-1)
    per_rel = (per_rel - far[:, None]) * LOG2E
    span = 3 * tile
    flat = jnp.tile(per_rel, (1, 2 * tile))[:, :2 * tile * (span - 1)]
    toep = flat.reshape(heads, 2 * tile, span - 1)[:, :, :tile]
    r = jnp.arange(tile)[:, None]
    c = jnp.arange(tile)[None, :]
    diag = jnp.where(((c // CHUNK) <= (r // CHUNK))[None], toep[:, :tile], MASK_VALUE)
    left = toep[:, tile:]
    return jnp.stack([left, diag], axis=1)


def _attn_kernel(q_ref, k_ref, v_ref, bias_ref, lamv_ref, subln_ref, o_ref,
                 s_buf, p_buf, alpha_buf, m_all, acc_all, *, tile, lam_init):
    seq = q_ref.shape[0]
    nq = seq // tile

    s_buf[...] = jnp.zeros(s_buf.shape, F32)
    p_buf[...] = jnp.zeros(p_buf.shape, BF16)
    alpha_buf[...] = jnp.zeros(alpha_buf.shape, F32)
    m_all[...] = jnp.full(m_all.shape, MASK_VALUE, F32)
    acc_all[...] = jnp.zeros(acc_all.shape, F32)

    lane = lax.broadcasted_iota(jnp.int32, (tile, LANES), 1)

    def stage1(i, j, slot):
        q = q_ref[pl.ds(pl.multiple_of(i * tile, tile), tile), :]
        zero = jnp.zeros_like(q)
        qs = jnp.concatenate([jnp.where(lane < LANES // 2, q, zero),
                              jnp.where(lane >= LANES // 2, q, zero)], axis=0)
        kj = k_ref[pl.ds(pl.multiple_of(j * tile, tile), tile), :]
        s_buf[slot] = _dot_nt(qs, kj)

    def stage2(i, bias, slot):
        for half in range(2):
            rows = slice(half * tile, (half + 1) * tile)
            s = s_buf[slot, rows, :]
            if bias is not None:
                s = s + bias
            m_prev = m_all[i, rows, :]
            m_next = jnp.maximum(m_prev, jnp.max(s, axis=-1, keepdims=True))
            p = jnp.exp2(s - jnp.concatenate([m_next] * (tile // LANES), axis=1))
            p_buf[slot, rows, :] = p.astype(BF16)
            alpha_buf[slot, rows, :] = jnp.exp2(m_prev - m_next)
            m_all[i, rows, :] = m_next

    def stage3(i, j, slot):
        vj = v_ref[pl.ds(pl.multiple_of(j * tile, tile), tile), :]
        acc_all[i] = alpha_buf[slot] * acc_all[i] + _dot(p_buf[slot], vj)

    def run_pipeline(n_pairs, first_pair, advance, with_bias, unroll):
        n_steps = pl.cdiv(n_pairs + 2, unroll) * unroll

        def step(t, pairs, slot_a, slot_b):
            (i0, j0), (i1, j1), (i2, j2) = pairs
            valid1 = jnp.logical_and(t >= 1, t <= n_pairs)
            valid2 = jnp.logical_and(t >= 2, t <= n_pairs + 1)
            stage3(jnp.where(valid2, i2, nq), j2, slot_a)
            stage2(jnp.where(valid1, i1, nq), bias_ref[0, j1 - i1 + 1] if with_bias else None, slot_b)
            stage1(i0, j0, slot_a)
            return (advance(i0, j0), (i0, j0), (i1, j1))

        def body(u, pairs):
            for r in range(unroll):
                pairs = step(unroll * u + r, pairs, r % 2, (r + 1) % 2)
            return pairs

        lax.fori_loop(0, n_steps // unroll, body, (first_pair,) * 3)

    def next_far(i, j):
        wrap = j == i - 2
        done = jnp.logical_and(wrap, i == nq - 1)
        step_i = jnp.logical_and(wrap, jnp.logical_not(done))
        return (jnp.where(step_i, i + 1, i), jnp.where(done, j, jnp.where(wrap, 0, j + 1)))

    def next_near(i, j):
        wrap = j == i
        done = jnp.logical_and(wrap, i == nq - 1)
        step_i = jnp.logical_and(wrap, jnp.logical_not(done))
        return (jnp.where(step_i, i + 1, i), jnp.where(jnp.logical_or(done, wrap), j, j + 1))

    zero = jnp.int32(0)
    if nq > 2:
        run_pipeline((nq - 1) * (nq - 2) // 2, (jnp.int32(2), zero), next_far, False, ATT_UNROLL_FAR)
    run_pipeline(2 * nq - 1, (zero, zero), next_near, True, ATT_UNROLL_NEAR)

    lv = lamv_ref[...]
    lam = (jnp.exp(jnp.sum(lv[0:1] * lv[1:2], axis=-1, keepdims=True))
           - jnp.exp(jnp.sum(lv[2:3] * lv[3:4], axis=-1, keepdims=True)) + lam_init)

    def finish(i):
        acc = acc_all[i]
        a1 = acc[0:tile]
        a2 = acc[tile:2 * tile]
        ones_col = lane == DIFF_DV
        ratio = jnp.sum(jnp.where(ones_col, a1 / jnp.where(ones_col, a2, 1.0), 0.0),
                        axis=-1, keepdims=True)
        u = a1 - lam * (ratio * a2)
        sq = jnp.where(ones_col, a1 * math.sqrt(DIFF_DV * EPS), u)
        ms = jnp.sum(sq * sq, axis=-1, keepdims=True) * (1.0 / DIFF_DV)
        y = u * lax.rsqrt(ms) * subln_ref[...] * (1.0 - lam_init)
        o_ref[pl.ds(pl.multiple_of(i * tile, tile), tile), :] = y.astype(BF16)

    group = math.gcd(nq, ATT_FINISH_GROUP)

    def finish_group(u, carry):
        for r in range(group):
            finish(u * group + r)
        return carry

    lax.fori_loop(0, nq // group, finish_group, 0)


def _attn(dq, dk, dv, bias_tiles, lamv, subln, batch, seq, lam_init):
    tile = min(ATT_TILE, seq)
    nq = seq // tile
    head_block = pl.BlockSpec((seq, LANES), lambda b, h: (b, h))
    return pl.pallas_call(
        functools.partial(_attn_kernel, tile=tile, lam_init=lam_init),
        out_shape=jax.ShapeDtypeStruct((batch * seq, DIFF_LANES), BF16),
        grid=(batch, DIFF_HEADS),
        in_specs=[
            head_block, head_block, head_block,
            pl.BlockSpec((1, 2, tile, tile), lambda b, h: (h, 0, 0, 0)),
            pl.BlockSpec((4, LANES), lambda b, h: (0, 0)),
            pl.BlockSpec((1, LANES), lambda b, h: (0, 0)),
        ],
        out_specs=head_block,
        scratch_shapes=[pltpu.VMEM((2, 2 * tile, tile), F32),
                        pltpu.VMEM((2, 2 * tile, tile), BF16),
                        pltpu.VMEM((2, 2 * tile, LANES), F32),
                        pltpu.VMEM((nq + 1, 2 * tile, LANES), F32),
                        pltpu.VMEM((nq + 1, 2 * tile, LANES), F32)],
        compiler_params=pltpu.CompilerParams(
            dimension_semantics=("parallel", "parallel"), vmem_limit_bytes=VMEM_LIMIT_BYTES),
        name="diff_attn",
    )(dq, dk, dv, bias_tiles, lamv, subln)


def _pad_cols(w, width):
    return jnp.pad(w, ((0, 0), (0, width - w.shape[1])))


def _slot_cols(w, heads, used, slot):
    lead = w.shape[0]
    w = w.reshape(lead, heads, used)
    return jnp.pad(w, ((0, 0), (0, 0), (0, slot - used))).reshape(lead, heads * slot)


def _layout_w_in(w_in):
    offs = np.cumsum([0, GLA_HEADS * GLA_DK, GLA_HEADS * GLA_DK, GLA_WIDTH, GLA_RANK, GLA_WIDTH,
                      DIFF_HEADS * 2 * DIFF_DQK, DIFF_HEADS * 2 * DIFF_DQK, DIFF_WIDTH,
                      CONV_DIM, CONV_DIM, CONV_DIM])
    gq, gk, gv, glr, gate, dq, dk, dv, cb, cc, ch = [w_in[:, offs[n]:offs[n + 1]] for n in range(11)]
    half = LANES // 2
    gq_slots = _slot_cols(gq, GLA_HEADS, GLA_DK, HEAD_SLOT)
    gq_slots = gq_slots.at[:, GLR_LANE:GLR_LANE + GLA_RANK].set(glr)
    cols = [
        gq_slots,
        _slot_cols(gk, GLA_HEADS, GLA_DK, HEAD_SLOT),
        gv,
        gate,
        _slot_cols(dq, 2 * DIFF_HEADS, DIFF_DQK, half),
        _slot_cols(dk, 2 * DIFF_HEADS, DIFF_DQK, half),
        _slot_cols(dv, DIFF_HEADS, DIFF_DV, LANES),
        cb, cc, ch,
    ]
    w = jnp.concatenate(cols, axis=1)
    assert w.shape[1] == N_PROJ
    return w.astype(BF16)


def kernel(x, ffn1_norm, ffn1_gate, ffn1_up, ffn1_down, mix_norm, w_in, gla_gk_up, gla_gk_bias,
           gla_norm, diff_lambda_q1, diff_lambda_k1, diff_lambda_q2, diff_lambda_k2, diff_subln,
           rel_bias, conv_w, w_out, ffn2_norm, ffn2_gate, ffn2_up, ffn2_down, final_norm):
    batch, seq, _ = x.shape
    depth = w_in.shape[0]
    t = batch * seq
    xf = x.reshape(t, D_MODEL)
    bias_tiles = _bias_tiles(rel_bias, min(ATT_TILE, seq))
    final_g = final_norm.reshape(1, D_MODEL)
    ffn1_w = (ffn1_gate[0].astype(BF16), ffn1_up[0].astype(BF16), ffn1_down[0].astype(BF16))

    for l in range(depth):
        lam_init = 0.8 - 0.6 * math.exp(-0.3 * l)
        xf, ffn2_w = _ffn(xf, ffn1_norm[l].reshape(1, D_MODEL), *ffn1_w, final_g, False,
                          cast=((ffn2_gate, ffn2_up, ffn2_down), l))

        w_pad = _layout_w_in(w_in[l])
        up = _slot_cols(gla_gk_up[l], GLA_HEADS, GLA_DK, HEAD_SLOT)
        up = jnp.pad(up, ((GLR_LANE, GLA_QK_LANES - GLR_LANE - GLA_RANK), (0, 0)))
        up_hi = up.astype(BF16)
        up_lo = (up - up_hi.astype(F32)).astype(BF16)
        gkup = jnp.stack([up_hi, up_lo], axis=0)
        gkb = _slot_cols(gla_gk_bias[l].reshape(1, -1), GLA_HEADS, GLA_DK, HEAD_SLOT)
        cw = jnp.pad(conv_w[l], ((0, 8 - CONV_WIDTH), (0, 0)))
        gq, gk, gg, gv, gate, dq, dk, dv, conv_o = _proj(
            xf, mix_norm[l].reshape(1, D_MODEL), w_pad, gkup, gkb, cw, seq)

        gn_tiled = jnp.tile(gla_norm[l], GLA_HEADS).reshape(1, GLA_WIDTH)
        gla_o = _gla(gq, gk, gg, gv, gate, gn_tiled, batch, seq)

        lamv = jnp.stack([diff_lambda_q1[l], diff_lambda_k1[l], diff_lambda_q2[l], diff_lambda_k2[l]])
        lamv = _pad_cols(lamv.astype(F32), LANES)
        subln = _pad_cols(diff_subln[l].reshape(1, DIFF_DV), LANES)
        diff_o = _attn(dq, dk, dv, bias_tiles, lamv, subln, batch, seq, lam_init)

        wo = w_out[l]
        w1 = wo[0:GLA_WIDTH].astype(BF16)
        w2 = wo[GLA_WIDTH:GLA_WIDTH + DIFF_WIDTH].reshape(DIFF_HEADS, DIFF_DV, D_MODEL)
        w2 = jnp.pad(w2, ((0, 0), (0, LANES - DIFF_DV), (0, 0))).reshape(DIFF_LANES, D_MODEL).astype(BF16)
        w3 = wo[GLA_WIDTH + DIFF_WIDTH:].astype(BF16)
        last = l == depth - 1
        xf, ffn1_w = _ffn(xf, ffn2_norm[l].reshape(1, D_MODEL), *ffn2_w, final_g, last,
                          mix=(gla_o, diff_o, conv_o, w1, w2, w3),
                          cast=None if last else ((ffn1_gate, ffn1_up, ffn1_down), l + 1))

    return xf.reshape(batch, seq, D_MODEL)
```

```python
import functools
import math

import numpy as np
import jax
import jax.numpy as jnp
from jax import lax
from jax.experimental import pallas as pl
from jax.experimental.pallas import tpu as pltpu

F32 = jnp.float32
BF16 = jnp.bfloat16

D_MODEL = 1024
D_FF = 2816
EPS = 1e-6
CHUNK = 64
GLA_HEADS = 4
GLA_DK = 48
GLA_DV = 96
GLA_RANK = 16
GLA_GATE_NORM = 16.0
DIFF_HEADS = 4
DIFF_DQK = 48
DIFF_DV = 96
CONV_DIM = 256
CONV_WIDTH = 3
NUM_BUCKETS = 32
MAX_DISTANCE = 128
GLA_WIDTH = GLA_HEADS * GLA_DV
DIFF_WIDTH = DIFF_HEADS * DIFF_DV

LANES = 128
F32_SUBLANES = 8
BF16_SUBLANES = 16
MXU_DEPTH = 256
VMEM_LIMIT_BYTES = 56 * 1024 * 1024

HEAD_SLOT = 64
GLA_QK_LANES = GLA_HEADS * HEAD_SLOT
SUB = 16
N_SUB = CHUNK // SUB
KST_ROWS = SUB * (N_SUB * (N_SUB - 1) // 2)
DIFF_LANES = DIFF_HEADS * LANES
MASK_VALUE = -1e30
GLA_SAFE_DECAY = 60.0
LOG2E = math.log2(math.e)

OFF_GQ = 0
OFF_GK = OFF_GQ + GLA_QK_LANES
OFF_GV = OFF_GK + GLA_QK_LANES
OFF_GATE = OFF_GV + GLA_WIDTH
OFF_DQ = OFF_GATE + GLA_WIDTH
OFF_DK = OFF_DQ + DIFF_LANES
OFF_DV = OFF_DK + DIFF_LANES
OFF_CB = OFF_DV + DIFF_LANES
OFF_CC = OFF_CB + CONV_DIM
OFF_CH = OFF_CC + CONV_DIM
N_PROJ = OFF_CH + CONV_DIM
GLR_LANE = GLA_DK
assert GLR_LANE + GLA_RANK <= HEAD_SLOT

FFN_ROWS = 512
FFN_F_SPLITS = (0, 6 * MXU_DEPTH, D_FF)
assert all((hi - lo) % MXU_DEPTH == 0 for lo, hi in zip(FFN_F_SPLITS[:-1], FFN_F_SPLITS[1:]))
PROJ_ROWS = 512
ATT_TILE = 256
ATT_UNROLL_FAR = 32
ATT_UNROLL_NEAR = 8
ATT_FINISH_GROUP = 4


def _dot(a, b):
    return jnp.dot(a, b, preferred_element_type=F32)


def _dot_nt(a, b):
    return lax.dot_general(a, b, (((1,), (1,)), ((), ())), preferred_element_type=F32)


def _dot_tn(a, b):
    return lax.dot_general(a, b, (((0,), (0,)), ((), ())), preferred_element_type=F32)


def _split_bf16(x):
    hi = x.astype(BF16)
    lo = (x - hi.astype(F32)).astype(BF16)
    return hi, lo


def _rms(x, g):
    return x * lax.rsqrt(jnp.mean(x * x, axis=-1, keepdims=True) + EPS) * g


def _sigmoid(x):
    return 1.0 / (1.0 + jnp.exp(-x))


def _ffn_kernel(*refs, final, mix, cast):
    refs = list(refs)
    x_ref, g_ref, wg_ref, wu_ref, wd_ref, fg_ref = refs[:6]
    mix_refs = refs[6:12] if mix else ()
    n_in = 6 + len(mix_refs)
    src_refs = refs[n_in:n_in + 3] if cast else ()
    o_ref = refs[n_in + len(src_refs)]
    dst_refs = refs[n_in + len(src_refs) + 1:]
    for src, dst in zip(src_refs, dst_refs):
        dst[...] = src[...].astype(BF16)

    x = x_ref[...]
    if mix:
        gla_ref, diff_ref, conv_ref, w1_ref, w2_ref, w3_ref = mix_refs
        x = (x + _dot(gla_ref[...], w1_ref[...]) + _dot(diff_ref[...], w2_ref[...])
             + _dot(conv_ref[...], w3_ref[...]))
    h = _rms(x, g_ref[...]).astype(BF16)
    acc = None
    for lo, hi in zip(FFN_F_SPLITS[:-1], FFN_F_SPLITS[1:]):
        sl = slice(lo, hi)
        gate = _dot(h, wg_ref[:, sl])
        up = _dot(h, wu_ref[:, sl])
        a = (gate * _sigmoid(gate) * up).astype(BF16)
        part = _dot(a, wd_ref[sl, :])
        acc = part if acc is None else acc + part
    y = x + 0.5 * acc
    if final:
        y = _rms(y, fg_ref[...])
    o_ref[...] = y


def _const_spec(shape):
    nd = len(shape)
    return pl.BlockSpec(shape, lambda *_: (0,) * nd, pipeline_mode=pl.Buffered(1))


def _cast_block(n_rows, steps):
    share = 1
    while (n_rows * share) % (steps * BF16_SUBLANES) or steps % share:
        share *= 2
        assert share <= steps
    return n_rows * share // steps, share


def _ffn(x, g, wg, wu, wd, final_g, final, mix=None, cast=None):
    t = x.shape[0]
    tm = min(FFN_ROWS, t)
    steps = t // tm
    row = lambda width: pl.BlockSpec((tm, width), lambda i: (i, 0))
    operands = [x, g, wg, wu, wd, final_g]
    in_specs = [row(D_MODEL), _const_spec((1, D_MODEL)), _const_spec((D_MODEL, D_FF)),
                _const_spec((D_MODEL, D_FF)), _const_spec((D_FF, D_MODEL)), _const_spec((1, D_MODEL))]
    out_shapes = [jax.ShapeDtypeStruct((t, D_MODEL), F32)]
    out_specs = [row(D_MODEL)]
    if mix is not None:
        operands += list(mix)
        in_specs += [row(a.shape[1]) for a in mix[:3]] + [_const_spec(w.shape) for w in mix[3:]]
    if cast is not None:
        stacked, layer = cast
        for w in stacked:
            rows, share = _cast_block(w.shape[1], steps)
            operands.append(w)
            in_specs.append(pl.BlockSpec((None, rows, w.shape[2]),
                                         lambda i, share=share: (layer, i // share, 0)))
            out_shapes.append(jax.ShapeDtypeStruct(w.shape[1:], BF16))
            out_specs.append(pl.BlockSpec((rows, w.shape[2]), lambda i, share=share: (i // share, 0)))
    outs = pl.pallas_call(
        functools.partial(_ffn_kernel, final=final, mix=mix is not None, cast=cast is not None),
        out_shape=tuple(out_shapes),
        grid=(steps,),
        in_specs=in_specs,
        out_specs=tuple(out_specs),
        compiler_params=pltpu.CompilerParams(
            dimension_semantics=("arbitrary",), vmem_limit_bytes=VMEM_LIMIT_BYTES),
        name="ffn",
    )(*operands)
    return outs[0], tuple(outs[1:])


def _gla_cum_matrix():
    t = np.arange(CHUNK)[:, None]
    u = np.arange(CHUNK)[None, :]
    tri = (u <= t)
    ref = (u <= (t // SUB) * SUB - 1)
    ones = np.ones((CHUNK, CHUNK), bool)
    parts = [tri, ref, ones]
    for j in range(1, N_SUB):
        s = np.arange(SUB * j)[:, None]
        parts.append((u > s) & (u <= SUB * j - 1))
    return np.concatenate(parts, axis=0).astype(np.float32)


def _gla_score_mask():
    t = np.arange(CHUNK)[:, None]
    group = np.concatenate([np.full(SUB * j, j) for j in range(1, N_SUB)])
    keep = (group[None, :] == (t // SUB))
    return np.tile(keep, (1, GLA_HEADS)).astype(np.float32)


def _gla_head_ids():
    qk_head = lax.broadcasted_iota(jnp.int32, (1, GLA_QK_LANES), 1) // HEAD_SLOT
    v_head = lax.broadcasted_iota(jnp.int32, (1, GLA_WIDTH), 1) // GLA_DV
    row_head_v = lax.broadcasted_iota(jnp.int32, (GLA_WIDTH, 1), 0) // GLA_DV
    return qk_head, v_head, row_head_v == qk_head


def _gla_fast(q, k, g, v, state, tri):
    rows = q.shape[0]
    n_chunks = rows // CHUNK
    qk_head, v_head, state_mask = _gla_head_ids()
    chunk = lambda x, c: x[c * CHUNK:(c + 1) * CHUNK]
    sel_r = lax.broadcasted_iota(jnp.int32, (F32_SUBLANES, rows), 0)
    sel_c = lax.broadcasted_iota(jnp.int32, (F32_SUBLANES, rows), 1) // CHUNK
    chunk_sel = jnp.where(sel_r == sel_c, 1.0, 0.0).astype(BF16)
    g_hi, g_lo = _split_bf16(g)
    totals = _dot(chunk_sel, g_hi) + _dot(chunk_sel, g_lo)
    b = jnp.concatenate([_dot(tri, chunk(g_hi, c)) + _dot(tri, chunk(g_lo, c))
                         for c in range(n_chunks)], axis=0)
    b_last = jnp.concatenate([jnp.broadcast_to(totals[c:c + 1], (CHUNK, GLA_QK_LANES))
                              for c in range(n_chunks)], axis=0)
    qd = (q * jnp.exp(b)).astype(BF16)
    kd = (k * jnp.exp(-b)).astype(BF16)
    kw = (k * jnp.exp(b_last - b)).astype(BF16)
    t_idx = lax.broadcasted_iota(jnp.int32, (CHUNK, GLA_QK_LANES), 0)
    s_idx = lax.broadcasted_iota(jnp.int32, (CHUNK, GLA_QK_LANES), 1) % HEAD_SLOT
    causal = s_idx <= t_idx
    intra, update = [], []
    for c in range(n_chunks):
        kd_c, v_c = chunk(kd, c), chunk(v, c)
        kd_bd = jnp.concatenate(
            [jnp.where(qk_head == hd, kd_c, jnp.zeros_like(kd_c)) for hd in range(GLA_HEADS)],
            axis=0)
        a = _dot_nt(chunk(qd, c), kd_bd)
        a = jnp.where(causal, a, 0.0).astype(BF16)
        v_bd = jnp.concatenate(
            [jnp.where(v_head == hd, v_c, jnp.zeros_like(v_c)) for hd in range(GLA_HEADS)],
            axis=0)
        intra.append(_dot(a, v_bd))
        update.append(_dot_tn(v_c, chunk(kw, c)))
    raw = []
    for c in range(n_chunks):
        raw.append(intra[c] + _dot_nt(chunk(qd, c), state.astype(BF16)))
        state = state * jnp.exp(totals[c:c + 1]) + jnp.where(state_mask, update[c], 0.0)
    return jnp.concatenate(raw, axis=0), state


def _gla_general(q_ref, k_ref, g_ref, v_ref, s_ref, raw_ref, cum_ref, amask_ref):
    n_chunks = q_ref.shape[0] // CHUNK
    qk_head, v_head, state_mask = _gla_head_ids()
    cum = cum_ref[...]
    amask = amask_ref[...] > 0.5
    ind_r = lax.broadcasted_iota(jnp.int32, (GLA_QK_LANES, LANES), 0) // HEAD_SLOT
    ind_c = lax.broadcasted_iota(jnp.int32, (GLA_QK_LANES, LANES), 1)
    ind_sum = jnp.where(ind_r == ind_c, 1.0, 0.0).astype(BF16)
    bc_r = lax.broadcasted_iota(jnp.int32, (LANES, GLA_WIDTH), 0)
    bc_c = lax.broadcasted_iota(jnp.int32, (LANES, GLA_WIDTH), 1) // GLA_DV
    ind_bcast = jnp.where(bc_r == bc_c, 1.0, 0.0).astype(BF16)
    row_in_sub = lax.broadcasted_iota(jnp.int32, (CHUNK, 1), 0) % SUB

    def chunk_body(c, carry):
        r0 = pl.multiple_of(c * CHUNK, CHUNK)
        q = q_ref[pl.ds(r0, CHUNK), :]
        k = k_ref[pl.ds(r0, CHUNK), :]
        g = g_ref[pl.ds(r0, CHUNK), :]
        v = v_ref[pl.ds(r0, CHUNK), :]
        g_hi, g_lo = _split_bf16(g)
        cums = _dot(cum, g_hi) + _dot(cum, g_lo)
        b = cums[0:CHUNK]
        b_ref = cums[CHUNK:2 * CHUNK]
        b_last = cums[2 * CHUNK:3 * CHUNK]
        e_kst = cums[3 * CHUNK:3 * CHUNK + KST_ROWS]

        state = s_ref[...]
        q_in = (q * jnp.exp(b)).astype(BF16)
        o = _dot_nt(q_in, state.astype(BF16))

        kw = (k * jnp.exp(b_last - b)).astype(BF16)
        decay = jnp.exp(jnp.concatenate([b_last] * (GLA_WIDTH // CHUNK), axis=0))
        s_ref[...] = state * decay + jnp.where(state_mask, _dot_tn(v, kw), 0.0)

        q_sub = (q * jnp.exp(jnp.minimum(b - b_ref, 0.0))).astype(BF16)
        k_st = jnp.concatenate([k[0:SUB * j] for j in range(1, N_SUB)], axis=0) * jnp.exp(e_kst)
        k_bd = jnp.concatenate(
            [jnp.where(qk_head == hd, k_st, 0.0) for hd in range(GLA_HEADS)], axis=0).astype(BF16)
        a = _dot_nt(q_sub, k_bd)
        a = jnp.where(amask, a, 0.0).astype(BF16)
        v_st = jnp.concatenate([v[0:SUB * j] for j in range(1, N_SUB)], axis=0)
        v_bd = jnp.concatenate(
            [jnp.where(v_head == hd, v_st, jnp.zeros_like(v_st)) for hd in range(GLA_HEADS)], axis=0)
        o = o + _dot(a, v_bd)

        vf = v.astype(F32)
        prods = []
        for d in range(SUB):
            ks = k if d == 0 else pltpu.roll(k, d, 0)
            bs = b if d == 0 else pltpu.roll(b, d, 0)
            prods.append((q * ks * jnp.exp(jnp.minimum(b - bs, 0.0))).astype(BF16))
        dsum = _dot(jnp.concatenate(prods, axis=0), ind_sum)
        valid = jnp.concatenate([row_in_sub >= d for d in range(SUB)], axis=0)
        dsum = jnp.where(valid, dsum, 0.0).astype(BF16)
        dbc = _dot(dsum, ind_bcast)
        for d in range(SUB):
            vs = vf if d == 0 else pltpu.roll(vf, d, 0)
            o = o + dbc[d * CHUNK:(d + 1) * CHUNK] * vs
        raw_ref[pl.ds(r0, CHUNK), :] = o
        return carry

    lax.fori_loop(0, n_chunks, chunk_body, 0)


def _gla_finish(o, gate, gain):
    hr = lax.broadcasted_iota(jnp.int32, (GLA_WIDTH, GLA_WIDTH), 0) // GLA_DV
    hc = lax.broadcasted_iota(jnp.int32, (GLA_WIDTH, GLA_WIDTH), 1) // GLA_DV
    head_ones = jnp.where(hr == hc, 1.0, 0.0).astype(BF16)
    ms = _dot((o * o).astype(BF16), head_ones) * (1.0 / GLA_DV)
    return (o * lax.rsqrt(ms + EPS) * gain * (gate * _sigmoid(gate))).astype(BF16)


def _proj_kernel(x_ref, g_ref, w_ref, gkup_ref, gkb_ref, cw_ref, cum_ref, tri_ref, amask_ref, gn_ref,
                 gla_ref, dq_ref, dk_ref, dv_ref, conv_ref,
                 zbuf_ref, q_s, k_s, g_s, v_s, s_ref, s_next_ref, raw_ref, *, tiles_per_seq):
    tm = x_ref.shape[0]
    h = _rms(x_ref[...], g_ref[...]).astype(BF16)

    @pl.when(pl.program_id(0) % tiles_per_seq == 0)
    def _():
        zbuf_ref[0:F32_SUBLANES, :] = jnp.zeros((F32_SUBLANES, CONV_DIM), F32)
        s_ref[...] = jnp.zeros_like(s_ref)

    def proj(off, width):
        return _dot(h, w_ref[:, off:off + width])

    qk = proj(OFF_GQ, 2 * GLA_QK_LANES)
    gq = qk[:, 0:GLA_QK_LANES]
    gk = qk[:, GLA_QK_LANES:]
    vg = proj(OFF_GV, 2 * GLA_WIDTH)
    gv = vg[:, 0:GLA_WIDTH].astype(BF16)
    gate = vg[:, GLA_WIDTH:]

    glr_hi, glr_lo = _split_bf16(gq)
    up_hi = gkup_ref[0]
    up_lo = gkup_ref[1]
    z = _dot(glr_hi, up_hi) + _dot(glr_lo, up_hi) + _dot(glr_hi, up_lo) + gkb_ref[...]
    logsig = jnp.minimum(z, 0.0) - jnp.log1p(jnp.exp(-jnp.abs(z)))
    lane = lax.broadcasted_iota(jnp.int32, (tm, GLA_QK_LANES), 1)
    gg = jnp.where(lane % HEAD_SLOT < GLA_DK, logsig * (1.0 / GLA_GATE_NORM), 0.0)

    dqk = proj(OFF_DQ, 2 * DIFF_LANES)
    dq_ref[...] = (dqk[:, 0:DIFF_LANES] * (DIFF_DQK ** -0.5 * LOG2E)).astype(BF16)
    dk_ref[...] = dqk[:, DIFF_LANES:].astype(BF16)
    lane_v = lax.broadcasted_iota(jnp.int32, (tm, DIFF_LANES), 1)
    dv_ref[...] = jnp.where(lane_v % LANES == DIFF_DV, 1.0, proj(OFF_DV, DIFF_LANES)).astype(BF16)

    conv = proj(OFF_CB, 3 * CONV_DIM)
    zc = conv[:, CONV_DIM:2 * CONV_DIM] * conv[:, 2 * CONV_DIM:]
    zbuf_ref[F32_SUBLANES:F32_SUBLANES + tm, :] = zc
    cw = cw_ref[...]
    y = (cw[2:3, :] * zc + cw[1:2, :] * zbuf_ref[F32_SUBLANES - 1:F32_SUBLANES - 1 + tm, :]
         + cw[0:1, :] * zbuf_ref[F32_SUBLANES - 2:F32_SUBLANES - 2 + tm, :])
    conv_ref[...] = (conv[:, 0:CONV_DIM] * y).astype(BF16)
    zbuf_ref[0:F32_SUBLANES, :] = zbuf_ref[tm:tm + F32_SUBLANES, :]

    q_scaled = gq * (GLA_DK ** -0.5)
    raw, new_state = _gla_fast(q_scaled, gk, gg, gv, s_ref[...], tri_ref[...])
    gla_ref[...] = _gla_finish(raw, gate, gn_ref[...])
    s_next_ref[...] = new_state
    q_s[...] = q_scaled
    k_s[...] = gk
    g_s[...] = gg
    v_s[...] = gv
    raw_ref[...] = gate
    safe = jnp.min(gg) * CHUNK >= -GLA_SAFE_DECAY

    @pl.when(safe)
    def _():
        s_ref[...] = s_next_ref[...]

    @pl.when(jnp.logical_not(safe))
    def _():
        gate_kept = raw_ref[...]
        _gla_general(q_s, k_s, g_s, v_s, s_ref, raw_ref, cum_ref, amask_ref)
        gla_ref[...] = _gla_finish(raw_ref[...], gate_kept, gn_ref[...])


def _proj(x, g, w, gkup, gkb, cw, gn_tiled, seq):
    t = x.shape[0]
    tm = min(PROJ_ROWS, seq)
    assert tm // CHUNK <= F32_SUBLANES
    row = lambda width: pl.BlockSpec((tm, width), lambda i: (i, 0))
    cum_np = _gla_cum_matrix()
    cum = jnp.asarray(cum_np, BF16)
    tri = jnp.asarray(cum_np[0:CHUNK], BF16)
    amask = jnp.asarray(_gla_score_mask(), F32)
    out_shapes = (
        jax.ShapeDtypeStruct((t, GLA_WIDTH), BF16),
        jax.ShapeDtypeStruct((t, DIFF_LANES), BF16),
        jax.ShapeDtypeStruct((t, DIFF_LANES), BF16),
        jax.ShapeDtypeStruct((t, DIFF_LANES), BF16),
        jax.ShapeDtypeStruct((t, CONV_DIM), BF16),
    )
    return pl.pallas_call(
        functools.partial(_proj_kernel, tiles_per_seq=seq // tm),
        out_shape=out_shapes,
        grid=(t // tm,),
        in_specs=[
            row(D_MODEL),
            _const_spec((1, D_MODEL)),
            _const_spec((D_MODEL, N_PROJ)),
            _const_spec((2, GLA_QK_LANES, GLA_QK_LANES)),
            _const_spec((1, GLA_QK_LANES)),
            _const_spec((F32_SUBLANES, CONV_DIM)),
            _const_spec(cum.shape),
            _const_spec(tri.shape),
            _const_spec(amask.shape),
            _const_spec((1, GLA_WIDTH)),
        ],
        out_specs=tuple(row(s.shape[1]) for s in out_shapes),
        scratch_shapes=[pltpu.VMEM((tm + F32_SUBLANES, CONV_DIM), F32),
                        pltpu.VMEM((tm, GLA_QK_LANES), F32),
                        pltpu.VMEM((tm, GLA_QK_LANES), F32),
                        pltpu.VMEM((tm, GLA_QK_LANES), F32),
                        pltpu.VMEM((tm, GLA_WIDTH), BF16),
                        pltpu.VMEM((GLA_WIDTH, GLA_QK_LANES), F32),
                        pltpu.VMEM((GLA_WIDTH, GLA_QK_LANES), F32),
                        pltpu.VMEM((tm, GLA_WIDTH), F32)],
        compiler_params=pltpu.CompilerParams(
            dimension_semantics=("arbitrary",), vmem_limit_bytes=VMEM_LIMIT_BYTES),
        name="mixer_proj_gla",
    )(x, g, w, gkup, gkb, cw, cum, tri, amask, gn_tiled)


def _t5_bucket(rel):
    nb = NUM_BUCKETS // 2
    max_exact = nb // 2
    ret = (rel > 0).astype(jnp.int32) * nb
    n = jnp.abs(rel)
    nf = jnp.maximum(n, 1).astype(jnp.float32)
    large = max_exact + (jnp.log(nf / max_exact) / math.log(MAX_DISTANCE / max_exact)
                         * (nb - max_exact)).astype(jnp.int32)
    large = jnp.minimum(large, nb - 1)
    return ret + jnp.where(n < max_exact, n, large)


def _bias_tiles(rel_bias, tile):
    assert tile >= MAX_DISTANCE and tile % CHUNK == 0
    table = rel_bias.astype(F32)
    far = table[NUM_BUCKETS // 2 - 1]
    heads = table.shape[1]
    rel = jnp.concatenate([jnp.arange(0, tile), jnp.arange(-2 * tile, 0)])
    onehot = _t5_bucket(rel)[None, :, None] == jnp.arange(NUM_BUCKETS)[None, None, :]
    per_rel = jnp.sum(jnp.where(onehot, table.T[:, None, :], 0.0), axis=-1)
    per_rel = (per_rel - far[:, None]) * LOG2E
    span = 3 * tile
    flat = jnp.tile(per_rel, (1, 2 * tile))[:, :2 * tile * (span - 1)]
    toep = flat.reshape(heads, 2 * tile, span - 1)[:, :, :tile]
    r = jnp.arange(tile)[:, None]
    c = jnp.arange(tile)[None, :]
    diag = jnp.where(((c // CHUNK) <= (r // CHUNK))[None], toep[:, :tile], MASK_VALUE)
    left = toep[:, tile:]
    return jnp.stack([left, diag], axis=1)


def _attn_kernel(q_ref, k_ref, v_ref, bias_ref, lamv_ref, subln_ref, o_ref,
                 s_buf, p_buf, alpha_buf, m_all, acc_all, *, tile, lam_init):
    seq = q_ref.shape[0]
    nq = seq // tile

    s_buf[...] = jnp.zeros(s_buf.shape, F32)
    p_buf[...] = jnp.zeros(p_buf.shape, BF16)
    alpha_buf[...] = jnp.zeros(alpha_buf.shape, F32)
    m_all[...] = jnp.full(m_all.shape, MASK_VALUE, F32)
    acc_all[...] = jnp.zeros(acc_all.shape, F32)

    lane = lax.broadcasted_iota(jnp.int32, (tile, LANES), 1)

    def stage1(i, j, slot):
        q = q_ref[pl.ds(pl.multiple_of(i * tile, tile), tile), :]
        zero = jnp.zeros_like(q)
        qs = jnp.concatenate([jnp.where(lane < LANES // 2, q, zero),
                              jnp.where(lane >= LANES // 2, q, zero)], axis=0)
        kj = k_ref[pl.ds(pl.multiple_of(j * tile, tile), tile), :]
        s_buf[slot] = _dot_nt(qs, kj)

    def stage2(i, bias, slot):
        for half in range(2):
            rows = slice(half * tile, (half + 1) * tile)
            s = s_buf[slot, rows, :]
            if bias is not None:
                s = s + bias
            m_prev = m_all[i, rows, :]
            m_next = jnp.maximum(m_prev, jnp.max(s, axis=-1, keepdims=True))
            p = jnp.exp2(s - jnp.concatenate([m_next] * (tile // LANES), axis=1))
            p_buf[slot, rows, :] = p.astype(BF16)
            alpha_buf[slot, rows, :] = jnp.exp2(m_prev - m_next)
            m_all[i, rows, :] = m_next

    def stage3(i, j, slot):
        vj = v_ref[pl.ds(pl.multiple_of(j * tile, tile), tile), :]
        acc_all[i] = alpha_buf[slot] * acc_all[i] + _dot(p_buf[slot], vj)

    def run_pipeline(n_pairs, first_pair, advance, with_bias, unroll):
        n_steps = pl.cdiv(n_pairs + 2, unroll) * unroll

        def step(t, pairs, slot_a, slot_b):
            (i0, j0), (i1, j1), (i2, j2) = pairs
            valid1 = jnp.logical_and(t >= 1, t <= n_pairs)
            valid2 = jnp.logical_and(t >= 2, t <= n_pairs + 1)
            stage3(jnp.where(valid2, i2, nq), j2, slot_a)
            stage2(jnp.where(valid1, i1, nq), bias_ref[0, j1 - i1 + 1] if with_bias else None, slot_b)
            stage1(i0, j0, slot_a)
            return (advance(i0, j0), (i0, j0), (i1, j1))

        def body(u, pairs):
            for r in range(unroll):
                pairs = step(unroll * u + r, pairs, r % 2, (r + 1) % 2)
            return pairs

        lax.fori_loop(0, n_steps // unroll, body, (first_pair,) * 3)

    def next_far(i, j):
        wrap = j == i - 2
        done = jnp.logical_and(wrap, i == nq - 1)
        step_i = jnp.logical_and(wrap, jnp.logical_not(done))
        return (jnp.where(step_i, i + 1, i), jnp.where(done, j, jnp.where(wrap, 0, j + 1)))

    def next_near(i, j):
        wrap = j == i
        done = jnp.logical_and(wrap, i == nq - 1)
        step_i = jnp.logical_and(wrap, jnp.logical_not(done))
        return (jnp.where(step_i, i + 1, i), jnp.where(jnp.logical_or(done, wrap), j, j + 1))

    zero = jnp.int32(0)
    if nq > 2:
        run_pipeline((nq - 1) * (nq - 2) // 2, (jnp.int32(2), zero), next_far, False, ATT_UNROLL_FAR)
    run_pipeline(2 * nq - 1, (zero, zero), next_near, True, ATT_UNROLL_NEAR)

    lv = lamv_ref[...]
    lam = (jnp.exp(jnp.sum(lv[0:1] * lv[1:2], axis=-1, keepdims=True))
           - jnp.exp(jnp.sum(lv[2:3] * lv[3:4], axis=-1, keepdims=True)) + lam_init)

    def finish(i):
        acc = acc_all[i]
        a1 = acc[0:tile]
        a2 = acc[tile:2 * tile]
        ones_col = lane == DIFF_DV
        ratio = jnp.sum(jnp.where(ones_col, a1 / jnp.where(ones_col, a2, 1.0), 0.0),
                        axis=-1, keepdims=True)
        u = a1 - lam * (ratio * a2)
        sq = jnp.where(ones_col, a1 * math.sqrt(DIFF_DV * EPS), u)
        ms = jnp.sum(sq * sq, axis=-1, keepdims=True) * (1.0 / DIFF_DV)
        y = u * lax.rsqrt(ms) * subln_ref[...] * (1.0 - lam_init)
        o_ref[pl.ds(pl.multiple_of(i * tile, tile), tile), :] = y.astype(BF16)

    group = math.gcd(nq, ATT_FINISH_GROUP)

    def finish_group(u, carry):
        for r in range(group):
            finish(u * group + r)
        return carry

    lax.fori_loop(0, nq // group, finish_group, 0)


def _attn(dq, dk, dv, bias_tiles, lamv, subln, batch, seq, lam_init):
    tile = min(ATT_TILE, seq)
    nq = seq // tile
    head_block = pl.BlockSpec((seq, LANES), lambda b, h: (b, h))
    return pl.pallas_call(
        functools.partial(_attn_kernel, tile=tile, lam_init=lam_init),
        out_shape=jax.ShapeDtypeStruct((batch * seq, DIFF_LANES), BF16),
        grid=(batch, DIFF_HEADS),
        in_specs=[
            head_block, head_block, head_block,
            pl.BlockSpec((1, 2, tile, tile), lambda b, h: (h, 0, 0, 0)),
            pl.BlockSpec((4, LANES), lambda b, h: (0, 0)),
            pl.BlockSpec((1, LANES), lambda b, h: (0, 0)),
        ],
        out_specs=head_block,
        scratch_shapes=[pltpu.VMEM((2, 2 * tile, tile), F32),
                        pltpu.VMEM((2, 2 * tile, tile), BF16),
                        pltpu.VMEM((2, 2 * tile, LANES), F32),
                        pltpu.VMEM((nq + 1, 2 * tile, LANES), F32),
                        pltpu.VMEM((nq + 1, 2 * tile, LANES), F32)],
        compiler_params=pltpu.CompilerParams(
            dimension_semantics=("parallel", "parallel"), vmem_limit_bytes=VMEM_LIMIT_BYTES),
        name="diff_attn",
    )(dq, dk, dv, bias_tiles, lamv, subln)


def _pad_cols(w, width):
    return jnp.pad(w, ((0, 0), (0, width - w.shape[1])))


def _slot_cols(w, heads, used, slot):
    lead = w.shape[0]
    w = w.reshape(lead, heads, used)
    return jnp.pad(w, ((0, 0), (0, 0), (0, slot - used))).reshape(lead, heads * slot)


def _layout_w_in(w_in):
    offs = np.cumsum([0, GLA_HEADS * GLA_DK, GLA_HEADS * GLA_DK, GLA_WIDTH, GLA_RANK, GLA_WIDTH,
                      DIFF_HEADS * 2 * DIFF_DQK, DIFF_HEADS * 2 * DIFF_DQK, DIFF_WIDTH,
                      CONV_DIM, CONV_DIM, CONV_DIM])
    gq, gk, gv, glr, gate, dq, dk, dv, cb, cc, ch = [w_in[:, offs[n]:offs[n + 1]] for n in range(11)]
    half = LANES // 2
    gq_slots = _slot_cols(gq, GLA_HEADS, GLA_DK, HEAD_SLOT)
    gq_slots = gq_slots.at[:, GLR_LANE:GLR_LANE + GLA_RANK].set(glr)
    cols = [
        gq_slots,
        _slot_cols(gk, GLA_HEADS, GLA_DK, HEAD_SLOT),
        gv,
        gate,
        _slot_cols(dq, 2 * DIFF_HEADS, DIFF_DQK, half),
        _slot_cols(dk, 2 * DIFF_HEADS, DIFF_DQK, half),
        _slot_cols(dv, DIFF_HEADS, DIFF_DV, LANES),
        cb, cc, ch,
    ]
    w = jnp.concatenate(cols, axis=1)
    assert w.shape[1] == N_PROJ
    return w.astype(BF16)


def kernel(x, ffn1_norm, ffn1_gate, ffn1_up, ffn1_down, mix_norm, w_in, gla_gk_up, gla_gk_bias,
           gla_norm, diff_lambda_q1, diff_lambda_k1, diff_lambda_q2, diff_lambda_k2, diff_subln,
           rel_bias, conv_w, w_out, ffn2_norm, ffn2_gate, ffn2_up, ffn2_down, final_norm):
    batch, seq, _ = x.shape
    depth = w_in.shape[0]
    t = batch * seq
    xf = x.reshape(t, D_MODEL)
    bias_tiles = _bias_tiles(rel_bias, min(ATT_TILE, seq))
    final_g = final_norm.reshape(1, D_MODEL)
    ffn1_w = (ffn1_gate[0].astype(BF16), ffn1_up[0].astype(BF16), ffn1_down[0].astype(BF16))

    for l in range(depth):
        lam_init = 0.8 - 0.6 * math.exp(-0.3 * l)
        xf, ffn2_w = _ffn(xf, ffn1_norm[l].reshape(1, D_MODEL), *ffn1_w, final_g, False,
                          cast=((ffn2_gate, ffn2_up, ffn2_down), l))

        w_pad = _layout_w_in(w_in[l])
        up = _slot_cols(gla_gk_up[l], GLA_HEADS, GLA_DK, HEAD_SLOT)
        up = jnp.pad(up, ((GLR_LANE, GLA_QK_LANES - GLR_LANE - GLA_RANK), (0, 0)))
        up_hi = up.astype(BF16)
        up_lo = (up - up_hi.astype(F32)).astype(BF16)
        gkup = jnp.stack([up_hi, up_lo], axis=0)
        gkb = _slot_cols(gla_gk_bias[l].reshape(1, -1), GLA_HEADS, GLA_DK, HEAD_SLOT)
        cw = jnp.pad(conv_w[l], ((0, F32_SUBLANES - CONV_WIDTH), (0, 0)))
        gn_tiled = jnp.tile(gla_norm[l], GLA_HEADS).reshape(1, GLA_WIDTH)
        gla_o, dq, dk, dv, conv_o = _proj(
            xf, mix_norm[l].reshape(1, D_MODEL), w_pad, gkup, gkb, cw, gn_tiled, seq)

        lamv = jnp.stack([diff_lambda_q1[l], diff_lambda_k1[l], diff_lambda_q2[l], diff_lambda_k2[l]])
        lamv = _pad_cols(lamv.astype(F32), LANES)
        subln = _pad_cols(diff_subln[l].reshape(1, DIFF_DV), LANES)
        diff_o = _attn(dq, dk, dv, bias_tiles, lamv, subln, batch, seq, lam_init)

        wo = w_out[l]
        w1 = wo[0:GLA_WIDTH].astype(BF16)
        w2 = wo[GLA_WIDTH:GLA_WIDTH + DIFF_WIDTH].reshape(DIFF_HEADS, DIFF_DV, D_MODEL)
        w2 = jnp.pad(w2, ((0, 0), (0, LANES - DIFF_DV), (0, 0))).reshape(DIFF_LANES, D_MODEL).astype(BF16)
        w3 = wo[GLA_WIDTH + DIFF_WIDTH:].astype(BF16)
        last = l == depth - 1
        xf, ffn1_w = _ffn(xf, ffn2_norm[l].reshape(1, D_MODEL), *ffn2_w, final_g, last,
                          mix=(gla_o, diff_o, conv_o, w1, w2, w3),
                          cast=None if last else ((ffn1_gate, ffn1_up, ffn1_down), l + 1))

    return xf.reshape(batch, seq, D_MODEL)
```

```python
import functools
import math

import numpy as np
import jax
import jax.numpy as jnp
from jax import lax
from jax.experimental import pallas as pl
from jax.experimental.pallas import tpu as pltpu

F32 = jnp.float32
BF16 = jnp.bfloat16

D_MODEL = 1024
D_FF = 2816
EPS = 1e-6
CHUNK = 64
GLA_HEADS = 4
GLA_DK = 48
GLA_DV = 96
GLA_RANK = 16
GLA_GATE_NORM = 16.0
DIFF_HEADS = 4
DIFF_DQK = 48
DIFF_DV = 96
CONV_DIM = 256
CONV_WIDTH = 3
NUM_BUCKETS = 32
MAX_DISTANCE = 128
GLA_WIDTH = GLA_HEADS * GLA_DV
DIFF_WIDTH = DIFF_HEADS * DIFF_DV

LANES = 128
F32_SUBLANES = 8
BF16_SUBLANES = 16
MXU_DEPTH = 256
VMEM_LIMIT_BYTES = 56 * 1024 * 1024

HEAD_SLOT = 64
GLA_QK_LANES = GLA_HEADS * HEAD_SLOT
SUB = 16
N_SUB = CHUNK // SUB
KST_ROWS = SUB * (N_SUB * (N_SUB - 1) // 2)
DIFF_LANES = DIFF_HEADS * LANES
MASK_VALUE = -1e30
GLA_SAFE_DECAY = 60.0
LOG2E = math.log2(math.e)

OFF_GQ = 0
OFF_GK = OFF_GQ + GLA_QK_LANES
OFF_GV = OFF_GK + GLA_QK_LANES
OFF_GATE = OFF_GV + GLA_WIDTH
OFF_DQ = OFF_GATE + GLA_WIDTH
OFF_DK = OFF_DQ + DIFF_LANES
OFF_DV = OFF_DK + DIFF_LANES
OFF_CB = OFF_DV + DIFF_LANES
OFF_CC = OFF_CB + CONV_DIM
OFF_CH = OFF_CC + CONV_DIM
N_PROJ = OFF_CH + CONV_DIM
GLR_LANE = GLA_DK
assert GLR_LANE + GLA_RANK <= HEAD_SLOT

FFN_ROWS = 512
FFN_F_SPLITS = (0, 6 * MXU_DEPTH, D_FF)
assert all((hi - lo) % MXU_DEPTH == 0 for lo, hi in zip(FFN_F_SPLITS[:-1], FFN_F_SPLITS[1:]))
PROJ_ROWS = 512
ATT_TILE = 256
ATT_UNROLL_FAR = 36
ATT_UNROLL_NEAR = 22
ATT_FINISH_GROUP = 4


def _dot(a, b):
    return jnp.dot(a, b, preferred_element_type=F32)


def _dot_nt(a, b):
    return lax.dot_general(a, b, (((1,), (1,)), ((), ())), preferred_element_type=F32)


def _dot_tn(a, b):
    return lax.dot_general(a, b, (((0,), (0,)), ((), ())), preferred_element_type=F32)


def _split_bf16(x):
    hi = x.astype(BF16)
    lo = (x - hi.astype(F32)).astype(BF16)
    return hi, lo


def _rms(x, g):
    return x * lax.rsqrt(jnp.mean(x * x, axis=-1, keepdims=True) + EPS) * g


def _sigmoid(x):
    return 1.0 / (1.0 + jnp.exp(-x))


def _ffn_kernel(*refs, final, mix, cast):
    refs = list(refs)
    x_ref, g_ref, wg_ref, wu_ref, wd_ref, fg_ref = refs[:6]
    mix_refs = refs[6:12] if mix else ()
    n_in = 6 + len(mix_refs)
    src_refs = refs[n_in:n_in + 3] if cast else ()
    o_ref = refs[n_in + len(src_refs)]
    dst_refs = refs[n_in + len(src_refs) + 1:]
    for src, dst in zip(src_refs, dst_refs):
        dst[...] = src[...].astype(BF16)

    x = x_ref[...]
    if mix:
        gla_ref, diff_ref, conv_ref, w1_ref, w2_ref, w3_ref = mix_refs
        x = (x + _dot(gla_ref[...], w1_ref[...]) + _dot(diff_ref[...], w2_ref[...])
             + _dot(conv_ref[...], w3_ref[...]))
    h = _rms(x, g_ref[...]).astype(BF16)
    acc = None
    for lo, hi in zip(FFN_F_SPLITS[:-1], FFN_F_SPLITS[1:]):
        sl = slice(lo, hi)
        gate = _dot(h, wg_ref[:, sl])
        up = _dot(h, wu_ref[:, sl])
        a = (gate * _sigmoid(gate) * up).astype(BF16)
        part = _dot(a, wd_ref[sl, :])
        acc = part if acc is None else acc + part
    y = x + 0.5 * acc
    if final:
        y = _rms(y, fg_ref[...])
    o_ref[...] = y


def _const_spec(shape):
    nd = len(shape)
    return pl.BlockSpec(shape, lambda *_: (0,) * nd, pipeline_mode=pl.Buffered(1))


def _cast_block(n_rows, steps):
    share = 1
    while (n_rows * share) % (steps * BF16_SUBLANES) or steps % share:
        share *= 2
        assert share <= steps
    return n_rows * share // steps, share


def _ffn(x, g, wg, wu, wd, final_g, final, mix=None, cast=None):
    t = x.shape[0]
    tm = min(FFN_ROWS, t)
    steps = t // tm
    row = lambda width: pl.BlockSpec((tm, width), lambda i: (i, 0))
    operands = [x, g, wg, wu, wd, final_g]
    in_specs = [row(D_MODEL), _const_spec((1, D_MODEL)), _const_spec((D_MODEL, D_FF)),
                _const_spec((D_MODEL, D_FF)), _const_spec((D_FF, D_MODEL)), _const_spec((1, D_MODEL))]
    out_shapes = [jax.ShapeDtypeStruct((t, D_MODEL), F32)]
    out_specs = [row(D_MODEL)]
    if mix is not None:
        operands += list(mix)
        in_specs += [row(a.shape[1]) for a in mix[:3]] + [_const_spec(w.shape) for w in mix[3:]]
    if cast is not None:
        stacked, layer = cast
        for w in stacked:
            rows, share = _cast_block(w.shape[1], steps)
            operands.append(w)
            in_specs.append(pl.BlockSpec((None, rows, w.shape[2]),
                                         lambda i, share=share: (layer, i // share, 0)))
            out_shapes.append(jax.ShapeDtypeStruct(w.shape[1:], BF16))
            out_specs.append(pl.BlockSpec((rows, w.shape[2]), lambda i, share=share: (i // share, 0)))
    outs = pl.pallas_call(
        functools.partial(_ffn_kernel, final=final, mix=mix is not None, cast=cast is not None),
        out_shape=tuple(out_shapes),
        grid=(steps,),
        in_specs=in_specs,
        out_specs=tuple(out_specs),
        compiler_params=pltpu.CompilerParams(
            dimension_semantics=("arbitrary",), vmem_limit_bytes=VMEM_LIMIT_BYTES),
        name="ffn",
    )(*operands)
    return outs[0], tuple(outs[1:])


def _gla_cum_matrix():
    t = np.arange(CHUNK)[:, None]
    u = np.arange(CHUNK)[None, :]
    tri = (u <= t)
    ref = (u <= (t // SUB) * SUB - 1)
    ones = np.ones((CHUNK, CHUNK), bool)
    parts = [tri, ref, ones]
    for j in range(1, N_SUB):
        s = np.arange(SUB * j)[:, None]
        parts.append((u > s) & (u <= SUB * j - 1))
    return np.concatenate(parts, axis=0).astype(np.float32)


def _gla_score_mask():
    t = np.arange(CHUNK)[:, None]
    group = np.concatenate([np.full(SUB * j, j) for j in range(1, N_SUB)])
    keep = (group[None, :] == (t // SUB))
    return np.tile(keep, (1, GLA_HEADS)).astype(np.float32)


def _gla_head_ids():
    qk_head = lax.broadcasted_iota(jnp.int32, (1, GLA_QK_LANES), 1) // HEAD_SLOT
    v_head = lax.broadcasted_iota(jnp.int32, (1, GLA_WIDTH), 1) // GLA_DV
    row_head_v = lax.broadcasted_iota(jnp.int32, (GLA_WIDTH, 1), 0) // GLA_DV
    return qk_head, v_head, row_head_v == qk_head


def _gla_fast(q, k, g, v, state, tri):
    rows = q.shape[0]
    n_chunks = rows // CHUNK
    qk_head, v_head, state_mask = _gla_head_ids()
    chunk = lambda x, c: x[c * CHUNK:(c + 1) * CHUNK]
    sel_r = lax.broadcasted_iota(jnp.int32, (F32_SUBLANES, rows), 0)
    sel_c = lax.broadcasted_iota(jnp.int32, (F32_SUBLANES, rows), 1) // CHUNK
    chunk_sel = jnp.where(sel_r == sel_c, 1.0, 0.0).astype(BF16)
    g_hi, g_lo = _split_bf16(g)
    totals = _dot(chunk_sel, g_hi) + _dot(chunk_sel, g_lo)
    b = jnp.concatenate([_dot(tri, chunk(g_hi, c)) + _dot(tri, chunk(g_lo, c))
                         for c in range(n_chunks)], axis=0)
    b_last = jnp.concatenate([jnp.broadcast_to(totals[c:c + 1], (CHUNK, GLA_QK_LANES))
                              for c in range(n_chunks)], axis=0)
    qd = (q * jnp.exp(b)).astype(BF16)
    kd = (k * jnp.exp(-b)).astype(BF16)
    kw = (k * jnp.exp(b_last - b)).astype(BF16)
    t_idx = lax.broadcasted_iota(jnp.int32, (CHUNK, GLA_QK_LANES), 0)
    s_idx = lax.broadcasted_iota(jnp.int32, (CHUNK, GLA_QK_LANES), 1) % HEAD_SLOT
    causal = s_idx <= t_idx
    intra, update = [], []
    for c in range(n_chunks):
        kd_c, v_c = chunk(kd, c), chunk(v, c)
        kd_bd = jnp.concatenate(
            [jnp.where(qk_head == hd, kd_c, jnp.zeros_like(kd_c)) for hd in range(GLA_HEADS)],
            axis=0)
        a = _dot_nt(chunk(qd, c), kd_bd)
        a = jnp.where(causal, a, 0.0).astype(BF16)
        v_bd = jnp.concatenate(
            [jnp.where(v_head == hd, v_c, jnp.zeros_like(v_c)) for hd in range(GLA_HEADS)],
            axis=0)
        intra.append(_dot(a, v_bd))
        update.append(_dot_tn(v_c, chunk(kw, c)))
    raw = []
    for c in range(n_chunks):
        raw.append(intra[c] + _dot_nt(chunk(qd, c), state.astype(BF16)))
        state = state * jnp.exp(totals[c:c + 1]) + jnp.where(state_mask, update[c], 0.0)
    return jnp.concatenate(raw, axis=0), state


def _gla_general(q_ref, k_ref, g_ref, v_ref, s_ref, raw_ref, cum_ref, amask_ref):
    n_chunks = q_ref.shape[0] // CHUNK
    qk_head, v_head, state_mask = _gla_head_ids()
    cum = cum_ref[...]
    amask = amask_ref[...] > 0.5
    ind_r = lax.broadcasted_iota(jnp.int32, (GLA_QK_LANES, LANES), 0) // HEAD_SLOT
    ind_c = lax.broadcasted_iota(jnp.int32, (GLA_QK_LANES, LANES), 1)
    ind_sum = jnp.where(ind_r == ind_c, 1.0, 0.0).astype(BF16)
    bc_r = lax.broadcasted_iota(jnp.int32, (LANES, GLA_WIDTH), 0)
    bc_c = lax.broadcasted_iota(jnp.int32, (LANES, GLA_WIDTH), 1) // GLA_DV
    ind_bcast = jnp.where(bc_r == bc_c, 1.0, 0.0).astype(BF16)
    row_in_sub = lax.broadcasted_iota(jnp.int32, (CHUNK, 1), 0) % SUB

    def chunk_body(c, carry):
        r0 = pl.multiple_of(c * CHUNK, CHUNK)
        q = q_ref[pl.ds(r0, CHUNK), :]
        k = k_ref[pl.ds(r0, CHUNK), :]
        g = g_ref[pl.ds(r0, CHUNK), :]
        v = v_ref[pl.ds(r0, CHUNK), :]
        g_hi, g_lo = _split_bf16(g)
        cums = _dot(cum, g_hi) + _dot(cum, g_lo)
        b = cums[0:CHUNK]
        b_ref = cums[CHUNK:2 * CHUNK]
        b_last = cums[2 * CHUNK:3 * CHUNK]
        e_kst = cums[3 * CHUNK:3 * CHUNK + KST_ROWS]

        state = s_ref[...]
        q_in = (q * jnp.exp(b)).astype(BF16)
        o = _dot_nt(q_in, state.astype(BF16))

        kw = (k * jnp.exp(b_last - b)).astype(BF16)
        decay = jnp.exp(jnp.concatenate([b_last] * (GLA_WIDTH // CHUNK), axis=0))
        s_ref[...] = state * decay + jnp.where(state_mask, _dot_tn(v, kw), 0.0)

        q_sub = (q * jnp.exp(jnp.minimum(b - b_ref, 0.0))).astype(BF16)
        k_st = jnp.concatenate([k[0:SUB * j] for j in range(1, N_SUB)], axis=0) * jnp.exp(e_kst)
        k_bd = jnp.concatenate(
            [jnp.where(qk_head == hd, k_st, 0.0) for hd in range(GLA_HEADS)], axis=0).astype(BF16)
        a = _dot_nt(q_sub, k_bd)
        a = jnp.where(amask, a, 0.0).astype(BF16)
        v_st = jnp.concatenate([v[0:SUB * j] for j in range(1, N_SUB)], axis=0)
        v_bd = jnp.concatenate(
            [jnp.where(v_head == hd, v_st, jnp.zeros_like(v_st)) for hd in range(GLA_HEADS)], axis=0)
        o = o + _dot(a, v_bd)

        vf = v.astype(F32)
        prods = []
        for d in range(SUB):
            ks = k if d == 0 else pltpu.roll(k, d, 0)
            bs = b if d == 0 else pltpu.roll(b, d, 0)
            prods.append((q * ks * jnp.exp(jnp.minimum(b - bs, 0.0))).astype(BF16))
        dsum = _dot(jnp.concatenate(prods, axis=0), ind_sum)
        valid = jnp.concatenate([row_in_sub >= d for d in range(SUB)], axis=0)
        dsum = jnp.where(valid, dsum, 0.0).astype(BF16)
        dbc = _dot(dsum, ind_bcast)
        for d in range(SUB):
            vs = vf if d == 0 else pltpu.roll(vf, d, 0)
            o = o + dbc[d * CHUNK:(d + 1) * CHUNK] * vs
        raw_ref[pl.ds(r0, CHUNK), :] = o
        return carry

    lax.fori_loop(0, n_chunks, chunk_body, 0)


def _gla_finish(o, gate, gain):
    hr = lax.broadcasted_iota(jnp.int32, (GLA_WIDTH, GLA_WIDTH), 0) // GLA_DV
    hc = lax.broadcasted_iota(jnp.int32, (GLA_WIDTH, GLA_WIDTH), 1) // GLA_DV
    head_ones = jnp.where(hr == hc, 1.0, 0.0).astype(BF16)
    ms = _dot((o * o).astype(BF16), head_ones) * (1.0 / GLA_DV)
    return (o * lax.rsqrt(ms + EPS) * gain * (gate * _sigmoid(gate))).astype(BF16)


def _proj_kernel(x_ref, g_ref, w_ref, gkup_ref, gkb_ref, cw_ref, cum_ref, tri_ref, amask_ref, gn_ref,
                 gla_ref, dq_ref, dk_ref, dv_ref, conv_ref,
                 zbuf_ref, q_s, k_s, g_s, v_s, s_ref, s_next_ref, raw_ref, *, tiles_per_seq):
    tm = x_ref.shape[0]
    h = _rms(x_ref[...], g_ref[...]).astype(BF16)

    @pl.when(pl.program_id(0) % tiles_per_seq == 0)
    def _():
        zbuf_ref[0:F32_SUBLANES, :] = jnp.zeros((F32_SUBLANES, CONV_DIM), F32)
        s_ref[...] = jnp.zeros_like(s_ref)

    def proj(off, width):
        return _dot(h, w_ref[:, off:off + width])

    qk = proj(OFF_GQ, 2 * GLA_QK_LANES)
    gq = qk[:, 0:GLA_QK_LANES]
    gk = qk[:, GLA_QK_LANES:]
    vg = proj(OFF_GV, 2 * GLA_WIDTH)
    gv = vg[:, 0:GLA_WIDTH].astype(BF16)
    gate = vg[:, GLA_WIDTH:]

    glr_hi, glr_lo = _split_bf16(gq)
    up_hi = gkup_ref[0]
    up_lo = gkup_ref[1]
    z = _dot(glr_hi, up_hi) + _dot(glr_lo, up_hi) + _dot(glr_hi, up_lo) + gkb_ref[...]
    logsig = jnp.minimum(z, 0.0) - jnp.log1p(jnp.exp(-jnp.abs(z)))
    lane = lax.broadcasted_iota(jnp.int32, (tm, GLA_QK_LANES), 1)
    gg = jnp.where(lane % HEAD_SLOT < GLA_DK, logsig * (1.0 / GLA_GATE_NORM), 0.0)

    dqk = proj(OFF_DQ, 2 * DIFF_LANES)
    dq_ref[...] = (dqk[:, 0:DIFF_LANES] * (DIFF_DQK ** -0.5 * LOG2E)).astype(BF16)
    dk_ref[...] = dqk[:, DIFF_LANES:].astype(BF16)
    lane_v = lax.broadcasted_iota(jnp.int32, (tm, DIFF_LANES), 1)
    dv_ref[...] = jnp.where(lane_v % LANES == DIFF_DV, 1.0, proj(OFF_DV, DIFF_LANES)).astype(BF16)

    conv = proj(OFF_CB, 3 * CONV_DIM)
    zc = conv[:, CONV_DIM:2 * CONV_DIM] * conv[:, 2 * CONV_DIM:]
    zbuf_ref[F32_SUBLANES:F32_SUBLANES + tm, :] = zc
    cw = cw_ref[...]
    y = (cw[2:3, :] * zc + cw[1:2, :] * zbuf_ref[F32_SUBLANES - 1:F32_SUBLANES - 1 + tm, :]
         + cw[0:1, :] * zbuf_ref[F32_SUBLANES - 2:F32_SUBLANES - 2 + tm, :])
    conv_ref[...] = (conv[:, 0:CONV_DIM] * y).astype(BF16)
    zbuf_ref[0:F32_SUBLANES, :] = zbuf_ref[tm:tm + F32_SUBLANES, :]

    q_scaled = gq * (GLA_DK ** -0.5)
    raw, new_state = _gla_fast(q_scaled, gk, gg, gv, s_ref[...], tri_ref[...])
    gla_ref[...] = _gla_finish(raw, gate, gn_ref[...])
    s_next_ref[...] = new_state
    q_s[...] = q_scaled
    k_s[...] = gk
    g_s[...] = gg
    v_s[...] = gv
    raw_ref[...] = gate
    safe = jnp.min(gg) * CHUNK >= -GLA_SAFE_DECAY

    @pl.when(safe)
    def _():
        s_ref[...] = s_next_ref[...]

    @pl.when(jnp.logical_not(safe))
    def _():
        gate_kept = raw_ref[...]
        _gla_general(q_s, k_s, g_s, v_s, s_ref, raw_ref, cum_ref, amask_ref)
        gla_ref[...] = _gla_finish(raw_ref[...], gate_kept, gn_ref[...])


def _proj(x, g, w, gkup, gkb, cw, gn_tiled, seq):
    t = x.shape[0]
    tm = min(PROJ_ROWS, seq)
    assert tm // CHUNK <= F32_SUBLANES
    row = lambda width: pl.BlockSpec((tm, width), lambda i: (i, 0))
    cum_np = _gla_cum_matrix()
    cum = jnp.asarray(cum_np, BF16)
    tri = jnp.asarray(cum_np[0:CHUNK], BF16)
    amask = jnp.asarray(_gla_score_mask(), F32)
    out_shapes = (
        jax.ShapeDtypeStruct((t, GLA_WIDTH), BF16),
        jax.ShapeDtypeStruct((t, DIFF_LANES), BF16),
        jax.ShapeDtypeStruct((t, DIFF_LANES), BF16),
        jax.ShapeDtypeStruct((t, DIFF_LANES), BF16),
        jax.ShapeDtypeStruct((t, CONV_DIM), BF16),
    )
    return pl.pallas_call(
        functools.partial(_proj_kernel, tiles_per_seq=seq // tm),
        out_shape=out_shapes,
        grid=(t // tm,),
        in_specs=[
            row(D_MODEL),
            _const_spec((1, D_MODEL)),
            _const_spec((D_MODEL, N_PROJ)),
            _const_spec((2, GLA_QK_LANES, GLA_QK_LANES)),
            _const_spec((1, GLA_QK_LANES)),
            _const_spec((F32_SUBLANES, CONV_DIM)),
            _const_spec(cum.shape),
            _const_spec(tri.shape),
            _const_spec(amask.shape),
            _const_spec((1, GLA_WIDTH)),
        ],
        out_specs=tuple(row(s.shape[1]) for s in out_shapes),
        scratch_shapes=[pltpu.VMEM((tm + F32_SUBLANES, CONV_DIM), F32),
                        pltpu.VMEM((tm, GLA_QK_LANES), F32),
                        pltpu.VMEM((tm, GLA_QK_LANES), F32),
                        pltpu.VMEM((tm, GLA_QK_LANES), F32),
                        pltpu.VMEM((tm, GLA_WIDTH), BF16),
                        pltpu.VMEM((GLA_WIDTH, GLA_QK_LANES), F32),
                        pltpu.VMEM((GLA_WIDTH, GLA_QK_LANES), F32),
                        pltpu.VMEM((tm, GLA_WIDTH), F32)],
        compiler_params=pltpu.CompilerParams(
            dimension_semantics=("arbitrary",), vmem_limit_bytes=VMEM_LIMIT_BYTES),
        name="mixer_proj_gla",
    )(x, g, w, gkup, gkb, cw, cum, tri, amask, gn_tiled)


def _t5_bucket(rel):
    nb = NUM_BUCKETS // 2
    max_exact = nb // 2
    ret = (rel > 0).astype(jnp.int32) * nb
    n = jnp.abs(rel)
    nf = jnp.maximum(n, 1).astype(jnp.float32)
    large = max_exact + (jnp.log(nf / max_exact) / math.log(MAX_DISTANCE / max_exact)
                         * (nb - max_exact)).astype(jnp.int32)
    large = jnp.minimum(large, nb - 1)
    return ret + jnp.where(n < max_exact, n, large)


def _bias_tiles(rel_bias, tile):
    assert tile >= MAX_DISTANCE and tile % CHUNK == 0
    table = rel_bias.astype(F32)
    far = table[NUM_BUCKETS // 2 - 1]
    heads = table.shape[1]
    rel = jnp.concatenate([jnp.arange(0, tile), jnp.arange(-2 * tile, 0)])
    onehot = _t5_bucket(rel)[None, :, None] == jnp.arange(NUM_BUCKETS)[None, None, :]
    per_rel = jnp.sum(jnp.where(onehot, table.T[:, None, :], 0.0), axis=-1)
    per_rel = (per_rel - far[:, None]) * LOG2E
    span = 3 * tile
    flat = jnp.tile(per_rel, (1, 2 * tile))[:, :2 * tile * (span - 1)]
    toep = flat.reshape(heads, 2 * tile, span - 1)[:, :, :tile]
    r = jnp.arange(tile)[:, None]
    c = jnp.arange(tile)[None, :]
    diag = jnp.where(((c // CHUNK) <= (r // CHUNK))[None], toep[:, :tile], MASK_VALUE)
    left = toep[:, tile:]
    return jnp.stack([left, diag], axis=1)


def _attn_kernel(q_ref, k_ref, v_ref, bias_ref, lamv_ref, subln_ref, o_ref,
                 s_buf, p_buf, alpha_buf, m_all, acc_all, *, tile, lam_init):
    seq = q_ref.shape[0]
    nq = seq // tile

    s_buf[...] = jnp.zeros(s_buf.shape, F32)
    p_buf[...] = jnp.zeros(p_buf.shape, BF16)
    alpha_buf[...] = jnp.zeros(alpha_buf.shape, F32)
    m_all[...] = jnp.full(m_all.shape, MASK_VALUE, F32)
    acc_all[...] = jnp.zeros(acc_all.shape, F32)

    lane = lax.broadcasted_iota(jnp.int32, (tile, LANES), 1)

    def stage1(i, j, slot):
        q = q_ref[pl.ds(pl.multiple_of(i * tile, tile), tile), :]
        zero = jnp.zeros_like(q)
        qs = jnp.concatenate([jnp.where(lane < LANES // 2, q, zero),
                              jnp.where(lane >= LANES // 2, q, zero)], axis=0)
        kj = k_ref[pl.ds(pl.multiple_of(j * tile, tile), tile), :]
        s_buf[slot] = _dot_nt(qs, kj)

    def stage2(i, bias, slot):
        for half in range(2):
            rows = slice(half * tile, (half + 1) * tile)
            s = s_buf[slot, rows, :]
            if bias is not None:
                s = s + bias
            m_prev = m_all[i, rows, :]
            m_next = jnp.maximum(m_prev, jnp.max(s, axis=-1, keepdims=True))
            p = jnp.exp2(s - jnp.concatenate([m_next] * (tile // LANES), axis=1))
            p_buf[slot, rows, :] = p.astype(BF16)
            alpha_buf[slot, rows, :] = jnp.exp2(m_prev - m_next)
            m_all[i, rows, :] = m_next

    def stage3(i, j, slot):
        vj = v_ref[pl.ds(pl.multiple_of(j * tile, tile), tile), :]
        acc_all[i] = alpha_buf[slot] * acc_all[i] + _dot(p_buf[slot], vj)

    def run_pipeline(n_pairs, first_pair, advance, with_bias, unroll):
        n_steps = pl.cdiv(n_pairs + 2, unroll) * unroll

        def step(t, pairs, slot_a, slot_b):
            (i0, j0), (i1, j1), (i2, j2) = pairs
            valid1 = jnp.logical_and(t >= 1, t <= n_pairs)
            valid2 = jnp.logical_and(t >= 2, t <= n_pairs + 1)
            stage3(jnp.where(valid2, i2, nq), j2, slot_a)
            stage2(jnp.where(valid1, i1, nq), bias_ref[0, j1 - i1 + 1] if with_bias else None, slot_b)
            stage1(i0, j0, slot_a)
            return (advance(i0, j0), (i0, j0), (i1, j1))

        def body(u, pairs):
            for r in range(unroll):
                pairs = step(unroll * u + r, pairs, r % 2, (r + 1) % 2)
            return pairs

        lax.fori_loop(0, n_steps // unroll, body, (first_pair,) * 3)

    def next_far(i, j):
        wrap = j == i - 2
        done = jnp.logical_and(wrap, i == nq - 1)
        step_i = jnp.logical_and(wrap, jnp.logical_not(done))
        return (jnp.where(step_i, i + 1, i), jnp.where(done, j, jnp.where(wrap, 0, j + 1)))

    def next_near(i, j):
        wrap = j == i
        done = jnp.logical_and(wrap, i == nq - 1)
        step_i = jnp.logical_and(wrap, jnp.logical_not(done))
        return (jnp.where(step_i, i + 1, i), jnp.where(jnp.logical_or(done, wrap), j, j + 1))

    zero = jnp.int32(0)
    if nq > 2:
        run_pipeline((nq - 1) * (nq - 2) // 2, (jnp.int32(2), zero), next_far, False, ATT_UNROLL_FAR)
    run_pipeline(2 * nq - 1, (zero, zero), next_near, True, ATT_UNROLL_NEAR)

    lv = lamv_ref[...]
    lam = (jnp.exp(jnp.sum(lv[0:1] * lv[1:2], axis=-1, keepdims=True))
           - jnp.exp(jnp.sum(lv[2:3] * lv[3:4], axis=-1, keepdims=True)) + lam_init)

    def finish(i):
        acc = acc_all[i]
        a1 = acc[0:tile]
        a2 = acc[tile:2 * tile]
        ones_col = lane == DIFF_DV
        ratio = jnp.sum(jnp.where(ones_col, a1 / jnp.where(ones_col, a2, 1.0), 0.0),
                        axis=-1, keepdims=True)
        u = a1 - lam * (ratio * a2)
        sq = jnp.where(ones_col, a1 * math.sqrt(DIFF_DV * EPS), u)
        ms = jnp.sum(sq * sq, axis=-1, keepdims=True) * (1.0 / DIFF_DV)
        y = u * lax.rsqrt(ms) * subln_ref[...] * (1.0 - lam_init)
        o_ref[pl.ds(pl.multiple_of(i * tile, tile), tile), :] = y.astype(BF16)

    group = math.gcd(nq, ATT_FINISH_GROUP)

    def finish_group(u, carry):
        for r in range(group):
            finish(u * group + r)
        return carry

    lax.fori_loop(0, nq // group, finish_group, 0)


def _attn(dq, dk, dv, bias_tiles, lamv, subln, batch, seq, lam_init):
    tile = min(ATT_TILE, seq)
    nq = seq // tile
    head_block = pl.BlockSpec((seq, LANES), lambda b, h: (b, h))
    return pl.pallas_call(
        functools.partial(_attn_kernel, tile=tile, lam_init=lam_init),
        out_shape=jax.ShapeDtypeStruct((batch * seq, DIFF_LANES), BF16),
        grid=(batch, DIFF_HEADS),
        in_specs=[
            head_block, head_block, head_block,
            pl.BlockSpec((1, 2, tile, tile), lambda b, h: (h, 0, 0, 0)),
            pl.BlockSpec((4, LANES), lambda b, h: (0, 0)),
            pl.BlockSpec((1, LANES), lambda b, h: (0, 0)),
        ],
        out_specs=head_block,
        scratch_shapes=[pltpu.VMEM((2, 2 * tile, tile), F32),
                        pltpu.VMEM((2, 2 * tile, tile), BF16),
                        pltpu.VMEM((2, 2 * tile, LANES), F32),
                        pltpu.VMEM((nq + 1, 2 * tile, LANES), F32),
                        pltpu.VMEM((nq + 1, 2 * tile, LANES), F32)],
        compiler_params=pltpu.CompilerParams(
            dimension_semantics=("parallel", "parallel"), vmem_limit_bytes=VMEM_LIMIT_BYTES),
        name="diff_attn",
    )(dq, dk, dv, bias_tiles, lamv, subln)


def _pad_cols(w, width):
    return jnp.pad(w, ((0, 0), (0, width - w.shape[1])))


def _slot_cols(w, heads, used, slot):
    lead = w.shape[0]
    w = w.reshape(lead, heads, used)
    return jnp.pad(w, ((0, 0), (0, 0), (0, slot - used))).reshape(lead, heads * slot)


def _layout_w_in(w_in):
    offs = np.cumsum([0, GLA_HEADS * GLA_DK, GLA_HEADS * GLA_DK, GLA_WIDTH, GLA_RANK, GLA_WIDTH,
                      DIFF_HEADS * 2 * DIFF_DQK, DIFF_HEADS * 2 * DIFF_DQK, DIFF_WIDTH,
                      CONV_DIM, CONV_DIM, CONV_DIM])
    gq, gk, gv, glr, gate, dq, dk, dv, cb, cc, ch = [w_in[:, offs[n]:offs[n + 1]] for n in range(11)]
    half = LANES // 2
    gq_slots = _slot_cols(gq, GLA_HEADS, GLA_DK, HEAD_SLOT)
    gq_slots = gq_slots.at[:, GLR_LANE:GLR_LANE + GLA_RANK].set(glr)
    cols = [
        gq_slots,
        _slot_cols(gk, GLA_HEADS, GLA_DK, HEAD_SLOT),
        gv,
        gate,
        _slot_cols(dq, 2 * DIFF_HEADS, DIFF_DQK, half),
        _slot_cols(dk, 2 * DIFF_HEADS, DIFF_DQK, half),
        _slot_cols(dv, DIFF_HEADS, DIFF_DV, LANES),
        cb, cc, ch,
    ]
    w = jnp.concatenate(cols, axis=1)
    assert w.shape[1] == N_PROJ
    return w.astype(BF16)


def kernel(x, ffn1_norm, ffn1_gate, ffn1_up, ffn1_down, mix_norm, w_in, gla_gk_up, gla_gk_bias,
           gla_norm, diff_lambda_q1, diff_lambda_k1, diff_lambda_q2, diff_lambda_k2, diff_subln,
           rel_bias, conv_w, w_out, ffn2_norm, ffn2_gate, ffn2_up, ffn2_down, final_norm):
    batch, seq, _ = x.shape
    depth = w_in.shape[0]
    t = batch * seq
    xf = x.reshape(t, D_MODEL)
    bias_tiles = _bias_tiles(rel_bias, min(ATT_TILE, seq))
    final_g = final_norm.reshape(1, D_MODEL)
    ffn1_w = (ffn1_gate[0].astype(BF16), ffn1_up[0].astype(BF16), ffn1_down[0].astype(BF16))

    for l in range(depth):
        lam_init = 0.8 - 0.6 * math.exp(-0.3 * l)
        xf, ffn2_w = _ffn(xf, ffn1_norm[l].reshape(1, D_MODEL), *ffn1_w, final_g, False,
                          cast=((ffn2_gate, ffn2_up, ffn2_down), l))

        w_pad = _layout_w_in(w_in[l])
        up = _slot_cols(gla_gk_up[l], GLA_HEADS, GLA_DK, HEAD_SLOT)
        up = jnp.pad(up, ((GLR_LANE, GLA_QK_LANES - GLR_LANE - GLA_RANK), (0, 0)))
        up_hi = up.astype(BF16)
        up_lo = (up - up_hi.astype(F32)).astype(BF16)
        gkup = jnp.stack([up_hi, up_lo], axis=0)
        gkb = _slot_cols(gla_gk_bias[l].reshape(1, -1), GLA_HEADS, GLA_DK, HEAD_SLOT)
        cw = jnp.pad(conv_w[l], ((0, F32_SUBLANES - CONV_WIDTH), (0, 0)))
        gn_tiled = jnp.tile(gla_norm[l], GLA_HEADS).reshape(1, GLA_WIDTH)
        gla_o, dq, dk, dv, conv_o = _proj(
            xf, mix_norm[l].reshape(1, D_MODEL), w_pad, gkup, gkb, cw, gn_tiled, seq)

        lamv = jnp.stack([diff_lambda_q1[l], diff_lambda_k1[l], diff_lambda_q2[l], diff_lambda_k2[l]])
        lamv = _pad_cols(lamv.astype(F32), LANES)
        subln = _pad_cols(diff_subln[l].reshape(1, DIFF_DV), LANES)
        diff_o = _attn(dq, dk, dv, bias_tiles, lamv, subln, batch, seq, lam_init)

        wo = w_out[l]
        w1 = wo[0:GLA_WIDTH].astype(BF16)
        w2 = wo[GLA_WIDTH:GLA_WIDTH + DIFF_WIDTH].reshape(DIFF_HEADS, DIFF_DV, D_MODEL)
        w2 = jnp.pad(w2, ((0, 0), (0, LANES - DIFF_DV), (0, 0))).reshape(DIFF_LANES, D_MODEL).astype(BF16)
        w3 = wo[GLA_WIDTH + DIFF_WIDTH:].astype(BF16)
        last = l == depth - 1
        xf, ffn1_w = _ffn(xf, ffn2_norm[l].reshape(1, D_MODEL), *ffn2_w, final_g, last,
                          mix=(gla_o, diff_o, conv_o, w1, w2, w3),
                          cast=None if last else ((ffn1_gate, ffn1_up, ffn1_down), l + 1))

    return xf.reshape(batch, seq, D_MODEL)
```

```python
import functools
import math

import numpy as np
import jax
import jax.numpy as jnp
from jax import lax
from jax.experimental import pallas as pl
from jax.experimental.pallas import tpu as pltpu

F32 = jnp.float32
BF16 = jnp.bfloat16

D_MODEL = 1024
D_FF = 2816
EPS = 1e-6
CHUNK = 64
GLA_HEADS = 4
GLA_DK = 48
GLA_DV = 96
GLA_RANK = 16
GLA_GATE_NORM = 16.0
DIFF_HEADS = 4
DIFF_DQK = 48
DIFF_DV = 96
CONV_DIM = 256
CONV_WIDTH = 3
NUM_BUCKETS = 32
MAX_DISTANCE = 128
GLA_WIDTH = GLA_HEADS * GLA_DV
DIFF_WIDTH = DIFF_HEADS * DIFF_DV

LANES = 128
F32_SUBLANES = 8
BF16_SUBLANES = 16
MXU_DEPTH = 256
VMEM_LIMIT_BYTES = 56 * 1024 * 1024

HEAD_SLOT = 64
GLA_QK_LANES = GLA_HEADS * HEAD_SLOT
SUB = 16
N_SUB = CHUNK // SUB
KST_ROWS = SUB * (N_SUB * (N_SUB - 1) // 2)
DIFF_LANES = DIFF_HEADS * LANES
MASK_VALUE = -1e30
GLA_SAFE_DECAY = 60.0
LOG2E = math.log2(math.e)

OFF_GQ = 0
OFF_GK = OFF_GQ + GLA_QK_LANES
OFF_GV = OFF_GK + GLA_QK_LANES
OFF_GATE = OFF_GV + GLA_WIDTH
OFF_DQ = OFF_GATE + GLA_WIDTH
OFF_DK = OFF_DQ + DIFF_LANES
OFF_DV = OFF_DK + DIFF_LANES
OFF_CB = OFF_DV + DIFF_LANES
OFF_CC = OFF_CB + CONV_DIM
OFF_CH = OFF_CC + CONV_DIM
N_PROJ = OFF_CH + CONV_DIM
GLR_LANE = GLA_DK
assert GLR_LANE + GLA_RANK <= HEAD_SLOT

FFN_ROWS = 512
FFN_ROWS_PLAIN = 1024
FFN_F_SPLITS = (0, 6 * MXU_DEPTH, D_FF)
assert all((hi - lo) % MXU_DEPTH == 0 for lo, hi in zip(FFN_F_SPLITS[:-1], FFN_F_SPLITS[1:]))
PROJ_ROWS = 1024
ATT_TILE = 256
ATT_UNROLL_FAR = 36
ATT_UNROLL_NEAR = 22
ATT_FINISH_GROUP = 4


def _dot(a, b):
    return jnp.dot(a, b, preferred_element_type=F32)


def _dot_nt(a, b):
    return lax.dot_general(a, b, (((1,), (1,)), ((), ())), preferred_element_type=F32)


def _dot_tn(a, b):
    return lax.dot_general(a, b, (((0,), (0,)), ((), ())), preferred_element_type=F32)


def _split_bf16(x):
    hi = x.astype(BF16)
    lo = (x - hi.astype(F32)).astype(BF16)
    return hi, lo


def _rms(x, g):
    return x * lax.rsqrt(jnp.mean(x * x, axis=-1, keepdims=True) + EPS) * g


def _sigmoid(x):
    return 1.0 / (1.0 + jnp.exp(-x))


def _ffn_kernel(*refs, final, mix, cast):
    refs = list(refs)
    x_ref, g_ref, wg_ref, wu_ref, wd_ref, fg_ref = refs[:6]
    mix_refs = refs[6:12] if mix else ()
    n_in = 6 + len(mix_refs)
    src_refs = refs[n_in:n_in + 3] if cast else ()
    o_ref = refs[n_in + len(src_refs)]
    dst_refs = refs[n_in + len(src_refs) + 1:]
    for src, dst in zip(src_refs, dst_refs):
        dst[...] = src[...].astype(BF16)

    x = x_ref[...]
    if mix:
        gla_ref, diff_ref, conv_ref, w1_ref, w2_ref, w3_ref = mix_refs
        x = (x + _dot(gla_ref[...], w1_ref[...]) + _dot(diff_ref[...], w2_ref[...])
             + _dot(conv_ref[...], w3_ref[...]))
    h = _rms(x, g_ref[...]).astype(BF16)
    acc = None
    for lo, hi in zip(FFN_F_SPLITS[:-1], FFN_F_SPLITS[1:]):
        sl = slice(lo, hi)
        gate = _dot(h, wg_ref[:, sl])
        up = _dot(h, wu_ref[:, sl])
        a = (gate * _sigmoid(gate) * up).astype(BF16)
        part = _dot(a, wd_ref[sl, :])
        acc = part if acc is None else acc + part
    y = x + 0.5 * acc
    if final:
        y = _rms(y, fg_ref[...])
    o_ref[...] = y


def _const_spec(shape):
    nd = len(shape)
    return pl.BlockSpec(shape, lambda *_: (0,) * nd, pipeline_mode=pl.Buffered(1))


def _cast_block(n_rows, steps):
    share = 1
    while (n_rows * share) % (steps * BF16_SUBLANES) or steps % share:
        share *= 2
        assert share <= steps
    return n_rows * share // steps, share


def _ffn(x, g, wg, wu, wd, final_g, final, mix=None, cast=None):
    t = x.shape[0]
    tm = min(FFN_ROWS if mix is not None else FFN_ROWS_PLAIN, t)
    steps = t // tm
    row = lambda width: pl.BlockSpec((tm, width), lambda i: (i, 0))
    operands = [x, g, wg, wu, wd, final_g]
    in_specs = [row(D_MODEL), _const_spec((1, D_MODEL)), _const_spec((D_MODEL, D_FF)),
                _const_spec((D_MODEL, D_FF)), _const_spec((D_FF, D_MODEL)), _const_spec((1, D_MODEL))]
    out_shapes = [jax.ShapeDtypeStruct((t, D_MODEL), F32)]
    out_specs = [row(D_MODEL)]
    if mix is not None:
        operands += list(mix)
        in_specs += [row(a.shape[1]) for a in mix[:3]] + [_const_spec(w.shape) for w in mix[3:]]
    if cast is not None:
        stacked, layer = cast
        for w in stacked:
            rows, share = _cast_block(w.shape[1], steps)
            operands.append(w)
            in_specs.append(pl.BlockSpec((None, rows, w.shape[2]),
                                         lambda i, share=share: (layer, i // share, 0)))
            out_shapes.append(jax.ShapeDtypeStruct(w.shape[1:], BF16))
            out_specs.append(pl.BlockSpec((rows, w.shape[2]), lambda i, share=share: (i // share, 0)))
    outs = pl.pallas_call(
        functools.partial(_ffn_kernel, final=final, mix=mix is not None, cast=cast is not None),
        out_shape=tuple(out_shapes),
        grid=(steps,),
        in_specs=in_specs,
        out_specs=tuple(out_specs),
        compiler_params=pltpu.CompilerParams(
            dimension_semantics=("arbitrary",), vmem_limit_bytes=VMEM_LIMIT_BYTES),
        name="ffn",
    )(*operands)
    return outs[0], tuple(outs[1:])


def _gla_cum_matrix():
    t = np.arange(CHUNK)[:, None]
    u = np.arange(CHUNK)[None, :]
    tri = (u <= t)
    ref = (u <= (t // SUB) * SUB - 1)
    ones = np.ones((CHUNK, CHUNK), bool)
    parts = [tri, ref, ones]
    for j in range(1, N_SUB):
        s = np.arange(SUB * j)[:, None]
        parts.append((u > s) & (u <= SUB * j - 1))
    return np.concatenate(parts, axis=0).astype(np.float32)


def _gla_score_mask():
    t = np.arange(CHUNK)[:, None]
    group = np.concatenate([np.full(SUB * j, j) for j in range(1, N_SUB)])
    keep = (group[None, :] == (t // SUB))
    return np.tile(keep, (1, GLA_HEADS)).astype(np.float32)


def _gla_head_ids():
    qk_head = lax.broadcasted_iota(jnp.int32, (1, GLA_QK_LANES), 1) // HEAD_SLOT
    v_head = lax.broadcasted_iota(jnp.int32, (1, GLA_WIDTH), 1) // GLA_DV
    row_head_v = lax.broadcasted_iota(jnp.int32, (GLA_WIDTH, 1), 0) // GLA_DV
    return qk_head, v_head, row_head_v == qk_head


def _gla_fast(q, k, g, v, state, tri):
    rows = q.shape[0]
    n_chunks = rows // CHUNK
    qk_head, v_head, state_mask = _gla_head_ids()
    chunk = lambda x, c: x[c * CHUNK:(c + 1) * CHUNK]
    sel_rows = pl.cdiv(n_chunks, BF16_SUBLANES) * BF16_SUBLANES
    sel_r = lax.broadcasted_iota(jnp.int32, (sel_rows, rows), 0)
    sel_c = lax.broadcasted_iota(jnp.int32, (sel_rows, rows), 1) // CHUNK
    chunk_sel = jnp.where(sel_r == sel_c, 1.0, 0.0).astype(BF16)
    g_hi, g_lo = _split_bf16(g)
    totals = _dot(chunk_sel, g_hi) + _dot(chunk_sel, g_lo)
    b = jnp.concatenate([_dot(tri, chunk(g_hi, c)) + _dot(tri, chunk(g_lo, c))
                         for c in range(n_chunks)], axis=0)
    b_last = jnp.concatenate([jnp.broadcast_to(totals[c:c + 1], (CHUNK, GLA_QK_LANES))
                              for c in range(n_chunks)], axis=0)
    qd = (q * jnp.exp(b)).astype(BF16)
    kd = (k * jnp.exp(-b)).astype(BF16)
    kw = (k * jnp.exp(b_last - b)).astype(BF16)
    t_idx = lax.broadcasted_iota(jnp.int32, (CHUNK, GLA_QK_LANES), 0)
    s_idx = lax.broadcasted_iota(jnp.int32, (CHUNK, GLA_QK_LANES), 1) % HEAD_SLOT
    causal = s_idx <= t_idx
    intra, update = [], []
    for c in range(n_chunks):
        kd_c, v_c = chunk(kd, c), chunk(v, c)
        kd_bd = jnp.concatenate(
            [jnp.where(qk_head == hd, kd_c, jnp.zeros_like(kd_c)) for hd in range(GLA_HEADS)],
            axis=0)
        a = _dot_nt(chunk(qd, c), kd_bd)
        a = jnp.where(causal, a, 0.0).astype(BF16)
        v_bd = jnp.concatenate(
            [jnp.where(v_head == hd, v_c, jnp.zeros_like(v_c)) for hd in range(GLA_HEADS)],
            axis=0)
        intra.append(_dot(a, v_bd))
        update.append(_dot_tn(v_c, chunk(kw, c)))
    raw = []
    for c in range(n_chunks):
        raw.append(intra[c] + _dot_nt(chunk(qd, c), state.astype(BF16)))
        state = state * jnp.exp(totals[c:c + 1]) + jnp.where(state_mask, update[c], 0.0)
    return jnp.concatenate(raw, axis=0), state


def _gla_general(q_ref, k_ref, g_ref, v_ref, s_ref, raw_ref, cum_ref, amask_ref):
    n_chunks = q_ref.shape[0] // CHUNK
    qk_head, v_head, state_mask = _gla_head_ids()
    cum = cum_ref[...]
    amask = amask_ref[...] > 0.5
    ind_r = lax.broadcasted_iota(jnp.int32, (GLA_QK_LANES, LANES), 0) // HEAD_SLOT
    ind_c = lax.broadcasted_iota(jnp.int32, (GLA_QK_LANES, LANES), 1)
    ind_sum = jnp.where(ind_r == ind_c, 1.0, 0.0).astype(BF16)
    bc_r = lax.broadcasted_iota(jnp.int32, (LANES, GLA_WIDTH), 0)
    bc_c = lax.broadcasted_iota(jnp.int32, (LANES, GLA_WIDTH), 1) // GLA_DV
    ind_bcast = jnp.where(bc_r == bc_c, 1.0, 0.0).astype(BF16)
    row_in_sub = lax.broadcasted_iota(jnp.int32, (CHUNK, 1), 0) % SUB

    def chunk_body(c, carry):
        r0 = pl.multiple_of(c * CHUNK, CHUNK)
        q = q_ref[pl.ds(r0, CHUNK), :]
        k = k_ref[pl.ds(r0, CHUNK), :]
        g = g_ref[pl.ds(r0, CHUNK), :]
        v = v_ref[pl.ds(r0, CHUNK), :]
        g_hi, g_lo = _split_bf16(g)
        cums = _dot(cum, g_hi) + _dot(cum, g_lo)
        b = cums[0:CHUNK]
        b_ref = cums[CHUNK:2 * CHUNK]
        b_last = cums[2 * CHUNK:3 * CHUNK]
        e_kst = cums[3 * CHUNK:3 * CHUNK + KST_ROWS]

        state = s_ref[...]
        q_in = (q * jnp.exp(b)).astype(BF16)
        o = _dot_nt(q_in, state.astype(BF16))

        kw = (k * jnp.exp(b_last - b)).astype(BF16)
        decay = jnp.exp(jnp.concatenate([b_last] * (GLA_WIDTH // CHUNK), axis=0))
        s_ref[...] = state * decay + jnp.where(state_mask, _dot_tn(v, kw), 0.0)

        q_sub = (q * jnp.exp(jnp.minimum(b - b_ref, 0.0))).astype(BF16)
        k_st = jnp.concatenate([k[0:SUB * j] for j in range(1, N_SUB)], axis=0) * jnp.exp(e_kst)
        k_bd = jnp.concatenate(
            [jnp.where(qk_head == hd, k_st, 0.0) for hd in range(GLA_HEADS)], axis=0).astype(BF16)
        a = _dot_nt(q_sub, k_bd)
        a = jnp.where(amask, a, 0.0).astype(BF16)
        v_st = jnp.concatenate([v[0:SUB * j] for j in range(1, N_SUB)], axis=0)
        v_bd = jnp.concatenate(
            [jnp.where(v_head == hd, v_st, jnp.zeros_like(v_st)) for hd in range(GLA_HEADS)], axis=0)
        o = o + _dot(a, v_bd)

        vf = v.astype(F32)
        prods = []
        for d in range(SUB):
            ks = k if d == 0 else pltpu.roll(k, d, 0)
            bs = b if d == 0 else pltpu.roll(b, d, 0)
            prods.append((q * ks * jnp.exp(jnp.minimum(b - bs, 0.0))).astype(BF16))
        dsum = _dot(jnp.concatenate(prods, axis=0), ind_sum)
        valid = jnp.concatenate([row_in_sub >= d for d in range(SUB)], axis=0)
        dsum = jnp.where(valid, dsum, 0.0).astype(BF16)
        dbc = _dot(dsum, ind_bcast)
        for d in range(SUB):
            vs = vf if d == 0 else pltpu.roll(vf, d, 0)
            o = o + dbc[d * CHUNK:(d + 1) * CHUNK] * vs
        raw_ref[pl.ds(r0, CHUNK), :] = o
        return carry

    lax.fori_loop(0, n_chunks, chunk_body, 0)


def _gla_finish(o, gate, gain):
    hr = lax.broadcasted_iota(jnp.int32, (GLA_WIDTH, GLA_WIDTH), 0) // GLA_DV
    hc = lax.broadcasted_iota(jnp.int32, (GLA_WIDTH, GLA_WIDTH), 1) // GLA_DV
    head_ones = jnp.where(hr == hc, 1.0, 0.0).astype(BF16)
    ms = _dot((o * o).astype(BF16), head_ones) * (1.0 / GLA_DV)
    return (o * lax.rsqrt(ms + EPS) * gain * (gate * _sigmoid(gate))).astype(BF16)


def _proj_kernel(x_ref, g_ref, w_ref, gkup_ref, gkb_ref, cw_ref, cum_ref, tri_ref, amask_ref, gn_ref,
                 gla_ref, dq_ref, dk_ref, dv_ref, conv_ref,
                 zbuf_ref, q_s, k_s, g_s, v_s, s_ref, s_next_ref, raw_ref, *, tiles_per_seq):
    tm = x_ref.shape[0]
    h = _rms(x_ref[...], g_ref[...]).astype(BF16)

    @pl.when(pl.program_id(0) % tiles_per_seq == 0)
    def _():
        zbuf_ref[0:F32_SUBLANES, :] = jnp.zeros((F32_SUBLANES, CONV_DIM), F32)
        s_ref[...] = jnp.zeros_like(s_ref)

    def proj(off, width):
        return _dot(h, w_ref[:, off:off + width])

    qk = proj(OFF_GQ, 2 * GLA_QK_LANES)
    gq = qk[:, 0:GLA_QK_LANES]
    gk = qk[:, GLA_QK_LANES:]
    vg = proj(OFF_GV, 2 * GLA_WIDTH)
    gv = vg[:, 0:GLA_WIDTH].astype(BF16)
    gate = vg[:, GLA_WIDTH:]

    glr_hi, glr_lo = _split_bf16(gq)
    up_hi = gkup_ref[0]
    up_lo = gkup_ref[1]
    z = _dot(glr_hi, up_hi) + _dot(glr_lo, up_hi) + _dot(glr_hi, up_lo) + gkb_ref[...]
    logsig = jnp.minimum(z, 0.0) - jnp.log1p(jnp.exp(-jnp.abs(z)))
    lane = lax.broadcasted_iota(jnp.int32, (tm, GLA_QK_LANES), 1)
    gg = jnp.where(lane % HEAD_SLOT < GLA_DK, logsig * (1.0 / GLA_GATE_NORM), 0.0)

    dqk = proj(OFF_DQ, 2 * DIFF_LANES)
    dq_ref[...] = (dqk[:, 0:DIFF_LANES] * (DIFF_DQK ** -0.5 * LOG2E)).astype(BF16)
    dk_ref[...] = dqk[:, DIFF_LANES:].astype(BF16)
    lane_v = lax.broadcasted_iota(jnp.int32, (tm, DIFF_LANES), 1)
    dv_ref[...] = jnp.where(lane_v % LANES == DIFF_DV, 1.0, proj(OFF_DV, DIFF_LANES)).astype(BF16)

    conv = proj(OFF_CB, 3 * CONV_DIM)
    zc = conv[:, CONV_DIM:2 * CONV_DIM] * conv[:, 2 * CONV_DIM:]
    zbuf_ref[F32_SUBLANES:F32_SUBLANES + tm, :] = zc
    cw = cw_ref[...]
    y = (cw[2:3, :] * zc + cw[1:2, :] * zbuf_ref[F32_SUBLANES - 1:F32_SUBLANES - 1 + tm, :]
         + cw[0:1, :] * zbuf_ref[F32_SUBLANES - 2:F32_SUBLANES - 2 + tm, :])
    conv_ref[...] = (conv[:, 0:CONV_DIM] * y).astype(BF16)
    zbuf_ref[0:F32_SUBLANES, :] = zbuf_ref[tm:tm + F32_SUBLANES, :]

    q_scaled = gq * (GLA_DK ** -0.5)
    raw, new_state = _gla_fast(q_scaled, gk, gg, gv, s_ref[...], tri_ref[...])
    gla_ref[...] = _gla_finish(raw, gate, gn_ref[...])
    s_next_ref[...] = new_state
    q_s[...] = q_scaled
    k_s[...] = gk
    g_s[...] = gg
    v_s[...] = gv
    raw_ref[...] = gate
    safe = jnp.min(gg) * CHUNK >= -GLA_SAFE_DECAY

    @pl.when(safe)
    def _():
        s_ref[...] = s_next_ref[...]

    @pl.when(jnp.logical_not(safe))
    def _():
        gate_kept = raw_ref[...]
        _gla_general(q_s, k_s, g_s, v_s, s_ref, raw_ref, cum_ref, amask_ref)
        gla_ref[...] = _gla_finish(raw_ref[...], gate_kept, gn_ref[...])


def _proj(x, g, w, gkup, gkb, cw, gn_tiled, seq):
    t = x.shape[0]
    tm = min(PROJ_ROWS, seq)
    row = lambda width: pl.BlockSpec((tm, width), lambda i: (i, 0))
    cum_np = _gla_cum_matrix()
    cum = jnp.asarray(cum_np, BF16)
    tri = jnp.asarray(cum_np[0:CHUNK], BF16)
    amask = jnp.asarray(_gla_score_mask(), F32)
    out_shapes = (
        jax.ShapeDtypeStruct((t, GLA_WIDTH), BF16),
        jax.ShapeDtypeStruct((t, DIFF_LANES), BF16),
        jax.ShapeDtypeStruct((t, DIFF_LANES), BF16),
        jax.ShapeDtypeStruct((t, DIFF_LANES), BF16),
        jax.ShapeDtypeStruct((t, CONV_DIM), BF16),
    )
    return pl.pallas_call(
        functools.partial(_proj_kernel, tiles_per_seq=seq // tm),
        out_shape=out_shapes,
        grid=(t // tm,),
        in_specs=[
            row(D_MODEL),
            _const_spec((1, D_MODEL)),
            _const_spec((D_MODEL, N_PROJ)),
            _const_spec((2, GLA_QK_LANES, GLA_QK_LANES)),
            _const_spec((1, GLA_QK_LANES)),
            _const_spec((F32_SUBLANES, CONV_DIM)),
            _const_spec(cum.shape),
            _const_spec(tri.shape),
            _const_spec(amask.shape),
            _const_spec((1, GLA_WIDTH)),
        ],
        out_specs=tuple(row(s.shape[1]) for s in out_shapes),
        scratch_shapes=[pltpu.VMEM((tm + F32_SUBLANES, CONV_DIM), F32),
                        pltpu.VMEM((tm, GLA_QK_LANES), F32),
                        pltpu.VMEM((tm, GLA_QK_LANES), F32),
                        pltpu.VMEM((tm, GLA_QK_LANES), F32),
                        pltpu.VMEM((tm, GLA_WIDTH), BF16),
                        pltpu.VMEM((GLA_WIDTH, GLA_QK_LANES), F32),
                        pltpu.VMEM((GLA_WIDTH, GLA_QK_LANES), F32),
                        pltpu.VMEM((tm, GLA_WIDTH), F32)],
        compiler_params=pltpu.CompilerParams(
            dimension_semantics=("arbitrary",), vmem_limit_bytes=VMEM_LIMIT_BYTES),
        name="mixer_proj_gla",
    )(x, g, w, gkup, gkb, cw, cum, tri, amask, gn_tiled)


def _t5_bucket(rel):
    nb = NUM_BUCKETS // 2
    max_exact = nb // 2
    ret = (rel > 0).astype(jnp.int32) * nb
    n = jnp.abs(rel)
    nf = jnp.maximum(n, 1).astype(jnp.float32)
    large = max_exact + (jnp.log(nf / max_exact) / math.log(MAX_DISTANCE / max_exact)
                         * (nb - max_exact)).astype(jnp.int32)
    large = jnp.minimum(large, nb - 1)
    return ret + jnp.where(n < max_exact, n, large)


def _bias_tiles(rel_bias, tile):
    assert tile >= MAX_DISTANCE and tile % CHUNK == 0
    table = rel_bias.astype(F32)
    far = table[NUM_BUCKETS // 2 - 1]
    heads = table.shape[1]
    rel = jnp.concatenate([jnp.arange(0, tile), jnp.arange(-2 * tile, 0)])
    onehot = _t5_bucket(rel)[None, :, None] == jnp.arange(NUM_BUCKETS)[None, None, :]
    per_rel = jnp.sum(jnp.where(onehot, table.T[:, None, :], 0.0), axis=-1)
    per_rel = (per_rel - far[:, None]) * LOG2E
    span = 3 * tile
    flat = jnp.tile(per_rel, (1, 2 * tile))[:, :2 * tile * (span - 1)]
    toep = flat.reshape(heads, 2 * tile, span - 1)[:, :, :tile]
    r = jnp.arange(tile)[:, None]
    c = jnp.arange(tile)[None, :]
    diag = jnp.where(((c // CHUNK) <= (r // CHUNK))[None], toep[:, :tile], MASK_VALUE)
    left = toep[:, tile:]
    return jnp.stack([left, diag], axis=1)


def _attn_kernel(q_ref, k_ref, v_ref, bias_ref, lamv_ref, subln_ref, o_ref,
                 s_buf, p_buf, alpha_buf, m_all, acc_all, *, tile, lam_init):
    seq = q_ref.shape[0]
    nq = seq // tile

    s_buf[...] = jnp.zeros(s_buf.shape, F32)
    p_buf[...] = jnp.zeros(p_buf.shape, BF16)
    alpha_buf[...] = jnp.zeros(alpha_buf.shape, F32)
    m_all[...] = jnp.full(m_all.shape, MASK_VALUE, F32)
    acc_all[...] = jnp.zeros(acc_all.shape, F32)

    lane = lax.broadcasted_iota(jnp.int32, (tile, LANES), 1)

    def stage1(i, j, slot):
        q = q_ref[pl.ds(pl.multiple_of(i * tile, tile), tile), :]
        zero = jnp.zeros_like(q)
        qs = jnp.concatenate([jnp.where(lane < LANES // 2, q, zero),
                              jnp.where(lane >= LANES // 2, q, zero)], axis=0)
        kj = k_ref[pl.ds(pl.multiple_of(j * tile, tile), tile), :]
        s_buf[slot] = _dot_nt(qs, kj)

    def stage2(i, bias, slot):
        for half in range(2):
            rows = slice(half * tile, (half + 1) * tile)
            s = s_buf[slot, rows, :]
            if bias is not None:
                s = s + bias
            m_prev = m_all[i, rows, :]
            m_next = jnp.maximum(m_prev, jnp.max(s, axis=-1, keepdims=True))
            p = jnp.exp2(s - jnp.concatenate([m_next] * (tile // LANES), axis=1))
            p_buf[slot, rows, :] = p.astype(BF16)
            alpha_buf[slot, rows, :] = jnp.exp2(m_prev - m_next)
            m_all[i, rows, :] = m_next

    def stage3(i, j, slot):
        vj = v_ref[pl.ds(pl.multiple_of(j * tile, tile), tile), :]
        acc_all[i] = alpha_buf[slot] * acc_all[i] + _dot(p_buf[slot], vj)

    def run_pipeline(n_pairs, first_pair, advance, with_bias, unroll):
        n_steps = pl.cdiv(n_pairs + 2, unroll) * unroll

        def step(t, pairs, slot_a, slot_b):
            (i0, j0), (i1, j1), (i2, j2) = pairs
            valid1 = jnp.logical_and(t >= 1, t <= n_pairs)
            valid2 = jnp.logical_and(t >= 2, t <= n_pairs + 1)
            stage3(jnp.where(valid2, i2, nq), j2, slot_a)
            stage2(jnp.where(valid1, i1, nq), bias_ref[0, j1 - i1 + 1] if with_bias else None, slot_b)
            stage1(i0, j0, slot_a)
            return (advance(i0, j0), (i0, j0), (i1, j1))

        def body(u, pairs):
            for r in range(unroll):
                pairs = step(unroll * u + r, pairs, r % 2, (r + 1) % 2)
            return pairs

        lax.fori_loop(0, n_steps // unroll, body, (first_pair,) * 3)

    def next_far(i, j):
        wrap = j == i - 2
        done = jnp.logical_and(wrap, i == nq - 1)
        step_i = jnp.logical_and(wrap, jnp.logical_not(done))
        return (jnp.where(step_i, i + 1, i), jnp.where(done, j, jnp.where(wrap, 0, j + 1)))

    def next_near(i, j):
        wrap = j == i
        done = jnp.logical_and(wrap, i == nq - 1)
        step_i = jnp.logical_and(wrap, jnp.logical_not(done))
        return (jnp.where(step_i, i + 1, i), jnp.where(jnp.logical_or(done, wrap), j, j + 1))

    zero = jnp.int32(0)
    if nq > 2:
        run_pipeline((nq - 1) * (nq - 2) // 2, (jnp.int32(2), zero), next_far, False, ATT_UNROLL_FAR)
    run_pipeline(2 * nq - 1, (zero, zero), next_near, True, ATT_UNROLL_NEAR)

    lv = lamv_ref[...]
    lam = (jnp.exp(jnp.sum(lv[0:1] * lv[1:2], axis=-1, keepdims=True))
           - jnp.exp(jnp.sum(lv[2:3] * lv[3:4], axis=-1, keepdims=True)) + lam_init)

    def finish(i):
        acc = acc_all[i]
        a1 = acc[0:tile]
        a2 = acc[tile:2 * tile]
        ones_col = lane == DIFF_DV
        ratio = jnp.sum(jnp.where(ones_col, a1 / jnp.where(ones_col, a2, 1.0), 0.0),
                        axis=-1, keepdims=True)
        u = a1 - lam * (ratio * a2)
        sq = jnp.where(ones_col, a1 * math.sqrt(DIFF_DV * EPS), u)
        ms = jnp.sum(sq * sq, axis=-1, keepdims=True) * (1.0 / DIFF_DV)
        y = u * lax.rsqrt(ms) * subln_ref[...] * (1.0 - lam_init)
        o_ref[pl.ds(pl.multiple_of(i * tile, tile), tile), :] = y.astype(BF16)

    group = math.gcd(nq, ATT_FINISH_GROUP)

    def finish_group(u, carry):
        for r in range(group):
            finish(u * group + r)
        return carry

    lax.fori_loop(0, nq // group, finish_group, 0)


def _attn(dq, dk, dv, bias_tiles, lamv, subln, batch, seq, lam_init):
    tile = min(ATT_TILE, seq)
    nq = seq // tile
    head_block = pl.BlockSpec((seq, LANES), lambda b, h: (b, h))
    return pl.pallas_call(
        functools.partial(_attn_kernel, tile=tile, lam_init=lam_init),
        out_shape=jax.ShapeDtypeStruct((batch * seq, DIFF_LANES), BF16),
        grid=(batch, DIFF_HEADS),
        in_specs=[
            head_block, head_block, head_block,
            pl.BlockSpec((1, 2, tile, tile), lambda b, h: (h, 0, 0, 0)),
            pl.BlockSpec((4, LANES), lambda b, h: (0, 0)),
            pl.BlockSpec((1, LANES), lambda b, h: (0, 0)),
        ],
        out_specs=head_block,
        scratch_shapes=[pltpu.VMEM((2, 2 * tile, tile), F32),
                        pltpu.VMEM((2, 2 * tile, tile), BF16),
                        pltpu.VMEM((2, 2 * tile, LANES), F32),
                        pltpu.VMEM((nq + 1, 2 * tile, LANES), F32),
                        pltpu.VMEM((nq + 1, 2 * tile, LANES), F32)],
        compiler_params=pltpu.CompilerParams(
            dimension_semantics=("parallel", "parallel"), vmem_limit_bytes=VMEM_LIMIT_BYTES),
        name="diff_attn",
    )(dq, dk, dv, bias_tiles, lamv, subln)


def _pad_cols(w, width):
    return jnp.pad(w, ((0, 0), (0, width - w.shape[1])))


def _slot_cols(w, heads, used, slot):
    lead = w.shape[0]
    w = w.reshape(lead, heads, used)
    return jnp.pad(w, ((0, 0), (0, 0), (0, slot - used))).reshape(lead, heads * slot)


def _layout_w_in(w_in):
    offs = np.cumsum([0, GLA_HEADS * GLA_DK, GLA_HEADS * GLA_DK, GLA_WIDTH, GLA_RANK, GLA_WIDTH,
                      DIFF_HEADS * 2 * DIFF_DQK, DIFF_HEADS * 2 * DIFF_DQK, DIFF_WIDTH,
                      CONV_DIM, CONV_DIM, CONV_DIM])
    gq, gk, gv, glr, gate, dq, dk, dv, cb, cc, ch = [w_in[:, offs[n]:offs[n + 1]] for n in range(11)]
    half = LANES // 2
    gq_slots = _slot_cols(gq, GLA_HEADS, GLA_DK, HEAD_SLOT)
    gq_slots = gq_slots.at[:, GLR_LANE:GLR_LANE + GLA_RANK].set(glr)
    cols = [
        gq_slots,
        _slot_cols(gk, GLA_HEADS, GLA_DK, HEAD_SLOT),
        gv,
        gate,
        _slot_cols(dq, 2 * DIFF_HEADS, DIFF_DQK, half),
        _slot_cols(dk, 2 * DIFF_HEADS, DIFF_DQK, half),
        _slot_cols(dv, DIFF_HEADS, DIFF_DV, LANES),
        cb, cc, ch,
    ]
    w = jnp.concatenate(cols, axis=1)
    assert w.shape[1] == N_PROJ
    return w.astype(BF16)


def kernel(x, ffn1_norm, ffn1_gate, ffn1_up, ffn1_down, mix_norm, w_in, gla_gk_up, gla_gk_bias,
           gla_norm, diff_lambda_q1, diff_lambda_k1, diff_lambda_q2, diff_lambda_k2, diff_subln,
           rel_bias, conv_w, w_out, ffn2_norm, ffn2_gate, ffn2_up, ffn2_down, final_norm):
    batch, seq, _ = x.shape
    depth = w_in.shape[0]
    t = batch * seq
    xf = x.reshape(t, D_MODEL)
    bias_tiles = _bias_tiles(rel_bias, min(ATT_TILE, seq))
    final_g = final_norm.reshape(1, D_MODEL)
    ffn1_w = (ffn1_gate[0].astype(BF16), ffn1_up[0].astype(BF16), ffn1_down[0].astype(BF16))

    for l in range(depth):
        lam_init = 0.8 - 0.6 * math.exp(-0.3 * l)
        xf, ffn2_w = _ffn(xf, ffn1_norm[l].reshape(1, D_MODEL), *ffn1_w, final_g, False,
                          cast=((ffn2_gate, ffn2_up, ffn2_down), l))

        w_pad = _layout_w_in(w_in[l])
        up = _slot_cols(gla_gk_up[l], GLA_HEADS, GLA_DK, HEAD_SLOT)
        up = jnp.pad(up, ((GLR_LANE, GLA_QK_LANES - GLR_LANE - GLA_RANK), (0, 0)))
        up_hi = up.astype(BF16)
        up_lo = (up - up_hi.astype(F32)).astype(BF16)
        gkup = jnp.stack([up_hi, up_lo], axis=0)
        gkb = _slot_cols(gla_gk_bias[l].reshape(1, -1), GLA_HEADS, GLA_DK, HEAD_SLOT)
        cw = jnp.pad(conv_w[l], ((0, F32_SUBLANES - CONV_WIDTH), (0, 0)))
        gn_tiled = jnp.tile(gla_norm[l], GLA_HEADS).reshape(1, GLA_WIDTH)
        gla_o, dq, dk, dv, conv_o = _proj(
            xf, mix_norm[l].reshape(1, D_MODEL), w_pad, gkup, gkb, cw, gn_tiled, seq)

        lamv = jnp.stack([diff_lambda_q1[l], diff_lambda_k1[l], diff_lambda_q2[l], diff_lambda_k2[l]])
        lamv = _pad_cols(lamv.astype(F32), LANES)
        subln = _pad_cols(diff_subln[l].reshape(1, DIFF_DV), LANES)
        diff_o = _attn(dq, dk, dv, bias_tiles, lamv, subln, batch, seq, lam_init)

        wo = w_out[l]
        w1 = wo[0:GLA_WIDTH].astype(BF16)
        w2 = wo[GLA_WIDTH:GLA_WIDTH + DIFF_WIDTH].reshape(DIFF_HEADS, DIFF_DV, D_MODEL)
        w2 = jnp.pad(w2, ((0, 0), (0, LANES - DIFF_DV), (0, 0))).reshape(DIFF_LANES, D_MODEL).astype(BF16)
        w3 = wo[GLA_WIDTH + DIFF_WIDTH:].astype(BF16)
        last = l == depth - 1
        xf, ffn1_w = _ffn(xf, ffn2_norm[l].reshape(1, D_MODEL), *ffn2_w, final_g, last,
                          mix=(gla_o, diff_o, conv_o, w1, w2, w3),
                          cast=None if last else ((ffn1_gate, ffn1_up, ffn1_down), l + 1))

    return xf.reshape(batch, seq, D_MODEL)
```

```python
import functools
import math

import numpy as np
import jax
import jax.numpy as jnp
from jax import lax
from jax.experimental import pallas as pl
from jax.experimental.pallas import tpu as pltpu

F32 = jnp.float32
BF16 = jnp.bfloat16

D_MODEL = 1024
D_FF = 2816
EPS = 1e-6
CHUNK = 64
GLA_HEADS = 4
GLA_DK = 48
GLA_DV = 96
GLA_RANK = 16
GLA_GATE_NORM = 16.0
DIFF_HEADS = 4
DIFF_DQK = 48
DIFF_DV = 96
CONV_DIM = 256
CONV_WIDTH = 3
NUM_BUCKETS = 32
MAX_DISTANCE = 128
GLA_WIDTH = GLA_HEADS * GLA_DV
DIFF_WIDTH = DIFF_HEADS * DIFF_DV

LANES = 128
F32_SUBLANES = 8
BF16_SUBLANES = 16
MXU_DEPTH = 256
VMEM_LIMIT_BYTES = 62 * 1024 * 1024

HEAD_SLOT = 64
GLA_QK_LANES = GLA_HEADS * HEAD_SLOT
SUB = 16
N_SUB = CHUNK // SUB
KST_ROWS = SUB * (N_SUB * (N_SUB - 1) // 2)
DIFF_LANES = DIFF_HEADS * LANES
MASK_VALUE = -1e30
GLA_SAFE_DECAY = 60.0
LOG2E = math.log2(math.e)

OFF_GQ = 0
OFF_GK = OFF_GQ + GLA_QK_LANES
OFF_GV = OFF_GK + GLA_QK_LANES
OFF_GATE = OFF_GV + GLA_WIDTH
OFF_DQ = OFF_GATE + GLA_WIDTH
OFF_DK = OFF_DQ + DIFF_LANES
OFF_DV = OFF_DK + DIFF_LANES
OFF_CB = OFF_DV + DIFF_LANES
OFF_CC = OFF_CB + CONV_DIM
OFF_CH = OFF_CC + CONV_DIM
N_PROJ = OFF_CH + CONV_DIM
GLR_LANE = GLA_DK
assert GLR_LANE + GLA_RANK <= HEAD_SLOT

FFN_ROWS = 1024
FFN_ROWS_PLAIN = 1024
FFN_F_SPLITS = (0, 6 * MXU_DEPTH, D_FF)
assert all((hi - lo) % MXU_DEPTH == 0 for lo, hi in zip(FFN_F_SPLITS[:-1], FFN_F_SPLITS[1:]))
PROJ_ROWS = 1024
ATT_TILE = 256
ATT_UNROLL_FAR = 36
ATT_UNROLL_NEAR = 22
ATT_FINISH_GROUP = 4


def _dot(a, b):
    return jnp.dot(a, b, preferred_element_type=F32)


def _dot_nt(a, b):
    return lax.dot_general(a, b, (((1,), (1,)), ((), ())), preferred_element_type=F32)


def _dot_tn(a, b):
    return lax.dot_general(a, b, (((0,), (0,)), ((), ())), preferred_element_type=F32)


def _split_bf16(x):
    hi = x.astype(BF16)
    lo = (x - hi.astype(F32)).astype(BF16)
    return hi, lo


def _rms(x, g):
    return x * lax.rsqrt(jnp.mean(x * x, axis=-1, keepdims=True) + EPS) * g


def _sigmoid(x):
    return 1.0 / (1.0 + jnp.exp(-x))


def _ffn_kernel(*refs, final, mix, cast):
    refs = list(refs)
    x_ref, g_ref, wg_ref, wu_ref, wd_ref, fg_ref = refs[:6]
    mix_refs = refs[6:12] if mix else ()
    n_in = 6 + len(mix_refs)
    src_refs = refs[n_in:n_in + 3] if cast else ()
    o_ref = refs[n_in + len(src_refs)]
    dst_refs = refs[n_in + len(src_refs) + 1:]
    for src, dst in zip(src_refs, dst_refs):
        dst[...] = src[...].astype(BF16)

    x = x_ref[...]
    if mix:
        gla_ref, diff_ref, conv_ref, w1_ref, w2_ref, w3_ref = mix_refs
        x = (x + _dot(gla_ref[...], w1_ref[...]) + _dot(diff_ref[...], w2_ref[...])
             + _dot(conv_ref[...], w3_ref[...]))
    h = _rms(x, g_ref[...]).astype(BF16)
    acc = None
    for lo, hi in zip(FFN_F_SPLITS[:-1], FFN_F_SPLITS[1:]):
        sl = slice(lo, hi)
        gate = _dot(h, wg_ref[:, sl])
        up = _dot(h, wu_ref[:, sl])
        a = (gate * _sigmoid(gate) * up).astype(BF16)
        part = _dot(a, wd_ref[sl, :])
        acc = part if acc is None else acc + part
    y = x + 0.5 * acc
    if final:
        y = _rms(y, fg_ref[...])
    o_ref[...] = y


def _const_spec(shape):
    nd = len(shape)
    return pl.BlockSpec(shape, lambda *_: (0,) * nd, pipeline_mode=pl.Buffered(1))


def _cast_block(n_rows, steps):
    share = 1
    while (n_rows * share) % (steps * BF16_SUBLANES) or steps % share:
        share *= 2
        assert share <= steps
    return n_rows * share // steps, share


def _ffn(x, g, wg, wu, wd, final_g, final, mix=None, cast=None):
    t = x.shape[0]
    tm = min(FFN_ROWS if mix is not None else FFN_ROWS_PLAIN, t)
    steps = t // tm
    row = lambda width: pl.BlockSpec((tm, width), lambda i: (i, 0))
    operands = [x, g, wg, wu, wd, final_g]
    in_specs = [row(D_MODEL), _const_spec((1, D_MODEL)), _const_spec((D_MODEL, D_FF)),
                _const_spec((D_MODEL, D_FF)), _const_spec((D_FF, D_MODEL)), _const_spec((1, D_MODEL))]
    out_shapes = [jax.ShapeDtypeStruct((t, D_MODEL), F32)]
    out_specs = [row(D_MODEL)]
    if mix is not None:
        operands += list(mix)
        in_specs += [row(a.shape[1]) for a in mix[:3]] + [_const_spec(w.shape) for w in mix[3:]]
    if cast is not None:
        stacked, layer = cast
        for w in stacked:
            rows, share = _cast_block(w.shape[1], steps)
            operands.append(w)
            in_specs.append(pl.BlockSpec((None, rows, w.shape[2]),
                                         lambda i, share=share: (layer, i // share, 0)))
            out_shapes.append(jax.ShapeDtypeStruct(w.shape[1:], BF16))
            out_specs.append(pl.BlockSpec((rows, w.shape[2]), lambda i, share=share: (i // share, 0)))
    outs = pl.pallas_call(
        functools.partial(_ffn_kernel, final=final, mix=mix is not None, cast=cast is not None),
        out_shape=tuple(out_shapes),
        grid=(steps,),
        in_specs=in_specs,
        out_specs=tuple(out_specs),
        compiler_params=pltpu.CompilerParams(
            dimension_semantics=("arbitrary",), vmem_limit_bytes=VMEM_LIMIT_BYTES),
        name="ffn",
    )(*operands)
    return outs[0], tuple(outs[1:])


def _gla_cum_matrix():
    t = np.arange(CHUNK)[:, None]
    u = np.arange(CHUNK)[None, :]
    tri = (u <= t)
    ref = (u <= (t // SUB) * SUB - 1)
    ones = np.ones((CHUNK, CHUNK), bool)
    parts = [tri, ref, ones]
    for j in range(1, N_SUB):
        s = np.arange(SUB * j)[:, None]
        parts.append((u > s) & (u <= SUB * j - 1))
    return np.concatenate(parts, axis=0).astype(np.float32)


def _gla_score_mask():
    t = np.arange(CHUNK)[:, None]
    group = np.concatenate([np.full(SUB * j, j) for j in range(1, N_SUB)])
    keep = (group[None, :] == (t // SUB))
    return np.tile(keep, (1, GLA_HEADS)).astype(np.float32)


def _gla_head_ids():
    qk_head = lax.broadcasted_iota(jnp.int32, (1, GLA_QK_LANES), 1) // HEAD_SLOT
    v_head = lax.broadcasted_iota(jnp.int32, (1, GLA_WIDTH), 1) // GLA_DV
    row_head_v = lax.broadcasted_iota(jnp.int32, (GLA_WIDTH, 1), 0) // GLA_DV
    return qk_head, v_head, row_head_v == qk_head


def _gla_fast(q, k, g, v, state, tri):
    rows = q.shape[0]
    n_chunks = rows // CHUNK
    qk_head, v_head, state_mask = _gla_head_ids()
    chunk = lambda x, c: x[c * CHUNK:(c + 1) * CHUNK]
    sel_rows = pl.cdiv(n_chunks, BF16_SUBLANES) * BF16_SUBLANES
    sel_r = lax.broadcasted_iota(jnp.int32, (sel_rows, rows), 0)
    sel_c = lax.broadcasted_iota(jnp.int32, (sel_rows, rows), 1) // CHUNK
    chunk_sel = jnp.where(sel_r == sel_c, 1.0, 0.0).astype(BF16)
    g_hi, g_lo = _split_bf16(g)
    totals = _dot(chunk_sel, g_hi) + _dot(chunk_sel, g_lo)
    b = jnp.concatenate([_dot(tri, chunk(g_hi, c)) + _dot(tri, chunk(g_lo, c))
                         for c in range(n_chunks)], axis=0)
    b_last = jnp.concatenate([jnp.broadcast_to(totals[c:c + 1], (CHUNK, GLA_QK_LANES))
                              for c in range(n_chunks)], axis=0)
    qd = (q * jnp.exp(b)).astype(BF16)
    kd = (k * jnp.exp(-b)).astype(BF16)
    kw = (k * jnp.exp(b_last - b)).astype(BF16)
    t_idx = lax.broadcasted_iota(jnp.int32, (CHUNK, GLA_QK_LANES), 0)
    s_idx = lax.broadcasted_iota(jnp.int32, (CHUNK, GLA_QK_LANES), 1) % HEAD_SLOT
    causal = s_idx <= t_idx
    intra, update = [], []
    for c in range(n_chunks):
        kd_c, v_c = chunk(kd, c), chunk(v, c)
        kd_bd = jnp.concatenate(
            [jnp.where(qk_head == hd, kd_c, jnp.zeros_like(kd_c)) for hd in range(GLA_HEADS)],
            axis=0)
        a = _dot_nt(chunk(qd, c), kd_bd)
        a = jnp.where(causal, a, 0.0).astype(BF16)
        v_bd = jnp.concatenate(
            [jnp.where(v_head == hd, v_c, jnp.zeros_like(v_c)) for hd in range(GLA_HEADS)],
            axis=0)
        intra.append(_dot(a, v_bd))
        update.append(_dot_tn(v_c, chunk(kw, c)))
    raw = []
    for c in range(n_chunks):
        raw.append(intra[c] + _dot_nt(chunk(qd, c), state.astype(BF16)))
        state = state * jnp.exp(totals[c:c + 1]) + jnp.where(state_mask, update[c], 0.0)
    return jnp.concatenate(raw, axis=0), state


def _gla_general(q_ref, k_ref, g_ref, v_ref, s_ref, raw_ref, cum_ref, amask_ref):
    n_chunks = q_ref.shape[0] // CHUNK
    qk_head, v_head, state_mask = _gla_head_ids()
    cum = cum_ref[...]
    amask = amask_ref[...] > 0.5
    ind_r = lax.broadcasted_iota(jnp.int32, (GLA_QK_LANES, LANES), 0) // HEAD_SLOT
    ind_c = lax.broadcasted_iota(jnp.int32, (GLA_QK_LANES, LANES), 1)
    ind_sum = jnp.where(ind_r == ind_c, 1.0, 0.0).astype(BF16)
    bc_r = lax.broadcasted_iota(jnp.int32, (LANES, GLA_WIDTH), 0)
    bc_c = lax.broadcasted_iota(jnp.int32, (LANES, GLA_WIDTH), 1) // GLA_DV
    ind_bcast = jnp.where(bc_r == bc_c, 1.0, 0.0).astype(BF16)
    row_in_sub = lax.broadcasted_iota(jnp.int32, (CHUNK, 1), 0) % SUB

    def chunk_body(c, carry):
        r0 = pl.multiple_of(c * CHUNK, CHUNK)
        q = q_ref[pl.ds(r0, CHUNK), :]
        k = k_ref[pl.ds(r0, CHUNK), :]
        g = g_ref[pl.ds(r0, CHUNK), :]
        v = v_ref[pl.ds(r0, CHUNK), :]
        g_hi, g_lo = _split_bf16(g)
        cums = _dot(cum, g_hi) + _dot(cum, g_lo)
        b = cums[0:CHUNK]
        b_ref = cums[CHUNK:2 * CHUNK]
        b_last = cums[2 * CHUNK:3 * CHUNK]
        e_kst = cums[3 * CHUNK:3 * CHUNK + KST_ROWS]

        state = s_ref[...]
        q_in = (q * jnp.exp(b)).astype(BF16)
        o = _dot_nt(q_in, state.astype(BF16))

        kw = (k * jnp.exp(b_last - b)).astype(BF16)
        decay = jnp.exp(jnp.concatenate([b_last] * (GLA_WIDTH // CHUNK), axis=0))
        s_ref[...] = state * decay + jnp.where(state_mask, _dot_tn(v, kw), 0.0)

        q_sub = (q * jnp.exp(jnp.minimum(b - b_ref, 0.0))).astype(BF16)
        k_st = jnp.concatenate([k[0:SUB * j] for j in range(1, N_SUB)], axis=0) * jnp.exp(e_kst)
        k_bd = jnp.concatenate(
            [jnp.where(qk_head == hd, k_st, 0.0) for hd in range(GLA_HEADS)], axis=0).astype(BF16)
        a = _dot_nt(q_sub, k_bd)
        a = jnp.where(amask, a, 0.0).astype(BF16)
        v_st = jnp.concatenate([v[0:SUB * j] for j in range(1, N_SUB)], axis=0)
        v_bd = jnp.concatenate(
            [jnp.where(v_head == hd, v_st, jnp.zeros_like(v_st)) for hd in range(GLA_HEADS)], axis=0)
        o = o + _dot(a, v_bd)

        vf = v.astype(F32)
        prods = []
        for d in range(SUB):
            ks = k if d == 0 else pltpu.roll(k, d, 0)
            bs = b if d == 0 else pltpu.roll(b, d, 0)
            prods.append((q * ks * jnp.exp(jnp.minimum(b - bs, 0.0))).astype(BF16))
        dsum = _dot(jnp.concatenate(prods, axis=0), ind_sum)
        valid = jnp.concatenate([row_in_sub >= d for d in range(SUB)], axis=0)
        dsum = jnp.where(valid, dsum, 0.0).astype(BF16)
        dbc = _dot(dsum, ind_bcast)
        for d in range(SUB):
            vs = vf if d == 0 else pltpu.roll(vf, d, 0)
            o = o + dbc[d * CHUNK:(d + 1) * CHUNK] * vs
        raw_ref[pl.ds(r0, CHUNK), :] = o
        return carry

    lax.fori_loop(0, n_chunks, chunk_body, 0)


def _gla_finish(o, gate, gain):
    hr = lax.broadcasted_iota(jnp.int32, (GLA_WIDTH, GLA_WIDTH), 0) // GLA_DV
    hc = lax.broadcasted_iota(jnp.int32, (GLA_WIDTH, GLA_WIDTH), 1) // GLA_DV
    head_ones = jnp.where(hr == hc, 1.0, 0.0).astype(BF16)
    ms = _dot((o * o).astype(BF16), head_ones) * (1.0 / GLA_DV)
    return (o * lax.rsqrt(ms + EPS) * gain * (gate * _sigmoid(gate))).astype(BF16)


def _proj_kernel(x_ref, g_ref, w_ref, gkup_ref, gkb_ref, cw_ref, cum_ref, tri_ref, amask_ref, gn_ref,
                 gla_ref, dq_ref, dk_ref, dv_ref, conv_ref,
                 zbuf_ref, q_s, k_s, g_s, v_s, s_ref, s_next_ref, raw_ref, *, tiles_per_seq):
    tm = x_ref.shape[0]
    h = _rms(x_ref[...], g_ref[...]).astype(BF16)

    @pl.when(pl.program_id(0) % tiles_per_seq == 0)
    def _():
        zbuf_ref[0:F32_SUBLANES, :] = jnp.zeros((F32_SUBLANES, CONV_DIM), F32)
        s_ref[...] = jnp.zeros_like(s_ref)

    def proj(off, width):
        return _dot(h, w_ref[:, off:off + width])

    qk = proj(OFF_GQ, 2 * GLA_QK_LANES)
    gq = qk[:, 0:GLA_QK_LANES]
    gk = qk[:, GLA_QK_LANES:]
    vg = proj(OFF_GV, 2 * GLA_WIDTH)
    gv = vg[:, 0:GLA_WIDTH].astype(BF16)
    gate = vg[:, GLA_WIDTH:]

    glr_hi, glr_lo = _split_bf16(gq)
    up_hi = gkup_ref[0]
    up_lo = gkup_ref[1]
    z = _dot(glr_hi, up_hi) + _dot(glr_lo, up_hi) + _dot(glr_hi, up_lo) + gkb_ref[...]
    logsig = jnp.minimum(z, 0.0) - jnp.log1p(jnp.exp(-jnp.abs(z)))
    lane = lax.broadcasted_iota(jnp.int32, (tm, GLA_QK_LANES), 1)
    gg = jnp.where(lane % HEAD_SLOT < GLA_DK, logsig * (1.0 / GLA_GATE_NORM), 0.0)

    dqk = proj(OFF_DQ, 2 * DIFF_LANES)
    dq_ref[...] = (dqk[:, 0:DIFF_LANES] * (DIFF_DQK ** -0.5 * LOG2E)).astype(BF16)
    dk_ref[...] = dqk[:, DIFF_LANES:].astype(BF16)
    lane_v = lax.broadcasted_iota(jnp.int32, (tm, DIFF_LANES), 1)
    dv_ref[...] = jnp.where(lane_v % LANES == DIFF_DV, 1.0, proj(OFF_DV, DIFF_LANES)).astype(BF16)

    conv = proj(OFF_CB, 3 * CONV_DIM)
    zc = conv[:, CONV_DIM:2 * CONV_DIM] * conv[:, 2 * CONV_DIM:]
    zbuf_ref[F32_SUBLANES:F32_SUBLANES + tm, :] = zc
    cw = cw_ref[...]
    y = (cw[2:3, :] * zc + cw[1:2, :] * zbuf_ref[F32_SUBLANES - 1:F32_SUBLANES - 1 + tm, :]
         + cw[0:1, :] * zbuf_ref[F32_SUBLANES - 2:F32_SUBLANES - 2 + tm, :])
    conv_ref[...] = (conv[:, 0:CONV_DIM] * y).astype(BF16)
    zbuf_ref[0:F32_SUBLANES, :] = zbuf_ref[tm:tm + F32_SUBLANES, :]

    q_scaled = gq * (GLA_DK ** -0.5)
    raw, new_state = _gla_fast(q_scaled, gk, gg, gv, s_ref[...], tri_ref[...])
    gla_ref[...] = _gla_finish(raw, gate, gn_ref[...])
    s_next_ref[...] = new_state
    q_s[...] = q_scaled
    k_s[...] = gk
    g_s[...] = gg
    v_s[...] = gv
    raw_ref[...] = gate
    safe = jnp.min(gg) * CHUNK >= -GLA_SAFE_DECAY

    @pl.when(safe)
    def _():
        s_ref[...] = s_next_ref[...]

    @pl.when(jnp.logical_not(safe))
    def _():
        gate_kept = raw_ref[...]
        _gla_general(q_s, k_s, g_s, v_s, s_ref, raw_ref, cum_ref, amask_ref)
        gla_ref[...] = _gla_finish(raw_ref[...], gate_kept, gn_ref[...])


def _proj(x, g, w, gkup, gkb, cw, gn_tiled, seq):
    t = x.shape[0]
    tm = min(PROJ_ROWS, seq)
    row = lambda width: pl.BlockSpec((tm, width), lambda i: (i, 0))
    cum_np = _gla_cum_matrix()
    cum = jnp.asarray(cum_np, BF16)
    tri = jnp.asarray(cum_np[0:CHUNK], BF16)
    amask = jnp.asarray(_gla_score_mask(), F32)
    out_shapes = (
        jax.ShapeDtypeStruct((t, GLA_WIDTH), BF16),
        jax.ShapeDtypeStruct((t, DIFF_LANES), BF16),
        jax.ShapeDtypeStruct((t, DIFF_LANES), BF16),
        jax.ShapeDtypeStruct((t, DIFF_LANES), BF16),
        jax.ShapeDtypeStruct((t, CONV_DIM), BF16),
    )
    return pl.pallas_call(
        functools.partial(_proj_kernel, tiles_per_seq=seq // tm),
        out_shape=out_shapes,
        grid=(t // tm,),
        in_specs=[
            row(D_MODEL),
            _const_spec((1, D_MODEL)),
            _const_spec((D_MODEL, N_PROJ)),
            _const_spec((2, GLA_QK_LANES, GLA_QK_LANES)),
            _const_spec((1, GLA_QK_LANES)),
            _const_spec((F32_SUBLANES, CONV_DIM)),
            _const_spec(cum.shape),
            _const_spec(tri.shape),
            _const_spec(amask.shape),
            _const_spec((1, GLA_WIDTH)),
        ],
        out_specs=tuple(row(s.shape[1]) for s in out_shapes),
        scratch_shapes=[pltpu.VMEM((tm + F32_SUBLANES, CONV_DIM), F32),
                        pltpu.VMEM((tm, GLA_QK_LANES), F32),
                        pltpu.VMEM((tm, GLA_QK_LANES), F32),
                        pltpu.VMEM((tm, GLA_QK_LANES), F32),
                        pltpu.VMEM((tm, GLA_WIDTH), BF16),
                        pltpu.VMEM((GLA_WIDTH, GLA_QK_LANES), F32),
                        pltpu.VMEM((GLA_WIDTH, GLA_QK_LANES), F32),
                        pltpu.VMEM((tm, GLA_WIDTH), F32)],
        compiler_params=pltpu.CompilerParams(
            dimension_semantics=("arbitrary",), vmem_limit_bytes=VMEM_LIMIT_BYTES),
        name="mixer_proj_gla",
    )(x, g, w, gkup, gkb, cw, cum, tri, amask, gn_tiled)


def _t5_bucket(rel):
    nb = NUM_BUCKETS // 2
    max_exact = nb // 2
    ret = (rel > 0).astype(jnp.int32) * nb
    n = jnp.abs(rel)
    nf = jnp.maximum(n, 1).astype(jnp.float32)
    large = max_exact + (jnp.log(nf / max_exact) / math.log(MAX_DISTANCE / max_exact)
                         * (nb - max_exact)).astype(jnp.int32)
    large = jnp.minimum(large, nb - 1)
    return ret + jnp.where(n < max_exact, n, large)


def _bias_tiles(rel_bias, tile):
    assert tile >= MAX_DISTANCE and tile % CHUNK == 0
    table = rel_bias.astype(F32)
    far = table[NUM_BUCKETS // 2 - 1]
    heads = table.shape[1]
    rel = jnp.concatenate([jnp.arange(0, tile), jnp.arange(-2 * tile, 0)])
    onehot = _t5_bucket(rel)[None, :, None] == jnp.arange(NUM_BUCKETS)[None, None, :]
    per_rel = jnp.sum(jnp.where(onehot, table.T[:, None, :], 0.0), axis=-1)
    per_rel = (per_rel - far[:, None]) * LOG2E
    span = 3 * tile
    flat = jnp.tile(per_rel, (1, 2 * tile))[:, :2 * tile * (span - 1)]
    toep = flat.reshape(heads, 2 * tile, span - 1)[:, :, :tile]
    r = jnp.arange(tile)[:, None]
    c = jnp.arange(tile)[None, :]
    diag = jnp.where(((c // CHUNK) <= (r // CHUNK))[None], toep[:, :tile], MASK_VALUE)
    left = toep[:, tile:]
    return jnp.stack([left, diag], axis=1)


def _attn_kernel(q_ref, k_ref, v_ref, bias_ref, lamv_ref, subln_ref, o_ref,
                 s_buf, p_buf, alpha_buf, m_all, acc_all, *, tile, lam_init):
    seq = q_ref.shape[0]
    nq = seq // tile

    s_buf[...] = jnp.zeros(s_buf.shape, F32)
    p_buf[...] = jnp.zeros(p_buf.shape, BF16)
    alpha_buf[...] = jnp.zeros(alpha_buf.shape, F32)
    m_all[...] = jnp.full(m_all.shape, MASK_VALUE, F32)
    acc_all[...] = jnp.zeros(acc_all.shape, F32)

    lane = lax.broadcasted_iota(jnp.int32, (tile, LANES), 1)

    def stage1(i, j, slot):
        q = q_ref[pl.ds(pl.multiple_of(i * tile, tile), tile), :]
        zero = jnp.zeros_like(q)
        qs = jnp.concatenate([jnp.where(lane < LANES // 2, q, zero),
                              jnp.where(lane >= LANES // 2, q, zero)], axis=0)
        kj = k_ref[pl.ds(pl.multiple_of(j * tile, tile), tile), :]
        s_buf[slot] = _dot_nt(qs, kj)

    def stage2(i, bias, slot):
        for half in range(2):
            rows = slice(half * tile, (half + 1) * tile)
            s = s_buf[slot, rows, :]
            if bias is not None:
                s = s + bias
            m_prev = m_all[i, rows, :]
            m_next = jnp.maximum(m_prev, jnp.max(s, axis=-1, keepdims=True))
            p = jnp.exp2(s - jnp.concatenate([m_next] * (tile // LANES), axis=1))
            p_buf[slot, rows, :] = p.astype(BF16)
            alpha_buf[slot, rows, :] = jnp.exp2(m_prev - m_next)
            m_all[i, rows, :] = m_next

    def stage3(i, j, slot):
        vj = v_ref[pl.ds(pl.multiple_of(j * tile, tile), tile), :]
        acc_all[i] = alpha_buf[slot] * acc_all[i] + _dot(p_buf[slot], vj)

    def run_pipeline(n_pairs, first_pair, advance, with_bias, unroll):
        n_steps = pl.cdiv(n_pairs + 2, unroll) * unroll

        def step(t, pairs, slot_a, slot_b):
            (i0, j0), (i1, j1), (i2, j2) = pairs
            valid1 = jnp.logical_and(t >= 1, t <= n_pairs)
            valid2 = jnp.logical_and(t >= 2, t <= n_pairs + 1)
            stage3(jnp.where(valid2, i2, nq), j2, slot_a)
            stage2(jnp.where(valid1, i1, nq), bias_ref[0, j1 - i1 + 1] if with_bias else None, slot_b)
            stage1(i0, j0, slot_a)
            return (advance(i0, j0), (i0, j0), (i1, j1))

        def body(u, pairs):
            for r in range(unroll):
                pairs = step(unroll * u + r, pairs, r % 2, (r + 1) % 2)
            return pairs

        lax.fori_loop(0, n_steps // unroll, body, (first_pair,) * 3)

    def next_far(i, j):
        wrap = j == i - 2
        done = jnp.logical_and(wrap, i == nq - 1)
        step_i = jnp.logical_and(wrap, jnp.logical_not(done))
        return (jnp.where(step_i, i + 1, i), jnp.where(done, j, jnp.where(wrap, 0, j + 1)))

    def next_near(i, j):
        wrap = j == i
        done = jnp.logical_and(wrap, i == nq - 1)
        step_i = jnp.logical_and(wrap, jnp.logical_not(done))
        return (jnp.where(step_i, i + 1, i), jnp.where(jnp.logical_or(done, wrap), j, j + 1))

    zero = jnp.int32(0)
    if nq > 2:
        run_pipeline((nq - 1) * (nq - 2) // 2, (jnp.int32(2), zero), next_far, False, ATT_UNROLL_FAR)
    run_pipeline(2 * nq - 1, (zero, zero), next_near, True, ATT_UNROLL_NEAR)

    lv = lamv_ref[...]
    lam = (jnp.exp(jnp.sum(lv[0:1] * lv[1:2], axis=-1, keepdims=True))
           - jnp.exp(jnp.sum(lv[2:3] * lv[3:4], axis=-1, keepdims=True)) + lam_init)

    def finish(i):
        acc = acc_all[i]
        a1 = acc[0:tile]
        a2 = acc[tile:2 * tile]
        ones_col = lane == DIFF_DV
        ratio = jnp.sum(jnp.where(ones_col, a1 / jnp.where(ones_col, a2, 1.0), 0.0),
                        axis=-1, keepdims=True)
        u = a1 - lam * (ratio * a2)
        sq = jnp.where(ones_col, a1 * math.sqrt(DIFF_DV * EPS), u)
        ms = jnp.sum(sq * sq, axis=-1, keepdims=True) * (1.0 / DIFF_DV)
        y = u * lax.rsqrt(ms) * subln_ref[...] * (1.0 - lam_init)
        o_ref[pl.ds(pl.multiple_of(i * tile, tile), tile), :] = y.astype(BF16)

    group = math.gcd(nq, ATT_FINISH_GROUP)

    def finish_group(u, carry):
        for r in range(group):
            finish(u * group + r)
        return carry

    lax.fori_loop(0, nq // group, finish_group, 0)


def _attn(dq, dk, dv, bias_tiles, lamv, subln, batch, seq, lam_init):
    tile = min(ATT_TILE, seq)
    nq = seq // tile
    head_block = pl.BlockSpec((seq, LANES), lambda b, h: (b, h))
    return pl.pallas_call(
        functools.partial(_attn_kernel, tile=tile, lam_init=lam_init),
        out_shape=jax.ShapeDtypeStruct((batch * seq, DIFF_LANES), BF16),
        grid=(batch, DIFF_HEADS),
        in_specs=[
            head_block, head_block, head_block,
            pl.BlockSpec((1, 2, tile, tile), lambda b, h: (h, 0, 0, 0)),
            pl.BlockSpec((4, LANES), lambda b, h: (0, 0)),
            pl.BlockSpec((1, LANES), lambda b, h: (0, 0)),
        ],
        out_specs=head_block,
        scratch_shapes=[pltpu.VMEM((2, 2 * tile, tile), F32),
                        pltpu.VMEM((2, 2 * tile, tile), BF16),
                        pltpu.VMEM((2, 2 * tile, LANES), F32),
                        pltpu.VMEM((nq + 1, 2 * tile, LANES), F32),
                        pltpu.VMEM((nq + 1, 2 * tile, LANES), F32)],
        compiler_params=pltpu.CompilerParams(
            dimension_semantics=("parallel", "parallel"), vmem_limit_bytes=VMEM_LIMIT_BYTES),
        name="diff_attn",
    )(dq, dk, dv, bias_tiles, lamv, subln)


def _pad_cols(w, width):
    return jnp.pad(w, ((0, 0), (0, width - w.shape[1])))


def _slot_cols(w, heads, used, slot):
    lead = w.shape[0]
    w = w.reshape(lead, heads, used)
    return jnp.pad(w, ((0, 0), (0, 0), (0, slot - used))).reshape(lead, heads * slot)


def _layout_w_in(w_in):
    offs = np.cumsum([0, GLA_HEADS * GLA_DK, GLA_HEADS * GLA_DK, GLA_WIDTH, GLA_RANK, GLA_WIDTH,
                      DIFF_HEADS * 2 * DIFF_DQK, DIFF_HEADS * 2 * DIFF_DQK, DIFF_WIDTH,
                      CONV_DIM, CONV_DIM, CONV_DIM])
    gq, gk, gv, glr, gate, dq, dk, dv, cb, cc, ch = [w_in[:, offs[n]:offs[n + 1]] for n in range(11)]
    half = LANES // 2
    gq_slots = _slot_cols(gq, GLA_HEADS, GLA_DK, HEAD_SLOT)
    gq_slots = gq_slots.at[:, GLR_LANE:GLR_LANE + GLA_RANK].set(glr)
    cols = [
        gq_slots,
        _slot_cols(gk, GLA_HEADS, GLA_DK, HEAD_SLOT),
        gv,
        gate,
        _slot_cols(dq, 2 * DIFF_HEADS, DIFF_DQK, half),
        _slot_cols(dk, 2 * DIFF_HEADS, DIFF_DQK, half),
        _slot_cols(dv, DIFF_HEADS, DIFF_DV, LANES),
        cb, cc, ch,
    ]
    w = jnp.concatenate(cols, axis=1)
    assert w.shape[1] == N_PROJ
    return w.astype(BF16)


def kernel(x, ffn1_norm, ffn1_gate, ffn1_up, ffn1_down, mix_norm, w_in, gla_gk_up, gla_gk_bias,
           gla_norm, diff_lambda_q1, diff_lambda_k1, diff_lambda_q2, diff_lambda_k2, diff_subln,
           rel_bias, conv_w, w_out, ffn2_norm, ffn2_gate, ffn2_up, ffn2_down, final_norm):
    batch, seq, _ = x.shape
    depth = w_in.shape[0]
    t = batch * seq
    xf = x.reshape(t, D_MODEL)
    bias_tiles = _bias_tiles(rel_bias, min(ATT_TILE, seq))
    final_g = final_norm.reshape(1, D_MODEL)
    ffn1_w = (ffn1_gate[0].astype(BF16), ffn1_up[0].astype(BF16), ffn1_down[0].astype(BF16))

    for l in range(depth):
        lam_init = 0.8 - 0.6 * math.exp(-0.3 * l)
        xf, ffn2_w = _ffn(xf, ffn1_norm[l].reshape(1, D_MODEL), *ffn1_w, final_g, False,
                          cast=((ffn2_gate, ffn2_up, ffn2_down), l))

        w_pad = _layout_w_in(w_in[l])
        up = _slot_cols(gla_gk_up[l], GLA_HEADS, GLA_DK, HEAD_SLOT)
        up = jnp.pad(up, ((GLR_LANE, GLA_QK_LANES - GLR_LANE - GLA_RANK), (0, 0)))
        up_hi = up.astype(BF16)
        up_lo = (up - up_hi.astype(F32)).astype(BF16)
        gkup = jnp.stack([up_hi, up_lo], axis=0)
        gkb = _slot_cols(gla_gk_bias[l].reshape(1, -1), GLA_HEADS, GLA_DK, HEAD_SLOT)
        cw = jnp.pad(conv_w[l], ((0, F32_SUBLANES - CONV_WIDTH), (0, 0)))
        gn_tiled = jnp.tile(gla_norm[l], GLA_HEADS).reshape(1, GLA_WIDTH)
        gla_o, dq, dk, dv, conv_o = _proj(
            xf, mix_norm[l].reshape(1, D_MODEL), w_pad, gkup, gkb, cw, gn_tiled, seq)

        lamv = jnp.stack([diff_lambda_q1[l], diff_lambda_k1[l], diff_lambda_q2[l], diff_lambda_k2[l]])
        lamv = _pad_cols(lamv.astype(F32), LANES)
        subln = _pad_cols(diff_subln[l].reshape(1, DIFF_DV), LANES)
        diff_o = _attn(dq, dk, dv, bias_tiles, lamv, subln, batch, seq, lam_init)

        wo = w_out[l]
        w1 = wo[0:GLA_WIDTH].astype(BF16)
        w2 = wo[GLA_WIDTH:GLA_WIDTH + DIFF_WIDTH].reshape(DIFF_HEADS, DIFF_DV, D_MODEL)
        w2 = jnp.pad(w2, ((0, 0), (0, LANES - DIFF_DV), (0, 0))).reshape(DIFF_LANES, D_MODEL).astype(BF16)
        w3 = wo[GLA_WIDTH + DIFF_WIDTH:].astype(BF16)
        last = l == depth - 1
        xf, ffn1_w = _ffn(xf, ffn2_norm[l].reshape(1, D_MODEL), *ffn2_w, final_g, last,
                          mix=(gla_o, diff_o, conv_o, w1, w2, w3),
                          cast=None if last else ((ffn1_gate, ffn1_up, ffn1_down), l + 1))

    return xf.reshape(batch, seq, D_MODEL)
```

```python
import functools
import math

import numpy as np
import jax
import jax.numpy as jnp
from jax import lax
from jax.experimental import pallas as pl
from jax.experimental.pallas import tpu as pltpu

F32 = jnp.float32
BF16 = jnp.bfloat16

D_MODEL = 1024
D_FF = 2816
EPS = 1e-6
CHUNK = 64
GLA_HEADS = 4
GLA_DK = 48
GLA_DV = 96
GLA_RANK = 16
GLA_GATE_NORM = 16.0
DIFF_HEADS = 4
DIFF_DQK = 48
DIFF_DV = 96
CONV_DIM = 256
CONV_WIDTH = 3
NUM_BUCKETS = 32
MAX_DISTANCE = 128
GLA_WIDTH = GLA_HEADS * GLA_DV
DIFF_WIDTH = DIFF_HEADS * DIFF_DV

LANES = 128
F32_SUBLANES = 8
BF16_SUBLANES = 16
MXU_DEPTH = 256
VMEM_LIMIT_BYTES = 62 * 1024 * 1024

HEAD_SLOT = 64
GLA_QK_LANES = GLA_HEADS * HEAD_SLOT
SUB = 16
N_SUB = CHUNK // SUB
KST_ROWS = SUB * (N_SUB * (N_SUB - 1) // 2)
DIFF_LANES = DIFF_HEADS * LANES
MASK_VALUE = -1e30
GLA_SAFE_DECAY = 60.0
LOG2E = math.log2(math.e)

OFF_GQ = 0
OFF_GK = OFF_GQ + GLA_QK_LANES
OFF_GV = OFF_GK + GLA_QK_LANES
OFF_GATE = OFF_GV + GLA_WIDTH
OFF_DQ = OFF_GATE + GLA_WIDTH
OFF_DK = OFF_DQ + DIFF_LANES
OFF_DV = OFF_DK + DIFF_LANES
OFF_CB = OFF_DV + DIFF_LANES
OFF_CC = OFF_CB + CONV_DIM
OFF_CH = OFF_CC + CONV_DIM
N_PROJ = OFF_CH + CONV_DIM
GLR_LANE = GLA_DK
assert GLR_LANE + GLA_RANK <= HEAD_SLOT

FFN_ROWS = 1024
CAST_STEPS = 16
FFN_F_SPLITS = (0, 6 * MXU_DEPTH, D_FF)
assert all((hi - lo) % MXU_DEPTH == 0 for lo, hi in zip(FFN_F_SPLITS[:-1], FFN_F_SPLITS[1:]))
PROJ_ROWS = 1024
ATT_TILE = 256
ATT_UNROLL_FAR = 36
ATT_UNROLL_NEAR = 22
ATT_FINISH_GROUP = 4


def _dot(a, b):
    return jnp.dot(a, b, preferred_element_type=F32)


def _dot_nt(a, b):
    return lax.dot_general(a, b, (((1,), (1,)), ((), ())), preferred_element_type=F32)


def _dot_tn(a, b):
    return lax.dot_general(a, b, (((0,), (0,)), ((), ())), preferred_element_type=F32)


def _split_bf16(x):
    hi = x.astype(BF16)
    lo = (x - hi.astype(F32)).astype(BF16)
    return hi, lo


def _rms(x, g):
    return x * lax.rsqrt(jnp.mean(x * x, axis=-1, keepdims=True) + EPS) * g


def _sigmoid(x):
    return 1.0 / (1.0 + jnp.exp(-x))


def _ffn_kernel(*refs, final, mix, cast):
    refs = list(refs)
    x_ref, g_ref, wg_ref, wu_ref, wd_ref, fg_ref = refs[:6]
    mix_refs = refs[6:12] if mix else ()
    n_in = 6 + len(mix_refs)
    src_refs = refs[n_in:n_in + 3] if cast else ()
    o_ref = refs[n_in + len(src_refs)]
    dst_refs = refs[n_in + len(src_refs) + 1:]
    for src, dst in zip(src_refs, dst_refs):
        dst[...] = src[...].astype(BF16)

    x = x_ref[...]
    if mix:
        gla_ref, diff_ref, conv_ref, w1_ref, w2_ref, w3_ref = mix_refs
        x = (x + _dot(gla_ref[...], w1_ref[...]) + _dot(diff_ref[...], w2_ref[...])
             + _dot(conv_ref[...], w3_ref[...]))
    h = _rms(x, g_ref[...]).astype(BF16)
    acc = None
    for lo, hi in zip(FFN_F_SPLITS[:-1], FFN_F_SPLITS[1:]):
        sl = slice(lo, hi)
        gate = _dot(h, wg_ref[:, sl])
        up = _dot(h, wu_ref[:, sl])
        a = (gate * _sigmoid(gate) * up).astype(BF16)
        part = _dot(a, wd_ref[sl, :])
        acc = part if acc is None else acc + part
    y = x + 0.5 * acc
    if final:
        y = _rms(y, fg_ref[...])
    o_ref[...] = y


def _const_spec(shape):
    nd = len(shape)
    return pl.BlockSpec(shape, lambda *_: (0,) * nd, pipeline_mode=pl.Buffered(1))


def _cast_block(n_rows, steps):
    share = 1
    while (n_rows * share) % (steps * BF16_SUBLANES) or steps % share:
        share *= 2
        assert share <= steps
    return n_rows * share // steps, share


def _cast_kernel(*refs):
    n = len(refs) // 2
    for src, dst in zip(refs[:n], refs[n:]):
        dst[...] = src[...].astype(BF16)


def _cast_weights(stacked, layer):
    in_specs, out_specs, out_shapes = [], [], []
    for w in stacked:
        rows, share = _cast_block(w.shape[1], CAST_STEPS)
        in_specs.append(pl.BlockSpec((None, rows, w.shape[2]),
                                     lambda i, share=share: (layer, i // share, 0)))
        out_specs.append(pl.BlockSpec((rows, w.shape[2]), lambda i, share=share: (i // share, 0)))
        out_shapes.append(jax.ShapeDtypeStruct(w.shape[1:], BF16))
    return pl.pallas_call(
        _cast_kernel,
        out_shape=tuple(out_shapes),
        grid=(CAST_STEPS,),
        in_specs=in_specs,
        out_specs=tuple(out_specs),
        compiler_params=pltpu.CompilerParams(
            dimension_semantics=("arbitrary",), vmem_limit_bytes=VMEM_LIMIT_BYTES),
        name="cast_weights",
    )(*stacked)


def _ffn(x, g, wg, wu, wd, final_g, final, mix=None, cast=None):
    t = x.shape[0]
    tm = min(FFN_ROWS, t)
    steps = t // tm
    row = lambda width: pl.BlockSpec((tm, width), lambda i: (i, 0))
    operands = [x, g, wg, wu, wd, final_g]
    in_specs = [row(D_MODEL), _const_spec((1, D_MODEL)), _const_spec((D_MODEL, D_FF)),
                _const_spec((D_MODEL, D_FF)), _const_spec((D_FF, D_MODEL)), _const_spec((1, D_MODEL))]
    out_shapes = [jax.ShapeDtypeStruct((t, D_MODEL), F32)]
    out_specs = [row(D_MODEL)]
    if mix is not None:
        operands += list(mix)
        in_specs += [row(a.shape[1]) for a in mix[:3]] + [_const_spec(w.shape) for w in mix[3:]]
    if cast is not None:
        stacked, layer = cast
        for w in stacked:
            rows, share = _cast_block(w.shape[1], steps)
            operands.append(w)
            in_specs.append(pl.BlockSpec((None, rows, w.shape[2]),
                                         lambda i, share=share: (layer, i // share, 0)))
            out_shapes.append(jax.ShapeDtypeStruct(w.shape[1:], BF16))
            out_specs.append(pl.BlockSpec((rows, w.shape[2]), lambda i, share=share: (i // share, 0)))
    outs = pl.pallas_call(
        functools.partial(_ffn_kernel, final=final, mix=mix is not None, cast=cast is not None),
        out_shape=tuple(out_shapes),
        grid=(steps,),
        in_specs=in_specs,
        out_specs=tuple(out_specs),
        compiler_params=pltpu.CompilerParams(
            dimension_semantics=("arbitrary",), vmem_limit_bytes=VMEM_LIMIT_BYTES),
        name="ffn",
    )(*operands)
    return outs[0], tuple(outs[1:])


def _gla_cum_matrix():
    t = np.arange(CHUNK)[:, None]
    u = np.arange(CHUNK)[None, :]
    tri = (u <= t)
    ref = (u <= (t // SUB) * SUB - 1)
    ones = np.ones((CHUNK, CHUNK), bool)
    parts = [tri, ref, ones]
    for j in range(1, N_SUB):
        s = np.arange(SUB * j)[:, None]
        parts.append((u > s) & (u <= SUB * j - 1))
    return np.concatenate(parts, axis=0).astype(np.float32)


def _gla_score_mask():
    t = np.arange(CHUNK)[:, None]
    group = np.concatenate([np.full(SUB * j, j) for j in range(1, N_SUB)])
    keep = (group[None, :] == (t // SUB))
    return np.tile(keep, (1, GLA_HEADS)).astype(np.float32)


def _gla_head_ids():
    qk_head = lax.broadcasted_iota(jnp.int32, (1, GLA_QK_LANES), 1) // HEAD_SLOT
    v_head = lax.broadcasted_iota(jnp.int32, (1, GLA_WIDTH), 1) // GLA_DV
    row_head_v = lax.broadcasted_iota(jnp.int32, (GLA_WIDTH, 1), 0) // GLA_DV
    return qk_head, v_head, row_head_v == qk_head


def _gla_fast(q, k, g, v, state, tri):
    rows = q.shape[0]
    n_chunks = rows // CHUNK
    qk_head, v_head, state_mask = _gla_head_ids()
    chunk = lambda x, c: x[c * CHUNK:(c + 1) * CHUNK]
    sel_rows = pl.cdiv(n_chunks, BF16_SUBLANES) * BF16_SUBLANES
    sel_r = lax.broadcasted_iota(jnp.int32, (sel_rows, rows), 0)
    sel_c = lax.broadcasted_iota(jnp.int32, (sel_rows, rows), 1) // CHUNK
    chunk_sel = jnp.where(sel_r == sel_c, 1.0, 0.0).astype(BF16)
    g_hi, g_lo = _split_bf16(g)
    totals = _dot(chunk_sel, g_hi) + _dot(chunk_sel, g_lo)
    b = jnp.concatenate([_dot(tri, chunk(g_hi, c)) + _dot(tri, chunk(g_lo, c))
                         for c in range(n_chunks)], axis=0)
    b_last = jnp.concatenate([jnp.broadcast_to(totals[c:c + 1], (CHUNK, GLA_QK_LANES))
                              for c in range(n_chunks)], axis=0)
    qd = (q * jnp.exp(b)).astype(BF16)
    kd = (k * jnp.exp(-b)).astype(BF16)
    kw = (k * jnp.exp(b_last - b)).astype(BF16)
    t_idx = lax.broadcasted_iota(jnp.int32, (CHUNK, GLA_QK_LANES), 0)
    s_idx = lax.broadcasted_iota(jnp.int32, (CHUNK, GLA_QK_LANES), 1) % HEAD_SLOT
    causal = s_idx <= t_idx
    intra, update = [], []
    for c in range(n_chunks):
        kd_c, v_c = chunk(kd, c), chunk(v, c)
        kd_bd = jnp.concatenate(
            [jnp.where(qk_head == hd, kd_c, jnp.zeros_like(kd_c)) for hd in range(GLA_HEADS)],
            axis=0)
        a = _dot_nt(chunk(qd, c), kd_bd)
        a = jnp.where(causal, a, 0.0).astype(BF16)
        v_bd = jnp.concatenate(
            [jnp.where(v_head == hd, v_c, jnp.zeros_like(v_c)) for hd in range(GLA_HEADS)],
            axis=0)
        intra.append(_dot(a, v_bd))
        update.append(_dot_tn(v_c, chunk(kw, c)))
    raw = []
    for c in range(n_chunks):
        raw.append(intra[c] + _dot_nt(chunk(qd, c), state.astype(BF16)))
        state = state * jnp.exp(totals[c:c + 1]) + jnp.where(state_mask, update[c], 0.0)
    return jnp.concatenate(raw, axis=0), state


def _gla_general(q_ref, k_ref, g_ref, v_ref, s_ref, raw_ref, cum_ref, amask_ref):
    n_chunks = q_ref.shape[0] // CHUNK
    qk_head, v_head, state_mask = _gla_head_ids()
    cum = cum_ref[...]
    amask = amask_ref[...] > 0.5
    ind_r = lax.broadcasted_iota(jnp.int32, (GLA_QK_LANES, LANES), 0) // HEAD_SLOT
    ind_c = lax.broadcasted_iota(jnp.int32, (GLA_QK_LANES, LANES), 1)
    ind_sum = jnp.where(ind_r == ind_c, 1.0, 0.0).astype(BF16)
    bc_r = lax.broadcasted_iota(jnp.int32, (LANES, GLA_WIDTH), 0)
    bc_c = lax.broadcasted_iota(jnp.int32, (LANES, GLA_WIDTH), 1) // GLA_DV
    ind_bcast = jnp.where(bc_r == bc_c, 1.0, 0.0).astype(BF16)
    row_in_sub = lax.broadcasted_iota(jnp.int32, (CHUNK, 1), 0) % SUB

    def chunk_body(c, carry):
        r0 = pl.multiple_of(c * CHUNK, CHUNK)
        q = q_ref[pl.ds(r0, CHUNK), :]
        k = k_ref[pl.ds(r0, CHUNK), :]
        g = g_ref[pl.ds(r0, CHUNK), :]
        v = v_ref[pl.ds(r0, CHUNK), :]
        g_hi, g_lo = _split_bf16(g)
        cums = _dot(cum, g_hi) + _dot(cum, g_lo)
        b = cums[0:CHUNK]
        b_ref = cums[CHUNK:2 * CHUNK]
        b_last = cums[2 * CHUNK:3 * CHUNK]
        e_kst = cums[3 * CHUNK:3 * CHUNK + KST_ROWS]

        state = s_ref[...]
        q_in = (q * jnp.exp(b)).astype(BF16)
        o = _dot_nt(q_in, state.astype(BF16))

        kw = (k * jnp.exp(b_last - b)).astype(BF16)
        decay = jnp.exp(jnp.concatenate([b_last] * (GLA_WIDTH // CHUNK), axis=0))
        s_ref[...] = state * decay + jnp.where(state_mask, _dot_tn(v, kw), 0.0)

        q_sub = (q * jnp.exp(jnp.minimum(b - b_ref, 0.0))).astype(BF16)
        k_st = jnp.concatenate([k[0:SUB * j] for j in range(1, N_SUB)], axis=0) * jnp.exp(e_kst)
        k_bd = jnp.concatenate(
            [jnp.where(qk_head == hd, k_st, 0.0) for hd in range(GLA_HEADS)], axis=0).astype(BF16)
        a = _dot_nt(q_sub, k_bd)
        a = jnp.where(amask, a, 0.0).astype(BF16)
        v_st = jnp.concatenate([v[0:SUB * j] for j in range(1, N_SUB)], axis=0)
        v_bd = jnp.concatenate(
            [jnp.where(v_head == hd, v_st, jnp.zeros_like(v_st)) for hd in range(GLA_HEADS)], axis=0)
        o = o + _dot(a, v_bd)

        vf = v.astype(F32)
        prods = []
        for d in range(SUB):
            ks = k if d == 0 else pltpu.roll(k, d, 0)
            bs = b if d == 0 else pltpu.roll(b, d, 0)
            prods.append((q * ks * jnp.exp(jnp.minimum(b - bs, 0.0))).astype(BF16))
        dsum = _dot(jnp.concatenate(prods, axis=0), ind_sum)
        valid = jnp.concatenate([row_in_sub >= d for d in range(SUB)], axis=0)
        dsum = jnp.where(valid, dsum, 0.0).astype(BF16)
        dbc = _dot(dsum, ind_bcast)
        for d in range(SUB):
            vs = vf if d == 0 else pltpu.roll(vf, d, 0)
            o = o + dbc[d * CHUNK:(d + 1) * CHUNK] * vs
        raw_ref[pl.ds(r0, CHUNK), :] = o
        return carry

    lax.fori_loop(0, n_chunks, chunk_body, 0)


def _gla_finish(o, gate, gain):
    hr = lax.broadcasted_iota(jnp.int32, (GLA_WIDTH, GLA_WIDTH), 0) // GLA_DV
    hc = lax.broadcasted_iota(jnp.int32, (GLA_WIDTH, GLA_WIDTH), 1) // GLA_DV
    head_ones = jnp.where(hr == hc, 1.0, 0.0).astype(BF16)
    ms = _dot((o * o).astype(BF16), head_ones) * (1.0 / GLA_DV)
    return (o * lax.rsqrt(ms + EPS) * gain * (gate * _sigmoid(gate))).astype(BF16)


def _proj_kernel(x_ref, g_ref, w_ref, gkup_ref, gkb_ref, cw_ref, cum_ref, tri_ref, amask_ref, gn_ref,
                 gla_ref, dq_ref, dk_ref, dv_ref, conv_ref,
                 zbuf_ref, q_s, k_s, g_s, v_s, s_ref, s_next_ref, raw_ref, *, tiles_per_seq):
    tm = x_ref.shape[0]
    h = _rms(x_ref[...], g_ref[...]).astype(BF16)

    @pl.when(pl.program_id(0) % tiles_per_seq == 0)
    def _():
        zbuf_ref[0:F32_SUBLANES, :] = jnp.zeros((F32_SUBLANES, CONV_DIM), F32)
        s_ref[...] = jnp.zeros_like(s_ref)

    def proj(off, width):
        return _dot(h, w_ref[:, off:off + width])

    qk = proj(OFF_GQ, 2 * GLA_QK_LANES)
    gq = qk[:, 0:GLA_QK_LANES]
    gk = qk[:, GLA_QK_LANES:]
    vg = proj(OFF_GV, 2 * GLA_WIDTH)
    gv = vg[:, 0:GLA_WIDTH].astype(BF16)
    gate = vg[:, GLA_WIDTH:]

    glr_hi, glr_lo = _split_bf16(gq)
    up_hi = gkup_ref[0]
    up_lo = gkup_ref[1]
    z = _dot(glr_hi, up_hi) + _dot(glr_lo, up_hi) + _dot(glr_hi, up_lo) + gkb_ref[...]
    logsig = jnp.minimum(z, 0.0) - jnp.log1p(jnp.exp(-jnp.abs(z)))
    lane = lax.broadcasted_iota(jnp.int32, (tm, GLA_QK_LANES), 1)
    gg = jnp.where(lane % HEAD_SLOT < GLA_DK, logsig * (1.0 / GLA_GATE_NORM), 0.0)

    dqk = proj(OFF_DQ, 2 * DIFF_LANES)
    dq_ref[...] = (dqk[:, 0:DIFF_LANES] * (DIFF_DQK ** -0.5 * LOG2E)).astype(BF16)
    dk_ref[...] = dqk[:, DIFF_LANES:].astype(BF16)
    lane_v = lax.broadcasted_iota(jnp.int32, (tm, DIFF_LANES), 1)
    dv_ref[...] = jnp.where(lane_v % LANES == DIFF_DV, 1.0, proj(OFF_DV, DIFF_LANES)).astype(BF16)

    conv = proj(OFF_CB, 3 * CONV_DIM)
    zc = conv[:, CONV_DIM:2 * CONV_DIM] * conv[:, 2 * CONV_DIM:]
    zbuf_ref[F32_SUBLANES:F32_SUBLANES + tm, :] = zc
    cw = cw_ref[...]
    y = (cw[2:3, :] * zc + cw[1:2, :] * zbuf_ref[F32_SUBLANES - 1:F32_SUBLANES - 1 + tm, :]
         + cw[0:1, :] * zbuf_ref[F32_SUBLANES - 2:F32_SUBLANES - 2 + tm, :])
    conv_ref[...] = (conv[:, 0:CONV_DIM] * y).astype(BF16)
    zbuf_ref[0:F32_SUBLANES, :] = zbuf_ref[tm:tm + F32_SUBLANES, :]

    q_scaled = gq * (GLA_DK ** -0.5)
    raw, new_state = _gla_fast(q_scaled, gk, gg, gv, s_ref[...], tri_ref[...])
    gla_ref[...] = _gla_finish(raw, gate, gn_ref[...])
    s_next_ref[...] = new_state
    q_s[...] = q_scaled
    k_s[...] = gk
    g_s[...] = gg
    v_s[...] = gv
    raw_ref[...] = gate
    safe = jnp.min(gg) * CHUNK >= -GLA_SAFE_DECAY

    @pl.when(safe)
    def _():
        s_ref[...] = s_next_ref[...]

    @pl.when(jnp.logical_not(safe))
    def _():
        gate_kept = raw_ref[...]
        _gla_general(q_s, k_s, g_s, v_s, s_ref, raw_ref, cum_ref, amask_ref)
        gla_ref[...] = _gla_finish(raw_ref[...], gate_kept, gn_ref[...])


def _proj(x, g, w, gkup, gkb, cw, gn_tiled, seq):
    t = x.shape[0]
    tm = min(PROJ_ROWS, seq)
    row = lambda width: pl.BlockSpec((tm, width), lambda i: (i, 0))
    cum_np = _gla_cum_matrix()
    cum = jnp.asarray(cum_np, BF16)
    tri = jnp.asarray(cum_np[0:CHUNK], BF16)
    amask = jnp.asarray(_gla_score_mask(), F32)
    out_shapes = (
        jax.ShapeDtypeStruct((t, GLA_WIDTH), BF16),
        jax.ShapeDtypeStruct((t, DIFF_LANES), BF16),
        jax.ShapeDtypeStruct((t, DIFF_LANES), BF16),
        jax.ShapeDtypeStruct((t, DIFF_LANES), BF16),
        jax.ShapeDtypeStruct((t, CONV_DIM), BF16),
    )
    return pl.pallas_call(
        functools.partial(_proj_kernel, tiles_per_seq=seq // tm),
        out_shape=out_shapes,
        grid=(t // tm,),
        in_specs=[
            row(D_MODEL),
            _const_spec((1, D_MODEL)),
            _const_spec((D_MODEL, N_PROJ)),
            _const_spec((2, GLA_QK_LANES, GLA_QK_LANES)),
            _const_spec((1, GLA_QK_LANES)),
            _const_spec((F32_SUBLANES, CONV_DIM)),
            _const_spec(cum.shape),
            _const_spec(tri.shape),
            _const_spec(amask.shape),
            _const_spec((1, GLA_WIDTH)),
        ],
        out_specs=tuple(row(s.shape[1]) for s in out_shapes),
        scratch_shapes=[pltpu.VMEM((tm + F32_SUBLANES, CONV_DIM), F32),
                        pltpu.VMEM((tm, GLA_QK_LANES), F32),
                        pltpu.VMEM((tm, GLA_QK_LANES), F32),
                        pltpu.VMEM((tm, GLA_QK_LANES), F32),
                        pltpu.VMEM((tm, GLA_WIDTH), BF16),
                        pltpu.VMEM((GLA_WIDTH, GLA_QK_LANES), F32),
                        pltpu.VMEM((GLA_WIDTH, GLA_QK_LANES), F32),
                        pltpu.VMEM((tm, GLA_WIDTH), F32)],
        compiler_params=pltpu.CompilerParams(
            dimension_semantics=("arbitrary",), vmem_limit_bytes=VMEM_LIMIT_BYTES),
        name="mixer_proj_gla",
    )(x, g, w, gkup, gkb, cw, cum, tri, amask, gn_tiled)


def _t5_bucket(rel):
    nb = NUM_BUCKETS // 2
    max_exact = nb // 2
    ret = (rel > 0).astype(jnp.int32) * nb
    n = jnp.abs(rel)
    nf = jnp.maximum(n, 1).astype(jnp.float32)
    large = max_exact + (jnp.log(nf / max_exact) / math.log(MAX_DISTANCE / max_exact)
                         * (nb - max_exact)).astype(jnp.int32)
    large = jnp.minimum(large, nb - 1)
    return ret + jnp.where(n < max_exact, n, large)


def _bias_tiles(rel_bias, tile):
    assert tile >= MAX_DISTANCE and tile % CHUNK == 0
    table = rel_bias.astype(F32)
    far = table[NUM_BUCKETS // 2 - 1]
    heads = table.shape[1]
    rel = jnp.concatenate([jnp.arange(0, tile), jnp.arange(-2 * tile, 0)])
    onehot = _t5_bucket(rel)[None, :, None] == jnp.arange(NUM_BUCKETS)[None, None, :]
    per_rel = jnp.sum(jnp.where(onehot, table.T[:, None, :], 0.0), axis=-1)
    per_rel = (per_rel - far[:, None]) * LOG2E
    span = 3 * tile
    flat = jnp.tile(per_rel, (1, 2 * tile))[:, :2 * tile * (span - 1)]
    toep = flat.reshape(heads, 2 * tile, span - 1)[:, :, :tile]
    r = jnp.arange(tile)[:, None]
    c = jnp.arange(tile)[None, :]
    diag = jnp.where(((c // CHUNK) <= (r // CHUNK))[None], toep[:, :tile], MASK_VALUE)
    left = toep[:, tile:]
    return jnp.stack([left, diag], axis=1)


def _attn_kernel(q_ref, k_ref, v_ref, bias_ref, lamv_ref, subln_ref, o_ref,
                 s_buf, p_buf, alpha_buf, m_all, acc_all, *, tile, lam_init):
    seq = q_ref.shape[0]
    nq = seq // tile

    s_buf[...] = jnp.zeros(s_buf.shape, F32)
    p_buf[...] = jnp.zeros(p_buf.shape, BF16)
    alpha_buf[...] = jnp.zeros(alpha_buf.shape, F32)
    m_all[...] = jnp.full(m_all.shape, MASK_VALUE, F32)
    acc_all[...] = jnp.zeros(acc_all.shape, F32)

    lane = lax.broadcasted_iota(jnp.int32, (tile, LANES), 1)

    def stage1(i, j, slot):
        q = q_ref[pl.ds(pl.multiple_of(i * tile, tile), tile), :]
        zero = jnp.zeros_like(q)
        qs = jnp.concatenate([jnp.where(lane < LANES // 2, q, zero),
                              jnp.where(lane >= LANES // 2, q, zero)], axis=0)
        kj = k_ref[pl.ds(pl.multiple_of(j * tile, tile), tile), :]
        s_buf[slot] = _dot_nt(qs, kj)

    def stage2(i, bias, slot):
        for half in range(2):
            rows = slice(half * tile, (half + 1) * tile)
            s = s_buf[slot, rows, :]
            if bias is not None:
                s = s + bias
            m_prev = m_all[i, rows, :]
            m_next = jnp.maximum(m_prev, jnp.max(s, axis=-1, keepdims=True))
            p = jnp.exp2(s - jnp.concatenate([m_next] * (tile // LANES), axis=1))
            p_buf[slot, rows, :] = p.astype(BF16)
            alpha_buf[slot, rows, :] = jnp.exp2(m_prev - m_next)
            m_all[i, rows, :] = m_next

    def stage3(i, j, slot):
        vj = v_ref[pl.ds(pl.multiple_of(j * tile, tile), tile), :]
        acc_all[i] = alpha_buf[slot] * acc_all[i] + _dot(p_buf[slot], vj)

    def run_pipeline(n_pairs, first_pair, advance, with_bias, unroll):
        n_steps = pl.cdiv(n_pairs + 2, unroll) * unroll

        def step(t, pairs, slot_a, slot_b):
            (i0, j0), (i1, j1), (i2, j2) = pairs
            valid1 = jnp.logical_and(t >= 1, t <= n_pairs)
            valid2 = jnp.logical_and(t >= 2, t <= n_pairs + 1)
            stage3(jnp.where(valid2, i2, nq), j2, slot_a)
            stage2(jnp.where(valid1, i1, nq), bias_ref[0, j1 - i1 + 1] if with_bias else None, slot_b)
            stage1(i0, j0, slot_a)
            return (advance(i0, j0), (i0, j0), (i1, j1))

        def body(u, pairs):
            for r in range(unroll):
                pairs = step(unroll * u + r, pairs, r % 2, (r + 1) % 2)
            return pairs

        lax.fori_loop(0, n_steps // unroll, body, (first_pair,) * 3)

    def next_far(i, j):
        wrap = j == i - 2
        done = jnp.logical_and(wrap, i == nq - 1)
        step_i = jnp.logical_and(wrap, jnp.logical_not(done))
        return (jnp.where(step_i, i + 1, i), jnp.where(done, j, jnp.where(wrap, 0, j + 1)))

    def next_near(i, j):
        wrap = j == i
        done = jnp.logical_and(wrap, i == nq - 1)
        step_i = jnp.logical_and(wrap, jnp.logical_not(done))
        return (jnp.where(step_i, i + 1, i), jnp.where(jnp.logical_or(done, wrap), j, j + 1))

    zero = jnp.int32(0)
    if nq > 2:
        run_pipeline((nq - 1) * (nq - 2) // 2, (jnp.int32(2), zero), next_far, False, ATT_UNROLL_FAR)
    run_pipeline(2 * nq - 1, (zero, zero), next_near, True, ATT_UNROLL_NEAR)

    lv = lamv_ref[...]
    lam = (jnp.exp(jnp.sum(lv[0:1] * lv[1:2], axis=-1, keepdims=True))
           - jnp.exp(jnp.sum(lv[2:3] * lv[3:4], axis=-1, keepdims=True)) + lam_init)

    def finish(i):
        acc = acc_all[i]
        a1 = acc[0:tile]
        a2 = acc[tile:2 * tile]
        ones_col = lane == DIFF_DV
        ratio = jnp.sum(jnp.where(ones_col, a1 / jnp.where(ones_col, a2, 1.0), 0.0),
                        axis=-1, keepdims=True)
        u = a1 - lam * (ratio * a2)
        sq = jnp.where(ones_col, a1 * math.sqrt(DIFF_DV * EPS), u)
        ms = jnp.sum(sq * sq, axis=-1, keepdims=True) * (1.0 / DIFF_DV)
        y = u * lax.rsqrt(ms) * subln_ref[...] * (1.0 - lam_init)
        o_ref[pl.ds(pl.multiple_of(i * tile, tile), tile), :] = y.astype(BF16)

    group = math.gcd(nq, ATT_FINISH_GROUP)

    def finish_group(u, carry):
        for r in range(group):
            finish(u * group + r)
        return carry

    lax.fori_loop(0, nq // group, finish_group, 0)


def _attn(dq, dk, dv, bias_tiles, lamv, subln, batch, seq, lam_init):
    tile = min(ATT_TILE, seq)
    nq = seq // tile
    head_block = pl.BlockSpec((seq, LANES), lambda b, h: (b, h))
    return pl.pallas_call(
        functools.partial(_attn_kernel, tile=tile, lam_init=lam_init),
        out_shape=jax.ShapeDtypeStruct((batch * seq, DIFF_LANES), BF16),
        grid=(batch, DIFF_HEADS),
        in_specs=[
            head_block, head_block, head_block,
            pl.BlockSpec((1, 2, tile, tile), lambda b, h: (h, 0, 0, 0)),
            pl.BlockSpec((4, LANES), lambda b, h: (0, 0)),
            pl.BlockSpec((1, LANES), lambda b, h: (0, 0)),
        ],
        out_specs=head_block,
        scratch_shapes=[pltpu.VMEM((2, 2 * tile, tile), F32),
                        pltpu.VMEM((2, 2 * tile, tile), BF16),
                        pltpu.VMEM((2, 2 * tile, LANES), F32),
                        pltpu.VMEM((nq + 1, 2 * tile, LANES), F32),
                        pltpu.VMEM((nq + 1, 2 * tile, LANES), F32)],
        compiler_params=pltpu.CompilerParams(
            dimension_semantics=("parallel", "parallel"), vmem_limit_bytes=VMEM_LIMIT_BYTES),
        name="diff_attn",
    )(dq, dk, dv, bias_tiles, lamv, subln)


def _pad_cols(w, width):
    return jnp.pad(w, ((0, 0), (0, width - w.shape[1])))


def _slot_cols(w, heads, used, slot):
    lead = w.shape[0]
    w = w.reshape(lead, heads, used)
    return jnp.pad(w, ((0, 0), (0, 0), (0, slot - used))).reshape(lead, heads * slot)


def _layout_w_in(w_in):
    offs = np.cumsum([0, GLA_HEADS * GLA_DK, GLA_HEADS * GLA_DK, GLA_WIDTH, GLA_RANK, GLA_WIDTH,
                      DIFF_HEADS * 2 * DIFF_DQK, DIFF_HEADS * 2 * DIFF_DQK, DIFF_WIDTH,
                      CONV_DIM, CONV_DIM, CONV_DIM])
    gq, gk, gv, glr, gate, dq, dk, dv, cb, cc, ch = [w_in[:, offs[n]:offs[n + 1]] for n in range(11)]
    half = LANES // 2
    gq_slots = _slot_cols(gq, GLA_HEADS, GLA_DK, HEAD_SLOT)
    gq_slots = gq_slots.at[:, GLR_LANE:GLR_LANE + GLA_RANK].set(glr)
    cols = [
        gq_slots,
        _slot_cols(gk, GLA_HEADS, GLA_DK, HEAD_SLOT),
        gv,
        gate,
        _slot_cols(dq, 2 * DIFF_HEADS, DIFF_DQK, half),
        _slot_cols(dk, 2 * DIFF_HEADS, DIFF_DQK, half),
        _slot_cols(dv, DIFF_HEADS, DIFF_DV, LANES),
        cb, cc, ch,
    ]
    w = jnp.concatenate(cols, axis=1)
    assert w.shape[1] == N_PROJ
    return w.astype(BF16)


def kernel(x, ffn1_norm, ffn1_gate, ffn1_up, ffn1_down, mix_norm, w_in, gla_gk_up, gla_gk_bias,
           gla_norm, diff_lambda_q1, diff_lambda_k1, diff_lambda_q2, diff_lambda_k2, diff_subln,
           rel_bias, conv_w, w_out, ffn2_norm, ffn2_gate, ffn2_up, ffn2_down, final_norm):
    batch, seq, _ = x.shape
    depth = w_in.shape[0]
    t = batch * seq
    xf = x.reshape(t, D_MODEL)
    bias_tiles = _bias_tiles(rel_bias, min(ATT_TILE, seq))
    final_g = final_norm.reshape(1, D_MODEL)
    ffn1_w = _cast_weights((ffn1_gate, ffn1_up, ffn1_down), 0)

    for l in range(depth):
        lam_init = 0.8 - 0.6 * math.exp(-0.3 * l)
        xf, ffn2_w = _ffn(xf, ffn1_norm[l].reshape(1, D_MODEL), *ffn1_w, final_g, False,
                          cast=((ffn2_gate, ffn2_up, ffn2_down), l))

        w_pad = _layout_w_in(w_in[l])
        up = _slot_cols(gla_gk_up[l], GLA_HEADS, GLA_DK, HEAD_SLOT)
        up = jnp.pad(up, ((GLR_LANE, GLA_QK_LANES - GLR_LANE - GLA_RANK), (0, 0)))
        up_hi = up.astype(BF16)
        up_lo = (up - up_hi.astype(F32)).astype(BF16)
        gkup = jnp.stack([up_hi, up_lo], axis=0)
        gkb = _slot_cols(gla_gk_bias[l].reshape(1, -1), GLA_HEADS, GLA_DK, HEAD_SLOT)
        cw = jnp.pad(conv_w[l], ((0, F32_SUBLANES - CONV_WIDTH), (0, 0)))
        gn_tiled = jnp.tile(gla_norm[l], GLA_HEADS).reshape(1, GLA_WIDTH)
        gla_o, dq, dk, dv, conv_o = _proj(
            xf, mix_norm[l].reshape(1, D_MODEL), w_pad, gkup, gkb, cw, gn_tiled, seq)

        lamv = jnp.stack([diff_lambda_q1[l], diff_lambda_k1[l], diff_lambda_q2[l], diff_lambda_k2[l]])
        lamv = _pad_cols(lamv.astype(F32), LANES)
        subln = _pad_cols(diff_subln[l].reshape(1, DIFF_DV), LANES)
        diff_o = _attn(dq, dk, dv, bias_tiles, lamv, subln, batch, seq, lam_init)

        wo = w_out[l]
        w1 = wo[0:GLA_WIDTH].astype(BF16)
        w2 = wo[GLA_WIDTH:GLA_WIDTH + DIFF_WIDTH].reshape(DIFF_HEADS, DIFF_DV, D_MODEL)
        w2 = jnp.pad(w2, ((0, 0), (0, LANES - DIFF_DV), (0, 0))).reshape(DIFF_LANES, D_MODEL).astype(BF16)
        w3 = wo[GLA_WIDTH + DIFF_WIDTH:].astype(BF16)
        last = l == depth - 1
        xf, ffn1_w = _ffn(xf, ffn2_norm[l].reshape(1, D_MODEL), *ffn2_w, final_g, last,
                          mix=(gla_o, diff_o, conv_o, w1, w2, w3),
                          cast=None if last else ((ffn1_gate, ffn1_up, ffn1_down), l + 1))

    return xf.reshape(batch, seq, D_MODEL)
```

```python
import functools
import math

import numpy as np
import jax
import jax.numpy as jnp
from jax import lax
from jax.experimental import pallas as pl
from jax.experimental.pallas import tpu as pltpu

F32 = jnp.float32
BF16 = jnp.bfloat16

D_MODEL = 1024
D_FF = 2816
EPS = 1e-6
CHUNK = 64
GLA_HEADS = 4
GLA_DK = 48
GLA_DV = 96
GLA_RANK = 16
GLA_GATE_NORM = 16.0
DIFF_HEADS = 4
DIFF_DQK = 48
DIFF_DV = 96
CONV_DIM = 256
CONV_WIDTH = 3
NUM_BUCKETS = 32
MAX_DISTANCE = 128
GLA_WIDTH = GLA_HEADS * GLA_DV
DIFF_WIDTH = DIFF_HEADS * DIFF_DV

LANES = 128
F32_SUBLANES = 8
BF16_SUBLANES = 16
MXU_DEPTH = 256
VMEM_LIMIT_BYTES = 62 * 1024 * 1024

HEAD_SLOT = 64
GLA_QK_LANES = GLA_HEADS * HEAD_SLOT
SUB = 16
N_SUB = CHUNK // SUB
KST_ROWS = SUB * (N_SUB * (N_SUB - 1) // 2)
DIFF_LANES = DIFF_HEADS * LANES
MASK_VALUE = -1e30
GLA_SAFE_DECAY = 60.0
LOG2E = math.log2(math.e)

OFF_GQ = 0
OFF_GK = OFF_GQ + GLA_QK_LANES
OFF_GV = OFF_GK + GLA_QK_LANES
OFF_GATE = OFF_GV + GLA_WIDTH
OFF_DQ = OFF_GATE + GLA_WIDTH
OFF_DK = OFF_DQ + DIFF_LANES
OFF_DV = OFF_DK + DIFF_LANES
OFF_CB = OFF_DV + DIFF_LANES
OFF_CC = OFF_CB + CONV_DIM
OFF_CH = OFF_CC + CONV_DIM
N_PROJ = OFF_CH + CONV_DIM
GLR_LANE = GLA_DK
assert GLR_LANE + GLA_RANK <= HEAD_SLOT

FFN_ROWS = 1024
FFN_F_SPLITS = (0, 6 * MXU_DEPTH, D_FF)
assert all((hi - lo) % MXU_DEPTH == 0 for lo, hi in zip(FFN_F_SPLITS[:-1], FFN_F_SPLITS[1:]))
PROJ_ROWS = 1024
GLA_EARLY_FILLERS = 2
GLA_FILL_EVERY_SCORES = 16
GLA_FILL_EVERY_STATE = 4
ATT_TILE = 256
ATT_UNROLL_FAR = 36
ATT_UNROLL_NEAR = 22
ATT_FINISH_GROUP = 4


def _dot(a, b):
    return jnp.dot(a, b, preferred_element_type=F32)


def _dot_nt(a, b):
    return lax.dot_general(a, b, (((1,), (1,)), ((), ())), preferred_element_type=F32)


def _dot_tn(a, b):
    return lax.dot_general(a, b, (((0,), (0,)), ((), ())), preferred_element_type=F32)


def _split_bf16(x):
    hi = x.astype(BF16)
    lo = (x - hi.astype(F32)).astype(BF16)
    return hi, lo


def _rms(x, g):
    return x * lax.rsqrt(jnp.mean(x * x, axis=-1, keepdims=True) + EPS) * g


def _sigmoid(x):
    return 1.0 / (1.0 + jnp.exp(-x))


def _ffn_kernel(*refs, final, mix, cast):
    refs = list(refs)
    x_ref, g_ref, wg_ref, wu_ref, wd_ref, fg_ref = refs[:6]
    mix_refs = refs[6:12] if mix else ()
    n_in = 6 + len(mix_refs)
    src_refs = refs[n_in:n_in + 3] if cast else ()
    o_ref = refs[n_in + len(src_refs)]
    dst_refs = refs[n_in + len(src_refs) + 1:]
    for src, dst in zip(src_refs, dst_refs):
        dst[...] = src[...].astype(BF16)

    x = x_ref[...]
    if mix:
        gla_ref, diff_ref, conv_ref, w1_ref, w2_ref, w3_ref = mix_refs
        x = (x + _dot(gla_ref[...], w1_ref[...]) + _dot(diff_ref[...], w2_ref[...])
             + _dot(conv_ref[...], w3_ref[...]))
    h = _rms(x, g_ref[...]).astype(BF16)
    acc = None
    for lo, hi in zip(FFN_F_SPLITS[:-1], FFN_F_SPLITS[1:]):
        sl = slice(lo, hi)
        gate = _dot(h, wg_ref[:, sl])
        up = _dot(h, wu_ref[:, sl])
        a = (gate * _sigmoid(gate) * up).astype(BF16)
        part = _dot(a, wd_ref[sl, :])
        acc = part if acc is None else acc + part
    y = x + 0.5 * acc
    if final:
        y = _rms(y, fg_ref[...])
    o_ref[...] = y


def _const_spec(shape):
    nd = len(shape)
    return pl.BlockSpec(shape, lambda *_: (0,) * nd, pipeline_mode=pl.Buffered(1))


def _cast_block(n_rows, steps):
    share = 1
    while (n_rows * share) % (steps * BF16_SUBLANES) or steps % share:
        share *= 2
        assert share <= steps
    return n_rows * share // steps, share


def _ffn(x, g, wg, wu, wd, final_g, final, mix=None, cast=None):
    t = x.shape[0]
    tm = min(FFN_ROWS, t)
    steps = t // tm
    row = lambda width: pl.BlockSpec((tm, width), lambda i: (i, 0))
    operands = [x, g, wg, wu, wd, final_g]
    in_specs = [row(D_MODEL), _const_spec((1, D_MODEL)), _const_spec((D_MODEL, D_FF)),
                _const_spec((D_MODEL, D_FF)), _const_spec((D_FF, D_MODEL)), _const_spec((1, D_MODEL))]
    out_shapes = [jax.ShapeDtypeStruct((t, D_MODEL), F32)]
    out_specs = [row(D_MODEL)]
    if mix is not None:
        operands += list(mix)
        in_specs += [row(a.shape[1]) for a in mix[:3]] + [_const_spec(w.shape) for w in mix[3:]]
    if cast is not None:
        stacked, layer = cast
        for w in stacked:
            rows, share = _cast_block(w.shape[1], steps)
            operands.append(w)
            in_specs.append(pl.BlockSpec((None, rows, w.shape[2]),
                                         lambda i, share=share: (layer, i // share, 0)))
            out_shapes.append(jax.ShapeDtypeStruct(w.shape[1:], BF16))
            out_specs.append(pl.BlockSpec((rows, w.shape[2]), lambda i, share=share: (i // share, 0)))
    outs = pl.pallas_call(
        functools.partial(_ffn_kernel, final=final, mix=mix is not None, cast=cast is not None),
        out_shape=tuple(out_shapes),
        grid=(steps,),
        in_specs=in_specs,
        out_specs=tuple(out_specs),
        compiler_params=pltpu.CompilerParams(
            dimension_semantics=("arbitrary",), vmem_limit_bytes=VMEM_LIMIT_BYTES),
        name="ffn",
    )(*operands)
    return outs[0], tuple(outs[1:])


def _gla_cum_matrix():
    t = np.arange(CHUNK)[:, None]
    u = np.arange(CHUNK)[None, :]
    tri = (u <= t)
    ref = (u <= (t // SUB) * SUB - 1)
    ones = np.ones((CHUNK, CHUNK), bool)
    parts = [tri, ref, ones]
    for j in range(1, N_SUB):
        s = np.arange(SUB * j)[:, None]
        parts.append((u > s) & (u <= SUB * j - 1))
    return np.concatenate(parts, axis=0).astype(np.float32)


def _gla_score_mask():
    t = np.arange(CHUNK)[:, None]
    group = np.concatenate([np.full(SUB * j, j) for j in range(1, N_SUB)])
    keep = (group[None, :] == (t // SUB))
    return np.tile(keep, (1, GLA_HEADS)).astype(np.float32)


def _gla_head_ids():
    qk_head = lax.broadcasted_iota(jnp.int32, (1, GLA_QK_LANES), 1) // HEAD_SLOT
    v_head = lax.broadcasted_iota(jnp.int32, (1, GLA_WIDTH), 1) // GLA_DV
    row_head_v = lax.broadcasted_iota(jnp.int32, (GLA_WIDTH, 1), 0) // GLA_DV
    return qk_head, v_head, row_head_v == qk_head


def _gla_fast(q, k, g, v, state, tri, fillers):
    rows = q.shape[0]
    n_chunks = rows // CHUNK

    def fill():
        if fillers:
            fillers.pop(0)()

    qk_head, v_head, state_mask = _gla_head_ids()
    chunk = lambda x, c: x[c * CHUNK:(c + 1) * CHUNK]
    sel_rows = pl.cdiv(n_chunks, BF16_SUBLANES) * BF16_SUBLANES
    sel_r = lax.broadcasted_iota(jnp.int32, (sel_rows, rows), 0)
    sel_c = lax.broadcasted_iota(jnp.int32, (sel_rows, rows), 1) // CHUNK
    chunk_sel = jnp.where(sel_r == sel_c, 1.0, 0.0).astype(BF16)
    g_hi, g_lo = _split_bf16(g)
    totals = _dot(chunk_sel, g_hi) + _dot(chunk_sel, g_lo)
    b = jnp.concatenate([_dot(tri, chunk(g_hi, c)) + _dot(tri, chunk(g_lo, c))
                         for c in range(n_chunks)], axis=0)
    b_last = jnp.concatenate([jnp.broadcast_to(totals[c:c + 1], (CHUNK, GLA_QK_LANES))
                              for c in range(n_chunks)], axis=0)
    qd = (q * jnp.exp(b)).astype(BF16)
    kd = (k * jnp.exp(-b)).astype(BF16)
    kw = (k * jnp.exp(b_last - b)).astype(BF16)
    t_idx = lax.broadcasted_iota(jnp.int32, (CHUNK, GLA_QK_LANES), 0)
    s_idx = lax.broadcasted_iota(jnp.int32, (CHUNK, GLA_QK_LANES), 1) % HEAD_SLOT
    causal = s_idx <= t_idx
    intra, update = [], []
    for c in range(n_chunks):
        kd_c, v_c = chunk(kd, c), chunk(v, c)
        kd_bd = jnp.concatenate(
            [jnp.where(qk_head == hd, kd_c, jnp.zeros_like(kd_c)) for hd in range(GLA_HEADS)],
            axis=0)
        a = _dot_nt(chunk(qd, c), kd_bd)
        a = jnp.where(causal, a, 0.0).astype(BF16)
        v_bd = jnp.concatenate(
            [jnp.where(v_head == hd, v_c, jnp.zeros_like(v_c)) for hd in range(GLA_HEADS)],
            axis=0)
        intra.append(_dot(a, v_bd))
        update.append(_dot_tn(v_c, chunk(kw, c)))
        if (c + 1) % GLA_FILL_EVERY_SCORES == 0:
            fill()
    raw = []
    for c in range(n_chunks):
        raw.append(intra[c] + _dot_nt(chunk(qd, c), state.astype(BF16)))
        state = state * jnp.exp(totals[c:c + 1]) + jnp.where(state_mask, update[c], 0.0)
        if (c + 1) % GLA_FILL_EVERY_STATE == 0:
            fill()
    return jnp.concatenate(raw, axis=0), state


def _gla_general(q_ref, k_ref, g_ref, v_ref, s_ref, raw_ref, cum_ref, amask_ref):
    n_chunks = q_ref.shape[0] // CHUNK
    qk_head, v_head, state_mask = _gla_head_ids()
    cum = cum_ref[...]
    amask = amask_ref[...] > 0.5
    ind_r = lax.broadcasted_iota(jnp.int32, (GLA_QK_LANES, LANES), 0) // HEAD_SLOT
    ind_c = lax.broadcasted_iota(jnp.int32, (GLA_QK_LANES, LANES), 1)
    ind_sum = jnp.where(ind_r == ind_c, 1.0, 0.0).astype(BF16)
    bc_r = lax.broadcasted_iota(jnp.int32, (LANES, GLA_WIDTH), 0)
    bc_c = lax.broadcasted_iota(jnp.int32, (LANES, GLA_WIDTH), 1) // GLA_DV
    ind_bcast = jnp.where(bc_r == bc_c, 1.0, 0.0).astype(BF16)
    row_in_sub = lax.broadcasted_iota(jnp.int32, (CHUNK, 1), 0) % SUB

    def chunk_body(c, carry):
        r0 = pl.multiple_of(c * CHUNK, CHUNK)
        q = q_ref[pl.ds(r0, CHUNK), :]
        k = k_ref[pl.ds(r0, CHUNK), :]
        g = g_ref[pl.ds(r0, CHUNK), :]
        v = v_ref[pl.ds(r0, CHUNK), :]
        g_hi, g_lo = _split_bf16(g)
        cums = _dot(cum, g_hi) + _dot(cum, g_lo)
        b = cums[0:CHUNK]
        b_ref = cums[CHUNK:2 * CHUNK]
        b_last = cums[2 * CHUNK:3 * CHUNK]
        e_kst = cums[3 * CHUNK:3 * CHUNK + KST_ROWS]

        state = s_ref[...]
        q_in = (q * jnp.exp(b)).astype(BF16)
        o = _dot_nt(q_in, state.astype(BF16))

        kw = (k * jnp.exp(b_last - b)).astype(BF16)
        decay = jnp.exp(jnp.concatenate([b_last] * (GLA_WIDTH // CHUNK), axis=0))
        s_ref[...] = state * decay + jnp.where(state_mask, _dot_tn(v, kw), 0.0)

        q_sub = (q * jnp.exp(jnp.minimum(b - b_ref, 0.0))).astype(BF16)
        k_st = jnp.concatenate([k[0:SUB * j] for j in range(1, N_SUB)], axis=0) * jnp.exp(e_kst)
        k_bd = jnp.concatenate(
            [jnp.where(qk_head == hd, k_st, 0.0) for hd in range(GLA_HEADS)], axis=0).astype(BF16)
        a = _dot_nt(q_sub, k_bd)
        a = jnp.where(amask, a, 0.0).astype(BF16)
        v_st = jnp.concatenate([v[0:SUB * j] for j in range(1, N_SUB)], axis=0)
        v_bd = jnp.concatenate(
            [jnp.where(v_head == hd, v_st, jnp.zeros_like(v_st)) for hd in range(GLA_HEADS)], axis=0)
        o = o + _dot(a, v_bd)

        vf = v.astype(F32)
        prods = []
        for d in range(SUB):
            ks = k if d == 0 else pltpu.roll(k, d, 0)
            bs = b if d == 0 else pltpu.roll(b, d, 0)
            prods.append((q * ks * jnp.exp(jnp.minimum(b - bs, 0.0))).astype(BF16))
        dsum = _dot(jnp.concatenate(prods, axis=0), ind_sum)
        valid = jnp.concatenate([row_in_sub >= d for d in range(SUB)], axis=0)
        dsum = jnp.where(valid, dsum, 0.0).astype(BF16)
        dbc = _dot(dsum, ind_bcast)
        for d in range(SUB):
            vs = vf if d == 0 else pltpu.roll(vf, d, 0)
            o = o + dbc[d * CHUNK:(d + 1) * CHUNK] * vs
        raw_ref[pl.ds(r0, CHUNK), :] = o
        return carry

    lax.fori_loop(0, n_chunks, chunk_body, 0)


def _gla_finish(o, gate, gain):
    hr = lax.broadcasted_iota(jnp.int32, (GLA_WIDTH, GLA_WIDTH), 0) // GLA_DV
    hc = lax.broadcasted_iota(jnp.int32, (GLA_WIDTH, GLA_WIDTH), 1) // GLA_DV
    head_ones = jnp.where(hr == hc, 1.0, 0.0).astype(BF16)
    ms = _dot((o * o).astype(BF16), head_ones) * (1.0 / GLA_DV)
    return (o * lax.rsqrt(ms + EPS) * gain * (gate * _sigmoid(gate))).astype(BF16)


def _proj_kernel(x_ref, g_ref, w_ref, gkup_ref, gkb_ref, cw_ref, cum_ref, tri_ref, amask_ref, gn_ref,
                 gla_ref, dq_ref, dk_ref, dv_ref, conv_ref,
                 zbuf_ref, q_s, k_s, g_s, v_s, s_ref, s_next_ref, raw_ref, *, tiles_per_seq):
    tm = x_ref.shape[0]
    h = _rms(x_ref[...], g_ref[...]).astype(BF16)

    @pl.when(pl.program_id(0) % tiles_per_seq == 0)
    def _():
        zbuf_ref[0:F32_SUBLANES, :] = jnp.zeros((F32_SUBLANES, CONV_DIM), F32)
        s_ref[...] = jnp.zeros_like(s_ref)

    def proj(off, width):
        return _dot(h, w_ref[:, off:off + width])

    qk = proj(OFF_GQ, 2 * GLA_QK_LANES)
    gq = qk[:, 0:GLA_QK_LANES]
    gk = qk[:, GLA_QK_LANES:]
    vg = proj(OFF_GV, 2 * GLA_WIDTH)
    gv = vg[:, 0:GLA_WIDTH].astype(BF16)
    gate = vg[:, GLA_WIDTH:]

    glr_hi, glr_lo = _split_bf16(gq)
    up_hi = gkup_ref[0]
    up_lo = gkup_ref[1]
    z = _dot(glr_hi, up_hi) + _dot(glr_lo, up_hi) + _dot(glr_hi, up_lo) + gkb_ref[...]

    slab = 2 * LANES

    def diff_q(c):
        dq_ref[:, c:c + slab] = (proj(OFF_DQ + c, slab) * (DIFF_DQK ** -0.5 * LOG2E)).astype(BF16)

    def diff_k(c):
        dk_ref[:, c:c + slab] = proj(OFF_DK + c, slab).astype(BF16)

    def diff_v(c):
        lane_v = lax.broadcasted_iota(jnp.int32, (tm, slab), 1)
        dv_ref[:, c:c + slab] = jnp.where(lane_v % LANES == DIFF_DV, 1.0,
                                          proj(OFF_DV + c, slab)).astype(BF16)

    def conv_z():
        zbuf_ref[F32_SUBLANES:F32_SUBLANES + tm, :] = proj(OFF_CC, CONV_DIM) * proj(OFF_CH, CONV_DIM)

    def conv_out():
        cw = cw_ref[...]
        y = (cw[2:3, :] * zbuf_ref[F32_SUBLANES:F32_SUBLANES + tm, :]
             + cw[1:2, :] * zbuf_ref[F32_SUBLANES - 1:F32_SUBLANES - 1 + tm, :]
             + cw[0:1, :] * zbuf_ref[F32_SUBLANES - 2:F32_SUBLANES - 2 + tm, :])
        conv_ref[...] = (proj(OFF_CB, CONV_DIM) * y).astype(BF16)
        zbuf_ref[0:F32_SUBLANES, :] = zbuf_ref[tm:tm + F32_SUBLANES, :]

    fillers = []
    for c in range(0, DIFF_LANES, slab):
        fillers += [functools.partial(diff_q, c), functools.partial(diff_k, c),
                    functools.partial(diff_v, c)]
    fillers += [conv_z, conv_out]

    for _ in range(GLA_EARLY_FILLERS):
        fillers.pop(0)()
    logsig = jnp.minimum(z, 0.0) - jnp.log1p(jnp.exp(-jnp.abs(z)))
    lane = lax.broadcasted_iota(jnp.int32, (tm, GLA_QK_LANES), 1)
    gg = jnp.where(lane % HEAD_SLOT < GLA_DK, logsig * (1.0 / GLA_GATE_NORM), 0.0)

    q_scaled = gq * (GLA_DK ** -0.5)
    raw, new_state = _gla_fast(q_scaled, gk, gg, gv, s_ref[...], tri_ref[...], fillers)
    while fillers:
        fillers.pop(0)()
    gla_ref[...] = _gla_finish(raw, gate, gn_ref[...])
    s_next_ref[...] = new_state
    q_s[...] = q_scaled
    k_s[...] = gk
    g_s[...] = gg
    v_s[...] = gv
    raw_ref[...] = gate
    safe = jnp.min(gg) * CHUNK >= -GLA_SAFE_DECAY

    @pl.when(safe)
    def _():
        s_ref[...] = s_next_ref[...]

    @pl.when(jnp.logical_not(safe))
    def _():
        gate_kept = raw_ref[...]
        _gla_general(q_s, k_s, g_s, v_s, s_ref, raw_ref, cum_ref, amask_ref)
        gla_ref[...] = _gla_finish(raw_ref[...], gate_kept, gn_ref[...])


def _proj(x, g, w, gkup, gkb, cw, gn_tiled, seq):
    t = x.shape[0]
    tm = min(PROJ_ROWS, seq)
    row = lambda width: pl.BlockSpec((tm, width), lambda i: (i, 0))
    cum_np = _gla_cum_matrix()
    cum = jnp.asarray(cum_np, BF16)
    tri = jnp.asarray(cum_np[0:CHUNK], BF16)
    amask = jnp.asarray(_gla_score_mask(), F32)
    out_shapes = (
        jax.ShapeDtypeStruct((t, GLA_WIDTH), BF16),
        jax.ShapeDtypeStruct((t, DIFF_LANES), BF16),
        jax.ShapeDtypeStruct((t, DIFF_LANES), BF16),
        jax.ShapeDtypeStruct((t, DIFF_LANES), BF16),
        jax.ShapeDtypeStruct((t, CONV_DIM), BF16),
    )
    return pl.pallas_call(
        functools.partial(_proj_kernel, tiles_per_seq=seq // tm),
        out_shape=out_shapes,
        grid=(t // tm,),
        in_specs=[
            row(D_MODEL),
            _const_spec((1, D_MODEL)),
            _const_spec((D_MODEL, N_PROJ)),
            _const_spec((2, GLA_QK_LANES, GLA_QK_LANES)),
            _const_spec((1, GLA_QK_LANES)),
            _const_spec((F32_SUBLANES, CONV_DIM)),
            _const_spec(cum.shape),
            _const_spec(tri.shape),
            _const_spec(amask.shape),
            _const_spec((1, GLA_WIDTH)),
        ],
        out_specs=tuple(row(s.shape[1]) for s in out_shapes),
        scratch_shapes=[pltpu.VMEM((tm + F32_SUBLANES, CONV_DIM), F32),
                        pltpu.VMEM((tm, GLA_QK_LANES), F32),
                        pltpu.VMEM((tm, GLA_QK_LANES), F32),
                        pltpu.VMEM((tm, GLA_QK_LANES), F32),
                        pltpu.VMEM((tm, GLA_WIDTH), BF16),
                        pltpu.VMEM((GLA_WIDTH, GLA_QK_LANES), F32),
                        pltpu.VMEM((GLA_WIDTH, GLA_QK_LANES), F32),
                        pltpu.VMEM((tm, GLA_WIDTH), F32)],
        compiler_params=pltpu.CompilerParams(
            dimension_semantics=("arbitrary",), vmem_limit_bytes=VMEM_LIMIT_BYTES),
        name="mixer_proj_gla",
    )(x, g, w, gkup, gkb, cw, cum, tri, amask, gn_tiled)


def _t5_bucket(rel):
    nb = NUM_BUCKETS // 2
    max_exact = nb // 2
    ret = (rel > 0).astype(jnp.int32) * nb
    n = jnp.abs(rel)
    nf = jnp.maximum(n, 1).astype(jnp.float32)
    large = max_exact + (jnp.log(nf / max_exact) / math.log(MAX_DISTANCE / max_exact)
                         * (nb - max_exact)).astype(jnp.int32)
    large = jnp.minimum(large, nb - 1)
    return ret + jnp.where(n < max_exact, n, large)


def _bias_tiles(rel_bias, tile):
    assert tile >= MAX_DISTANCE and tile % CHUNK == 0
    table = rel_bias.astype(F32)
    far = table[NUM_BUCKETS // 2 - 1]
    heads = table.shape[1]
    rel = jnp.concatenate([jnp.arange(0, tile), jnp.arange(-2 * tile, 0)])
    onehot = _t5_bucket(rel)[None, :, None] == jnp.arange(NUM_BUCKETS)[None, None, :]
    per_rel = jnp.sum(jnp.where(onehot, table.T[:, None, :], 0.0), axis=-1)
    per_rel = (per_rel - far[:, None]) * LOG2E
    span = 3 * tile
    flat = jnp.tile(per_rel, (1, 2 * tile))[:, :2 * tile * (span - 1)]
    toep = flat.reshape(heads, 2 * tile, span - 1)[:, :, :tile]
    r = jnp.arange(tile)[:, None]
    c = jnp.arange(tile)[None, :]
    diag = jnp.where(((c // CHUNK) <= (r // CHUNK))[None], toep[:, :tile], MASK_VALUE)
    left = toep[:, tile:]
    return jnp.stack([left, diag], axis=1)


def _attn_kernel(q_ref, k_ref, v_ref, bias_ref, lamv_ref, subln_ref, o_ref,
                 s_buf, p_buf, alpha_buf, m_all, acc_all, *, tile, lam_init):
    seq = q_ref.shape[0]
    nq = seq // tile

    s_buf[...] = jnp.zeros(s_buf.shape, F32)
    p_buf[...] = jnp.zeros(p_buf.shape, BF16)
    alpha_buf[...] = jnp.zeros(alpha_buf.shape, F32)
    m_all[...] = jnp.full(m_all.shape, MASK_VALUE, F32)
    acc_all[...] = jnp.zeros(acc_all.shape, F32)

    lane = lax.broadcasted_iota(jnp.int32, (tile, LANES), 1)

    def stage1(i, j, slot):
        q = q_ref[pl.ds(pl.multiple_of(i * tile, tile), tile), :]
        zero = jnp.zeros_like(q)
        qs = jnp.concatenate([jnp.where(lane < LANES // 2, q, zero),
                              jnp.where(lane >= LANES // 2, q, zero)], axis=0)
        kj = k_ref[pl.ds(pl.multiple_of(j * tile, tile), tile), :]
        s_buf[slot] = _dot_nt(qs, kj)

    def stage2(i, bias, slot):
        for half in range(2):
            rows = slice(half * tile, (half + 1) * tile)
            s = s_buf[slot, rows, :]
            if bias is not None:
                s = s + bias
            m_prev = m_all[i, rows, :]
            m_next = jnp.maximum(m_prev, jnp.max(s, axis=-1, keepdims=True))
            p = jnp.exp2(s - jnp.concatenate([m_next] * (tile // LANES), axis=1))
            p_buf[slot, rows, :] = p.astype(BF16)
            alpha_buf[slot, rows, :] = jnp.exp2(m_prev - m_next)
            m_all[i, rows, :] = m_next

    def stage3(i, j, slot):
        vj = v_ref[pl.ds(pl.multiple_of(j * tile, tile), tile), :]
        acc_all[i] = alpha_buf[slot] * acc_all[i] + _dot(p_buf[slot], vj)

    def run_pipeline(n_pairs, first_pair, advance, with_bias, unroll):
        n_steps = pl.cdiv(n_pairs + 2, unroll) * unroll

        def step(t, pairs, slot_a, slot_b):
            (i0, j0), (i1, j1), (i2, j2) = pairs
            valid1 = jnp.logical_and(t >= 1, t <= n_pairs)
            valid2 = jnp.logical_and(t >= 2, t <= n_pairs + 1)
            stage3(jnp.where(valid2, i2, nq), j2, slot_a)
            stage2(jnp.where(valid1, i1, nq), bias_ref[0, j1 - i1 + 1] if with_bias else None, slot_b)
            stage1(i0, j0, slot_a)
            return (advance(i0, j0), (i0, j0), (i1, j1))

        def body(u, pairs):
            for r in range(unroll):
                pairs = step(unroll * u + r, pairs, r % 2, (r + 1) % 2)
            return pairs

        lax.fori_loop(0, n_steps // unroll, body, (first_pair,) * 3)

    def next_far(i, j):
        wrap = j == i - 2
        done = jnp.logical_and(wrap, i == nq - 1)
        step_i = jnp.logical_and(wrap, jnp.logical_not(done))
        return (jnp.where(step_i, i + 1, i), jnp.where(done, j, jnp.where(wrap, 0, j + 1)))

    def next_near(i, j):
        wrap = j == i
        done = jnp.logical_and(wrap, i == nq - 1)
        step_i = jnp.logical_and(wrap, jnp.logical_not(done))
        return (jnp.where(step_i, i + 1, i), jnp.where(jnp.logical_or(done, wrap), j, j + 1))

    zero = jnp.int32(0)
    if nq > 2:
        run_pipeline((nq - 1) * (nq - 2) // 2, (jnp.int32(2), zero), next_far, False, ATT_UNROLL_FAR)
    run_pipeline(2 * nq - 1, (zero, zero), next_near, True, ATT_UNROLL_NEAR)

    lv = lamv_ref[...]
    lam = (jnp.exp(jnp.sum(lv[0:1] * lv[1:2], axis=-1, keepdims=True))
           - jnp.exp(jnp.sum(lv[2:3] * lv[3:4], axis=-1, keepdims=True)) + lam_init)

    def finish(i):
        acc = acc_all[i]
        a1 = acc[0:tile]
        a2 = acc[tile:2 * tile]
        ones_col = lane == DIFF_DV
        ratio = jnp.sum(jnp.where(ones_col, a1 / jnp.where(ones_col, a2, 1.0), 0.0),
                        axis=-1, keepdims=True)
        u = a1 - lam * (ratio * a2)
        sq = jnp.where(ones_col, a1 * math.sqrt(DIFF_DV * EPS), u)
        ms = jnp.sum(sq * sq, axis=-1, keepdims=True) * (1.0 / DIFF_DV)
        y = u * lax.rsqrt(ms) * subln_ref[...] * (1.0 - lam_init)
        o_ref[pl.ds(pl.multiple_of(i * tile, tile), tile), :] = y.astype(BF16)

    group = math.gcd(nq, ATT_FINISH_GROUP)

    def finish_group(u, carry):
        for r in range(group):
            finish(u * group + r)
        return carry

    lax.fori_loop(0, nq // group, finish_group, 0)


def _attn(dq, dk, dv, bias_tiles, lamv, subln, batch, seq, lam_init):
    tile = min(ATT_TILE, seq)
    nq = seq // tile
    head_block = pl.BlockSpec((seq, LANES), lambda b, h: (b, h))
    return pl.pallas_call(
        functools.partial(_attn_kernel, tile=tile, lam_init=lam_init),
        out_shape=jax.ShapeDtypeStruct((batch * seq, DIFF_LANES), BF16),
        grid=(batch, DIFF_HEADS),
        in_specs=[
            head_block, head_block, head_block,
            pl.BlockSpec((1, 2, tile, tile), lambda b, h: (h, 0, 0, 0)),
            pl.BlockSpec((4, LANES), lambda b, h: (0, 0)),
            pl.BlockSpec((1, LANES), lambda b, h: (0, 0)),
        ],
        out_specs=head_block,
        scratch_shapes=[pltpu.VMEM((2, 2 * tile, tile), F32),
                        pltpu.VMEM((2, 2 * tile, tile), BF16),
                        pltpu.VMEM((2, 2 * tile, LANES), F32),
                        pltpu.VMEM((nq + 1, 2 * tile, LANES), F32),
                        pltpu.VMEM((nq + 1, 2 * tile, LANES), F32)],
        compiler_params=pltpu.CompilerParams(
            dimension_semantics=("parallel", "parallel"), vmem_limit_bytes=VMEM_LIMIT_BYTES),
        name="diff_attn",
    )(dq, dk, dv, bias_tiles, lamv, subln)


def _pad_cols(w, width):
    return jnp.pad(w, ((0, 0), (0, width - w.shape[1])))


def _slot_cols(w, heads, used, slot):
    lead = w.shape[0]
    w = w.reshape(lead, heads, used)
    return jnp.pad(w, ((0, 0), (0, 0), (0, slot - used))).reshape(lead, heads * slot)


def _layout_w_in(w_in):
    offs = np.cumsum([0, GLA_HEADS * GLA_DK, GLA_HEADS * GLA_DK, GLA_WIDTH, GLA_RANK, GLA_WIDTH,
                      DIFF_HEADS * 2 * DIFF_DQK, DIFF_HEADS * 2 * DIFF_DQK, DIFF_WIDTH,
                      CONV_DIM, CONV_DIM, CONV_DIM])
    gq, gk, gv, glr, gate, dq, dk, dv, cb, cc, ch = [w_in[:, offs[n]:offs[n + 1]] for n in range(11)]
    half = LANES // 2
    gq_slots = _slot_cols(gq, GLA_HEADS, GLA_DK, HEAD_SLOT)
    gq_slots = gq_slots.at[:, GLR_LANE:GLR_LANE + GLA_RANK].set(glr)
    cols = [
        gq_slots,
        _slot_cols(gk, GLA_HEADS, GLA_DK, HEAD_SLOT),
        gv,
        gate,
        _slot_cols(dq, 2 * DIFF_HEADS, DIFF_DQK, half),
        _slot_cols(dk, 2 * DIFF_HEADS, DIFF_DQK, half),
        _slot_cols(dv, DIFF_HEADS, DIFF_DV, LANES),
        cb, cc, ch,
    ]
    w = jnp.concatenate(cols, axis=1)
    assert w.shape[1] == N_PROJ
    return w.astype(BF16)


def kernel(x, ffn1_norm, ffn1_gate, ffn1_up, ffn1_down, mix_norm, w_in, gla_gk_up, gla_gk_bias,
           gla_norm, diff_lambda_q1, diff_lambda_k1, diff_lambda_q2, diff_lambda_k2, diff_subln,
           rel_bias, conv_w, w_out, ffn2_norm, ffn2_gate, ffn2_up, ffn2_down, final_norm):
    batch, seq, _ = x.shape
    depth = w_in.shape[0]
    t = batch * seq
    xf = x.reshape(t, D_MODEL)
    bias_tiles = _bias_tiles(rel_bias, min(ATT_TILE, seq))
    final_g = final_norm.reshape(1, D_MODEL)
    ffn1_w = (ffn1_gate[0].astype(BF16), ffn1_up[0].astype(BF16), ffn1_down[0].astype(BF16))

    for l in range(depth):
        lam_init = 0.8 - 0.6 * math.exp(-0.3 * l)
        xf, ffn2_w = _ffn(xf, ffn1_norm[l].reshape(1, D_MODEL), *ffn1_w, final_g, False,
                          cast=((ffn2_gate, ffn2_up, ffn2_down), l))

        w_pad = _layout_w_in(w_in[l])
        up = _slot_cols(gla_gk_up[l], GLA_HEADS, GLA_DK, HEAD_SLOT)
        up = jnp.pad(up, ((GLR_LANE, GLA_QK_LANES - GLR_LANE - GLA_RANK), (0, 0)))
        up_hi = up.astype(BF16)
        up_lo = (up - up_hi.astype(F32)).astype(BF16)
        gkup = jnp.stack([up_hi, up_lo], axis=0)
        gkb = _slot_cols(gla_gk_bias[l].reshape(1, -1), GLA_HEADS, GLA_DK, HEAD_SLOT)
        cw = jnp.pad(conv_w[l], ((0, F32_SUBLANES - CONV_WIDTH), (0, 0)))
        gn_tiled = jnp.tile(gla_norm[l], GLA_HEADS).reshape(1, GLA_WIDTH)
        gla_o, dq, dk, dv, conv_o = _proj(
            xf, mix_norm[l].reshape(1, D_MODEL), w_pad, gkup, gkb, cw, gn_tiled, seq)

        lamv = jnp.stack([diff_lambda_q1[l], diff_lambda_k1[l], diff_lambda_q2[l], diff_lambda_k2[l]])
        lamv = _pad_cols(lamv.astype(F32), LANES)
        subln = _pad_cols(diff_subln[l].reshape(1, DIFF_DV), LANES)
        diff_o = _attn(dq, dk, dv, bias_tiles, lamv, subln, batch, seq, lam_init)

        wo = w_out[l]
        w1 = wo[0:GLA_WIDTH].astype(BF16)
        w2 = wo[GLA_WIDTH:GLA_WIDTH + DIFF_WIDTH].reshape(DIFF_HEADS, DIFF_DV, D_MODEL)
        w2 = jnp.pad(w2, ((0, 0), (0, LANES - DIFF_DV), (0, 0))).reshape(DIFF_LANES, D_MODEL).astype(BF16)
        w3 = wo[GLA_WIDTH + DIFF_WIDTH:].astype(BF16)
        last = l == depth - 1
        xf, ffn1_w = _ffn(xf, ffn2_norm[l].reshape(1, D_MODEL), *ffn2_w, final_g, last,
                          mix=(gla_o, diff_o, conv_o, w1, w2, w3),
                          cast=None if last else ((ffn1_gate, ffn1_up, ffn1_down), l + 1))

    return xf.reshape(batch, seq, D_MODEL)
```

```python
import functools
import math

import numpy as np
import jax
import jax.numpy as jnp
from jax import lax
from jax.experimental import pallas as pl
from jax.experimental.pallas import tpu as pltpu

F32 = jnp.float32
BF16 = jnp.bfloat16

D_MODEL = 1024
D_FF = 2816
EPS = 1e-6
CHUNK = 64
GLA_HEADS = 4
GLA_DK = 48
GLA_DV = 96
GLA_RANK = 16
GLA_GATE_NORM = 16.0
DIFF_HEADS = 4
DIFF_DQK = 48
DIFF_DV = 96
CONV_DIM = 256
CONV_WIDTH = 3
NUM_BUCKETS = 32
MAX_DISTANCE = 128
GLA_WIDTH = GLA_HEADS * GLA_DV
DIFF_WIDTH = DIFF_HEADS * DIFF_DV

LANES = 128
F32_SUBLANES = 8
BF16_SUBLANES = 16
MXU_DEPTH = 256
VMEM_LIMIT_BYTES = 62 * 1024 * 1024

HEAD_SLOT = 64
GLA_QK_LANES = GLA_HEADS * HEAD_SLOT
SUB = 16
N_SUB = CHUNK // SUB
KST_ROWS = SUB * (N_SUB * (N_SUB - 1) // 2)
DIFF_LANES = DIFF_HEADS * LANES
MASK_VALUE = -1e30
GLA_SAFE_DECAY = 60.0
LOG2E = math.log2(math.e)

OFF_GQ = 0
OFF_GK = OFF_GQ + GLA_QK_LANES
OFF_GV = OFF_GK + GLA_QK_LANES
OFF_GATE = OFF_GV + GLA_WIDTH
OFF_DQ = OFF_GATE + GLA_WIDTH
OFF_DK = OFF_DQ + DIFF_LANES
OFF_DV = OFF_DK + DIFF_LANES
OFF_CB = OFF_DV + DIFF_LANES
OFF_CC = OFF_CB + CONV_DIM
OFF_CH = OFF_CC + CONV_DIM
N_PROJ = OFF_CH + CONV_DIM
GLR_LANE = GLA_DK
assert GLR_LANE + GLA_RANK <= HEAD_SLOT

FFN_ROWS = 1024
FFN_F_SPLITS = (0, 6 * MXU_DEPTH, D_FF)
assert all((hi - lo) % MXU_DEPTH == 0 for lo, hi in zip(FFN_F_SPLITS[:-1], FFN_F_SPLITS[1:]))
PROJ_ROWS = 1024
ATT_TILE = 256
ATT_UNROLL_FAR = 36
ATT_UNROLL_NEAR = 22
ATT_FINISH_GROUP = 4


def _dot(a, b):
    return jnp.dot(a, b, preferred_element_type=F32)


def _dot_nt(a, b):
    return lax.dot_general(a, b, (((1,), (1,)), ((), ())), preferred_element_type=F32)


def _dot_tn(a, b):
    return lax.dot_general(a, b, (((0,), (0,)), ((), ())), preferred_element_type=F32)


def _split_bf16(x):
    hi = x.astype(BF16)
    lo = (x - hi.astype(F32)).astype(BF16)
    return hi, lo


def _rms(x, g):
    return x * lax.rsqrt(jnp.mean(x * x, axis=-1, keepdims=True) + EPS) * g


def _sigmoid(x):
    return 1.0 / (1.0 + jnp.exp(-x))


def _ffn_kernel(*refs, final, mix, cast):
    refs = list(refs)
    x_ref, g_ref, wg_ref, wu_ref, wd_ref, fg_ref = refs[:6]
    mix_refs = refs[6:12] if mix else ()
    n_in = 6 + len(mix_refs)
    src_refs = refs[n_in:n_in + 3] if cast else ()
    o_ref = refs[n_in + len(src_refs)]
    dst_refs = refs[n_in + len(src_refs) + 1:]
    for src, dst in zip(src_refs, dst_refs):
        dst[...] = src[...].astype(BF16)

    x = x_ref[...]
    if mix:
        gla_ref, diff_ref, conv_ref, w1_ref, w2_ref, w3_ref = mix_refs
        x = (x + _dot(gla_ref[...], w1_ref[...]) + _dot(diff_ref[...], w2_ref[...])
             + _dot(conv_ref[...], w3_ref[...]))
    h = _rms(x, g_ref[...]).astype(BF16)
    acc = None
    for lo, hi in zip(FFN_F_SPLITS[:-1], FFN_F_SPLITS[1:]):
        sl = slice(lo, hi)
        gate = _dot(h, wg_ref[:, sl])
        up = _dot(h, wu_ref[:, sl])
        a = (gate * _sigmoid(gate) * up).astype(BF16)
        part = _dot(a, wd_ref[sl, :])
        acc = part if acc is None else acc + part
    y = x + 0.5 * acc
    if final:
        y = _rms(y, fg_ref[...])
    o_ref[...] = y


def _const_spec(shape):
    nd = len(shape)
    return pl.BlockSpec(shape, lambda *_: (0,) * nd, pipeline_mode=pl.Buffered(1))


def _cast_block(n_rows, steps):
    share = 1
    while (n_rows * share) % (steps * BF16_SUBLANES) or steps % share:
        share *= 2
        assert share <= steps
    return n_rows * share // steps, share


def _ffn(x, g, wg, wu, wd, final_g, final, mix=None, cast=None):
    t = x.shape[0]
    tm = min(FFN_ROWS, t)
    steps = t // tm
    row = lambda width: pl.BlockSpec((tm, width), lambda i: (i, 0))
    operands = [x, g, wg, wu, wd, final_g]
    in_specs = [row(D_MODEL), _const_spec((1, D_MODEL)), _const_spec((D_MODEL, D_FF)),
                _const_spec((D_MODEL, D_FF)), _const_spec((D_FF, D_MODEL)), _const_spec((1, D_MODEL))]
    out_shapes = [jax.ShapeDtypeStruct((t, D_MODEL), F32)]
    out_specs = [row(D_MODEL)]
    if mix is not None:
        operands += list(mix)
        in_specs += [row(a.shape[1]) for a in mix[:3]] + [_const_spec(w.shape) for w in mix[3:]]
    if cast is not None:
        stacked, layer = cast
        for w in stacked:
            rows, share = _cast_block(w.shape[1], steps)
            operands.append(w)
            in_specs.append(pl.BlockSpec((None, rows, w.shape[2]),
                                         lambda i, share=share: (layer, i // share, 0)))
            out_shapes.append(jax.ShapeDtypeStruct(w.shape[1:], BF16))
            out_specs.append(pl.BlockSpec((rows, w.shape[2]), lambda i, share=share: (i // share, 0)))
    outs = pl.pallas_call(
        functools.partial(_ffn_kernel, final=final, mix=mix is not None, cast=cast is not None),
        out_shape=tuple(out_shapes),
        grid=(steps,),
        in_specs=in_specs,
        out_specs=tuple(out_specs),
        compiler_params=pltpu.CompilerParams(
            dimension_semantics=("arbitrary",), vmem_limit_bytes=VMEM_LIMIT_BYTES),
        name="ffn",
    )(*operands)
    return outs[0], tuple(outs[1:])


def _gla_cum_matrix():
    t = np.arange(CHUNK)[:, None]
    u = np.arange(CHUNK)[None, :]
    tri = (u <= t)
    ref = (u <= (t // SUB) * SUB - 1)
    ones = np.ones((CHUNK, CHUNK), bool)
    parts = [tri, ref, ones]
    for j in range(1, N_SUB):
        s = np.arange(SUB * j)[:, None]
        parts.append((u > s) & (u <= SUB * j - 1))
    return np.concatenate(parts, axis=0).astype(np.float32)


def _gla_score_mask():
    t = np.arange(CHUNK)[:, None]
    group = np.concatenate([np.full(SUB * j, j) for j in range(1, N_SUB)])
    keep = (group[None, :] == (t // SUB))
    return np.tile(keep, (1, GLA_HEADS)).astype(np.float32)


def _gla_head_ids():
    qk_head = lax.broadcasted_iota(jnp.int32, (1, GLA_QK_LANES), 1) // HEAD_SLOT
    v_head = lax.broadcasted_iota(jnp.int32, (1, GLA_WIDTH), 1) // GLA_DV
    row_head_v = lax.broadcasted_iota(jnp.int32, (GLA_WIDTH, 1), 0) // GLA_DV
    return qk_head, v_head, row_head_v == qk_head


def _gla_fast(q, k, g, v, state, tri):
    rows = q.shape[0]
    n_chunks = rows // CHUNK
    qk_head, v_head, state_mask = _gla_head_ids()
    chunk = lambda x, c: x[c * CHUNK:(c + 1) * CHUNK]
    sel_rows = pl.cdiv(n_chunks, BF16_SUBLANES) * BF16_SUBLANES
    sel_r = lax.broadcasted_iota(jnp.int32, (sel_rows, rows), 0)
    sel_c = lax.broadcasted_iota(jnp.int32, (sel_rows, rows), 1) // CHUNK
    chunk_sel = jnp.where(sel_r == sel_c, 1.0, 0.0).astype(BF16)
    g_hi, g_lo = _split_bf16(g)
    totals = _dot(chunk_sel, g_hi) + _dot(chunk_sel, g_lo)
    b = jnp.concatenate([_dot(tri, chunk(g_hi, c)) + _dot(tri, chunk(g_lo, c))
                         for c in range(n_chunks)], axis=0)
    b_last = jnp.concatenate([jnp.broadcast_to(totals[c:c + 1], (CHUNK, GLA_QK_LANES))
                              for c in range(n_chunks)], axis=0)
    qd = (q * jnp.exp(b)).astype(BF16)
    kd = (k * jnp.exp(-b)).astype(BF16)
    kw = (k * jnp.exp(b_last - b)).astype(BF16)
    t_idx = lax.broadcasted_iota(jnp.int32, (CHUNK, GLA_QK_LANES), 0)
    s_idx = lax.broadcasted_iota(jnp.int32, (CHUNK, GLA_QK_LANES), 1) % HEAD_SLOT
    causal = s_idx <= t_idx
    intra, update = [], []
    for c in range(n_chunks):
        kd_c, v_c = chunk(kd, c), chunk(v, c)
        kd_bd = jnp.concatenate(
            [jnp.where(qk_head == hd, kd_c, jnp.zeros_like(kd_c)) for hd in range(GLA_HEADS)],
            axis=0)
        a = _dot_nt(chunk(qd, c), kd_bd)
        a = jnp.where(causal, a, 0.0).astype(BF16)
        v_bd = jnp.concatenate(
            [jnp.where(v_head == hd, v_c, jnp.zeros_like(v_c)) for hd in range(GLA_HEADS)],
            axis=0)
        intra.append(_dot(a, v_bd))
        update.append(_dot_tn(v_c, chunk(kw, c)))
    raw = []
    for c in range(n_chunks):
        raw.append(intra[c] + _dot_nt(chunk(qd, c), state.astype(BF16)))
        state = state * jnp.exp(totals[c:c + 1]) + jnp.where(state_mask, update[c], 0.0)
    return jnp.concatenate(raw, axis=0), state


def _gla_general(q_ref, k_ref, g_ref, v_ref, s_ref, raw_ref, cum_ref, amask_ref):
    n_chunks = q_ref.shape[0] // CHUNK
    qk_head, v_head, state_mask = _gla_head_ids()
    cum = cum_ref[...]
    amask = amask_ref[...] > 0.5
    ind_r = lax.broadcasted_iota(jnp.int32, (GLA_QK_LANES, LANES), 0) // HEAD_SLOT
    ind_c = lax.broadcasted_iota(jnp.int32, (GLA_QK_LANES, LANES), 1)
    ind_sum = jnp.where(ind_r == ind_c, 1.0, 0.0).astype(BF16)
    bc_r = lax.broadcasted_iota(jnp.int32, (LANES, GLA_WIDTH), 0)
    bc_c = lax.broadcasted_iota(jnp.int32, (LANES, GLA_WIDTH), 1) // GLA_DV
    ind_bcast = jnp.where(bc_r == bc_c, 1.0, 0.0).astype(BF16)
    row_in_sub = lax.broadcasted_iota(jnp.int32, (CHUNK, 1), 0) % SUB

    def chunk_body(c, carry):
        r0 = pl.multiple_of(c * CHUNK, CHUNK)
        q = q_ref[pl.ds(r0, CHUNK), :]
        k = k_ref[pl.ds(r0, CHUNK), :]
        g = g_ref[pl.ds(r0, CHUNK), :]
        v = v_ref[pl.ds(r0, CHUNK), :]
        g_hi, g_lo = _split_bf16(g)
        cums = _dot(cum, g_hi) + _dot(cum, g_lo)
        b = cums[0:CHUNK]
        b_ref = cums[CHUNK:2 * CHUNK]
        b_last = cums[2 * CHUNK:3 * CHUNK]
        e_kst = cums[3 * CHUNK:3 * CHUNK + KST_ROWS]

        state = s_ref[...]
        q_in = (q * jnp.exp(b)).astype(BF16)
        o = _dot_nt(q_in, state.astype(BF16))

        kw = (k * jnp.exp(b_last - b)).astype(BF16)
        decay = jnp.exp(jnp.concatenate([b_last] * (GLA_WIDTH // CHUNK), axis=0))
        s_ref[...] = state * decay + jnp.where(state_mask, _dot_tn(v, kw), 0.0)

        q_sub = (q * jnp.exp(jnp.minimum(b - b_ref, 0.0))).astype(BF16)
        k_st = jnp.concatenate([k[0:SUB * j] for j in range(1, N_SUB)], axis=0) * jnp.exp(e_kst)
        k_bd = jnp.concatenate(
            [jnp.where(qk_head == hd, k_st, 0.0) for hd in range(GLA_HEADS)], axis=0).astype(BF16)
        a = _dot_nt(q_sub, k_bd)
        a = jnp.where(amask, a, 0.0).astype(BF16)
        v_st = jnp.concatenate([v[0:SUB * j] for j in range(1, N_SUB)], axis=0)
        v_bd = jnp.concatenate(
            [jnp.where(v_head == hd, v_st, jnp.zeros_like(v_st)) for hd in range(GLA_HEADS)], axis=0)
        o = o + _dot(a, v_bd)

        vf = v.astype(F32)
        prods = []
        for d in range(SUB):
            ks = k if d == 0 else pltpu.roll(k, d, 0)
            bs = b if d == 0 else pltpu.roll(b, d, 0)
            prods.append((q * ks * jnp.exp(jnp.minimum(b - bs, 0.0))).astype(BF16))
        dsum = _dot(jnp.concatenate(prods, axis=0), ind_sum)
        valid = jnp.concatenate([row_in_sub >= d for d in range(SUB)], axis=0)
        dsum = jnp.where(valid, dsum, 0.0).astype(BF16)
        dbc = _dot(dsum, ind_bcast)
        for d in range(SUB):
            vs = vf if d == 0 else pltpu.roll(vf, d, 0)
            o = o + dbc[d * CHUNK:(d + 1) * CHUNK] * vs
        raw_ref[pl.ds(r0, CHUNK), :] = o
        return carry

    lax.fori_loop(0, n_chunks, chunk_body, 0)


def _gla_finish(o, gate, gain):
    hr = lax.broadcasted_iota(jnp.int32, (GLA_WIDTH, GLA_WIDTH), 0) // GLA_DV
    hc = lax.broadcasted_iota(jnp.int32, (GLA_WIDTH, GLA_WIDTH), 1) // GLA_DV
    head_ones = jnp.where(hr == hc, 1.0, 0.0).astype(BF16)
    ms = _dot((o * o).astype(BF16), head_ones) * (1.0 / GLA_DV)
    return (o * lax.rsqrt(ms + EPS) * gain * (gate * _sigmoid(gate))).astype(BF16)


def _proj_kernel(x_ref, g_ref, w_ref, gkup_ref, gkb_ref, cw_ref, cum_ref, tri_ref, amask_ref, gn_ref,
                 gla_ref, dq_ref, dk_ref, dv_ref, conv_ref,
                 zbuf_ref, q_s, k_s, g_s, v_s, s_ref, s_next_ref, raw_ref, *, tiles_per_seq):
    tm = x_ref.shape[0]
    h = _rms(x_ref[...], g_ref[...]).astype(BF16)

    @pl.when(pl.program_id(0) % tiles_per_seq == 0)
    def _():
        zbuf_ref[0:F32_SUBLANES, :] = jnp.zeros((F32_SUBLANES, CONV_DIM), F32)
        s_ref[...] = jnp.zeros_like(s_ref)

    def proj(off, width):
        return _dot(h, w_ref[:, off:off + width])

    qk = proj(OFF_GQ, 2 * GLA_QK_LANES)
    gq = qk[:, 0:GLA_QK_LANES]
    gk = qk[:, GLA_QK_LANES:]
    vg = proj(OFF_GV, 2 * GLA_WIDTH)
    gv = vg[:, 0:GLA_WIDTH].astype(BF16)
    gate = vg[:, GLA_WIDTH:]

    glr_hi, glr_lo = _split_bf16(gq)
    up_hi = gkup_ref[0]
    up_lo = gkup_ref[1]
    z = _dot(glr_hi, up_hi) + _dot(glr_lo, up_hi) + _dot(glr_hi, up_lo) + gkb_ref[...]
    logsig = jnp.minimum(z, 0.0) - jnp.log1p(jnp.exp(-jnp.abs(z)))
    lane = lax.broadcasted_iota(jnp.int32, (tm, GLA_QK_LANES), 1)
    gg = jnp.where(lane % HEAD_SLOT < GLA_DK, logsig * (1.0 / GLA_GATE_NORM), 0.0)

    dqk = proj(OFF_DQ, 2 * DIFF_LANES)
    dq_ref[...] = (dqk[:, 0:DIFF_LANES] * (DIFF_DQK ** -0.5 * LOG2E)).astype(BF16)
    dk_ref[...] = dqk[:, DIFF_LANES:].astype(BF16)
    lane_v = lax.broadcasted_iota(jnp.int32, (tm, DIFF_LANES), 1)
    dv_ref[...] = jnp.where(lane_v % LANES == DIFF_DV, 1.0, proj(OFF_DV, DIFF_LANES)).astype(BF16)

    conv = proj(OFF_CB, 3 * CONV_DIM)
    zc = conv[:, CONV_DIM:2 * CONV_DIM] * conv[:, 2 * CONV_DIM:]
    zbuf_ref[F32_SUBLANES:F32_SUBLANES + tm, :] = zc
    cw = cw_ref[...]
    y = (cw[2:3, :] * zc + cw[1:2, :] * zbuf_ref[F32_SUBLANES - 1:F32_SUBLANES - 1 + tm, :]
         + cw[0:1, :] * zbuf_ref[F32_SUBLANES - 2:F32_SUBLANES - 2 + tm, :])
    conv_ref[...] = (conv[:, 0:CONV_DIM] * y).astype(BF16)
    zbuf_ref[0:F32_SUBLANES, :] = zbuf_ref[tm:tm + F32_SUBLANES, :]

    q_scaled = gq * (GLA_DK ** -0.5)
    raw, new_state = _gla_fast(q_scaled, gk, gg, gv, s_ref[...], tri_ref[...])
    gla_ref[...] = _gla_finish(raw, gate, gn_ref[...])
    s_next_ref[...] = new_state
    q_s[...] = q_scaled
    k_s[...] = gk
    g_s[...] = gg
    v_s[...] = gv
    raw_ref[...] = gate
    safe = jnp.min(gg) * CHUNK >= -GLA_SAFE_DECAY

    @pl.when(safe)
    def _():
        s_ref[...] = s_next_ref[...]

    @pl.when(jnp.logical_not(safe))
    def _():
        gate_kept = raw_ref[...]
        _gla_general(q_s, k_s, g_s, v_s, s_ref, raw_ref, cum_ref, amask_ref)
        gla_ref[...] = _gla_finish(raw_ref[...], gate_kept, gn_ref[...])


def _proj(x, g, w, gkup, gkb, cw, gn_tiled, seq):
    t = x.shape[0]
    tm = min(PROJ_ROWS, seq)
    row = lambda width: pl.BlockSpec((tm, width), lambda i: (i, 0))
    cum_np = _gla_cum_matrix()
    cum = jnp.asarray(cum_np, BF16)
    tri = jnp.asarray(cum_np[0:CHUNK], BF16)
    amask = jnp.asarray(_gla_score_mask(), F32)
    out_shapes = (
        jax.ShapeDtypeStruct((t, GLA_WIDTH), BF16),
        jax.ShapeDtypeStruct((t, DIFF_LANES), BF16),
        jax.ShapeDtypeStruct((t, DIFF_LANES), BF16),
        jax.ShapeDtypeStruct((t, DIFF_LANES), BF16),
        jax.ShapeDtypeStruct((t, CONV_DIM), BF16),
    )
    return pl.pallas_call(
        functools.partial(_proj_kernel, tiles_per_seq=seq // tm),
        out_shape=out_shapes,
        grid=(t // tm,),
        in_specs=[
            row(D_MODEL),
            _const_spec((1, D_MODEL)),
            _const_spec((D_MODEL, N_PROJ)),
            _const_spec((2, GLA_QK_LANES, GLA_QK_LANES)),
            _const_spec((1, GLA_QK_LANES)),
            _const_spec((F32_SUBLANES, CONV_DIM)),
            _const_spec(cum.shape),
            _const_spec(tri.shape),
            _const_spec(amask.shape),
            _const_spec((1, GLA_WIDTH)),
        ],
        out_specs=tuple(row(s.shape[1]) for s in out_shapes),
        scratch_shapes=[pltpu.VMEM((tm + F32_SUBLANES, CONV_DIM), F32),
                        pltpu.VMEM((tm, GLA_QK_LANES), F32),
                        pltpu.VMEM((tm, GLA_QK_LANES), F32),
                        pltpu.VMEM((tm, GLA_QK_LANES), F32),
                        pltpu.VMEM((tm, GLA_WIDTH), BF16),
                        pltpu.VMEM((GLA_WIDTH, GLA_QK_LANES), F32),
                        pltpu.VMEM((GLA_WIDTH, GLA_QK_LANES), F32),
                        pltpu.VMEM((tm, GLA_WIDTH), F32)],
        compiler_params=pltpu.CompilerParams(
            dimension_semantics=("arbitrary",), vmem_limit_bytes=VMEM_LIMIT_BYTES),
        name="mixer_proj_gla",
    )(x, g, w, gkup, gkb, cw, cum, tri, amask, gn_tiled)


def _t5_bucket(rel):
    nb = NUM_BUCKETS // 2
    max_exact = nb // 2
    ret = (rel > 0).astype(jnp.int32) * nb
    n = jnp.abs(rel)
    nf = jnp.maximum(n, 1).astype(jnp.float32)
    large = max_exact + (jnp.log(nf / max_exact) / math.log(MAX_DISTANCE / max_exact)
                         * (nb - max_exact)).astype(jnp.int32)
    large = jnp.minimum(large, nb - 1)
    return ret + jnp.where(n < max_exact, n, large)


def _bias_tiles(rel_bias, tile):
    assert tile >= MAX_DISTANCE and tile % CHUNK == 0
    table = rel_bias.astype(F32)
    far = table[NUM_BUCKETS // 2 - 1]
    heads = table.shape[1]
    rel = jnp.concatenate([jnp.arange(0, tile), jnp.arange(-2 * tile, 0)])
    onehot = _t5_bucket(rel)[None, :, None] == jnp.arange(NUM_BUCKETS)[None, None, :]
    per_rel = jnp.sum(jnp.where(onehot, table.T[:, None, :], 0.0), axis=-1)
    per_rel = (per_rel - far[:, None]) * LOG2E
    span = 3 * tile
    flat = jnp.tile(per_rel, (1, 2 * tile))[:, :2 * tile * (span - 1)]
    toep = flat.reshape(heads, 2 * tile, span - 1)[:, :, :tile]
    r = jnp.arange(tile)[:, None]
    c = jnp.arange(tile)[None, :]
    diag = jnp.where(((c // CHUNK) <= (r // CHUNK))[None], toep[:, :tile], MASK_VALUE)
    left = toep[:, tile:]
    return jnp.stack([left, diag], axis=1)


def _attn_kernel(q_ref, k_ref, v_ref, bias_ref, lamv_ref, subln_ref, o_ref,
                 s_buf, p_buf, alpha_buf, m_all, acc_all, *, tile, lam_init):
    seq = q_ref.shape[0]
    nq = seq // tile

    s_buf[...] = jnp.zeros(s_buf.shape, F32)
    p_buf[...] = jnp.zeros(p_buf.shape, BF16)
    alpha_buf[...] = jnp.zeros(alpha_buf.shape, F32)
    m_all[...] = jnp.full(m_all.shape, MASK_VALUE, F32)
    acc_all[...] = jnp.zeros(acc_all.shape, F32)

    lane = lax.broadcasted_iota(jnp.int32, (tile, LANES), 1)

    def stage1(i, j, slot):
        q = q_ref[pl.ds(pl.multiple_of(i * tile, tile), tile), :]
        zero = jnp.zeros_like(q)
        qs = jnp.concatenate([jnp.where(lane < LANES // 2, q, zero),
                              jnp.where(lane >= LANES // 2, q, zero)], axis=0)
        kj = k_ref[pl.ds(pl.multiple_of(j * tile, tile), tile), :]
        s_buf[slot] = _dot_nt(qs, kj)

    def stage2(i, bias, slot):
        for half in range(2):
            rows = slice(half * tile, (half + 1) * tile)
            s = s_buf[slot, rows, :]
            if bias is not None:
                s = s + bias
            m_prev = m_all[i, rows, :]
            m_next = jnp.maximum(m_prev, jnp.max(s, axis=-1, keepdims=True))
            p = jnp.exp2(s - jnp.concatenate([m_next] * (tile // LANES), axis=1))
            p_buf[slot, rows, :] = p.astype(BF16)
            alpha_buf[slot, rows, :] = jnp.exp2(m_prev - m_next)
            m_all[i, rows, :] = m_next

    def stage3(i, j, slot):
        vj = v_ref[pl.ds(pl.multiple_of(j * tile, tile), tile), :]
        acc_all[i] = alpha_buf[slot] * acc_all[i] + _dot(p_buf[slot], vj)

    def run_pipeline(n_pairs, first_pair, advance, with_bias, unroll):
        n_steps = pl.cdiv(n_pairs + 2, unroll) * unroll

        def step(t, pairs, slot_a, slot_b):
            (i0, j0), (i1, j1), (i2, j2) = pairs
            valid1 = jnp.logical_and(t >= 1, t <= n_pairs)
            valid2 = jnp.logical_and(t >= 2, t <= n_pairs + 1)
            stage3(jnp.where(valid2, i2, nq), j2, slot_a)
            stage2(jnp.where(valid1, i1, nq), bias_ref[0, j1 - i1 + 1] if with_bias else None, slot_b)
            stage1(i0, j0, slot_a)
            return (advance(i0, j0), (i0, j0), (i1, j1))

        def body(u, pairs):
            for r in range(unroll):
                pairs = step(unroll * u + r, pairs, r % 2, (r + 1) % 2)
            return pairs

        lax.fori_loop(0, n_steps // unroll, body, (first_pair,) * 3)

    def next_far(i, j):
        wrap = j == i - 2
        done = jnp.logical_and(wrap, i == nq - 1)
        step_i = jnp.logical_and(wrap, jnp.logical_not(done))
        return (jnp.where(step_i, i + 1, i), jnp.where(done, j, jnp.where(wrap, 0, j + 1)))

    def next_near(i, j):
        wrap = j == i
        done = jnp.logical_and(wrap, i == nq - 1)
        step_i = jnp.logical_and(wrap, jnp.logical_not(done))
        return (jnp.where(step_i, i + 1, i), jnp.where(jnp.logical_or(done, wrap), j, j + 1))

    zero = jnp.int32(0)
    if nq > 2:
        run_pipeline((nq - 1) * (nq - 2) // 2, (jnp.int32(2), zero), next_far, False, ATT_UNROLL_FAR)
    run_pipeline(2 * nq - 1, (zero, zero), next_near, True, ATT_UNROLL_NEAR)

    lv = lamv_ref[...]
    lam = (jnp.exp(jnp.sum(lv[0:1] * lv[1:2], axis=-1, keepdims=True))
           - jnp.exp(jnp.sum(lv[2:3] * lv[3:4], axis=-1, keepdims=True)) + lam_init)

    def finish(i):
        acc = acc_all[i]
        a1 = acc[0:tile]
        a2 = acc[tile:2 * tile]
        ones_col = lane == DIFF_DV
        ratio = jnp.sum(jnp.where(ones_col, a1 / jnp.where(ones_col, a2, 1.0), 0.0),
                        axis=-1, keepdims=True)
        u = a1 - lam * (ratio * a2)
        sq = jnp.where(ones_col, a1 * math.sqrt(DIFF_DV * EPS), u)
        ms = jnp.sum(sq * sq, axis=-1, keepdims=True) * (1.0 / DIFF_DV)
        y = u * lax.rsqrt(ms) * subln_ref[...] * (1.0 - lam_init)
        o_ref[pl.ds(pl.multiple_of(i * tile, tile), tile), :] = y.astype(BF16)

    group = math.gcd(nq, ATT_FINISH_GROUP)

    def finish_group(u, carry):
        for r in range(group):
            finish(u * group + r)
        return carry

    lax.fori_loop(0, nq // group, finish_group, 0)


def _attn(dq, dk, dv, bias_tiles, lamv, subln, batch, seq, lam_init):
    tile = min(ATT_TILE, seq)
    nq = seq // tile
    head_block = pl.BlockSpec((seq, LANES), lambda b, h: (b, h))
    return pl.pallas_call(
        functools.partial(_attn_kernel, tile=tile, lam_init=lam_init),
        out_shape=jax.ShapeDtypeStruct((batch * seq, DIFF_LANES), BF16),
        grid=(batch, DIFF_HEADS),
        in_specs=[
            head_block, head_block, head_block,
            pl.BlockSpec((1, 2, tile, tile), lambda b, h: (h, 0, 0, 0)),
            pl.BlockSpec((4, LANES), lambda b, h: (0, 0)),
            pl.BlockSpec((1, LANES), lambda b, h: (0, 0)),
        ],
        out_specs=head_block,
        scratch_shapes=[pltpu.VMEM((2, 2 * tile, tile), F32),
                        pltpu.VMEM((2, 2 * tile, tile), BF16),
                        pltpu.VMEM((2, 2 * tile, LANES), F32),
                        pltpu.VMEM((nq + 1, 2 * tile, LANES), F32),
                        pltpu.VMEM((nq + 1, 2 * tile, LANES), F32)],
        compiler_params=pltpu.CompilerParams(
            dimension_semantics=("parallel", "parallel"), vmem_limit_bytes=VMEM_LIMIT_BYTES),
        name="diff_attn",
    )(dq, dk, dv, bias_tiles, lamv, subln)


def _pad_cols(w, width):
    return jnp.pad(w, ((0, 0), (0, width - w.shape[1])))


def _slot_cols(w, heads, used, slot):
    lead = w.shape[0]
    w = w.reshape(lead, heads, used)
    return jnp.pad(w, ((0, 0), (0, 0), (0, slot - used))).reshape(lead, heads * slot)


def _layout_w_in(w_in):
    offs = np.cumsum([0, GLA_HEADS * GLA_DK, GLA_HEADS * GLA_DK, GLA_WIDTH, GLA_RANK, GLA_WIDTH,
                      DIFF_HEADS * 2 * DIFF_DQK, DIFF_HEADS * 2 * DIFF_DQK, DIFF_WIDTH,
                      CONV_DIM, CONV_DIM, CONV_DIM])
    gq, gk, gv, glr, gate, dq, dk, dv, cb, cc, ch = [w_in[:, offs[n]:offs[n + 1]] for n in range(11)]
    half = LANES // 2
    gq_slots = _slot_cols(gq, GLA_HEADS, GLA_DK, HEAD_SLOT)
    gq_slots = gq_slots.at[:, GLR_LANE:GLR_LANE + GLA_RANK].set(glr)
    cols = [
        gq_slots,
        _slot_cols(gk, GLA_HEADS, GLA_DK, HEAD_SLOT),
        gv,
        gate,
        _slot_cols(dq, 2 * DIFF_HEADS, DIFF_DQK, half),
        _slot_cols(dk, 2 * DIFF_HEADS, DIFF_DQK, half),
        _slot_cols(dv, DIFF_HEADS, DIFF_DV, LANES),
        cb, cc, ch,
    ]
    w = jnp.concatenate(cols, axis=1)
    assert w.shape[1] == N_PROJ
    return w.astype(BF16)


def kernel(x, ffn1_norm, ffn1_gate, ffn1_up, ffn1_down, mix_norm, w_in, gla_gk_up, gla_gk_bias,
           gla_norm, diff_lambda_q1, diff_lambda_k1, diff_lambda_q2, diff_lambda_k2, diff_subln,
           rel_bias, conv_w, w_out, ffn2_norm, ffn2_gate, ffn2_up, ffn2_down, final_norm):
    batch, seq, _ = x.shape
    depth = w_in.shape[0]
    t = batch * seq
    xf = x.reshape(t, D_MODEL)
    bias_tiles = _bias_tiles(rel_bias, min(ATT_TILE, seq))
    final_g = final_norm.reshape(1, D_MODEL)
    ffn1_w = (ffn1_gate[0].astype(BF16), ffn1_up[0].astype(BF16), ffn1_down[0].astype(BF16))

    for l in range(depth):
        lam_init = 0.8 - 0.6 * math.exp(-0.3 * l)
        xf, ffn2_w = _ffn(xf, ffn1_norm[l].reshape(1, D_MODEL), *ffn1_w, final_g, False,
                          cast=((ffn2_gate, ffn2_up, ffn2_down), l))

        w_pad = _layout_w_in(w_in[l])
        up = _slot_cols(gla_gk_up[l], GLA_HEADS, GLA_DK, HEAD_SLOT)
        up = jnp.pad(up, ((GLR_LANE, GLA_QK_LANES - GLR_LANE - GLA_RANK), (0, 0)))
        up_hi = up.astype(BF16)
        up_lo = (up - up_hi.astype(F32)).astype(BF16)
        gkup = jnp.stack([up_hi, up_lo], axis=0)
        gkb = _slot_cols(gla_gk_bias[l].reshape(1, -1), GLA_HEADS, GLA_DK, HEAD_SLOT)
        cw = jnp.pad(conv_w[l], ((0, F32_SUBLANES - CONV_WIDTH), (0, 0)))
        gn_tiled = jnp.tile(gla_norm[l], GLA_HEADS).reshape(1, GLA_WIDTH)
        gla_o, dq, dk, dv, conv_o = _proj(
            xf, mix_norm[l].reshape(1, D_MODEL), w_pad, gkup, gkb, cw, gn_tiled, seq)

        lamv = jnp.stack([diff_lambda_q1[l], diff_lambda_k1[l], diff_lambda_q2[l], diff_lambda_k2[l]])
        lamv = _pad_cols(lamv.astype(F32), LANES)
        subln = _pad_cols(diff_subln[l].reshape(1, DIFF_DV), LANES)
        diff_o = _attn(dq, dk, dv, bias_tiles, lamv, subln, batch, seq, lam_init)

        wo = w_out[l]
        w1 = wo[0:GLA_WIDTH].astype(BF16)
        w2 = wo[GLA_WIDTH:GLA_WIDTH + DIFF_WIDTH].reshape(DIFF_HEADS, DIFF_DV, D_MODEL)
        w2 = jnp.pad(w2, ((0, 0), (0, LANES - DIFF_DV), (0, 0))).reshape(DIFF_LANES, D_MODEL).astype(BF16)
        w3 = wo[GLA_WIDTH + DIFF_WIDTH:].astype(BF16)
        last = l == depth - 1
        xf, ffn1_w = _ffn(xf, ffn2_norm[l].reshape(1, D_MODEL), *ffn2_w, final_g, last,
                          mix=(gla_o, diff_o, conv_o, w1, w2, w3),
                          cast=None if last else ((ffn1_gate, ffn1_up, ffn1_down), l + 1))

    return xf.reshape(batch, seq, D_MODEL)
```

```python
import functools
import math

import numpy as np
import jax
import jax.numpy as jnp
from jax import lax
from jax.experimental import pallas as pl
from jax.experimental.pallas import tpu as pltpu

F32 = jnp.float32
BF16 = jnp.bfloat16

D_MODEL = 1024
D_FF = 2816
EPS = 1e-6
CHUNK = 64
GLA_HEADS = 4
GLA_DK = 48
GLA_DV = 96
GLA_RANK = 16
GLA_GATE_NORM = 16.0
DIFF_HEADS = 4
DIFF_DQK = 48
DIFF_DV = 96
CONV_DIM = 256
CONV_WIDTH = 3
NUM_BUCKETS = 32
MAX_DISTANCE = 128
GLA_WIDTH = GLA_HEADS * GLA_DV
DIFF_WIDTH = DIFF_HEADS * DIFF_DV

LANES = 128
F32_SUBLANES = 8
BF16_SUBLANES = 16
MXU_DEPTH = 256
VMEM_LIMIT_BYTES = 62 * 1024 * 1024

HEAD_SLOT = 64
GLA_QK_LANES = GLA_HEADS * HEAD_SLOT
SUB = 16
N_SUB = CHUNK // SUB
KST_ROWS = SUB * (N_SUB * (N_SUB - 1) // 2)
DIFF_LANES = DIFF_HEADS * LANES
MASK_VALUE = -1e30
GLA_SAFE_DECAY = 60.0
LOG2E = math.log2(math.e)

OFF_GQ = 0
OFF_GK = OFF_GQ + GLA_QK_LANES
OFF_GV = OFF_GK + GLA_QK_LANES
OFF_GATE = OFF_GV + GLA_WIDTH
OFF_DQ = OFF_GATE + GLA_WIDTH
OFF_DK = OFF_DQ + DIFF_LANES
OFF_DV = OFF_DK + DIFF_LANES
OFF_CB = OFF_DV + DIFF_LANES
OFF_CC = OFF_CB + CONV_DIM
OFF_CH = OFF_CC + CONV_DIM
N_PROJ = OFF_CH + CONV_DIM
GLR_LANE = GLA_DK
assert GLR_LANE + GLA_RANK <= HEAD_SLOT

FFN_ROWS = 1024
FFN_F_SPLITS = (0, 6 * MXU_DEPTH, D_FF)
assert all((hi - lo) % MXU_DEPTH == 0 for lo, hi in zip(FFN_F_SPLITS[:-1], FFN_F_SPLITS[1:]))
PROJ_ROWS = 1024
ATT_TILE = 256
ATT_UNROLL_FAR = 36
ATT_UNROLL_NEAR = 22
ATT_FINISH_GROUP = 4


def _dot(a, b):
    return jnp.dot(a, b, preferred_element_type=F32)


def _dot_nt(a, b):
    return lax.dot_general(a, b, (((1,), (1,)), ((), ())), preferred_element_type=F32)


def _dot_tn(a, b):
    return lax.dot_general(a, b, (((0,), (0,)), ((), ())), preferred_element_type=F32)


def _split_bf16(x):
    hi = x.astype(BF16)
    lo = (x - hi.astype(F32)).astype(BF16)
    return hi, lo


def _rms(x, g):
    return x * lax.rsqrt(jnp.mean(x * x, axis=-1, keepdims=True) + EPS) * g


def _sigmoid(x):
    return 1.0 / (1.0 + jnp.exp(-x))


def _ffn_kernel(*refs, final, mix, cast):
    refs = list(refs)
    x_ref, g_ref, wg_ref, wu_ref, wd_ref, fg_ref = refs[:6]
    mix_refs = refs[6:12] if mix else ()
    n_in = 6 + len(mix_refs)
    src_refs = refs[n_in:n_in + 3] if cast else ()
    o_ref = refs[n_in + len(src_refs)]
    dst_refs = refs[n_in + len(src_refs) + 1:]
    for src, dst in zip(src_refs, dst_refs):
        dst[...] = src[...].astype(BF16)

    x = x_ref[...]
    if mix:
        gla_ref, diff_ref, conv_ref, w1_ref, w2_ref, w3_ref = mix_refs
        x = (x + _dot(gla_ref[...], w1_ref[...]) + _dot(diff_ref[...], w2_ref[...])
             + _dot(conv_ref[...], w3_ref[...]))
    h = _rms(x, g_ref[...]).astype(BF16)
    acc = None
    for lo, hi in zip(FFN_F_SPLITS[:-1], FFN_F_SPLITS[1:]):
        sl = slice(lo, hi)
        gate = _dot(h, wg_ref[:, sl])
        up = _dot(h, wu_ref[:, sl])
        a = (gate * _sigmoid(gate) * up).astype(BF16)
        part = _dot(a, wd_ref[sl, :])
        acc = part if acc is None else acc + part
    y = x + 0.5 * acc
    if final:
        y = _rms(y, fg_ref[...])
    o_ref[...] = y


def _const_spec(shape):
    nd = len(shape)
    return pl.BlockSpec(shape, lambda *_: (0,) * nd, pipeline_mode=pl.Buffered(1))


def _cast_block(n_rows, steps):
    share = 1
    while (n_rows * share) % (steps * BF16_SUBLANES) or steps % share:
        share *= 2
        assert share <= steps
    return n_rows * share // steps, share


def _ffn(x, g, wg, wu, wd, final_g, final, mix=None, cast=None):
    t = x.shape[0]
    tm = min(FFN_ROWS, t)
    steps = t // tm
    row = lambda width: pl.BlockSpec((tm, width), lambda i: (i, 0))
    operands = [x, g, wg, wu, wd, final_g]
    in_specs = [row(D_MODEL), _const_spec((1, D_MODEL)), _const_spec((D_MODEL, D_FF)),
                _const_spec((D_MODEL, D_FF)), _const_spec((D_FF, D_MODEL)), _const_spec((1, D_MODEL))]
    out_shapes = [jax.ShapeDtypeStruct((t, D_MODEL), F32)]
    out_specs = [row(D_MODEL)]
    if mix is not None:
        operands += list(mix)
        in_specs += [row(a.shape[1]) for a in mix[:3]] + [_const_spec(w.shape) for w in mix[3:]]
    if cast is not None:
        stacked, layer = cast
        for w in stacked:
            rows, share = _cast_block(w.shape[1], steps)
            operands.append(w)
            in_specs.append(pl.BlockSpec((None, rows, w.shape[2]),
                                         lambda i, share=share: (layer, i // share, 0)))
            out_shapes.append(jax.ShapeDtypeStruct(w.shape[1:], BF16))
            out_specs.append(pl.BlockSpec((rows, w.shape[2]), lambda i, share=share: (i // share, 0)))
    outs = pl.pallas_call(
        functools.partial(_ffn_kernel, final=final, mix=mix is not None, cast=cast is not None),
        out_shape=tuple(out_shapes),
        grid=(steps,),
        in_specs=in_specs,
        out_specs=tuple(out_specs),
        compiler_params=pltpu.CompilerParams(
            dimension_semantics=("arbitrary",), vmem_limit_bytes=VMEM_LIMIT_BYTES),
        name="ffn",
    )(*operands)
    return outs[0], tuple(outs[1:])


def _gla_cum_matrix():
    t = np.arange(CHUNK)[:, None]
    u = np.arange(CHUNK)[None, :]
    tri = (u <= t)
    ref = (u <= (t // SUB) * SUB - 1)
    ones = np.ones((CHUNK, CHUNK), bool)
    parts = [tri, ref, ones]
    for j in range(1, N_SUB):
        s = np.arange(SUB * j)[:, None]
        parts.append((u > s) & (u <= SUB * j - 1))
    return np.concatenate(parts, axis=0).astype(np.float32)


def _gla_score_mask():
    t = np.arange(CHUNK)[:, None]
    group = np.concatenate([np.full(SUB * j, j) for j in range(1, N_SUB)])
    keep = (group[None, :] == (t // SUB))
    return np.tile(keep, (1, GLA_HEADS)).astype(np.float32)


def _gla_head_ids():
    qk_head = lax.broadcasted_iota(jnp.int32, (1, GLA_QK_LANES), 1) // HEAD_SLOT
    v_head = lax.broadcasted_iota(jnp.int32, (1, GLA_WIDTH), 1) // GLA_DV
    row_head_v = lax.broadcasted_iota(jnp.int32, (GLA_WIDTH, 1), 0) // GLA_DV
    return qk_head, v_head, row_head_v == qk_head


def _gla_fast(q, k, g, v, state, tri):
    rows = q.shape[0]
    n_chunks = rows // CHUNK
    qk_head, v_head, state_mask = _gla_head_ids()
    chunk = lambda x, c: x[c * CHUNK:(c + 1) * CHUNK]
    sel_rows = pl.cdiv(n_chunks, BF16_SUBLANES) * BF16_SUBLANES
    sel_r = lax.broadcasted_iota(jnp.int32, (sel_rows, rows), 0)
    sel_c = lax.broadcasted_iota(jnp.int32, (sel_rows, rows), 1) // CHUNK
    chunk_sel = jnp.where(sel_r == sel_c, 1.0, 0.0).astype(BF16)
    g_hi, g_lo = _split_bf16(g)
    totals = _dot(chunk_sel, g_hi) + _dot(chunk_sel, g_lo)
    b = jnp.concatenate([_dot(tri, chunk(g_hi, c)) + _dot(tri, chunk(g_lo, c))
                         for c in range(n_chunks)], axis=0)
    b_last = jnp.concatenate([jnp.broadcast_to(totals[c:c + 1], (CHUNK, GLA_QK_LANES))
                              for c in range(n_chunks)], axis=0)
    qd = (q * jnp.exp(b)).astype(BF16)
    kd = (k * jnp.exp(-b)).astype(BF16)
    kw = (k * jnp.exp(b_last - b)).astype(BF16)
    t_idx = lax.broadcasted_iota(jnp.int32, (CHUNK, GLA_QK_LANES), 0)
    s_idx = lax.broadcasted_iota(jnp.int32, (CHUNK, GLA_QK_LANES), 1) % HEAD_SLOT
    causal = s_idx <= t_idx
    intra, update = [], []
    for c in range(n_chunks):
        kd_c, v_c = chunk(kd, c), chunk(v, c)
        kd_bd = jnp.concatenate(
            [jnp.where(qk_head == hd, kd_c, jnp.zeros_like(kd_c)) for hd in range(GLA_HEADS)],
            axis=0)
        a = _dot_nt(chunk(qd, c), kd_bd)
        a = jnp.where(causal, a, 0.0).astype(BF16)
        v_bd = jnp.concatenate(
            [jnp.where(v_head == hd, v_c, jnp.zeros_like(v_c)) for hd in range(GLA_HEADS)],
            axis=0)
        intra.append(_dot(a, v_bd))
        update.append(_dot_tn(v_c, chunk(kw, c)))
    raw = []
    for c in range(n_chunks):
        raw.append(intra[c] + _dot_nt(chunk(qd, c), state.astype(BF16)))
        state = state * jnp.exp(totals[c:c + 1]) + jnp.where(state_mask, update[c], 0.0)
    return jnp.concatenate(raw, axis=0), state


def _gla_general(q_ref, k_ref, g_ref, v_ref, s_ref, raw_ref, cum_ref, amask_ref):
    n_chunks = q_ref.shape[0] // CHUNK
    qk_head, v_head, state_mask = _gla_head_ids()
    cum = cum_ref[...]
    amask = amask_ref[...] > 0.5
    ind_r = lax.broadcasted_iota(jnp.int32, (GLA_QK_LANES, LANES), 0) // HEAD_SLOT
    ind_c = lax.broadcasted_iota(jnp.int32, (GLA_QK_LANES, LANES), 1)
    ind_sum = jnp.where(ind_r == ind_c, 1.0, 0.0).astype(BF16)
    bc_r = lax.broadcasted_iota(jnp.int32, (LANES, GLA_WIDTH), 0)
    bc_c = lax.broadcasted_iota(jnp.int32, (LANES, GLA_WIDTH), 1) // GLA_DV
    ind_bcast = jnp.where(bc_r == bc_c, 1.0, 0.0).astype(BF16)
    row_in_sub = lax.broadcasted_iota(jnp.int32, (CHUNK, 1), 0) % SUB

    def chunk_body(c, carry):
        r0 = pl.multiple_of(c * CHUNK, CHUNK)
        q = q_ref[pl.ds(r0, CHUNK), :]
        k = k_ref[pl.ds(r0, CHUNK), :]
        g = g_ref[pl.ds(r0, CHUNK), :]
        v = v_ref[pl.ds(r0, CHUNK), :]
        g_hi, g_lo = _split_bf16(g)
        cums = _dot(cum, g_hi) + _dot(cum, g_lo)
        b = cums[0:CHUNK]
        b_ref = cums[CHUNK:2 * CHUNK]
        b_last = cums[2 * CHUNK:3 * CHUNK]
        e_kst = cums[3 * CHUNK:3 * CHUNK + KST_ROWS]

        state = s_ref[...]
        q_in = (q * jnp.exp(b)).astype(BF16)
        o = _dot_nt(q_in, state.astype(BF16))

        kw = (k * jnp.exp(b_last - b)).astype(BF16)
        decay = jnp.exp(jnp.concatenate([b_last] * (GLA_WIDTH // CHUNK), axis=0))
        s_ref[...] = state * decay + jnp.where(state_mask, _dot_tn(v, kw), 0.0)

        q_sub = (q * jnp.exp(jnp.minimum(b - b_ref, 0.0))).astype(BF16)
        k_st = jnp.concatenate([k[0:SUB * j] for j in range(1, N_SUB)], axis=0) * jnp.exp(e_kst)
        k_bd = jnp.concatenate(
            [jnp.where(qk_head == hd, k_st, 0.0) for hd in range(GLA_HEADS)], axis=0).astype(BF16)
        a = _dot_nt(q_sub, k_bd)
        a = jnp.where(amask, a, 0.0).astype(BF16)
        v_st = jnp.concatenate([v[0:SUB * j] for j in range(1, N_SUB)], axis=0)
        v_bd = jnp.concatenate(
            [jnp.where(v_head == hd, v_st, jnp.zeros_like(v_st)) for hd in range(GLA_HEADS)], axis=0)
        o = o + _dot(a, v_bd)

        vf = v.astype(F32)
        prods = []
        for d in range(SUB):
            ks = k if d == 0 else pltpu.roll(k, d, 0)
            bs = b if d == 0 else pltpu.roll(b, d, 0)
            prods.append((q * ks * jnp.exp(jnp.minimum(b - bs, 0.0))).astype(BF16))
        dsum = _dot(jnp.concatenate(prods, axis=0), ind_sum)
        valid = jnp.concatenate([row_in_sub >= d for d in range(SUB)], axis=0)
        dsum = jnp.where(valid, dsum, 0.0).astype(BF16)
        dbc = _dot(dsum, ind_bcast)
        for d in range(SUB):
            vs = vf if d == 0 else pltpu.roll(vf, d, 0)
            o = o + dbc[d * CHUNK:(d + 1) * CHUNK] * vs
        raw_ref[pl.ds(r0, CHUNK), :] = o
        return carry

    lax.fori_loop(0, n_chunks, chunk_body, 0)


def _gla_finish(o, gate, gain):
    hr = lax.broadcasted_iota(jnp.int32, (GLA_WIDTH, GLA_WIDTH), 0) // GLA_DV
    hc = lax.broadcasted_iota(jnp.int32, (GLA_WIDTH, GLA_WIDTH), 1) // GLA_DV
    head_ones = jnp.where(hr == hc, 1.0, 0.0).astype(BF16)
    ms = _dot((o * o).astype(BF16), head_ones) * (1.0 / GLA_DV)
    return (o * lax.rsqrt(ms + EPS) * gain * (gate * _sigmoid(gate))).astype(BF16)


def _proj_kernel(x_ref, g_ref, w_ref, gkup_ref, gkb_ref, cw_ref, cum_ref, tri_ref, amask_ref, gn_ref,
                 gla_ref, dq_ref, dk_ref, dv_ref, conv_ref,
                 zbuf_ref, q_s, k_s, g_s, v_s, s_ref, s_next_ref, raw_ref, *, tiles_per_seq):
    tm = x_ref.shape[0]

    @pl.when(pl.program_id(0) % tiles_per_seq == 0)
    def _():
        zbuf_ref[0:F32_SUBLANES, :] = jnp.zeros((F32_SUBLANES, CONV_DIM), F32)
        s_ref[...] = jnp.zeros_like(s_ref)

    h = _rms(x_ref[...], g_ref[...]).astype(BF16)

    def proj(off, width):
        return _dot(h, w_ref[:, off:off + width])

    qk = proj(OFF_GQ, 2 * GLA_QK_LANES)
    gq = qk[:, 0:GLA_QK_LANES]
    gk = qk[:, GLA_QK_LANES:]
    vg = proj(OFF_GV, 2 * GLA_WIDTH)
    gv = vg[:, 0:GLA_WIDTH].astype(BF16)
    gate = vg[:, GLA_WIDTH:]

    glr_hi, glr_lo = _split_bf16(gq)
    up_hi = gkup_ref[0]
    up_lo = gkup_ref[1]
    z = _dot(glr_hi, up_hi) + _dot(glr_lo, up_hi) + _dot(glr_hi, up_lo) + gkb_ref[...]
    logsig = jnp.minimum(z, 0.0) - jnp.log1p(jnp.exp(-jnp.abs(z)))
    lane = lax.broadcasted_iota(jnp.int32, (tm, GLA_QK_LANES), 1)
    gg = jnp.where(lane % HEAD_SLOT < GLA_DK, logsig * (1.0 / GLA_GATE_NORM), 0.0)

    dqk = proj(OFF_DQ, 2 * DIFF_LANES)
    dq_ref[...] = (dqk[:, 0:DIFF_LANES] * (DIFF_DQK ** -0.5 * LOG2E)).astype(BF16)
    dk_ref[...] = dqk[:, DIFF_LANES:].astype(BF16)
    lane_v = lax.broadcasted_iota(jnp.int32, (tm, DIFF_LANES), 1)
    dv_ref[...] = jnp.where(lane_v % LANES == DIFF_DV, 1.0, proj(OFF_DV, DIFF_LANES)).astype(BF16)

    conv = proj(OFF_CB, 3 * CONV_DIM)
    zc = conv[:, CONV_DIM:2 * CONV_DIM] * conv[:, 2 * CONV_DIM:]
    zbuf_ref[F32_SUBLANES:F32_SUBLANES + tm, :] = zc
    cw = cw_ref[...]
    y = (cw[2:3, :] * zc + cw[1:2, :] * zbuf_ref[F32_SUBLANES - 1:F32_SUBLANES - 1 + tm, :]
         + cw[0:1, :] * zbuf_ref[F32_SUBLANES - 2:F32_SUBLANES - 2 + tm, :])
    conv_ref[...] = (conv[:, 0:CONV_DIM] * y).astype(BF16)
    zbuf_ref[0:F32_SUBLANES, :] = zbuf_ref[tm:tm + F32_SUBLANES, :]

    q_scaled = gq * (GLA_DK ** -0.5)
    raw, new_state = _gla_fast(q_scaled, gk, gg, gv, s_ref[...], tri_ref[...])
    gla_ref[...] = _gla_finish(raw, gate, gn_ref[...])
    s_next_ref[...] = new_state
    q_s[...] = q_scaled
    k_s[...] = gk
    g_s[...] = gg
    v_s[...] = gv
    raw_ref[...] = gate
    safe = jnp.min(gg) * CHUNK >= -GLA_SAFE_DECAY

    @pl.when(safe)
    def _():
        s_ref[...] = s_next_ref[...]

    @pl.when(jnp.logical_not(safe))
    def _():
        gate_kept = raw_ref[...]
        _gla_general(q_s, k_s, g_s, v_s, s_ref, raw_ref, cum_ref, amask_ref)
        gla_ref[...] = _gla_finish(raw_ref[...], gate_kept, gn_ref[...])


def _proj(x, g, w, gkup, gkb, cw, gn_tiled, seq):
    t = x.shape[0]
    tm = min(PROJ_ROWS, seq)
    row = lambda width: pl.BlockSpec((tm, width), lambda i: (i, 0))
    cum_np = _gla_cum_matrix()
    cum = jnp.asarray(cum_np, BF16)
    tri = jnp.asarray(cum_np[0:CHUNK], BF16)
    amask = jnp.asarray(_gla_score_mask(), F32)
    out_shapes = (
        jax.ShapeDtypeStruct((t, GLA_WIDTH), BF16),
        jax.ShapeDtypeStruct((t, DIFF_LANES), BF16),
        jax.ShapeDtypeStruct((t, DIFF_LANES), BF16),
        jax.ShapeDtypeStruct((t, DIFF_LANES), BF16),
        jax.ShapeDtypeStruct((t, CONV_DIM), BF16),
    )
    return pl.pallas_call(
        functools.partial(_proj_kernel, tiles_per_seq=seq // tm),
        out_shape=out_shapes,
        grid=(t // tm,),
        in_specs=[
            row(D_MODEL),
            _const_spec((1, D_MODEL)),
            _const_spec((D_MODEL, N_PROJ)),
            _const_spec((2, GLA_QK_LANES, GLA_QK_LANES)),
            _const_spec((1, GLA_QK_LANES)),
            _const_spec((F32_SUBLANES, CONV_DIM)),
            _const_spec(cum.shape),
            _const_spec(tri.shape),
            _const_spec(amask.shape),
            _const_spec((1, GLA_WIDTH)),
        ],
        out_specs=tuple(row(s.shape[1]) for s in out_shapes),
        scratch_shapes=[pltpu.VMEM((tm + F32_SUBLANES, CONV_DIM), F32),
                        pltpu.VMEM((tm, GLA_QK_LANES), F32),
                        pltpu.VMEM((tm, GLA_QK_LANES), F32),
                        pltpu.VMEM((tm, GLA_QK_LANES), F32),
                        pltpu.VMEM((tm, GLA_WIDTH), BF16),
                        pltpu.VMEM((GLA_WIDTH, GLA_QK_LANES), F32),
                        pltpu.VMEM((GLA_WIDTH, GLA_QK_LANES), F32),
                        pltpu.VMEM((tm, GLA_WIDTH), F32)],
        compiler_params=pltpu.CompilerParams(
            dimension_semantics=("arbitrary",), vmem_limit_bytes=VMEM_LIMIT_BYTES),
        name="mixer_proj_gla",
    )(x, g, w, gkup, gkb, cw, cum, tri, amask, gn_tiled)


def _t5_bucket(rel):
    nb = NUM_BUCKETS // 2
    max_exact = nb // 2
    ret = (rel > 0).astype(jnp.int32) * nb
    n = jnp.abs(rel)
    nf = jnp.maximum(n, 1).astype(jnp.float32)
    large = max_exact + (jnp.log(nf / max_exact) / math.log(MAX_DISTANCE / max_exact)
                         * (nb - max_exact)).astype(jnp.int32)
    large = jnp.minimum(large, nb - 1)
    return ret + jnp.where(n < max_exact, n, large)


def _bias_tiles(rel_bias, tile):
    assert tile >= MAX_DISTANCE and tile % CHUNK == 0
    table = rel_bias.astype(F32)
    far = table[NUM_BUCKETS // 2 - 1]
    heads = table.shape[1]
    rel = jnp.concatenate([jnp.arange(0, tile), jnp.arange(-2 * tile, 0)])
    onehot = _t5_bucket(rel)[None, :, None] == jnp.arange(NUM_BUCKETS)[None, None, :]
    per_rel = jnp.sum(jnp.where(onehot, table.T[:, None, :], 0.0), axis=-1)
    per_rel = (per_rel - far[:, None]) * LOG2E
    span = 3 * tile
    flat = jnp.tile(per_rel, (1, 2 * tile))[:, :2 * tile * (span - 1)]
    toep = flat.reshape(heads, 2 * tile, span - 1)[:, :, :tile]
    r = jnp.arange(tile)[:, None]
    c = jnp.arange(tile)[None, :]
    diag = jnp.where(((c // CHUNK) <= (r // CHUNK))[None], toep[:, :tile], MASK_VALUE)
    left = toep[:, tile:]
    return jnp.stack([left, diag], axis=1)


def _attn_kernel(q_ref, k_ref, v_ref, bias_ref, lamv_ref, subln_ref, o_ref,
                 s_buf, p_buf, alpha_buf, m_all, acc_all, *, tile, lam_init):
    seq = q_ref.shape[0]
    nq = seq // tile

    s_buf[...] = jnp.zeros(s_buf.shape, F32)
    p_buf[...] = jnp.zeros(p_buf.shape, BF16)
    alpha_buf[...] = jnp.zeros(alpha_buf.shape, F32)
    m_all[...] = jnp.full(m_all.shape, MASK_VALUE, F32)
    acc_all[...] = jnp.zeros(acc_all.shape, F32)

    lane = lax.broadcasted_iota(jnp.int32, (tile, LANES), 1)

    def stage1(i, j, slot):
        q = q_ref[pl.ds(pl.multiple_of(i * tile, tile), tile), :]
        zero = jnp.zeros_like(q)
        qs = jnp.concatenate([jnp.where(lane < LANES // 2, q, zero),
                              jnp.where(lane >= LANES // 2, q, zero)], axis=0)
        kj = k_ref[pl.ds(pl.multiple_of(j * tile, tile), tile), :]
        s_buf[slot] = _dot_nt(qs, kj)

    def stage2(i, bias, slot):
        for half in range(2):
            rows = slice(half * tile, (half + 1) * tile)
            s = s_buf[slot, rows, :]
            if bias is not None:
                s = s + bias
            m_prev = m_all[i, rows, :]
            m_next = jnp.maximum(m_prev, jnp.max(s, axis=-1, keepdims=True))
            p = jnp.exp2(s - jnp.concatenate([m_next] * (tile // LANES), axis=1))
            p_buf[slot, rows, :] = p.astype(BF16)
            alpha_buf[slot, rows, :] = jnp.exp2(m_prev - m_next)
            m_all[i, rows, :] = m_next

    def stage3(i, j, slot):
        vj = v_ref[pl.ds(pl.multiple_of(j * tile, tile), tile), :]
        acc_all[i] = alpha_buf[slot] * acc_all[i] + _dot(p_buf[slot], vj)

    def run_pipeline(n_pairs, first_pair, advance, with_bias, unroll):
        n_steps = pl.cdiv(n_pairs + 2, unroll) * unroll

        def step(t, pairs, slot_a, slot_b):
            (i0, j0), (i1, j1), (i2, j2) = pairs
            valid1 = jnp.logical_and(t >= 1, t <= n_pairs)
            valid2 = jnp.logical_and(t >= 2, t <= n_pairs + 1)
            stage3(jnp.where(valid2, i2, nq), j2, slot_a)
            stage2(jnp.where(valid1, i1, nq), bias_ref[0, j1 - i1 + 1] if with_bias else None, slot_b)
            stage1(i0, j0, slot_a)
            return (advance(i0, j0), (i0, j0), (i1, j1))

        def body(u, pairs):
            for r in range(unroll):
                pairs = step(unroll * u + r, pairs, r % 2, (r + 1) % 2)
            return pairs

        lax.fori_loop(0, n_steps // unroll, body, (first_pair,) * 3)

    def next_far(i, j):
        wrap = j == i - 2
        done = jnp.logical_and(wrap, i == nq - 1)
        step_i = jnp.logical_and(wrap, jnp.logical_not(done))
        return (jnp.where(step_i, i + 1, i), jnp.where(done, j, jnp.where(wrap, 0, j + 1)))

    def next_near(i, j):
        wrap = j == i
        done = jnp.logical_and(wrap, i == nq - 1)
        step_i = jnp.logical_and(wrap, jnp.logical_not(done))
        return (jnp.where(step_i, i + 1, i), jnp.where(jnp.logical_or(done, wrap), j, j + 1))

    zero = jnp.int32(0)
    if nq > 2:
        run_pipeline((nq - 1) * (nq - 2) // 2, (jnp.int32(2), zero), next_far, False, ATT_UNROLL_FAR)
    run_pipeline(2 * nq - 1, (zero, zero), next_near, True, ATT_UNROLL_NEAR)

    lv = lamv_ref[...]
    lam = (jnp.exp(jnp.sum(lv[0:1] * lv[1:2], axis=-1, keepdims=True))
           - jnp.exp(jnp.sum(lv[2:3] * lv[3:4], axis=-1, keepdims=True)) + lam_init)

    def finish(i):
        acc = acc_all[i]
        a1 = acc[0:tile]
        a2 = acc[tile:2 * tile]
        ones_col = lane == DIFF_DV
        ratio = jnp.sum(jnp.where(ones_col, a1 / jnp.where(ones_col, a2, 1.0), 0.0),
                        axis=-1, keepdims=True)
        u = a1 - lam * (ratio * a2)
        sq = jnp.where(ones_col, a1 * math.sqrt(DIFF_DV * EPS), u)
        ms = jnp.sum(sq * sq, axis=-1, keepdims=True) * (1.0 / DIFF_DV)
        y = u * lax.rsqrt(ms) * subln_ref[...] * (1.0 - lam_init)
        o_ref[pl.ds(pl.multiple_of(i * tile, tile), tile), :] = y.astype(BF16)

    group = math.gcd(nq, ATT_FINISH_GROUP)

    def finish_group(u, carry):
        for r in range(group):
            finish(u * group + r)
        return carry

    lax.fori_loop(0, nq // group, finish_group, 0)


def _attn(dq, dk, dv, bias_tiles, lamv, subln, batch, seq, lam_init):
    tile = min(ATT_TILE, seq)
    nq = seq // tile
    head_block = pl.BlockSpec((seq, LANES), lambda b, h: (b, h))
    return pl.pallas_call(
        functools.partial(_attn_kernel, tile=tile, lam_init=lam_init),
        out_shape=jax.ShapeDtypeStruct((batch * seq, DIFF_LANES), BF16),
        grid=(batch, DIFF_HEADS),
        in_specs=[
            head_block, head_block, head_block,
            pl.BlockSpec((1, 2, tile, tile), lambda b, h: (h, 0, 0, 0)),
            pl.BlockSpec((4, LANES), lambda b, h: (0, 0)),
            pl.BlockSpec((1, LANES), lambda b, h: (0, 0)),
        ],
        out_specs=head_block,
        scratch_shapes=[pltpu.VMEM((2, 2 * tile, tile), F32),
                        pltpu.VMEM((2, 2 * tile, tile), BF16),
                        pltpu.VMEM((2, 2 * tile, LANES), F32),
                        pltpu.VMEM((nq + 1, 2 * tile, LANES), F32),
                        pltpu.VMEM((nq + 1, 2 * tile, LANES), F32)],
        compiler_params=pltpu.CompilerParams(
            dimension_semantics=("parallel", "parallel"), vmem_limit_bytes=VMEM_LIMIT_BYTES),
        name="diff_attn",
    )(dq, dk, dv, bias_tiles, lamv, subln)


def _pad_cols(w, width):
    return jnp.pad(w, ((0, 0), (0, width - w.shape[1])))


def _slot_cols(w, heads, used, slot):
    lead = w.shape[0]
    w = w.reshape(lead, heads, used)
    return jnp.pad(w, ((0, 0), (0, 0), (0, slot - used))).reshape(lead, heads * slot)


def _layout_w_in(w_in):
    offs = np.cumsum([0, GLA_HEADS * GLA_DK, GLA_HEADS * GLA_DK, GLA_WIDTH, GLA_RANK, GLA_WIDTH,
                      DIFF_HEADS * 2 * DIFF_DQK, DIFF_HEADS * 2 * DIFF_DQK, DIFF_WIDTH,
                      CONV_DIM, CONV_DIM, CONV_DIM])
    gq, gk, gv, glr, gate, dq, dk, dv, cb, cc, ch = [w_in[:, offs[n]:offs[n + 1]] for n in range(11)]
    half = LANES // 2
    gq_slots = _slot_cols(gq, GLA_HEADS, GLA_DK, HEAD_SLOT)
    gq_slots = gq_slots.at[:, GLR_LANE:GLR_LANE + GLA_RANK].set(glr)
    cols = [
        gq_slots,
        _slot_cols(gk, GLA_HEADS, GLA_DK, HEAD_SLOT),
        gv,
        gate,
        _slot_cols(dq, 2 * DIFF_HEADS, DIFF_DQK, half),
        _slot_cols(dk, 2 * DIFF_HEADS, DIFF_DQK, half),
        _slot_cols(dv, DIFF_HEADS, DIFF_DV, LANES),
        cb, cc, ch,
    ]
    w = jnp.concatenate(cols, axis=1)
    assert w.shape[1] == N_PROJ
    return w.astype(BF16)


def kernel(x, ffn1_norm, ffn1_gate, ffn1_up, ffn1_down, mix_norm, w_in, gla_gk_up, gla_gk_bias,
           gla_norm, diff_lambda_q1, diff_lambda_k1, diff_lambda_q2, diff_lambda_k2, diff_subln,
           rel_bias, conv_w, w_out, ffn2_norm, ffn2_gate, ffn2_up, ffn2_down, final_norm):
    batch, seq, _ = x.shape
    depth = w_in.shape[0]
    t = batch * seq
    xf = x.reshape(t, D_MODEL)
    bias_tiles = _bias_tiles(rel_bias, min(ATT_TILE, seq))
    final_g = final_norm.reshape(1, D_MODEL)
    ffn1_w = (ffn1_gate[0].astype(BF16), ffn1_up[0].astype(BF16), ffn1_down[0].astype(BF16))

    for l in range(depth):
        lam_init = 0.8 - 0.6 * math.exp(-0.3 * l)
        xf, ffn2_w = _ffn(xf, ffn1_norm[l].reshape(1, D_MODEL), *ffn1_w, final_g, False,
                          cast=((ffn2_gate, ffn2_up, ffn2_down), l))

        w_pad = _layout_w_in(w_in[l])
        up = _slot_cols(gla_gk_up[l], GLA_HEADS, GLA_DK, HEAD_SLOT)
        up = jnp.pad(up, ((GLR_LANE, GLA_QK_LANES - GLR_LANE - GLA_RANK), (0, 0)))
        up_hi = up.astype(BF16)
        up_lo = (up - up_hi.astype(F32)).astype(BF16)
        gkup = jnp.stack([up_hi, up_lo], axis=0)
        gkb = _slot_cols(gla_gk_bias[l].reshape(1, -1), GLA_HEADS, GLA_DK, HEAD_SLOT)
        cw = jnp.pad(conv_w[l], ((0, F32_SUBLANES - CONV_WIDTH), (0, 0)))
        gn_tiled = jnp.tile(gla_norm[l], GLA_HEADS).reshape(1, GLA_WIDTH)
        gla_o, dq, dk, dv, conv_o = _proj(
            xf, mix_norm[l].reshape(1, D_MODEL), w_pad, gkup, gkb, cw, gn_tiled, seq)

        lamv = jnp.stack([diff_lambda_q1[l], diff_lambda_k1[l], diff_lambda_q2[l], diff_lambda_k2[l]])
        lamv = _pad_cols(lamv.astype(F32), LANES)
        subln = _pad_cols(diff_subln[l].reshape(1, DIFF_DV), LANES)
        diff_o = _attn(dq, dk, dv, bias_tiles, lamv, subln, batch, seq, lam_init)

        wo = w_out[l]
        w1 = wo[0:GLA_WIDTH].astype(BF16)
        w2 = wo[GLA_WIDTH:GLA_WIDTH + DIFF_WIDTH].reshape(DIFF_HEADS, DIFF_DV, D_MODEL)
        w2 = jnp.pad(w2, ((0, 0), (0, LANES - DIFF_DV), (0, 0))).reshape(DIFF_LANES, D_MODEL).astype(BF16)
        w3 = wo[GLA_WIDTH + DIFF_WIDTH:].astype(BF16)
        last = l == depth - 1
        xf, ffn1_w = _ffn(xf, ffn2_norm[l].reshape(1, D_MODEL), *ffn2_w, final_g, last,
                          mix=(gla_o, diff_o, conv_o, w1, w2, w3),
                          cast=None if last else ((ffn1_gate, ffn1_up, ffn1_down), l + 1))

    return xf.reshape(batch, seq, D_MODEL)
```

```python
import functools
import math

import numpy as np
import jax
import jax.numpy as jnp
from jax import lax
from jax.experimental import pallas as pl
from jax.experimental.pallas import tpu as pltpu

F32 = jnp.float32
BF16 = jnp.bfloat16

D_MODEL = 1024
D_FF = 2816
EPS = 1e-6
CHUNK = 64
GLA_HEADS = 4
GLA_DK = 48
GLA_DV = 96
GLA_RANK = 16
GLA_GATE_NORM = 16.0
DIFF_HEADS = 4
DIFF_DQK = 48
DIFF_DV = 96
CONV_DIM = 256
CONV_WIDTH = 3
NUM_BUCKETS = 32
MAX_DISTANCE = 128
GLA_WIDTH = GLA_HEADS * GLA_DV
DIFF_WIDTH = DIFF_HEADS * DIFF_DV

LANES = 128
F32_SUBLANES = 8
BF16_SUBLANES = 16
MXU_DEPTH = 256
VMEM_LIMIT_BYTES = 62 * 1024 * 1024

HEAD_SLOT = 64
GLA_QK_LANES = GLA_HEADS * HEAD_SLOT
SUB = 16
N_SUB = CHUNK // SUB
KST_ROWS = SUB * (N_SUB * (N_SUB - 1) // 2)
DIFF_LANES = DIFF_HEADS * LANES
MASK_VALUE = -1e30
GLA_SAFE_DECAY = 60.0
LOG2E = math.log2(math.e)

OFF_GQ = 0
OFF_GK = OFF_GQ + GLA_QK_LANES
OFF_GV = OFF_GK + GLA_QK_LANES
OFF_GATE = OFF_GV + GLA_WIDTH
OFF_DQ = OFF_GATE + GLA_WIDTH
OFF_DK = OFF_DQ + DIFF_LANES
OFF_DV = OFF_DK + DIFF_LANES
OFF_CB = OFF_DV + DIFF_LANES
OFF_CC = OFF_CB + CONV_DIM
OFF_CH = OFF_CC + CONV_DIM
N_PROJ = OFF_CH + CONV_DIM
GLR_LANE = GLA_DK
assert GLR_LANE + GLA_RANK <= HEAD_SLOT

FFN_ROWS = 1024
FFN_F_SPLITS = (0, 6 * MXU_DEPTH, D_FF)
assert all((hi - lo) % MXU_DEPTH == 0 for lo, hi in zip(FFN_F_SPLITS[:-1], FFN_F_SPLITS[1:]))
PROJ_ROWS = 1024
ATT_TILE = 256
ATT_UNROLL_FAR = 36
ATT_UNROLL_NEAR = 22
ATT_FINISH_GROUP = 4


def _dot(a, b):
    return jnp.dot(a, b, preferred_element_type=F32)


def _dot_nt(a, b):
    return lax.dot_general(a, b, (((1,), (1,)), ((), ())), preferred_element_type=F32)


def _dot_tn(a, b):
    return lax.dot_general(a, b, (((0,), (0,)), ((), ())), preferred_element_type=F32)


def _split_bf16(x):
    hi = x.astype(BF16)
    lo = (x - hi.astype(F32)).astype(BF16)
    return hi, lo


def _rms(x, g):
    return x * lax.rsqrt(jnp.mean(x * x, axis=-1, keepdims=True) + EPS) * g


def _sigmoid(x):
    return 1.0 / (1.0 + jnp.exp(-x))


def _ffn_kernel(*refs, final, mix, cast):
    refs = list(refs)
    x_ref, g_ref, wg_ref, wu_ref, wd_ref, fg_ref = refs[:6]
    mix_refs = refs[6:12] if mix else ()
    n_in = 6 + len(mix_refs)
    src_refs = refs[n_in:n_in + 3] if cast else ()
    o_ref = refs[n_in + len(src_refs)]
    dst_refs = refs[n_in + len(src_refs) + 1:]
    for src, dst in zip(src_refs, dst_refs):
        dst[...] = src[...].astype(BF16)

    x = x_ref[...]
    if mix:
        gla_ref, diff_ref, conv_ref, w1_ref, w2_ref, w3_ref = mix_refs
        x = (x + _dot(gla_ref[...], w1_ref[...]) + _dot(diff_ref[...], w2_ref[...])
             + _dot(conv_ref[...], w3_ref[...]))
    h = _rms(x, g_ref[...]).astype(BF16)
    acc = None
    for lo, hi in zip(FFN_F_SPLITS[:-1], FFN_F_SPLITS[1:]):
        sl = slice(lo, hi)
        gate = _dot(h, wg_ref[:, sl])
        up = _dot(h, wu_ref[:, sl])
        a = (gate * _sigmoid(gate) * up).astype(BF16)
        part = _dot(a, wd_ref[sl, :])
        acc = part if acc is None else acc + part
    y = x + 0.5 * acc
    if final:
        y = _rms(y, fg_ref[...])
    o_ref[...] = y


def _const_spec(shape):
    nd = len(shape)
    return pl.BlockSpec(shape, lambda *_: (0,) * nd, pipeline_mode=pl.Buffered(1))


def _layer_spec(stacked, layer):
    nd = stacked.ndim - 1
    return pl.BlockSpec((None,) + tuple(stacked.shape[1:]), lambda *_: (layer,) + (0,) * nd,
                        pipeline_mode=pl.Buffered(1))


def _cast_block(n_rows, steps):
    share = 1
    while (n_rows * share) % (steps * BF16_SUBLANES) or steps % share:
        share *= 2
        assert share <= steps
    return n_rows * share // steps, share


def _ffn(x, gains, layer, wg, wu, wd, final_g, final, mix=None, cast=None):
    t = x.shape[0]
    tm = min(FFN_ROWS, t)
    steps = t // tm
    row = lambda width: pl.BlockSpec((tm, width), lambda i: (i, 0))
    operands = [x, gains, wg, wu, wd, final_g]
    in_specs = [row(D_MODEL), _layer_spec(gains, layer), _const_spec((D_MODEL, D_FF)),
                _const_spec((D_MODEL, D_FF)), _const_spec((D_FF, D_MODEL)), _const_spec((1, D_MODEL))]
    out_shapes = [jax.ShapeDtypeStruct((t, D_MODEL), F32)]
    out_specs = [row(D_MODEL)]
    if mix is not None:
        operands += list(mix)
        in_specs += [row(a.shape[1]) for a in mix[:3]] + [_layer_spec(w, layer) for w in mix[3:]]
    if cast is not None:
        stacked, layer = cast
        for w in stacked:
            rows, share = _cast_block(w.shape[1], steps)
            operands.append(w)
            in_specs.append(pl.BlockSpec((None, rows, w.shape[2]),
                                         lambda i, share=share: (layer, i // share, 0)))
            out_shapes.append(jax.ShapeDtypeStruct(w.shape[1:], BF16))
            out_specs.append(pl.BlockSpec((rows, w.shape[2]), lambda i, share=share: (i // share, 0)))
    outs = pl.pallas_call(
        functools.partial(_ffn_kernel, final=final, mix=mix is not None, cast=cast is not None),
        out_shape=tuple(out_shapes),
        grid=(steps,),
        in_specs=in_specs,
        out_specs=tuple(out_specs),
        compiler_params=pltpu.CompilerParams(
            dimension_semantics=("arbitrary",), vmem_limit_bytes=VMEM_LIMIT_BYTES),
        name="ffn",
    )(*operands)
    return outs[0], tuple(outs[1:])


def _gla_cum_matrix():
    t = np.arange(CHUNK)[:, None]
    u = np.arange(CHUNK)[None, :]
    tri = (u <= t)
    ref = (u <= (t // SUB) * SUB - 1)
    ones = np.ones((CHUNK, CHUNK), bool)
    parts = [tri, ref, ones]
    for j in range(1, N_SUB):
        s = np.arange(SUB * j)[:, None]
        parts.append((u > s) & (u <= SUB * j - 1))
    return np.concatenate(parts, axis=0).astype(np.float32)


def _gla_score_mask():
    t = np.arange(CHUNK)[:, None]
    group = np.concatenate([np.full(SUB * j, j) for j in range(1, N_SUB)])
    keep = (group[None, :] == (t // SUB))
    return np.tile(keep, (1, GLA_HEADS)).astype(np.float32)


def _gla_head_ids():
    qk_head = lax.broadcasted_iota(jnp.int32, (1, GLA_QK_LANES), 1) // HEAD_SLOT
    v_head = lax.broadcasted_iota(jnp.int32, (1, GLA_WIDTH), 1) // GLA_DV
    row_head_v = lax.broadcasted_iota(jnp.int32, (GLA_WIDTH, 1), 0) // GLA_DV
    return qk_head, v_head, row_head_v == qk_head


def _gla_fast(q, k, g, v, state, tri):
    rows = q.shape[0]
    n_chunks = rows // CHUNK
    qk_head, v_head, state_mask = _gla_head_ids()
    chunk = lambda x, c: x[c * CHUNK:(c + 1) * CHUNK]
    sel_rows = pl.cdiv(n_chunks, BF16_SUBLANES) * BF16_SUBLANES
    sel_r = lax.broadcasted_iota(jnp.int32, (sel_rows, rows), 0)
    sel_c = lax.broadcasted_iota(jnp.int32, (sel_rows, rows), 1) // CHUNK
    chunk_sel = jnp.where(sel_r == sel_c, 1.0, 0.0).astype(BF16)
    g_hi, g_lo = _split_bf16(g)
    totals = _dot(chunk_sel, g_hi) + _dot(chunk_sel, g_lo)
    b = jnp.concatenate([_dot(tri, chunk(g_hi, c)) + _dot(tri, chunk(g_lo, c))
                         for c in range(n_chunks)], axis=0)
    b_last = jnp.concatenate([jnp.broadcast_to(totals[c:c + 1], (CHUNK, GLA_QK_LANES))
                              for c in range(n_chunks)], axis=0)
    qd = (q * jnp.exp(b)).astype(BF16)
    kd = (k * jnp.exp(-b)).astype(BF16)
    kw = (k * jnp.exp(b_last - b)).astype(BF16)
    t_idx = lax.broadcasted_iota(jnp.int32, (CHUNK, GLA_QK_LANES), 0)
    s_idx = lax.broadcasted_iota(jnp.int32, (CHUNK, GLA_QK_LANES), 1) % HEAD_SLOT
    causal = s_idx <= t_idx
    intra, update = [], []
    for c in range(n_chunks):
        kd_c, v_c = chunk(kd, c), chunk(v, c)
        kd_bd = jnp.concatenate(
            [jnp.where(qk_head == hd, kd_c, jnp.zeros_like(kd_c)) for hd in range(GLA_HEADS)],
            axis=0)
        a = _dot_nt(chunk(qd, c), kd_bd)
        a = jnp.where(causal, a, 0.0).astype(BF16)
        v_bd = jnp.concatenate(
            [jnp.where(v_head == hd, v_c, jnp.zeros_like(v_c)) for hd in range(GLA_HEADS)],
            axis=0)
        intra.append(_dot(a, v_bd))
        update.append(_dot_tn(v_c, chunk(kw, c)))
    raw = []
    for c in range(n_chunks):
        raw.append(intra[c] + _dot_nt(chunk(qd, c), state.astype(BF16)))
        state = state * jnp.exp(totals[c:c + 1]) + jnp.where(state_mask, update[c], 0.0)
    return jnp.concatenate(raw, axis=0), state


def _gla_general(q_ref, k_ref, g_ref, v_ref, s_ref, raw_ref, cum_ref, amask_ref):
    n_chunks = q_ref.shape[0] // CHUNK
    qk_head, v_head, state_mask = _gla_head_ids()
    cum = cum_ref[...]
    amask = amask_ref[...] > 0.5
    ind_r = lax.broadcasted_iota(jnp.int32, (GLA_QK_LANES, LANES), 0) // HEAD_SLOT
    ind_c = lax.broadcasted_iota(jnp.int32, (GLA_QK_LANES, LANES), 1)
    ind_sum = jnp.where(ind_r == ind_c, 1.0, 0.0).astype(BF16)
    bc_r = lax.broadcasted_iota(jnp.int32, (LANES, GLA_WIDTH), 0)
    bc_c = lax.broadcasted_iota(jnp.int32, (LANES, GLA_WIDTH), 1) // GLA_DV
    ind_bcast = jnp.where(bc_r == bc_c, 1.0, 0.0).astype(BF16)
    row_in_sub = lax.broadcasted_iota(jnp.int32, (CHUNK, 1), 0) % SUB

    def chunk_body(c, carry):
        r0 = pl.multiple_of(c * CHUNK, CHUNK)
        q = q_ref[pl.ds(r0, CHUNK), :]
        k = k_ref[pl.ds(r0, CHUNK), :]
        g = g_ref[pl.ds(r0, CHUNK), :]
        v = v_ref[pl.ds(r0, CHUNK), :]
        g_hi, g_lo = _split_bf16(g)
        cums = _dot(cum, g_hi) + _dot(cum, g_lo)
        b = cums[0:CHUNK]
        b_ref = cums[CHUNK:2 * CHUNK]
        b_last = cums[2 * CHUNK:3 * CHUNK]
        e_kst = cums[3 * CHUNK:3 * CHUNK + KST_ROWS]

        state = s_ref[...]
        q_in = (q * jnp.exp(b)).astype(BF16)
        o = _dot_nt(q_in, state.astype(BF16))

        kw = (k * jnp.exp(b_last - b)).astype(BF16)
        decay = jnp.exp(jnp.concatenate([b_last] * (GLA_WIDTH // CHUNK), axis=0))
        s_ref[...] = state * decay + jnp.where(state_mask, _dot_tn(v, kw), 0.0)

        q_sub = (q * jnp.exp(jnp.minimum(b - b_ref, 0.0))).astype(BF16)
        k_st = jnp.concatenate([k[0:SUB * j] for j in range(1, N_SUB)], axis=0) * jnp.exp(e_kst)
        k_bd = jnp.concatenate(
            [jnp.where(qk_head == hd, k_st, 0.0) for hd in range(GLA_HEADS)], axis=0).astype(BF16)
        a = _dot_nt(q_sub, k_bd)
        a = jnp.where(amask, a, 0.0).astype(BF16)
        v_st = jnp.concatenate([v[0:SUB * j] for j in range(1, N_SUB)], axis=0)
        v_bd = jnp.concatenate(
            [jnp.where(v_head == hd, v_st, jnp.zeros_like(v_st)) for hd in range(GLA_HEADS)], axis=0)
        o = o + _dot(a, v_bd)

        vf = v.astype(F32)
        prods = []
        for d in range(SUB):
            ks = k if d == 0 else pltpu.roll(k, d, 0)
            bs = b if d == 0 else pltpu.roll(b, d, 0)
            prods.append((q * ks * jnp.exp(jnp.minimum(b - bs, 0.0))).astype(BF16))
        dsum = _dot(jnp.concatenate(prods, axis=0), ind_sum)
        valid = jnp.concatenate([row_in_sub >= d for d in range(SUB)], axis=0)
        dsum = jnp.where(valid, dsum, 0.0).astype(BF16)
        dbc = _dot(dsum, ind_bcast)
        for d in range(SUB):
            vs = vf if d == 0 else pltpu.roll(vf, d, 0)
            o = o + dbc[d * CHUNK:(d + 1) * CHUNK] * vs
        raw_ref[pl.ds(r0, CHUNK), :] = o
        return carry

    lax.fori_loop(0, n_chunks, chunk_body, 0)


def _gla_finish(o, gate, gain):
    hr = lax.broadcasted_iota(jnp.int32, (GLA_WIDTH, GLA_WIDTH), 0) // GLA_DV
    hc = lax.broadcasted_iota(jnp.int32, (GLA_WIDTH, GLA_WIDTH), 1) // GLA_DV
    head_ones = jnp.where(hr == hc, 1.0, 0.0).astype(BF16)
    ms = _dot((o * o).astype(BF16), head_ones) * (1.0 / GLA_DV)
    return (o * lax.rsqrt(ms + EPS) * gain * (gate * _sigmoid(gate))).astype(BF16)


def _proj_kernel(x_ref, g_ref, w_ref, gkup_ref, gkb_ref, cw_ref, cum_ref, tri_ref, amask_ref, gn_ref,
                 gla_ref, dq_ref, dk_ref, dv_ref, conv_ref,
                 zbuf_ref, q_s, k_s, g_s, v_s, s_ref, s_next_ref, raw_ref, *, tiles_per_seq):
    tm = x_ref.shape[0]

    @pl.when(pl.program_id(0) % tiles_per_seq == 0)
    def _():
        zbuf_ref[0:F32_SUBLANES, :] = jnp.zeros((F32_SUBLANES, CONV_DIM), F32)
        s_ref[...] = jnp.zeros_like(s_ref)

    h = _rms(x_ref[...], g_ref[...]).astype(BF16)

    def proj(off, width):
        return _dot(h, w_ref[:, off:off + width])

    qk = proj(OFF_GQ, 2 * GLA_QK_LANES)
    gq = qk[:, 0:GLA_QK_LANES]
    gk = qk[:, GLA_QK_LANES:]
    vg = proj(OFF_GV, 2 * GLA_WIDTH)
    gv = vg[:, 0:GLA_WIDTH].astype(BF16)
    gate = vg[:, GLA_WIDTH:]

    glr_hi, glr_lo = _split_bf16(gq)
    up_hi = gkup_ref[0]
    up_lo = gkup_ref[1]
    z = _dot(glr_hi, up_hi) + _dot(glr_lo, up_hi) + _dot(glr_hi, up_lo) + gkb_ref[...]
    logsig = jnp.minimum(z, 0.0) - jnp.log1p(jnp.exp(-jnp.abs(z)))
    lane = lax.broadcasted_iota(jnp.int32, (tm, GLA_QK_LANES), 1)
    gg = jnp.where(lane % HEAD_SLOT < GLA_DK, logsig * (1.0 / GLA_GATE_NORM), 0.0)

    dqk = proj(OFF_DQ, 2 * DIFF_LANES)
    dq_ref[...] = (dqk[:, 0:DIFF_LANES] * (DIFF_DQK ** -0.5 * LOG2E)).astype(BF16)
    dk_ref[...] = dqk[:, DIFF_LANES:].astype(BF16)
    lane_v = lax.broadcasted_iota(jnp.int32, (tm, DIFF_LANES), 1)
    dv_ref[...] = jnp.where(lane_v % LANES == DIFF_DV, 1.0, proj(OFF_DV, DIFF_LANES)).astype(BF16)

    conv = proj(OFF_CB, 3 * CONV_DIM)
    zc = conv[:, CONV_DIM:2 * CONV_DIM] * conv[:, 2 * CONV_DIM:]
    zbuf_ref[F32_SUBLANES:F32_SUBLANES + tm, :] = zc
    cw = cw_ref[...]
    y = (cw[2:3, :] * zc + cw[1:2, :] * zbuf_ref[F32_SUBLANES - 1:F32_SUBLANES - 1 + tm, :]
         + cw[0:1, :] * zbuf_ref[F32_SUBLANES - 2:F32_SUBLANES - 2 + tm, :])
    conv_ref[...] = (conv[:, 0:CONV_DIM] * y).astype(BF16)
    zbuf_ref[0:F32_SUBLANES, :] = zbuf_ref[tm:tm + F32_SUBLANES, :]

    q_scaled = gq * (GLA_DK ** -0.5)
    raw, new_state = _gla_fast(q_scaled, gk, gg, gv, s_ref[...], tri_ref[...])
    gla_ref[...] = _gla_finish(raw, gate, gn_ref[...])
    s_next_ref[...] = new_state
    q_s[...] = q_scaled
    k_s[...] = gk
    g_s[...] = gg
    v_s[...] = gv
    raw_ref[...] = gate
    safe = jnp.min(gg) * CHUNK >= -GLA_SAFE_DECAY

    @pl.when(safe)
    def _():
        s_ref[...] = s_next_ref[...]

    @pl.when(jnp.logical_not(safe))
    def _():
        gate_kept = raw_ref[...]
        _gla_general(q_s, k_s, g_s, v_s, s_ref, raw_ref, cum_ref, amask_ref)
        gla_ref[...] = _gla_finish(raw_ref[...], gate_kept, gn_ref[...])


def _proj(x, layer, g, w, gkup, gkb, cw, gn_tiled, seq):
    t = x.shape[0]
    tm = min(PROJ_ROWS, seq)
    row = lambda width: pl.BlockSpec((tm, width), lambda i: (i, 0))
    cum_np = _gla_cum_matrix()
    cum = jnp.asarray(cum_np, BF16)
    tri = jnp.asarray(cum_np[0:CHUNK], BF16)
    amask = jnp.asarray(_gla_score_mask(), F32)
    out_shapes = (
        jax.ShapeDtypeStruct((t, GLA_WIDTH), BF16),
        jax.ShapeDtypeStruct((t, DIFF_LANES), BF16),
        jax.ShapeDtypeStruct((t, DIFF_LANES), BF16),
        jax.ShapeDtypeStruct((t, DIFF_LANES), BF16),
        jax.ShapeDtypeStruct((t, CONV_DIM), BF16),
    )
    return pl.pallas_call(
        functools.partial(_proj_kernel, tiles_per_seq=seq // tm),
        out_shape=out_shapes,
        grid=(t // tm,),
        in_specs=[
            row(D_MODEL),
            _layer_spec(g, layer),
            _layer_spec(w, layer),
            _layer_spec(gkup, layer),
            _layer_spec(gkb, layer),
            _layer_spec(cw, layer),
            _const_spec(cum.shape),
            _const_spec(tri.shape),
            _const_spec(amask.shape),
            _layer_spec(gn_tiled, layer),
        ],
        out_specs=tuple(row(s.shape[1]) for s in out_shapes),
        scratch_shapes=[pltpu.VMEM((tm + F32_SUBLANES, CONV_DIM), F32),
                        pltpu.VMEM((tm, GLA_QK_LANES), F32),
                        pltpu.VMEM((tm, GLA_QK_LANES), F32),
                        pltpu.VMEM((tm, GLA_QK_LANES), F32),
                        pltpu.VMEM((tm, GLA_WIDTH), BF16),
                        pltpu.VMEM((GLA_WIDTH, GLA_QK_LANES), F32),
                        pltpu.VMEM((GLA_WIDTH, GLA_QK_LANES), F32),
                        pltpu.VMEM((tm, GLA_WIDTH), F32)],
        compiler_params=pltpu.CompilerParams(
            dimension_semantics=("arbitrary",), vmem_limit_bytes=VMEM_LIMIT_BYTES),
        name="mixer_proj_gla",
    )(x, g, w, gkup, gkb, cw, cum, tri, amask, gn_tiled)


def _t5_bucket(rel):
    nb = NUM_BUCKETS // 2
    max_exact = nb // 2
    ret = (rel > 0).astype(jnp.int32) * nb
    n = jnp.abs(rel)
    nf = jnp.maximum(n, 1).astype(jnp.float32)
    large = max_exact + (jnp.log(nf / max_exact) / math.log(MAX_DISTANCE / max_exact)
                         * (nb - max_exact)).astype(jnp.int32)
    large = jnp.minimum(large, nb - 1)
    return ret + jnp.where(n < max_exact, n, large)


def _bias_tiles(rel_bias, tile):
    assert tile >= MAX_DISTANCE and tile % CHUNK == 0
    table = rel_bias.astype(F32)
    far = table[NUM_BUCKETS // 2 - 1]
    heads = table.shape[1]
    rel = jnp.concatenate([jnp.arange(0, tile), jnp.arange(-2 * tile, 0)])
    onehot = _t5_bucket(rel)[None, :, None] == jnp.arange(NUM_BUCKETS)[None, None, :]
    per_rel = jnp.sum(jnp.where(onehot, table.T[:, None, :], 0.0), axis=-1)
    per_rel = (per_rel - far[:, None]) * LOG2E
    span = 3 * tile
    flat = jnp.tile(per_rel, (1, 2 * tile))[:, :2 * tile * (span - 1)]
    toep = flat.reshape(heads, 2 * tile, span - 1)[:, :, :tile]
    r = jnp.arange(tile)[:, None]
    c = jnp.arange(tile)[None, :]
    diag = jnp.where(((c // CHUNK) <= (r // CHUNK))[None], toep[:, :tile], MASK_VALUE)
    left = toep[:, tile:]
    return jnp.stack([left, diag], axis=1)


def _attn_kernel(q_ref, k_ref, v_ref, bias_ref, lamv_ref, subln_ref, o_ref,
                 s_buf, p_buf, alpha_buf, m_all, acc_all, *, tile, lam_init):
    seq = q_ref.shape[0]
    nq = seq // tile

    s_buf[...] = jnp.zeros(s_buf.shape, F32)
    p_buf[...] = jnp.zeros(p_buf.shape, BF16)
    alpha_buf[...] = jnp.zeros(alpha_buf.shape, F32)
    m_all[...] = jnp.full(m_all.shape, MASK_VALUE, F32)
    acc_all[...] = jnp.zeros(acc_all.shape, F32)

    lane = lax.broadcasted_iota(jnp.int32, (tile, LANES), 1)

    def stage1(i, j, slot):
        q = q_ref[pl.ds(pl.multiple_of(i * tile, tile), tile), :]
        zero = jnp.zeros_like(q)
        qs = jnp.concatenate([jnp.where(lane < LANES // 2, q, zero),
                              jnp.where(lane >= LANES // 2, q, zero)], axis=0)
        kj = k_ref[pl.ds(pl.multiple_of(j * tile, tile), tile), :]
        s_buf[slot] = _dot_nt(qs, kj)

    def stage2(i, bias, slot):
        for half in range(2):
            rows = slice(half * tile, (half + 1) * tile)
            s = s_buf[slot, rows, :]
            if bias is not None:
                s = s + bias
            m_prev = m_all[i, rows, :]
            m_next = jnp.maximum(m_prev, jnp.max(s, axis=-1, keepdims=True))
            p = jnp.exp2(s - jnp.concatenate([m_next] * (tile // LANES), axis=1))
            p_buf[slot, rows, :] = p.astype(BF16)
            alpha_buf[slot, rows, :] = jnp.exp2(m_prev - m_next)
            m_all[i, rows, :] = m_next

    def stage3(i, j, slot):
        vj = v_ref[pl.ds(pl.multiple_of(j * tile, tile), tile), :]
        acc_all[i] = alpha_buf[slot] * acc_all[i] + _dot(p_buf[slot], vj)

    def run_pipeline(n_pairs, first_pair, advance, with_bias, unroll):
        n_steps = pl.cdiv(n_pairs + 2, unroll) * unroll

        def step(t, pairs, slot_a, slot_b):
            (i0, j0), (i1, j1), (i2, j2) = pairs
            valid1 = jnp.logical_and(t >= 1, t <= n_pairs)
            valid2 = jnp.logical_and(t >= 2, t <= n_pairs + 1)
            stage3(jnp.where(valid2, i2, nq), j2, slot_a)
            stage2(jnp.where(valid1, i1, nq), bias_ref[0, j1 - i1 + 1] if with_bias else None, slot_b)
            stage1(i0, j0, slot_a)
            return (advance(i0, j0), (i0, j0), (i1, j1))

        def body(u, pairs):
            for r in range(unroll):
                pairs = step(unroll * u + r, pairs, r % 2, (r + 1) % 2)
            return pairs

        lax.fori_loop(0, n_steps // unroll, body, (first_pair,) * 3)

    def next_far(i, j):
        wrap = j == i - 2
        done = jnp.logical_and(wrap, i == nq - 1)
        step_i = jnp.logical_and(wrap, jnp.logical_not(done))
        return (jnp.where(step_i, i + 1, i), jnp.where(done, j, jnp.where(wrap, 0, j + 1)))

    def next_near(i, j):
        wrap = j == i
        done = jnp.logical_and(wrap, i == nq - 1)
        step_i = jnp.logical_and(wrap, jnp.logical_not(done))
        return (jnp.where(step_i, i + 1, i), jnp.where(jnp.logical_or(done, wrap), j, j + 1))

    zero = jnp.int32(0)
    if nq > 2:
        run_pipeline((nq - 1) * (nq - 2) // 2, (jnp.int32(2), zero), next_far, False, ATT_UNROLL_FAR)
    run_pipeline(2 * nq - 1, (zero, zero), next_near, True, ATT_UNROLL_NEAR)

    lv = lamv_ref[...]
    lam = (jnp.exp(jnp.sum(lv[0:1] * lv[1:2], axis=-1, keepdims=True))
           - jnp.exp(jnp.sum(lv[2:3] * lv[3:4], axis=-1, keepdims=True)) + lam_init)

    def finish(i):
        acc = acc_all[i]
        a1 = acc[0:tile]
        a2 = acc[tile:2 * tile]
        ones_col = lane == DIFF_DV
        ratio = jnp.sum(jnp.where(ones_col, a1 / jnp.where(ones_col, a2, 1.0), 0.0),
                        axis=-1, keepdims=True)
        u = a1 - lam * (ratio * a2)
        sq = jnp.where(ones_col, a1 * math.sqrt(DIFF_DV * EPS), u)
        ms = jnp.sum(sq * sq, axis=-1, keepdims=True) * (1.0 / DIFF_DV)
        y = u * lax.rsqrt(ms) * subln_ref[...] * (1.0 - lam_init)
        o_ref[pl.ds(pl.multiple_of(i * tile, tile), tile), :] = y.astype(BF16)

    group = math.gcd(nq, ATT_FINISH_GROUP)

    def finish_group(u, carry):
        for r in range(group):
            finish(u * group + r)
        return carry

    lax.fori_loop(0, nq // group, finish_group, 0)


def _attn(dq, dk, dv, bias_tiles, layer, lamv, subln, batch, seq, lam_init):
    tile = min(ATT_TILE, seq)
    nq = seq // tile
    head_block = pl.BlockSpec((seq, LANES), lambda b, h: (b, h))
    return pl.pallas_call(
        functools.partial(_attn_kernel, tile=tile, lam_init=lam_init),
        out_shape=jax.ShapeDtypeStruct((batch * seq, DIFF_LANES), BF16),
        grid=(batch, DIFF_HEADS),
        in_specs=[
            head_block, head_block, head_block,
            pl.BlockSpec((1, 2, tile, tile), lambda b, h: (h, 0, 0, 0)),
            pl.BlockSpec((None, 4, LANES), lambda b, h: (layer, 0, 0)),
            pl.BlockSpec((None, 1, LANES), lambda b, h: (layer, 0, 0)),
        ],
        out_specs=head_block,
        scratch_shapes=[pltpu.VMEM((2, 2 * tile, tile), F32),
                        pltpu.VMEM((2, 2 * tile, tile), BF16),
                        pltpu.VMEM((2, 2 * tile, LANES), F32),
                        pltpu.VMEM((nq + 1, 2 * tile, LANES), F32),
                        pltpu.VMEM((nq + 1, 2 * tile, LANES), F32)],
        compiler_params=pltpu.CompilerParams(
            dimension_semantics=("parallel", "parallel"), vmem_limit_bytes=VMEM_LIMIT_BYTES),
        name="diff_attn",
    )(dq, dk, dv, bias_tiles, lamv, subln)


def _pad_last(w, width):
    return jnp.pad(w, [(0, 0)] * (w.ndim - 1) + [(0, width - w.shape[-1])])


def _slot_cols(w, heads, used, slot):
    lead = w.shape[:-1]
    w = _pad_last(w.reshape(lead + (heads, used)), slot)
    return w.reshape(lead + (heads * slot,))


def _layout_w_in(w_in):
    offs = np.cumsum([0, GLA_HEADS * GLA_DK, GLA_HEADS * GLA_DK, GLA_WIDTH, GLA_RANK, GLA_WIDTH,
                      DIFF_HEADS * 2 * DIFF_DQK, DIFF_HEADS * 2 * DIFF_DQK, DIFF_WIDTH,
                      CONV_DIM, CONV_DIM, CONV_DIM])
    gq, gk, gv, glr, gate, dq, dk, dv, cb, cc, ch = [w_in[..., offs[n]:offs[n + 1]] for n in range(11)]
    half = LANES // 2
    gq_slots = _slot_cols(gq, GLA_HEADS, GLA_DK, HEAD_SLOT)
    gq_slots = gq_slots.at[..., GLR_LANE:GLR_LANE + GLA_RANK].set(glr)
    cols = [
        gq_slots,
        _slot_cols(gk, GLA_HEADS, GLA_DK, HEAD_SLOT),
        gv,
        gate,
        _slot_cols(dq, 2 * DIFF_HEADS, DIFF_DQK, half),
        _slot_cols(dk, 2 * DIFF_HEADS, DIFF_DQK, half),
        _slot_cols(dv, DIFF_HEADS, DIFF_DV, LANES),
        cb, cc, ch,
    ]
    w = jnp.concatenate(cols, axis=-1)
    assert w.shape[-1] == N_PROJ
    return w.astype(BF16)


def kernel(x, ffn1_norm, ffn1_gate, ffn1_up, ffn1_down, mix_norm, w_in, gla_gk_up, gla_gk_bias,
           gla_norm, diff_lambda_q1, diff_lambda_k1, diff_lambda_q2, diff_lambda_k2, diff_subln,
           rel_bias, conv_w, w_out, ffn2_norm, ffn2_gate, ffn2_up, ffn2_down, final_norm):
    batch, seq, _ = x.shape
    depth = w_in.shape[0]
    t = batch * seq
    xf = x.reshape(t, D_MODEL)
    bias_tiles = _bias_tiles(rel_bias, min(ATT_TILE, seq))
    final_g = final_norm.reshape(1, D_MODEL)
    ffn1_w = (ffn1_gate[0].astype(BF16), ffn1_up[0].astype(BF16), ffn1_down[0].astype(BF16))

    ffn1_g = ffn1_norm.reshape(depth, 1, D_MODEL)
    ffn2_g = ffn2_norm.reshape(depth, 1, D_MODEL)
    mix_g = mix_norm.reshape(depth, 1, D_MODEL)
    w_pad = _layout_w_in(w_in)
    up = _slot_cols(gla_gk_up, GLA_HEADS, GLA_DK, HEAD_SLOT)
    up = jnp.pad(up, ((0, 0), (GLR_LANE, GLA_QK_LANES - GLR_LANE - GLA_RANK), (0, 0)))
    up_hi = up.astype(BF16)
    up_lo = (up - up_hi.astype(F32)).astype(BF16)
    gkup = jnp.stack([up_hi, up_lo], axis=1)
    gkb = _slot_cols(gla_gk_bias, GLA_HEADS, GLA_DK, HEAD_SLOT)[:, None, :]
    cw = jnp.pad(conv_w, ((0, 0), (0, F32_SUBLANES - CONV_WIDTH), (0, 0)))
    gn_tiled = jnp.tile(gla_norm, (1, GLA_HEADS))[:, None, :]
    lamv = jnp.stack([diff_lambda_q1, diff_lambda_k1, diff_lambda_q2, diff_lambda_k2], axis=1)
    lamv = _pad_last(lamv.astype(F32), LANES)
    subln = _pad_last(diff_subln[:, None, :], LANES)
    w1 = w_out[:, 0:GLA_WIDTH].astype(BF16)
    w2 = w_out[:, GLA_WIDTH:GLA_WIDTH + DIFF_WIDTH].reshape(depth, DIFF_HEADS, DIFF_DV, D_MODEL)
    w2 = jnp.pad(w2, ((0, 0), (0, 0), (0, LANES - DIFF_DV), (0, 0)))
    w2 = w2.reshape(depth, DIFF_LANES, D_MODEL).astype(BF16)
    w3 = w_out[:, GLA_WIDTH + DIFF_WIDTH:].astype(BF16)

    for l in range(depth):
        lam_init = 0.8 - 0.6 * math.exp(-0.3 * l)
        xf, ffn2_w = _ffn(xf, ffn1_g, l, *ffn1_w, final_g, False,
                          cast=((ffn2_gate, ffn2_up, ffn2_down), l))
        gla_o, dq, dk, dv, conv_o = _proj(xf, l, mix_g, w_pad, gkup, gkb, cw, gn_tiled, seq)
        diff_o = _attn(dq, dk, dv, bias_tiles, l, lamv, subln, batch, seq, lam_init)
        last = l == depth - 1
        xf, ffn1_w = _ffn(xf, ffn2_g, l, *ffn2_w, final_g, last,
                          mix=(gla_o, diff_o, conv_o, w1, w2, w3),
                          cast=None if last else ((ffn1_gate, ffn1_up, ffn1_down), l + 1))

    return xf.reshape(batch, seq, D_MODEL)
```

```python
import functools
import math

import numpy as np
import jax
import jax.numpy as jnp
from jax import lax
from jax.experimental import pallas as pl
from jax.experimental.pallas import tpu as pltpu

F32 = jnp.float32
BF16 = jnp.bfloat16

D_MODEL = 1024
D_FF = 2816
EPS = 1e-6
CHUNK = 64
GLA_HEADS = 4
GLA_DK = 48
GLA_DV = 96
GLA_RANK = 16
GLA_GATE_NORM = 16.0
DIFF_HEADS = 4
DIFF_DQK = 48
DIFF_DV = 96
CONV_DIM = 256
CONV_WIDTH = 3
NUM_BUCKETS = 32
MAX_DISTANCE = 128
GLA_WIDTH = GLA_HEADS * GLA_DV
DIFF_WIDTH = DIFF_HEADS * DIFF_DV

LANES = 128
F32_SUBLANES = 8
BF16_SUBLANES = 16
MXU_DEPTH = 256
VMEM_LIMIT_BYTES = 62 * 1024 * 1024

HEAD_SLOT = 64
GLA_QK_LANES = GLA_HEADS * HEAD_SLOT
SUB = 16
N_SUB = CHUNK // SUB
KST_ROWS = SUB * (N_SUB * (N_SUB - 1) // 2)
DIFF_LANES = DIFF_HEADS * LANES
MASK_VALUE = -1e30
GLA_SAFE_DECAY = 60.0
LOG2E = math.log2(math.e)

OFF_GQ = 0
OFF_GK = OFF_GQ + GLA_QK_LANES
OFF_GV = OFF_GK + GLA_QK_LANES
OFF_GATE = OFF_GV + GLA_WIDTH
OFF_DQ = OFF_GATE + GLA_WIDTH
OFF_DK = OFF_DQ + DIFF_LANES
OFF_DV = OFF_DK + DIFF_LANES
OFF_CB = OFF_DV + DIFF_LANES
OFF_CC = OFF_CB + CONV_DIM
OFF_CH = OFF_CC + CONV_DIM
N_PROJ = OFF_CH + CONV_DIM
GLR_LANE = GLA_DK
assert GLR_LANE + GLA_RANK <= HEAD_SLOT

FFN_ROWS = 1024
FFN_F_SPLITS = (0, 6 * MXU_DEPTH, D_FF)
assert all((hi - lo) % MXU_DEPTH == 0 for lo, hi in zip(FFN_F_SPLITS[:-1], FFN_F_SPLITS[1:]))
PROJ_ROWS = 1024
ATT_TILE = 256
ATT_UNROLL_FAR = 36
ATT_UNROLL_NEAR = 22
ATT_FINISH_GROUP = 4


def _dot(a, b):
    return jnp.dot(a, b, preferred_element_type=F32)


def _dot_nt(a, b):
    return lax.dot_general(a, b, (((1,), (1,)), ((), ())), preferred_element_type=F32)


def _dot_tn(a, b):
    return lax.dot_general(a, b, (((0,), (0,)), ((), ())), preferred_element_type=F32)


def _split_bf16(x):
    hi = x.astype(BF16)
    lo = (x - hi.astype(F32)).astype(BF16)
    return hi, lo


def _rms(x, g):
    return x * lax.rsqrt(jnp.mean(x * x, axis=-1, keepdims=True) + EPS) * g


def _sigmoid(x):
    return 1.0 / (1.0 + jnp.exp(-x))


def _ffn_kernel(*refs, final, mix, cast):
    refs = list(refs)
    x_ref, g_ref, wg_ref, wu_ref, wd_ref, fg_ref = refs[:6]
    mix_refs = refs[6:12] if mix else ()
    n_in = 6 + len(mix_refs)
    src_refs = refs[n_in:n_in + 3] if cast else ()
    o_ref = refs[n_in + len(src_refs)]
    dst_refs = refs[n_in + len(src_refs) + 1:]
    for src, dst in zip(src_refs, dst_refs):
        dst[...] = src[...].astype(BF16)

    x = x_ref[...]
    if mix:
        gla_ref, diff_ref, conv_ref, w1_ref, w2_ref, w3_ref = mix_refs
        x = (x + _dot(gla_ref[...], w1_ref[...]) + _dot(diff_ref[...], w2_ref[...])
             + _dot(conv_ref[...], w3_ref[...]))
    h = _rms(x, g_ref[...]).astype(BF16)
    acc = None
    for lo, hi in zip(FFN_F_SPLITS[:-1], FFN_F_SPLITS[1:]):
        sl = slice(lo, hi)
        gate = _dot(h, wg_ref[:, sl])
        up = _dot(h, wu_ref[:, sl])
        a = (gate * _sigmoid(gate) * up).astype(BF16)
        part = _dot(a, wd_ref[sl, :])
        acc = part if acc is None else acc + part
    y = x + 0.5 * acc
    if final:
        y = _rms(y, fg_ref[...])
    o_ref[...] = y


def _const_spec(shape):
    nd = len(shape)
    return pl.BlockSpec(shape, lambda *_: (0,) * nd, pipeline_mode=pl.Buffered(1))


def _layer_spec(stacked, layer):
    nd = stacked.ndim - 1
    return pl.BlockSpec((None,) + tuple(stacked.shape[1:]), lambda *_: (layer,) + (0,) * nd,
                        pipeline_mode=pl.Buffered(1))


def _cast_block(n_rows, steps):
    share = 1
    while (n_rows * share) % (steps * BF16_SUBLANES) or steps % share:
        share *= 2
        assert share <= steps
    return n_rows * share // steps, share


def _ffn(x, gains, layer, wg, wu, wd, final_g, final, mix=None, cast=None):
    t = x.shape[0]
    tm = min(FFN_ROWS, t)
    steps = t // tm
    row = lambda width: pl.BlockSpec((tm, width), lambda i: (i, 0))
    operands = [x, gains, wg, wu, wd, final_g]
    in_specs = [row(D_MODEL), _layer_spec(gains, layer), _const_spec((D_MODEL, D_FF)),
                _const_spec((D_MODEL, D_FF)), _const_spec((D_FF, D_MODEL)), _const_spec((1, D_MODEL))]
    out_shapes = [jax.ShapeDtypeStruct((t, D_MODEL), F32)]
    out_specs = [row(D_MODEL)]
    if mix is not None:
        operands += list(mix)
        in_specs += [row(a.shape[1]) for a in mix[:3]] + [_layer_spec(w, layer) for w in mix[3:]]
    if cast is not None:
        stacked, layer = cast
        for w in stacked:
            rows, share = _cast_block(w.shape[1], steps)
            operands.append(w)
            in_specs.append(pl.BlockSpec((None, rows, w.shape[2]),
                                         lambda i, share=share: (layer, i // share, 0)))
            out_shapes.append(jax.ShapeDtypeStruct(w.shape[1:], BF16))
            out_specs.append(pl.BlockSpec((rows, w.shape[2]), lambda i, share=share: (i // share, 0)))
    outs = pl.pallas_call(
        functools.partial(_ffn_kernel, final=final, mix=mix is not None, cast=cast is not None),
        out_shape=tuple(out_shapes),
        grid=(steps,),
        in_specs=in_specs,
        out_specs=tuple(out_specs),
        compiler_params=pltpu.CompilerParams(
            dimension_semantics=("arbitrary",), vmem_limit_bytes=VMEM_LIMIT_BYTES),
        name="ffn",
    )(*operands)
    return outs[0], tuple(outs[1:])


def _gla_cum_matrix():
    t = np.arange(CHUNK)[:, None]
    u = np.arange(CHUNK)[None, :]
    tri = (u <= t)
    ref = (u <= (t // SUB) * SUB - 1)
    ones = np.ones((CHUNK, CHUNK), bool)
    parts = [tri, ref, ones]
    for j in range(1, N_SUB):
        s = np.arange(SUB * j)[:, None]
        parts.append((u > s) & (u <= SUB * j - 1))
    return np.concatenate(parts, axis=0).astype(np.float32)


def _gla_score_mask():
    t = np.arange(CHUNK)[:, None]
    group = np.concatenate([np.full(SUB * j, j) for j in range(1, N_SUB)])
    keep = (group[None, :] == (t // SUB))
    return np.tile(keep, (1, GLA_HEADS)).astype(np.float32)


def _gla_head_ids():
    qk_head = lax.broadcasted_iota(jnp.int32, (1, GLA_QK_LANES), 1) // HEAD_SLOT
    v_head = lax.broadcasted_iota(jnp.int32, (1, GLA_WIDTH), 1) // GLA_DV
    row_head_v = lax.broadcasted_iota(jnp.int32, (GLA_WIDTH, 1), 0) // GLA_DV
    return qk_head, v_head, row_head_v == qk_head


def _gla_fast(q, k, g, v, state, tri):
    rows = q.shape[0]
    n_chunks = rows // CHUNK
    qk_head, v_head, state_mask = _gla_head_ids()
    chunk = lambda x, c: x[c * CHUNK:(c + 1) * CHUNK]
    sel_rows = pl.cdiv(n_chunks, BF16_SUBLANES) * BF16_SUBLANES
    sel_r = lax.broadcasted_iota(jnp.int32, (sel_rows, rows), 0)
    sel_c = lax.broadcasted_iota(jnp.int32, (sel_rows, rows), 1) // CHUNK
    chunk_sel = jnp.where(sel_r == sel_c, 1.0, 0.0).astype(BF16)
    g_hi, g_lo = _split_bf16(g)
    totals = _dot(chunk_sel, g_hi) + _dot(chunk_sel, g_lo)
    b = jnp.concatenate([_dot(tri, chunk(g_hi, c)) + _dot(tri, chunk(g_lo, c))
                         for c in range(n_chunks)], axis=0)
    b_last = jnp.concatenate([jnp.broadcast_to(totals[c:c + 1], (CHUNK, GLA_QK_LANES))
                              for c in range(n_chunks)], axis=0)
    qd = (q * jnp.exp(b)).astype(BF16)
    kd = (k * jnp.exp(-b)).astype(BF16)
    kw = (k * jnp.exp(b_last - b)).astype(BF16)
    t_idx = lax.broadcasted_iota(jnp.int32, (CHUNK, GLA_QK_LANES), 0)
    s_idx = lax.broadcasted_iota(jnp.int32, (CHUNK, GLA_QK_LANES), 1) % HEAD_SLOT
    causal = s_idx <= t_idx
    intra, update = [], []
    for c in range(n_chunks):
        kd_c, v_c = chunk(kd, c), chunk(v, c)
        kd_bd = jnp.concatenate(
            [jnp.where(qk_head == hd, kd_c, jnp.zeros_like(kd_c)) for hd in range(GLA_HEADS)],
            axis=0)
        a = _dot_nt(chunk(qd, c), kd_bd)
        a = jnp.where(causal, a, 0.0).astype(BF16)
        v_bd = jnp.concatenate(
            [jnp.where(v_head == hd, v_c, jnp.zeros_like(v_c)) for hd in range(GLA_HEADS)],
            axis=0)
        intra.append(_dot(a, v_bd))
        update.append(_dot_tn(v_c, chunk(kw, c)))
    raw = []
    for c in range(n_chunks):
        raw.append(intra[c] + _dot_nt(chunk(qd, c), state.astype(BF16)))
        state = state * jnp.exp(totals[c:c + 1]) + jnp.where(state_mask, update[c], 0.0)
    return jnp.concatenate(raw, axis=0), state


def _gla_general(q_ref, k_ref, g_ref, v_ref, s_ref, raw_ref, cum_ref, amask_ref):
    n_chunks = q_ref.shape[0] // CHUNK
    qk_head, v_head, state_mask = _gla_head_ids()
    cum = cum_ref[...]
    amask = amask_ref[...] > 0.5
    ind_r = lax.broadcasted_iota(jnp.int32, (GLA_QK_LANES, LANES), 0) // HEAD_SLOT
    ind_c = lax.broadcasted_iota(jnp.int32, (GLA_QK_LANES, LANES), 1)
    ind_sum = jnp.where(ind_r == ind_c, 1.0, 0.0).astype(BF16)
    bc_r = lax.broadcasted_iota(jnp.int32, (LANES, GLA_WIDTH), 0)
    bc_c = lax.broadcasted_iota(jnp.int32, (LANES, GLA_WIDTH), 1) // GLA_DV
    ind_bcast = jnp.where(bc_r == bc_c, 1.0, 0.0).astype(BF16)
    row_in_sub = lax.broadcasted_iota(jnp.int32, (CHUNK, 1), 0) % SUB

    def chunk_body(c, carry):
        r0 = pl.multiple_of(c * CHUNK, CHUNK)
        q = q_ref[pl.ds(r0, CHUNK), :]
        k = k_ref[pl.ds(r0, CHUNK), :]
        g = g_ref[pl.ds(r0, CHUNK), :]
        v = v_ref[pl.ds(r0, CHUNK), :]
        g_hi, g_lo = _split_bf16(g)
        cums = _dot(cum, g_hi) + _dot(cum, g_lo)
        b = cums[0:CHUNK]
        b_ref = cums[CHUNK:2 * CHUNK]
        b_last = cums[2 * CHUNK:3 * CHUNK]
        e_kst = cums[3 * CHUNK:3 * CHUNK + KST_ROWS]

        state = s_ref[...]
        q_in = (q * jnp.exp(b)).astype(BF16)
        o = _dot_nt(q_in, state.astype(BF16))

        kw = (k * jnp.exp(b_last - b)).astype(BF16)
        decay = jnp.exp(jnp.concatenate([b_last] * (GLA_WIDTH // CHUNK), axis=0))
        s_ref[...] = state * decay + jnp.where(state_mask, _dot_tn(v, kw), 0.0)

        q_sub = (q * jnp.exp(jnp.minimum(b - b_ref, 0.0))).astype(BF16)
        k_st = jnp.concatenate([k[0:SUB * j] for j in range(1, N_SUB)], axis=0) * jnp.exp(e_kst)
        k_bd = jnp.concatenate(
            [jnp.where(qk_head == hd, k_st, 0.0) for hd in range(GLA_HEADS)], axis=0).astype(BF16)
        a = _dot_nt(q_sub, k_bd)
        a = jnp.where(amask, a, 0.0).astype(BF16)
        v_st = jnp.concatenate([v[0:SUB * j] for j in range(1, N_SUB)], axis=0)
        v_bd = jnp.concatenate(
            [jnp.where(v_head == hd, v_st, jnp.zeros_like(v_st)) for hd in range(GLA_HEADS)], axis=0)
        o = o + _dot(a, v_bd)

        vf = v.astype(F32)
        prods = []
        for d in range(SUB):
            ks = k if d == 0 else pltpu.roll(k, d, 0)
            bs = b if d == 0 else pltpu.roll(b, d, 0)
            prods.append((q * ks * jnp.exp(jnp.minimum(b - bs, 0.0))).astype(BF16))
        dsum = _dot(jnp.concatenate(prods, axis=0), ind_sum)
        valid = jnp.concatenate([row_in_sub >= d for d in range(SUB)], axis=0)
        dsum = jnp.where(valid, dsum, 0.0).astype(BF16)
        dbc = _dot(dsum, ind_bcast)
        for d in range(SUB):
            vs = vf if d == 0 else pltpu.roll(vf, d, 0)
            o = o + dbc[d * CHUNK:(d + 1) * CHUNK] * vs
        raw_ref[pl.ds(r0, CHUNK), :] = o
        return carry

    lax.fori_loop(0, n_chunks, chunk_body, 0)


def _gla_finish(o, gate, gain):
    hr = lax.broadcasted_iota(jnp.int32, (GLA_WIDTH, GLA_WIDTH), 0) // GLA_DV
    hc = lax.broadcasted_iota(jnp.int32, (GLA_WIDTH, GLA_WIDTH), 1) // GLA_DV
    head_ones = jnp.where(hr == hc, 1.0, 0.0).astype(BF16)
    ms = _dot((o * o).astype(BF16), head_ones) * (1.0 / GLA_DV)
    return (o * lax.rsqrt(ms + EPS) * gain * (gate * _sigmoid(gate))).astype(BF16)


def _proj_kernel(x_ref, g_ref, w_ref, gkup_ref, gkb_ref, cw_ref, cum_ref, tri_ref, amask_ref, gn_ref,
                 gla_ref, dq_ref, dk_ref, dv_ref, conv_ref,
                 zbuf_ref, q_s, k_s, g_s, v_s, s_ref, s_next_ref, raw_ref, *, tiles_per_seq):
    tm = x_ref.shape[0]

    @pl.when(pl.program_id(0) % tiles_per_seq == 0)
    def _():
        zbuf_ref[0:F32_SUBLANES, :] = jnp.zeros((F32_SUBLANES, CONV_DIM), F32)
        s_ref[...] = jnp.zeros_like(s_ref)

    h = _rms(x_ref[...], g_ref[...]).astype(BF16)

    def proj(off, width):
        return _dot(h, w_ref[:, off:off + width])

    qk = proj(OFF_GQ, 2 * GLA_QK_LANES)
    gq = qk[:, 0:GLA_QK_LANES]
    gk = qk[:, GLA_QK_LANES:]
    vg = proj(OFF_GV, 2 * GLA_WIDTH)
    gv = vg[:, 0:GLA_WIDTH].astype(BF16)
    gate = vg[:, GLA_WIDTH:]

    glr_hi, glr_lo = _split_bf16(gq)
    up_hi = gkup_ref[0]
    up_lo = gkup_ref[1]
    z = _dot(glr_hi, up_hi) + _dot(glr_lo, up_hi) + _dot(glr_hi, up_lo) + gkb_ref[...]
    logsig = jnp.minimum(z, 0.0) - jnp.log1p(jnp.exp(-jnp.abs(z)))
    lane = lax.broadcasted_iota(jnp.int32, (tm, GLA_QK_LANES), 1)
    gg = jnp.where(lane % HEAD_SLOT < GLA_DK, logsig * (1.0 / GLA_GATE_NORM), 0.0)

    dqk = proj(OFF_DQ, 2 * DIFF_LANES)
    dq_ref[...] = (dqk[:, 0:DIFF_LANES] * (DIFF_DQK ** -0.5 * LOG2E)).astype(BF16)
    dk_ref[...] = dqk[:, DIFF_LANES:].astype(BF16)
    lane_v = lax.broadcasted_iota(jnp.int32, (tm, DIFF_LANES), 1)
    dv_ref[...] = jnp.where(lane_v % LANES == DIFF_DV, 1.0, proj(OFF_DV, DIFF_LANES)).astype(BF16)

    conv = proj(OFF_CB, 3 * CONV_DIM)
    zc = conv[:, CONV_DIM:2 * CONV_DIM] * conv[:, 2 * CONV_DIM:]
    zbuf_ref[F32_SUBLANES:F32_SUBLANES + tm, :] = zc
    cw = cw_ref[...]
    y = (cw[2:3, :] * zc + cw[1:2, :] * zbuf_ref[F32_SUBLANES - 1:F32_SUBLANES - 1 + tm, :]
         + cw[0:1, :] * zbuf_ref[F32_SUBLANES - 2:F32_SUBLANES - 2 + tm, :])
    conv_ref[...] = (conv[:, 0:CONV_DIM] * y).astype(BF16)
    zbuf_ref[0:F32_SUBLANES, :] = zbuf_ref[tm:tm + F32_SUBLANES, :]

    q_scaled = gq * (GLA_DK ** -0.5)
    raw, new_state = _gla_fast(q_scaled, gk, gg, gv, s_ref[...], tri_ref[...])
    gla_ref[...] = _gla_finish(raw, gate, gn_ref[...])
    s_next_ref[...] = new_state
    q_s[...] = q_scaled
    k_s[...] = gk
    g_s[...] = gg
    v_s[...] = gv
    raw_ref[...] = gate
    safe = jnp.min(gg) * CHUNK >= -GLA_SAFE_DECAY

    @pl.when(safe)
    def _():
        s_ref[...] = s_next_ref[...]

    @pl.when(jnp.logical_not(safe))
    def _():
        gate_kept = raw_ref[...]
        _gla_general(q_s, k_s, g_s, v_s, s_ref, raw_ref, cum_ref, amask_ref)
        gla_ref[...] = _gla_finish(raw_ref[...], gate_kept, gn_ref[...])


def _proj(x, layer, g, w, gkup, gkb, cw, gn_tiled, seq):
    t = x.shape[0]
    tm = min(PROJ_ROWS, seq)
    row = lambda width: pl.BlockSpec((tm, width), lambda i: (i, 0))
    cum_np = _gla_cum_matrix()
    cum = jnp.asarray(cum_np, BF16)
    tri = jnp.asarray(cum_np[0:CHUNK], BF16)
    amask = jnp.asarray(_gla_score_mask(), F32)
    out_shapes = (
        jax.ShapeDtypeStruct((t, GLA_WIDTH), BF16),
        jax.ShapeDtypeStruct((t, DIFF_LANES), BF16),
        jax.ShapeDtypeStruct((t, DIFF_LANES), BF16),
        jax.ShapeDtypeStruct((t, DIFF_LANES), BF16),
        jax.ShapeDtypeStruct((t, CONV_DIM), BF16),
    )
    return pl.pallas_call(
        functools.partial(_proj_kernel, tiles_per_seq=seq // tm),
        out_shape=out_shapes,
        grid=(t // tm,),
        in_specs=[
            row(D_MODEL),
            _layer_spec(g, layer),
            _layer_spec(w, layer),
            _layer_spec(gkup, layer),
            _layer_spec(gkb, layer),
            _layer_spec(cw, layer),
            _const_spec(cum.shape),
            _const_spec(tri.shape),
            _const_spec(amask.shape),
            _layer_spec(gn_tiled, layer),
        ],
        out_specs=tuple(row(s.shape[1]) for s in out_shapes),
        scratch_shapes=[pltpu.VMEM((tm + F32_SUBLANES, CONV_DIM), F32),
                        pltpu.VMEM((tm, GLA_QK_LANES), F32),
                        pltpu.VMEM((tm, GLA_QK_LANES), F32),
                        pltpu.VMEM((tm, GLA_QK_LANES), F32),
                        pltpu.VMEM((tm, GLA_WIDTH), BF16),
                        pltpu.VMEM((GLA_WIDTH, GLA_QK_LANES), F32),
                        pltpu.VMEM((GLA_WIDTH, GLA_QK_LANES), F32),
                        pltpu.VMEM((tm, GLA_WIDTH), F32)],
        compiler_params=pltpu.CompilerParams(
            dimension_semantics=("arbitrary",), vmem_limit_bytes=VMEM_LIMIT_BYTES),
        name="mixer_proj_gla",
    )(x, g, w, gkup, gkb, cw, cum, tri, amask, gn_tiled)


def _t5_bucket(rel):
    nb = NUM_BUCKETS // 2
    max_exact = nb // 2
    ret = (rel > 0).astype(jnp.int32) * nb
    n = jnp.abs(rel)
    nf = jnp.maximum(n, 1).astype(jnp.float32)
    large = max_exact + (jnp.log(nf / max_exact) / math.log(MAX_DISTANCE / max_exact)
                         * (nb - max_exact)).astype(jnp.int32)
    large = jnp.minimum(large, nb - 1)
    return ret + jnp.where(n < max_exact, n, large)


def _bias_tiles(rel_bias, tile):
    assert tile >= MAX_DISTANCE and tile % CHUNK == 0
    table = rel_bias.astype(F32)
    far = table[NUM_BUCKETS // 2 - 1]
    heads = table.shape[1]
    rel = jnp.concatenate([jnp.arange(0, tile), jnp.arange(-2 * tile, 0)])
    onehot = _t5_bucket(rel)[None, :, None] == jnp.arange(NUM_BUCKETS)[None, None, :]
    per_rel = jnp.sum(jnp.where(onehot, table.T[:, None, :], 0.0), axis=-1)
    per_rel = (per_rel - far[:, None]) * LOG2E
    span = 3 * tile
    flat = jnp.tile(per_rel, (1, 2 * tile))[:, :2 * tile * (span - 1)]
    toep = flat.reshape(heads, 2 * tile, span - 1)[:, :, :tile]
    r = jnp.arange(tile)[:, None]
    c = jnp.arange(tile)[None, :]
    diag = jnp.where(((c // CHUNK) <= (r // CHUNK))[None], toep[:, :tile], MASK_VALUE)
    left = toep[:, tile:]
    return jnp.stack([left, diag], axis=1)


def _attn_kernel(q_ref, k_ref, v_ref, bias_ref, lamv_ref, subln_ref, o_ref,
                 s_buf, p_buf, alpha_buf, m_all, acc_all, *, tile, lam_init):
    seq = q_ref.shape[0]
    nq = seq // tile

    s_buf[...] = jnp.zeros(s_buf.shape, F32)
    p_buf[...] = jnp.zeros(p_buf.shape, BF16)
    alpha_buf[...] = jnp.zeros(alpha_buf.shape, F32)
    m_all[...] = jnp.full(m_all.shape, MASK_VALUE, F32)
    acc_all[...] = jnp.zeros(acc_all.shape, F32)

    lane = lax.broadcasted_iota(jnp.int32, (tile, LANES), 1)

    def stage1(i, j, slot):
        q = q_ref[pl.ds(pl.multiple_of(i * tile, tile), tile), :]
        zero = jnp.zeros_like(q)
        qs = jnp.concatenate([jnp.where(lane < LANES // 2, q, zero),
                              jnp.where(lane >= LANES // 2, q, zero)], axis=0)
        kj = k_ref[pl.ds(pl.multiple_of(j * tile, tile), tile), :]
        s_buf[slot] = _dot_nt(qs, kj)

    def stage2(i, bias, slot):
        for half in range(2):
            rows = slice(half * tile, (half + 1) * tile)
            s = s_buf[slot, rows, :]
            if bias is not None:
                s = s + bias
            m_prev = m_all[i, rows, :]
            m_next = jnp.maximum(m_prev, jnp.max(s, axis=-1, keepdims=True))
            p = jnp.exp2(s - jnp.concatenate([m_next] * (tile // LANES), axis=1))
            p_buf[slot, rows, :] = p.astype(BF16)
            alpha_buf[slot, rows, :] = jnp.exp2(m_prev - m_next)
            m_all[i, rows, :] = m_next

    def stage3(i, j, slot):
        vj = v_ref[pl.ds(pl.multiple_of(j * tile, tile), tile), :]
        acc_all[i] = alpha_buf[slot] * acc_all[i] + _dot(p_buf[slot], vj)

    def run_pipeline(n_pairs, first_pair, advance, with_bias, unroll):
        n_steps = pl.cdiv(n_pairs + 2, unroll) * unroll

        def step(t, pairs, slot_a, slot_b):
            (i0, j0), (i1, j1), (i2, j2) = pairs
            valid1 = jnp.logical_and(t >= 1, t <= n_pairs)
            valid2 = jnp.logical_and(t >= 2, t <= n_pairs + 1)
            stage3(jnp.where(valid2, i2, nq), j2, slot_a)
            stage2(jnp.where(valid1, i1, nq), bias_ref[0, j1 - i1 + 1] if with_bias else None, slot_b)
            stage1(i0, j0, slot_a)
            return (advance(i0, j0), (i0, j0), (i1, j1))

        def body(u, pairs):
            for r in range(unroll):
                pairs = step(unroll * u + r, pairs, r % 2, (r + 1) % 2)
            return pairs

        lax.fori_loop(0, n_steps // unroll, body, (first_pair,) * 3)

    def next_far(i, j):
        wrap = j == i - 2
        done = jnp.logical_and(wrap, i == nq - 1)
        step_i = jnp.logical_and(wrap, jnp.logical_not(done))
        return (jnp.where(step_i, i + 1, i), jnp.where(done, j, jnp.where(wrap, 0, j + 1)))

    def next_near(i, j):
        wrap = j == i
        done = jnp.logical_and(wrap, i == nq - 1)
        step_i = jnp.logical_and(wrap, jnp.logical_not(done))
        return (jnp.where(step_i, i + 1, i), jnp.where(jnp.logical_or(done, wrap), j, j + 1))

    zero = jnp.int32(0)
    if nq > 2:
        run_pipeline((nq - 1) * (nq - 2) // 2, (jnp.int32(2), zero), next_far, False, ATT_UNROLL_FAR)
    run_pipeline(2 * nq - 1, (zero, zero), next_near, True, ATT_UNROLL_NEAR)

    lv = lamv_ref[...]
    lam = (jnp.exp(jnp.sum(lv[0:1] * lv[1:2], axis=-1, keepdims=True))
           - jnp.exp(jnp.sum(lv[2:3] * lv[3:4], axis=-1, keepdims=True)) + lam_init)

    def finish(i):
        acc = acc_all[i]
        a1 = acc[0:tile]
        a2 = acc[tile:2 * tile]
        ones_col = lane == DIFF_DV
        ratio = jnp.sum(jnp.where(ones_col, a1 / jnp.where(ones_col, a2, 1.0), 0.0),
                        axis=-1, keepdims=True)
        u = a1 - lam * (ratio * a2)
        sq = jnp.where(ones_col, a1 * math.sqrt(DIFF_DV * EPS), u)
        ms = jnp.sum(sq * sq, axis=-1, keepdims=True) * (1.0 / DIFF_DV)
        y = u * lax.rsqrt(ms) * subln_ref[...] * (1.0 - lam_init)
        o_ref[pl.ds(pl.multiple_of(i * tile, tile), tile), :] = y.astype(BF16)

    group = math.gcd(nq, ATT_FINISH_GROUP)

    def finish_group(u, carry):
        for r in range(group):
            finish(u * group + r)
        return carry

    lax.fori_loop(0, nq // group, finish_group, 0)


def _attn(dq, dk, dv, bias_tiles, layer, lamv, subln, batch, seq, lam_init):
    tile = min(ATT_TILE, seq)
    nq = seq // tile
    head_block = pl.BlockSpec((seq, LANES), lambda b, h: (b, h))
    return pl.pallas_call(
        functools.partial(_attn_kernel, tile=tile, lam_init=lam_init),
        out_shape=jax.ShapeDtypeStruct((batch * seq, DIFF_LANES), BF16),
        grid=(batch, DIFF_HEADS),
        in_specs=[
            head_block, head_block, head_block,
            pl.BlockSpec((1, 2, tile, tile), lambda b, h: (h, 0, 0, 0)),
            pl.BlockSpec((None, 4, LANES), lambda b, h: (layer, 0, 0)),
            pl.BlockSpec((None, 1, LANES), lambda b, h: (layer, 0, 0)),
        ],
        out_specs=head_block,
        scratch_shapes=[pltpu.VMEM((2, 2 * tile, tile), F32),
                        pltpu.VMEM((2, 2 * tile, tile), BF16),
                        pltpu.VMEM((2, 2 * tile, LANES), F32),
                        pltpu.VMEM((nq + 1, 2 * tile, LANES), F32),
                        pltpu.VMEM((nq + 1, 2 * tile, LANES), F32)],
        compiler_params=pltpu.CompilerParams(
            dimension_semantics=("parallel", "parallel"), vmem_limit_bytes=VMEM_LIMIT_BYTES),
        name="diff_attn",
    )(dq, dk, dv, bias_tiles, lamv, subln)


def _pad_last(w, width):
    return jnp.pad(w, [(0, 0)] * (w.ndim - 1) + [(0, width - w.shape[-1])])


def _slot_cols(w, heads, used, slot):
    lead = w.shape[:-1]
    w = _pad_last(w.reshape(lead + (heads, used)), slot)
    return w.reshape(lead + (heads * slot,))


def _layout_w_in(w_in):
    offs = np.cumsum([0, GLA_HEADS * GLA_DK, GLA_HEADS * GLA_DK, GLA_WIDTH, GLA_RANK, GLA_WIDTH,
                      DIFF_HEADS * 2 * DIFF_DQK, DIFF_HEADS * 2 * DIFF_DQK, DIFF_WIDTH,
                      CONV_DIM, CONV_DIM, CONV_DIM])
    w_in = w_in.astype(BF16)
    gq, gk, gv, glr, gate, dq, dk, dv, cb, cc, ch = [w_in[..., offs[n]:offs[n + 1]] for n in range(11)]
    half = LANES // 2
    gq_slots = _slot_cols(gq, GLA_HEADS, GLA_DK, HEAD_SLOT)
    gq_slots = gq_slots.at[..., GLR_LANE:GLR_LANE + GLA_RANK].set(glr)
    cols = [
        gq_slots,
        _slot_cols(gk, GLA_HEADS, GLA_DK, HEAD_SLOT),
        gv,
        gate,
        _slot_cols(dq, 2 * DIFF_HEADS, DIFF_DQK, half),
        _slot_cols(dk, 2 * DIFF_HEADS, DIFF_DQK, half),
        _slot_cols(dv, DIFF_HEADS, DIFF_DV, LANES),
        cb, cc, ch,
    ]
    w = jnp.concatenate(cols, axis=-1)
    assert w.shape[-1] == N_PROJ
    return w


def kernel(x, ffn1_norm, ffn1_gate, ffn1_up, ffn1_down, mix_norm, w_in, gla_gk_up, gla_gk_bias,
           gla_norm, diff_lambda_q1, diff_lambda_k1, diff_lambda_q2, diff_lambda_k2, diff_subln,
           rel_bias, conv_w, w_out, ffn2_norm, ffn2_gate, ffn2_up, ffn2_down, final_norm):
    batch, seq, _ = x.shape
    depth = w_in.shape[0]
    t = batch * seq
    xf = x.reshape(t, D_MODEL)
    bias_tiles = _bias_tiles(rel_bias, min(ATT_TILE, seq))
    final_g = final_norm.reshape(1, D_MODEL)
    ffn1_w = (ffn1_gate[0].astype(BF16), ffn1_up[0].astype(BF16), ffn1_down[0].astype(BF16))

    ffn1_g = ffn1_norm.reshape(depth, 1, D_MODEL)
    ffn2_g = ffn2_norm.reshape(depth, 1, D_MODEL)
    mix_g = mix_norm.reshape(depth, 1, D_MODEL)
    w_pad = _layout_w_in(w_in)
    up = _slot_cols(gla_gk_up, GLA_HEADS, GLA_DK, HEAD_SLOT)
    up = jnp.pad(up, ((0, 0), (GLR_LANE, GLA_QK_LANES - GLR_LANE - GLA_RANK), (0, 0)))
    up_hi = up.astype(BF16)
    up_lo = (up - up_hi.astype(F32)).astype(BF16)
    gkup = jnp.stack([up_hi, up_lo], axis=1)
    gkb = _slot_cols(gla_gk_bias, GLA_HEADS, GLA_DK, HEAD_SLOT)[:, None, :]
    cw = jnp.pad(conv_w, ((0, 0), (0, F32_SUBLANES - CONV_WIDTH), (0, 0)))
    gn_tiled = jnp.tile(gla_norm, (1, GLA_HEADS))[:, None, :]
    lamv = jnp.stack([diff_lambda_q1, diff_lambda_k1, diff_lambda_q2, diff_lambda_k2], axis=1)
    lamv = _pad_last(lamv.astype(F32), LANES)
    subln = _pad_last(diff_subln[:, None, :], LANES)
    w1 = w_out[:, 0:GLA_WIDTH].astype(BF16)
    w2 = w_out[:, GLA_WIDTH:GLA_WIDTH + DIFF_WIDTH].reshape(depth, DIFF_HEADS, DIFF_DV, D_MODEL)
    w2 = jnp.pad(w2, ((0, 0), (0, 0), (0, LANES - DIFF_DV), (0, 0)))
    w2 = w2.reshape(depth, DIFF_LANES, D_MODEL).astype(BF16)
    w3 = w_out[:, GLA_WIDTH + DIFF_WIDTH:].astype(BF16)

    for l in range(depth):
        lam_init = 0.8 - 0.6 * math.exp(-0.3 * l)
        xf, ffn2_w = _ffn(xf, ffn1_g, l, *ffn1_w, final_g, False,
                          cast=((ffn2_gate, ffn2_up, ffn2_down), l))
        gla_o, dq, dk, dv, conv_o = _proj(xf, l, mix_g, w_pad, gkup, gkb, cw, gn_tiled, seq)
        diff_o = _attn(dq, dk, dv, bias_tiles, l, lamv, subln, batch, seq, lam_init)
        last = l == depth - 1
        xf, ffn1_w = _ffn(xf, ffn2_g, l, *ffn2_w, final_g, last,
                          mix=(gla_o, diff_o, conv_o, w1, w2, w3),
                          cast=None if last else ((ffn1_gate, ffn1_up, ffn1_down), l + 1))

    return xf.reshape(batch, seq, D_MODEL)
```

```python
import functools
import math

import numpy as np
import jax
import jax.numpy as jnp
from jax import lax
from jax.experimental import pallas as pl
from jax.experimental.pallas import tpu as pltpu

F32 = jnp.float32
BF16 = jnp.bfloat16

D_MODEL = 1024
D_FF = 2816
EPS = 1e-6
CHUNK = 64
GLA_HEADS = 4
GLA_DK = 48
GLA_DV = 96
GLA_RANK = 16
GLA_GATE_NORM = 16.0
DIFF_HEADS = 4
DIFF_DQK = 48
DIFF_DV = 96
CONV_DIM = 256
CONV_WIDTH = 3
NUM_BUCKETS = 32
MAX_DISTANCE = 128
GLA_WIDTH = GLA_HEADS * GLA_DV
DIFF_WIDTH = DIFF_HEADS * DIFF_DV

LANES = 128
F32_SUBLANES = 8
BF16_SUBLANES = 16
MXU_DEPTH = 256
VMEM_LIMIT_BYTES = 62 * 1024 * 1024

HEAD_SLOT = 64
GLA_QK_LANES = GLA_HEADS * HEAD_SLOT
SUB = 16
N_SUB = CHUNK // SUB
KST_ROWS = SUB * (N_SUB * (N_SUB - 1) // 2)
DIFF_LANES = DIFF_HEADS * LANES
MASK_VALUE = -1e30
GLA_SAFE_DECAY = 60.0
LOG2E = math.log2(math.e)

OFF_GQ = 0
OFF_GK = OFF_GQ + GLA_QK_LANES
OFF_GV = OFF_GK + GLA_QK_LANES
OFF_GATE = OFF_GV + GLA_WIDTH
OFF_DQ = OFF_GATE + GLA_WIDTH
OFF_DK = OFF_DQ + DIFF_LANES
OFF_DV = OFF_DK + DIFF_LANES
OFF_CB = OFF_DV + DIFF_LANES
OFF_CC = OFF_CB + CONV_DIM
OFF_CH = OFF_CC + CONV_DIM
N_PROJ = OFF_CH + CONV_DIM
GLR_LANE = GLA_DK
assert GLR_LANE + GLA_RANK <= HEAD_SLOT

FFN_ROWS = 1024
FFN_F_SPLITS = (0, 6 * MXU_DEPTH, D_FF)
assert all((hi - lo) % MXU_DEPTH == 0 for lo, hi in zip(FFN_F_SPLITS[:-1], FFN_F_SPLITS[1:]))
PROJ_ROWS = 1024
ATT_TILE = 256
ATT_UNROLL_FAR = 36
ATT_UNROLL_NEAR = 22
ATT_FINISH_GROUP = 4


def _dot(a, b):
    return jnp.dot(a, b, preferred_element_type=F32)


def _dot_nt(a, b):
    return lax.dot_general(a, b, (((1,), (1,)), ((), ())), preferred_element_type=F32)


def _dot_tn(a, b):
    return lax.dot_general(a, b, (((0,), (0,)), ((), ())), preferred_element_type=F32)


def _split_bf16(x):
    hi = x.astype(BF16)
    lo = (x - hi.astype(F32)).astype(BF16)
    return hi, lo


def _rms(x, g):
    return x * lax.rsqrt(jnp.mean(x * x, axis=-1, keepdims=True) + EPS) * g


def _sigmoid(x):
    return 1.0 / (1.0 + jnp.exp(-x))


def _ffn_kernel(*refs, final, mix, cast):
    refs = list(refs)
    x_ref, g_ref, wg_ref, wu_ref, wd_ref, fg_ref = refs[:6]
    mix_refs = refs[6:12] if mix else ()
    n_in = 6 + len(mix_refs)
    src_refs = refs[n_in:n_in + 3] if cast else ()
    o_ref = refs[n_in + len(src_refs)]
    dst_refs = refs[n_in + len(src_refs) + 1:]
    for src, dst in zip(src_refs, dst_refs):
        dst[...] = src[...].astype(BF16)

    x = x_ref[...]
    if mix:
        gla_ref, diff_ref, conv_ref, w1_ref, w2_ref, w3_ref = mix_refs
        x = (x + _dot(gla_ref[...], w1_ref[...]) + _dot(diff_ref[...], w2_ref[...])
             + _dot(conv_ref[...], w3_ref[...]))
    h = _rms(x, g_ref[...]).astype(BF16)
    acc = None
    for lo, hi in zip(FFN_F_SPLITS[:-1], FFN_F_SPLITS[1:]):
        sl = slice(lo, hi)
        gate = _dot(h, wg_ref[:, sl])
        up = _dot(h, wu_ref[:, sl])
        a = (gate * _sigmoid(gate) * up).astype(BF16)
        part = _dot(a, wd_ref[sl, :])
        acc = part if acc is None else acc + part
    y = x + 0.5 * acc
    if final:
        y = _rms(y, fg_ref[...])
    o_ref[...] = y


def _const_spec(shape):
    nd = len(shape)
    return pl.BlockSpec(shape, lambda *_: (0,) * nd, pipeline_mode=pl.Buffered(1))


def _layer_spec(stacked, layer):
    nd = stacked.ndim - 1
    return pl.BlockSpec((None,) + tuple(stacked.shape[1:]), lambda *_: (layer,) + (0,) * nd,
                        pipeline_mode=pl.Buffered(1))


def _cast_block(n_rows, steps):
    share = 1
    while (n_rows * share) % (steps * BF16_SUBLANES) or steps % share:
        share *= 2
        assert share <= steps
    return n_rows * share // steps, share


def _ffn(x, gains, layer, wg, wu, wd, final_g, final, mix=None, cast=None):
    t = x.shape[0]
    tm = min(FFN_ROWS, t)
    steps = t // tm
    row = lambda width: pl.BlockSpec((tm, width), lambda i: (i, 0))
    operands = [x, gains, wg, wu, wd, final_g]
    in_specs = [row(D_MODEL), _layer_spec(gains, layer), _const_spec((D_MODEL, D_FF)),
                _const_spec((D_MODEL, D_FF)), _const_spec((D_FF, D_MODEL)), _const_spec((1, D_MODEL))]
    out_shapes = [jax.ShapeDtypeStruct((t, D_MODEL), F32)]
    out_specs = [row(D_MODEL)]
    if mix is not None:
        operands += list(mix)
        in_specs += [row(a.shape[1]) for a in mix[:3]] + [_layer_spec(w, layer) for w in mix[3:]]
    if cast is not None:
        stacked, layer = cast
        for w in stacked:
            rows, share = _cast_block(w.shape[1], steps)
            operands.append(w)
            in_specs.append(pl.BlockSpec((None, rows, w.shape[2]),
                                         lambda i, share=share: (layer, i // share, 0)))
            out_shapes.append(jax.ShapeDtypeStruct(w.shape[1:], BF16))
            out_specs.append(pl.BlockSpec((rows, w.shape[2]), lambda i, share=share: (i // share, 0)))
    outs = pl.pallas_call(
        functools.partial(_ffn_kernel, final=final, mix=mix is not None, cast=cast is not None),
        out_shape=tuple(out_shapes),
        grid=(steps,),
        in_specs=in_specs,
        out_specs=tuple(out_specs),
        compiler_params=pltpu.CompilerParams(
            dimension_semantics=("arbitrary",), vmem_limit_bytes=VMEM_LIMIT_BYTES),
        name="ffn",
    )(*operands)
    return outs[0], tuple(outs[1:])


def _gla_cum_matrix():
    t = np.arange(CHUNK)[:, None]
    u = np.arange(CHUNK)[None, :]
    tri = (u <= t)
    ref = (u <= (t // SUB) * SUB - 1)
    ones = np.ones((CHUNK, CHUNK), bool)
    parts = [tri, ref, ones]
    for j in range(1, N_SUB):
        s = np.arange(SUB * j)[:, None]
        parts.append((u > s) & (u <= SUB * j - 1))
    return np.concatenate(parts, axis=0).astype(np.float32)


def _gla_score_mask():
    t = np.arange(CHUNK)[:, None]
    group = np.concatenate([np.full(SUB * j, j) for j in range(1, N_SUB)])
    keep = (group[None, :] == (t // SUB))
    return np.tile(keep, (1, GLA_HEADS)).astype(np.float32)


def _gla_head_ids():
    qk_head = lax.broadcasted_iota(jnp.int32, (1, GLA_QK_LANES), 1) // HEAD_SLOT
    v_head = lax.broadcasted_iota(jnp.int32, (1, GLA_WIDTH), 1) // GLA_DV
    row_head_v = lax.broadcasted_iota(jnp.int32, (GLA_WIDTH, 1), 0) // GLA_DV
    return qk_head, v_head, row_head_v == qk_head


def _gla_fast(q, k, g, v, state, tri):
    rows = q.shape[0]
    n_chunks = rows // CHUNK
    qk_head, v_head, state_mask = _gla_head_ids()
    chunk = lambda x, c: x[c * CHUNK:(c + 1) * CHUNK]
    sel_rows = pl.cdiv(n_chunks, BF16_SUBLANES) * BF16_SUBLANES
    sel_r = lax.broadcasted_iota(jnp.int32, (sel_rows, rows), 0)
    sel_c = lax.broadcasted_iota(jnp.int32, (sel_rows, rows), 1) // CHUNK
    chunk_sel = jnp.where(sel_r == sel_c, 1.0, 0.0).astype(BF16)
    g_hi, g_lo = _split_bf16(g)
    totals = _dot(chunk_sel, g_hi) + _dot(chunk_sel, g_lo)
    b = jnp.concatenate([_dot(tri, chunk(g_hi, c)) + _dot(tri, chunk(g_lo, c))
                         for c in range(n_chunks)], axis=0)
    b_last = jnp.concatenate([jnp.broadcast_to(totals[c:c + 1], (CHUNK, GLA_QK_LANES))
                              for c in range(n_chunks)], axis=0)
    qd = (q * jnp.exp(b)).astype(BF16)
    kd = (k * jnp.exp(-b)).astype(BF16)
    kw = (k * jnp.exp(b_last - b)).astype(BF16)
    t_idx = lax.broadcasted_iota(jnp.int32, (CHUNK, GLA_QK_LANES), 0)
    s_idx = lax.broadcasted_iota(jnp.int32, (CHUNK, GLA_QK_LANES), 1) % HEAD_SLOT
    causal = s_idx <= t_idx
    intra, update = [], []
    for c in range(n_chunks):
        kd_c, v_c = chunk(kd, c), chunk(v, c)
        kd_bd = jnp.concatenate(
            [jnp.where(qk_head == hd, kd_c, jnp.zeros_like(kd_c)) for hd in range(GLA_HEADS)],
            axis=0)
        a = _dot_nt(chunk(qd, c), kd_bd)
        a = jnp.where(causal, a, 0.0).astype(BF16)
        v_bd = jnp.concatenate(
            [jnp.where(v_head == hd, v_c, jnp.zeros_like(v_c)) for hd in range(GLA_HEADS)],
            axis=0)
        intra.append(_dot(a, v_bd))
        update.append(_dot_tn(v_c, chunk(kw, c)))
    raw = []
    for c in range(n_chunks):
        raw.append(intra[c] + _dot_nt(chunk(qd, c), state.astype(BF16)))
        state = state * jnp.exp(totals[c:c + 1]) + jnp.where(state_mask, update[c], 0.0)
    return jnp.concatenate(raw, axis=0), state


def _gla_general(q_ref, k_ref, g_ref, v_ref, s_ref, raw_ref, cum_ref, amask_ref):
    n_chunks = q_ref.shape[0] // CHUNK
    qk_head, v_head, state_mask = _gla_head_ids()
    cum = cum_ref[...]
    amask = amask_ref[...] > 0.5
    ind_r = lax.broadcasted_iota(jnp.int32, (GLA_QK_LANES, LANES), 0) // HEAD_SLOT
    ind_c = lax.broadcasted_iota(jnp.int32, (GLA_QK_LANES, LANES), 1)
    ind_sum = jnp.where(ind_r == ind_c, 1.0, 0.0).astype(BF16)
    bc_r = lax.broadcasted_iota(jnp.int32, (LANES, GLA_WIDTH), 0)
    bc_c = lax.broadcasted_iota(jnp.int32, (LANES, GLA_WIDTH), 1) // GLA_DV
    ind_bcast = jnp.where(bc_r == bc_c, 1.0, 0.0).astype(BF16)
    row_in_sub = lax.broadcasted_iota(jnp.int32, (CHUNK, 1), 0) % SUB

    def chunk_body(c, carry):
        r0 = pl.multiple_of(c * CHUNK, CHUNK)
        q = q_ref[pl.ds(r0, CHUNK), :]
        k = k_ref[pl.ds(r0, CHUNK), :]
        g = g_ref[pl.ds(r0, CHUNK), :]
        v = v_ref[pl.ds(r0, CHUNK), :]
        g_hi, g_lo = _split_bf16(g)
        cums = _dot(cum, g_hi) + _dot(cum, g_lo)
        b = cums[0:CHUNK]
        b_ref = cums[CHUNK:2 * CHUNK]
        b_last = cums[2 * CHUNK:3 * CHUNK]
        e_kst = cums[3 * CHUNK:3 * CHUNK + KST_ROWS]

        state = s_ref[...]
        q_in = (q * jnp.exp(b)).astype(BF16)
        o = _dot_nt(q_in, state.astype(BF16))

        kw = (k * jnp.exp(b_last - b)).astype(BF16)
        decay = jnp.exp(jnp.concatenate([b_last] * (GLA_WIDTH // CHUNK), axis=0))
        s_ref[...] = state * decay + jnp.where(state_mask, _dot_tn(v, kw), 0.0)

        q_sub = (q * jnp.exp(jnp.minimum(b - b_ref, 0.0))).astype(BF16)
        k_st = jnp.concatenate([k[0:SUB * j] for j in range(1, N_SUB)], axis=0) * jnp.exp(e_kst)
        k_bd = jnp.concatenate(
            [jnp.where(qk_head == hd, k_st, 0.0) for hd in range(GLA_HEADS)], axis=0).astype(BF16)
        a = _dot_nt(q_sub, k_bd)
        a = jnp.where(amask, a, 0.0).astype(BF16)
        v_st = jnp.concatenate([v[0:SUB * j] for j in range(1, N_SUB)], axis=0)
        v_bd = jnp.concatenate(
            [jnp.where(v_head == hd, v_st, jnp.zeros_like(v_st)) for hd in range(GLA_HEADS)], axis=0)
        o = o + _dot(a, v_bd)

        vf = v.astype(F32)
        prods = []
        for d in range(SUB):
            ks = k if d == 0 else pltpu.roll(k, d, 0)
            bs = b if d == 0 else pltpu.roll(b, d, 0)
            prods.append((q * ks * jnp.exp(jnp.minimum(b - bs, 0.0))).astype(BF16))
        dsum = _dot(jnp.concatenate(prods, axis=0), ind_sum)
        valid = jnp.concatenate([row_in_sub >= d for d in range(SUB)], axis=0)
        dsum = jnp.where(valid, dsum, 0.0).astype(BF16)
        dbc = _dot(dsum, ind_bcast)
        for d in range(SUB):
            vs = vf if d == 0 else pltpu.roll(vf, d, 0)
            o = o + dbc[d * CHUNK:(d + 1) * CHUNK] * vs
        raw_ref[pl.ds(r0, CHUNK), :] = o
        return carry

    lax.fori_loop(0, n_chunks, chunk_body, 0)


def _gla_finish(o, gate, gain):
    hr = lax.broadcasted_iota(jnp.int32, (GLA_WIDTH, GLA_WIDTH), 0) // GLA_DV
    hc = lax.broadcasted_iota(jnp.int32, (GLA_WIDTH, GLA_WIDTH), 1) // GLA_DV
    head_ones = jnp.where(hr == hc, 1.0, 0.0).astype(BF16)
    ms = _dot((o * o).astype(BF16), head_ones) * (1.0 / GLA_DV)
    return (o * lax.rsqrt(ms + EPS) * gain * (gate * _sigmoid(gate))).astype(BF16)


def _proj_kernel(x_ref, g_ref, w_ref, gkup_ref, gkb_ref, cw_ref, cum_ref, tri_ref, amask_ref, gn_ref,
                 gla_ref, dq_ref, dk_ref, dv_ref, conv_ref,
                 zbuf_ref, q_s, k_s, g_s, v_s, s_ref, s_next_ref, raw_ref, *, tiles_per_seq):
    tm = x_ref.shape[0]

    @pl.when(pl.program_id(0) % tiles_per_seq == 0)
    def _():
        zbuf_ref[0:F32_SUBLANES, :] = jnp.zeros((F32_SUBLANES, CONV_DIM), F32)
        s_ref[...] = jnp.zeros_like(s_ref)

    h = _rms(x_ref[...], g_ref[...]).astype(BF16)

    def proj(off, width):
        return _dot(h, w_ref[:, off:off + width])

    qk = proj(OFF_GQ, 2 * GLA_QK_LANES)
    gq = qk[:, 0:GLA_QK_LANES]
    gk = qk[:, GLA_QK_LANES:]
    vg = proj(OFF_GV, 2 * GLA_WIDTH)
    gv = vg[:, 0:GLA_WIDTH].astype(BF16)
    gate = vg[:, GLA_WIDTH:]

    glr_hi, glr_lo = _split_bf16(gq)
    up_hi = gkup_ref[0]
    up_lo = gkup_ref[1]
    z = _dot(glr_hi, up_hi) + _dot(glr_lo, up_hi) + _dot(glr_hi, up_lo) + gkb_ref[...]
    logsig = jnp.minimum(z, 0.0) - jnp.log1p(jnp.exp(-jnp.abs(z)))
    lane = lax.broadcasted_iota(jnp.int32, (tm, GLA_QK_LANES), 1)
    gg = jnp.where(lane % HEAD_SLOT < GLA_DK, logsig * (1.0 / GLA_GATE_NORM), 0.0)

    dqk = proj(OFF_DQ, 2 * DIFF_LANES)
    dq_ref[...] = (dqk[:, 0:DIFF_LANES] * (DIFF_DQK ** -0.5 * LOG2E)).astype(BF16)
    dk_ref[...] = dqk[:, DIFF_LANES:].astype(BF16)
    lane_v = lax.broadcasted_iota(jnp.int32, (tm, DIFF_LANES), 1)
    dv_ref[...] = jnp.where(lane_v % LANES == DIFF_DV, 1.0, proj(OFF_DV, DIFF_LANES)).astype(BF16)

    conv = proj(OFF_CB, 3 * CONV_DIM)
    zc = conv[:, CONV_DIM:2 * CONV_DIM] * conv[:, 2 * CONV_DIM:]
    zbuf_ref[F32_SUBLANES:F32_SUBLANES + tm, :] = zc
    cw = cw_ref[...]
    y = (cw[2:3, :] * zc + cw[1:2, :] * zbuf_ref[F32_SUBLANES - 1:F32_SUBLANES - 1 + tm, :]
         + cw[0:1, :] * zbuf_ref[F32_SUBLANES - 2:F32_SUBLANES - 2 + tm, :])
    conv_ref[...] = (conv[:, 0:CONV_DIM] * y).astype(BF16)
    zbuf_ref[0:F32_SUBLANES, :] = zbuf_ref[tm:tm + F32_SUBLANES, :]

    q_scaled = gq * (GLA_DK ** -0.5)
    raw, new_state = _gla_fast(q_scaled, gk, gg, gv, s_ref[...], tri_ref[...])
    gla_ref[...] = _gla_finish(raw, gate, gn_ref[...])
    s_next_ref[...] = new_state
    q_s[...] = q_scaled
    k_s[...] = gk
    g_s[...] = gg
    v_s[...] = gv
    raw_ref[...] = gate
    safe = jnp.min(gg) * CHUNK >= -GLA_SAFE_DECAY

    @pl.when(safe)
    def _():
        s_ref[...] = s_next_ref[...]

    @pl.when(jnp.logical_not(safe))
    def _():
        gate_kept = raw_ref[...]
        _gla_general(q_s, k_s, g_s, v_s, s_ref, raw_ref, cum_ref, amask_ref)
        gla_ref[...] = _gla_finish(raw_ref[...], gate_kept, gn_ref[...])


def _proj(x, layer, g, w, gkup, gkb, cw, gn_tiled, seq):
    t = x.shape[0]
    tm = min(PROJ_ROWS, seq)
    row = lambda width: pl.BlockSpec((tm, width), lambda i: (i, 0))
    cum_np = _gla_cum_matrix()
    cum = jnp.asarray(cum_np, BF16)
    tri = jnp.asarray(cum_np[0:CHUNK], BF16)
    amask = jnp.asarray(_gla_score_mask(), F32)
    out_shapes = (
        jax.ShapeDtypeStruct((t, GLA_WIDTH), BF16),
        jax.ShapeDtypeStruct((t, DIFF_LANES), BF16),
        jax.ShapeDtypeStruct((t, DIFF_LANES), BF16),
        jax.ShapeDtypeStruct((t, DIFF_LANES), BF16),
        jax.ShapeDtypeStruct((t, CONV_DIM), BF16),
    )
    return pl.pallas_call(
        functools.partial(_proj_kernel, tiles_per_seq=seq // tm),
        out_shape=out_shapes,
        grid=(t // tm,),
        in_specs=[
            row(D_MODEL),
            _layer_spec(g, layer),
            _layer_spec(w, layer),
            _layer_spec(gkup, layer),
            _layer_spec(gkb, layer),
            _layer_spec(cw, layer),
            _const_spec(cum.shape),
            _const_spec(tri.shape),
            _const_spec(amask.shape),
            _layer_spec(gn_tiled, layer),
        ],
        out_specs=tuple(row(s.shape[1]) for s in out_shapes),
        scratch_shapes=[pltpu.VMEM((tm + F32_SUBLANES, CONV_DIM), F32),
                        pltpu.VMEM((tm, GLA_QK_LANES), F32),
                        pltpu.VMEM((tm, GLA_QK_LANES), F32),
                        pltpu.VMEM((tm, GLA_QK_LANES), F32),
                        pltpu.VMEM((tm, GLA_WIDTH), BF16),
                        pltpu.VMEM((GLA_WIDTH, GLA_QK_LANES), F32),
                        pltpu.VMEM((GLA_WIDTH, GLA_QK_LANES), F32),
                        pltpu.VMEM((tm, GLA_WIDTH), F32)],
        compiler_params=pltpu.CompilerParams(
            dimension_semantics=("arbitrary",), vmem_limit_bytes=VMEM_LIMIT_BYTES),
        name="mixer_proj_gla",
    )(x, g, w, gkup, gkb, cw, cum, tri, amask, gn_tiled)


def _t5_bucket(rel):
    nb = NUM_BUCKETS // 2
    max_exact = nb // 2
    ret = (rel > 0).astype(jnp.int32) * nb
    n = jnp.abs(rel)
    nf = jnp.maximum(n, 1).astype(jnp.float32)
    large = max_exact + (jnp.log(nf / max_exact) / math.log(MAX_DISTANCE / max_exact)
                         * (nb - max_exact)).astype(jnp.int32)
    large = jnp.minimum(large, nb - 1)
    return ret + jnp.where(n < max_exact, n, large)


def _bias_tiles(rel_bias, tile):
    assert tile >= MAX_DISTANCE and tile % CHUNK == 0
    table = rel_bias.astype(F32)
    far = table[NUM_BUCKETS // 2 - 1]
    heads = table.shape[1]
    rel = jnp.concatenate([jnp.arange(0, tile), jnp.arange(-2 * tile, 0)])
    onehot = _t5_bucket(rel)[None, :, None] == jnp.arange(NUM_BUCKETS)[None, None, :]
    per_rel = jnp.sum(jnp.where(onehot, table.T[:, None, :], 0.0), axis=-1)
    per_rel = (per_rel - far[:, None]) * LOG2E
    span = 3 * tile
    flat = jnp.tile(per_rel, (1, 2 * tile))[:, :2 * tile * (span - 1)]
    toep = flat.reshape(heads, 2 * tile, span - 1)[:, :, :tile]
    r = jnp.arange(tile)[:, None]
    c = jnp.arange(tile)[None, :]
    diag = jnp.where(((c // CHUNK) <= (r // CHUNK))[None], toep[:, :tile], MASK_VALUE)
    left = toep[:, tile:]
    return jnp.stack([left, diag], axis=1)


def _attn_kernel(q_ref, k_ref, v_ref, bias_ref, lamv_ref, subln_ref, o_ref,
                 s_buf, p_buf, alpha_buf, m_all, acc_all, *, tile, lam_init):
    seq = q_ref.shape[0]
    nq = seq // tile

    s_buf[...] = jnp.zeros(s_buf.shape, F32)
    p_buf[...] = jnp.zeros(p_buf.shape, BF16)
    alpha_buf[...] = jnp.zeros(alpha_buf.shape, F32)
    m_all[...] = jnp.full(m_all.shape, MASK_VALUE, F32)
    acc_all[...] = jnp.zeros(acc_all.shape, F32)

    lane = lax.broadcasted_iota(jnp.int32, (tile, LANES), 1)

    def stage1(i, j, slot):
        q = q_ref[pl.ds(pl.multiple_of(i * tile, tile), tile), :]
        zero = jnp.zeros_like(q)
        qs = jnp.concatenate([jnp.where(lane < LANES // 2, q, zero),
                              jnp.where(lane >= LANES // 2, q, zero)], axis=0)
        kj = k_ref[pl.ds(pl.multiple_of(j * tile, tile), tile), :]
        s_buf[slot] = _dot_nt(qs, kj)

    def stage2(i, bias, slot):
        for half in range(2):
            rows = slice(half * tile, (half + 1) * tile)
            s = s_buf[slot, rows, :]
            if bias is not None:
                s = s + bias
            m_prev = m_all[i, rows, :]
            m_next = jnp.maximum(m_prev, jnp.max(s, axis=-1, keepdims=True))
            p = jnp.exp2(s - jnp.concatenate([m_next] * (tile // LANES), axis=1))
            p_buf[slot, rows, :] = p.astype(BF16)
            alpha_buf[slot, rows, :] = jnp.exp2(m_prev - m_next)
            m_all[i, rows, :] = m_next

    def stage3(i, j, slot):
        vj = v_ref[pl.ds(pl.multiple_of(j * tile, tile), tile), :]
        acc_all[i] = alpha_buf[slot] * acc_all[i] + _dot(p_buf[slot], vj)

    def run_pipeline(n_pairs, first_pair, advance, with_bias, unroll):
        n_steps = pl.cdiv(n_pairs + 2, unroll) * unroll

        def step(t, pairs, slot_a, slot_b):
            (i0, j0), (i1, j1), (i2, j2) = pairs
            valid1 = jnp.logical_and(t >= 1, t <= n_pairs)
            valid2 = jnp.logical_and(t >= 2, t <= n_pairs + 1)
            stage3(jnp.where(valid2, i2, nq), j2, slot_a)
            stage2(jnp.where(valid1, i1, nq), bias_ref[0, j1 - i1 + 1] if with_bias else None, slot_b)
            stage1(i0, j0, slot_a)
            return (advance(i0, j0), (i0, j0), (i1, j1))

        def body(u, pairs):
            for r in range(unroll):
                pairs = step(unroll * u + r, pairs, r % 2, (r + 1) % 2)
            return pairs

        lax.fori_loop(0, n_steps // unroll, body, (first_pair,) * 3)

    def next_far(i, j):
        wrap = j == i - 2
        done = jnp.logical_and(wrap, i == nq - 1)
        step_i = jnp.logical_and(wrap, jnp.logical_not(done))
        return (jnp.where(step_i, i + 1, i), jnp.where(done, j, jnp.where(wrap, 0, j + 1)))

    def next_near(i, j):
        wrap = j == i
        done = jnp.logical_and(wrap, i == nq - 1)
        step_i = jnp.logical_and(wrap, jnp.logical_not(done))
        return (jnp.where(step_i, i + 1, i), jnp.where(jnp.logical_or(done, wrap), j, j + 1))

    zero = jnp.int32(0)
    if nq > 2:
        run_pipeline((nq - 1) * (nq - 2) // 2, (jnp.int32(2), zero), next_far, False, ATT_UNROLL_FAR)
    run_pipeline(2 * nq - 1, (zero, zero), next_near, True, ATT_UNROLL_NEAR)

    lv = lamv_ref[...]
    lam = (jnp.exp(jnp.sum(lv[0:1] * lv[1:2], axis=-1, keepdims=True))
           - jnp.exp(jnp.sum(lv[2:3] * lv[3:4], axis=-1, keepdims=True)) + lam_init)

    def finish(i):
        acc = acc_all[i]
        a1 = acc[0:tile]
        a2 = acc[tile:2 * tile]
        ones_col = lane == DIFF_DV
        ratio = jnp.sum(jnp.where(ones_col, a1 / jnp.where(ones_col, a2, 1.0), 0.0),
                        axis=-1, keepdims=True)
        u = a1 - lam * (ratio * a2)
        sq = jnp.where(ones_col, a1 * math.sqrt(DIFF_DV * EPS), u)
        ms = jnp.sum(sq * sq, axis=-1, keepdims=True) * (1.0 / DIFF_DV)
        y = u * lax.rsqrt(ms) * subln_ref[...] * (1.0 - lam_init)
        o_ref[pl.ds(pl.multiple_of(i * tile, tile), tile), :] = y.astype(BF16)

    group = math.gcd(nq, ATT_FINISH_GROUP)

    def finish_group(u, carry):
        for r in range(group):
            finish(u * group + r)
        return carry

    lax.fori_loop(0, nq // group, finish_group, 0)


def _attn(dq, dk, dv, bias_tiles, layer, lamv, subln, batch, seq, lam_init):
    tile = min(ATT_TILE, seq)
    nq = seq // tile
    head_block = pl.BlockSpec((seq, LANES), lambda b, h: (b, h))
    return pl.pallas_call(
        functools.partial(_attn_kernel, tile=tile, lam_init=lam_init),
        out_shape=jax.ShapeDtypeStruct((batch * seq, DIFF_LANES), BF16),
        grid=(batch, DIFF_HEADS),
        in_specs=[
            head_block, head_block, head_block,
            pl.BlockSpec((1, 2, tile, tile), lambda b, h: (h, 0, 0, 0)),
            pl.BlockSpec((None, 4, LANES), lambda b, h: (layer, 0, 0)),
            pl.BlockSpec((None, 1, LANES), lambda b, h: (layer, 0, 0)),
        ],
        out_specs=head_block,
        scratch_shapes=[pltpu.VMEM((2, 2 * tile, tile), F32),
                        pltpu.VMEM((2, 2 * tile, tile), BF16),
                        pltpu.VMEM((2, 2 * tile, LANES), F32),
                        pltpu.VMEM((nq + 1, 2 * tile, LANES), F32),
                        pltpu.VMEM((nq + 1, 2 * tile, LANES), F32)],
        compiler_params=pltpu.CompilerParams(
            dimension_semantics=("parallel", "parallel"), vmem_limit_bytes=VMEM_LIMIT_BYTES),
        name="diff_attn",
    )(dq, dk, dv, bias_tiles, lamv, subln)


def _pad_last(w, width):
    return jnp.pad(w, [(0, 0)] * (w.ndim - 1) + [(0, width - w.shape[-1])])


def _slot_cols(w, heads, used, slot):
    lead = w.shape[:-1]
    w = _pad_last(w.reshape(lead + (heads, used)), slot)
    return w.reshape(lead + (heads * slot,))


def _layout_w_in(w_in):
    offs = np.cumsum([0, GLA_HEADS * GLA_DK, GLA_HEADS * GLA_DK, GLA_WIDTH, GLA_RANK, GLA_WIDTH,
                      DIFF_HEADS * 2 * DIFF_DQK, DIFF_HEADS * 2 * DIFF_DQK, DIFF_WIDTH,
                      CONV_DIM, CONV_DIM, CONV_DIM])
    w_in = w_in.astype(BF16)
    gq, gk, gv, glr, gate, dq, dk, dv, cb, cc, ch = [w_in[..., offs[n]:offs[n + 1]] for n in range(11)]
    zeros = lambda n: jnp.zeros(w_in.shape[:-1] + (n,), BF16)

    def slots(w, heads, used, slot, first_pad=None):
        out = []
        for hd in range(heads):
            out.append(w[..., hd * used:(hd + 1) * used])
            out.append(first_pad if (hd == 0 and first_pad is not None) else zeros(slot - used))
        return out

    half = LANES // 2
    assert GLR_LANE + GLA_RANK == HEAD_SLOT
    pieces = (slots(gq, GLA_HEADS, GLA_DK, HEAD_SLOT, first_pad=glr)
              + slots(gk, GLA_HEADS, GLA_DK, HEAD_SLOT)
              + [gv, gate]
              + slots(dq, 2 * DIFF_HEADS, DIFF_DQK, half)
              + slots(dk, 2 * DIFF_HEADS, DIFF_DQK, half)
              + slots(dv, DIFF_HEADS, DIFF_DV, LANES)
              + [cb, cc, ch])
    w = jnp.concatenate(pieces, axis=-1)
    assert w.shape[-1] == N_PROJ
    return w


def kernel(x, ffn1_norm, ffn1_gate, ffn1_up, ffn1_down, mix_norm, w_in, gla_gk_up, gla_gk_bias,
           gla_norm, diff_lambda_q1, diff_lambda_k1, diff_lambda_q2, diff_lambda_k2, diff_subln,
           rel_bias, conv_w, w_out, ffn2_norm, ffn2_gate, ffn2_up, ffn2_down, final_norm):
    batch, seq, _ = x.shape
    depth = w_in.shape[0]
    t = batch * seq
    xf = x.reshape(t, D_MODEL)
    bias_tiles = _bias_tiles(rel_bias, min(ATT_TILE, seq))
    final_g = final_norm.reshape(1, D_MODEL)
    ffn1_w = (ffn1_gate[0].astype(BF16), ffn1_up[0].astype(BF16), ffn1_down[0].astype(BF16))

    ffn1_g = ffn1_norm.reshape(depth, 1, D_MODEL)
    ffn2_g = ffn2_norm.reshape(depth, 1, D_MODEL)
    mix_g = mix_norm.reshape(depth, 1, D_MODEL)
    w_pad = _layout_w_in(w_in)
    up = _slot_cols(gla_gk_up, GLA_HEADS, GLA_DK, HEAD_SLOT)
    up = jnp.pad(up, ((0, 0), (GLR_LANE, GLA_QK_LANES - GLR_LANE - GLA_RANK), (0, 0)))
    up_hi = up.astype(BF16)
    up_lo = (up - up_hi.astype(F32)).astype(BF16)
    gkup = jnp.stack([up_hi, up_lo], axis=1)
    gkb = _slot_cols(gla_gk_bias, GLA_HEADS, GLA_DK, HEAD_SLOT)[:, None, :]
    cw = jnp.pad(conv_w, ((0, 0), (0, F32_SUBLANES - CONV_WIDTH), (0, 0)))
    gn_tiled = jnp.tile(gla_norm, (1, GLA_HEADS))[:, None, :]
    lamv = jnp.stack([diff_lambda_q1, diff_lambda_k1, diff_lambda_q2, diff_lambda_k2], axis=1)
    lamv = _pad_last(lamv.astype(F32), LANES)
    subln = _pad_last(diff_subln[:, None, :], LANES)
    w1 = w_out[:, 0:GLA_WIDTH].astype(BF16)
    w2 = w_out[:, GLA_WIDTH:GLA_WIDTH + DIFF_WIDTH].reshape(depth, DIFF_HEADS, DIFF_DV, D_MODEL)
    w2 = jnp.pad(w2, ((0, 0), (0, 0), (0, LANES - DIFF_DV), (0, 0)))
    w2 = w2.reshape(depth, DIFF_LANES, D_MODEL).astype(BF16)
    w3 = w_out[:, GLA_WIDTH + DIFF_WIDTH:].astype(BF16)

    for l in range(depth):
        lam_init = 0.8 - 0.6 * math.exp(-0.3 * l)
        xf, ffn2_w = _ffn(xf, ffn1_g, l, *ffn1_w, final_g, False,
                          cast=((ffn2_gate, ffn2_up, ffn2_down), l))
        gla_o, dq, dk, dv, conv_o = _proj(xf, l, mix_g, w_pad, gkup, gkb, cw, gn_tiled, seq)
        diff_o = _attn(dq, dk, dv, bias_tiles, l, lamv, subln, batch, seq, lam_init)
        last = l == depth - 1
        xf, ffn1_w = _ffn(xf, ffn2_g, l, *ffn2_w, final_g, last,
                          mix=(gla_o, diff_o, conv_o, w1, w2, w3),
                          cast=None if last else ((ffn1_gate, ffn1_up, ffn1_down), l + 1))

    return xf.reshape(batch, seq, D_MODEL)
```

```python
import functools
import math

import numpy as np
import jax
import jax.numpy as jnp
from jax import lax
from jax.experimental import pallas as pl
from jax.experimental.pallas import tpu as pltpu

F32 = jnp.float32
BF16 = jnp.bfloat16

D_MODEL = 1024
D_FF = 2816
EPS = 1e-6
CHUNK = 64
GLA_HEADS = 4
GLA_DK = 48
GLA_DV = 96
GLA_RANK = 16
GLA_GATE_NORM = 16.0
DIFF_HEADS = 4
DIFF_DQK = 48
DIFF_DV = 96
CONV_DIM = 256
CONV_WIDTH = 3
NUM_BUCKETS = 32
MAX_DISTANCE = 128
GLA_WIDTH = GLA_HEADS * GLA_DV
DIFF_WIDTH = DIFF_HEADS * DIFF_DV

LANES = 128
F32_SUBLANES = 8
BF16_SUBLANES = 16
MXU_DEPTH = 256
VMEM_LIMIT_BYTES = 56 * 1024 * 1024

HEAD_SLOT = 64
GLA_QK_LANES = GLA_HEADS * HEAD_SLOT
SUB = 16
N_SUB = CHUNK // SUB
KST_ROWS = SUB * (N_SUB * (N_SUB - 1) // 2)
DIFF_LANES = DIFF_HEADS * LANES
MASK_VALUE = -1e30
GLA_SAFE_DECAY = 60.0
LOG2E = math.log2(math.e)

OFF_GQ = 0
OFF_GK = OFF_GQ + GLA_QK_LANES
OFF_GV = OFF_GK + GLA_QK_LANES
OFF_GATE = OFF_GV + GLA_WIDTH
OFF_DQ = OFF_GATE + GLA_WIDTH
OFF_DK = OFF_DQ + DIFF_LANES
OFF_DV = OFF_DK + DIFF_LANES
OFF_CB = OFF_DV + DIFF_LANES
OFF_CC = OFF_CB + CONV_DIM
OFF_CH = OFF_CC + CONV_DIM
N_PROJ = OFF_CH + CONV_DIM
GLR_LANE = GLA_DK
assert GLR_LANE + GLA_RANK <= HEAD_SLOT

FFN_ROWS = 1024
FFN_SUBTILES = 2
FFN_F_SPLITS = (0, 6 * MXU_DEPTH, D_FF)
assert all((hi - lo) % MXU_DEPTH == 0 for lo, hi in zip(FFN_F_SPLITS[:-1], FFN_F_SPLITS[1:]))
PROJ_ROWS = 1024
ATT_TILE = 256
ATT_UNROLL_FAR = 36
ATT_UNROLL_NEAR = 22
ATT_FINISH_GROUP = 4


def _dot(a, b):
    return jnp.dot(a, b, preferred_element_type=F32)


def _dot_nt(a, b):
    return lax.dot_general(a, b, (((1,), (1,)), ((), ())), preferred_element_type=F32)


def _dot_tn(a, b):
    return lax.dot_general(a, b, (((0,), (0,)), ((), ())), preferred_element_type=F32)


def _split_bf16(x):
    hi = x.astype(BF16)
    lo = (x - hi.astype(F32)).astype(BF16)
    return hi, lo


def _rms(x, g):
    return x * lax.rsqrt(jnp.mean(x * x, axis=-1, keepdims=True) + EPS) * g


def _sigmoid(x):
    return 1.0 / (1.0 + jnp.exp(-x))


def _ffn_kernel(*refs, final, mix, cast):
    refs = list(refs)
    x_ref, g_ref, wg_ref, wu_ref, wd_ref, fg_ref = refs[:6]
    mix_refs = refs[6:12] if mix else ()
    n_in = 6 + len(mix_refs)
    src_refs = refs[n_in:n_in + 3] if cast else ()
    o_ref = refs[n_in + len(src_refs)]
    dst_refs = refs[n_in + len(src_refs) + 1:]
    for src, dst in zip(src_refs, dst_refs):
        dst[...] = src[...].astype(BF16)

    sub = x_ref.shape[0] // FFN_SUBTILES
    for r in range(FFN_SUBTILES):
        rows = slice(r * sub, (r + 1) * sub)
        x = x_ref[rows, :]
        if mix:
            gla_ref, diff_ref, conv_ref, w1_ref, w2_ref, w3_ref = mix_refs
            x = (x + _dot(gla_ref[rows, :], w1_ref[...]) + _dot(diff_ref[rows, :], w2_ref[...])
                 + _dot(conv_ref[rows, :], w3_ref[...]))
        h = _rms(x, g_ref[...]).astype(BF16)
        acc = None
        for lo, hi in zip(FFN_F_SPLITS[:-1], FFN_F_SPLITS[1:]):
            sl = slice(lo, hi)
            gate = _dot(h, wg_ref[:, sl])
            up = _dot(h, wu_ref[:, sl])
            a = (gate * _sigmoid(gate) * up).astype(BF16)
            part = _dot(a, wd_ref[sl, :])
            acc = part if acc is None else acc + part
        y = x + 0.5 * acc
        if final:
            y = _rms(y, fg_ref[...])
        o_ref[rows, :] = y


def _const_spec(shape):
    nd = len(shape)
    return pl.BlockSpec(shape, lambda *_: (0,) * nd, pipeline_mode=pl.Buffered(1))


def _layer_spec(stacked, layer):
    nd = stacked.ndim - 1
    return pl.BlockSpec((None,) + tuple(stacked.shape[1:]), lambda *_: (layer,) + (0,) * nd,
                        pipeline_mode=pl.Buffered(1))


def _cast_block(n_rows, steps):
    share = 1
    while (n_rows * share) % (steps * BF16_SUBLANES) or steps % share:
        share *= 2
        assert share <= steps
    return n_rows * share // steps, share


def _ffn(x, gains, layer, wg, wu, wd, final_g, final, mix=None, cast=None):
    t = x.shape[0]
    tm = min(FFN_ROWS, t)
    steps = t // tm
    row = lambda width: pl.BlockSpec((tm, width), lambda i: (i, 0))
    operands = [x, gains, wg, wu, wd, final_g]
    in_specs = [row(D_MODEL), _layer_spec(gains, layer), _const_spec((D_MODEL, D_FF)),
                _const_spec((D_MODEL, D_FF)), _const_spec((D_FF, D_MODEL)), _const_spec((1, D_MODEL))]
    out_shapes = [jax.ShapeDtypeStruct((t, D_MODEL), F32)]
    out_specs = [row(D_MODEL)]
    if mix is not None:
        operands += list(mix)
        in_specs += [row(a.shape[1]) for a in mix[:3]] + [_layer_spec(w, layer) for w in mix[3:]]
    if cast is not None:
        stacked, layer = cast
        for w in stacked:
            rows, share = _cast_block(w.shape[1], steps)
            operands.append(w)
            in_specs.append(pl.BlockSpec((None, rows, w.shape[2]),
                                         lambda i, share=share: (layer, i // share, 0)))
            out_shapes.append(jax.ShapeDtypeStruct(w.shape[1:], BF16))
            out_specs.append(pl.BlockSpec((rows, w.shape[2]), lambda i, share=share: (i // share, 0)))
    outs = pl.pallas_call(
        functools.partial(_ffn_kernel, final=final, mix=mix is not None, cast=cast is not None),
        out_shape=tuple(out_shapes),
        grid=(steps,),
        in_specs=in_specs,
        out_specs=tuple(out_specs),
        compiler_params=pltpu.CompilerParams(
            dimension_semantics=("arbitrary",), vmem_limit_bytes=VMEM_LIMIT_BYTES),
        name="ffn",
    )(*operands)
    return outs[0], tuple(outs[1:])


def _gla_cum_matrix():
    t = np.arange(CHUNK)[:, None]
    u = np.arange(CHUNK)[None, :]
    tri = (u <= t)
    ref = (u <= (t // SUB) * SUB - 1)
    ones = np.ones((CHUNK, CHUNK), bool)
    parts = [tri, ref, ones]
    for j in range(1, N_SUB):
        s = np.arange(SUB * j)[:, None]
        parts.append((u > s) & (u <= SUB * j - 1))
    return np.concatenate(parts, axis=0).astype(np.float32)


def _gla_score_mask():
    t = np.arange(CHUNK)[:, None]
    group = np.concatenate([np.full(SUB * j, j) for j in range(1, N_SUB)])
    keep = (group[None, :] == (t // SUB))
    return np.tile(keep, (1, GLA_HEADS)).astype(np.float32)


def _gla_head_ids():
    qk_head = lax.broadcasted_iota(jnp.int32, (1, GLA_QK_LANES), 1) // HEAD_SLOT
    v_head = lax.broadcasted_iota(jnp.int32, (1, GLA_WIDTH), 1) // GLA_DV
    row_head_v = lax.broadcasted_iota(jnp.int32, (GLA_WIDTH, 1), 0) // GLA_DV
    return qk_head, v_head, row_head_v == qk_head


def _gla_fast(q, k, g, v, state, tri):
    rows = q.shape[0]
    n_chunks = rows // CHUNK
    qk_head, v_head, state_mask = _gla_head_ids()
    chunk = lambda x, c: x[c * CHUNK:(c + 1) * CHUNK]
    sel_rows = pl.cdiv(n_chunks, BF16_SUBLANES) * BF16_SUBLANES
    sel_r = lax.broadcasted_iota(jnp.int32, (sel_rows, rows), 0)
    sel_c = lax.broadcasted_iota(jnp.int32, (sel_rows, rows), 1) // CHUNK
    chunk_sel = jnp.where(sel_r == sel_c, 1.0, 0.0).astype(BF16)
    g_hi, g_lo = _split_bf16(g)
    totals = _dot(chunk_sel, g_hi) + _dot(chunk_sel, g_lo)
    b = jnp.concatenate([_dot(tri, chunk(g_hi, c)) + _dot(tri, chunk(g_lo, c))
                         for c in range(n_chunks)], axis=0)
    b_last = jnp.concatenate([jnp.broadcast_to(totals[c:c + 1], (CHUNK, GLA_QK_LANES))
                              for c in range(n_chunks)], axis=0)
    qd = (q * jnp.exp(b)).astype(BF16)
    kd = (k * jnp.exp(-b)).astype(BF16)
    kw = (k * jnp.exp(b_last - b)).astype(BF16)
    t_idx = lax.broadcasted_iota(jnp.int32, (CHUNK, GLA_QK_LANES), 0)
    s_idx = lax.broadcasted_iota(jnp.int32, (CHUNK, GLA_QK_LANES), 1) % HEAD_SLOT
    causal = s_idx <= t_idx
    intra, update = [], []
    for c in range(n_chunks):
        kd_c, v_c = chunk(kd, c), chunk(v, c)
        kd_bd = jnp.concatenate(
            [jnp.where(qk_head == hd, kd_c, jnp.zeros_like(kd_c)) for hd in range(GLA_HEADS)],
            axis=0)
        a = _dot_nt(chunk(qd, c), kd_bd)
        a = jnp.where(causal, a, 0.0).astype(BF16)
        v_bd = jnp.concatenate(
            [jnp.where(v_head == hd, v_c, jnp.zeros_like(v_c)) for hd in range(GLA_HEADS)],
            axis=0)
        intra.append(_dot(a, v_bd))
        update.append(_dot_tn(v_c, chunk(kw, c)))
    raw = []
    for c in range(n_chunks):
        raw.append(intra[c] + _dot_nt(chunk(qd, c), state.astype(BF16)))
        state = state * jnp.exp(totals[c:c + 1]) + jnp.where(state_mask, update[c], 0.0)
    return jnp.concatenate(raw, axis=0), state


def _gla_general(q_ref, k_ref, g_ref, v_ref, s_ref, raw_ref, cum_ref, amask_ref):
    n_chunks = q_ref.shape[0] // CHUNK
    qk_head, v_head, state_mask = _gla_head_ids()
    cum = cum_ref[...]
    amask = amask_ref[...] > 0.5
    ind_r = lax.broadcasted_iota(jnp.int32, (GLA_QK_LANES, LANES), 0) // HEAD_SLOT
    ind_c = lax.broadcasted_iota(jnp.int32, (GLA_QK_LANES, LANES), 1)
    ind_sum = jnp.where(ind_r == ind_c, 1.0, 0.0).astype(BF16)
    bc_r = lax.broadcasted_iota(jnp.int32, (LANES, GLA_WIDTH), 0)
    bc_c = lax.broadcasted_iota(jnp.int32, (LANES, GLA_WIDTH), 1) // GLA_DV
    ind_bcast = jnp.where(bc_r == bc_c, 1.0, 0.0).astype(BF16)
    row_in_sub = lax.broadcasted_iota(jnp.int32, (CHUNK, 1), 0) % SUB

    def chunk_body(c, carry):
        r0 = pl.multiple_of(c * CHUNK, CHUNK)
        q = q_ref[pl.ds(r0, CHUNK), :]
        k = k_ref[pl.ds(r0, CHUNK), :]
        g = g_ref[pl.ds(r0, CHUNK), :]
        v = v_ref[pl.ds(r0, CHUNK), :]
        g_hi, g_lo = _split_bf16(g)
        cums = _dot(cum, g_hi) + _dot(cum, g_lo)
        b = cums[0:CHUNK]
        b_ref = cums[CHUNK:2 * CHUNK]
        b_last = cums[2 * CHUNK:3 * CHUNK]
        e_kst = cums[3 * CHUNK:3 * CHUNK + KST_ROWS]

        state = s_ref[...]
        q_in = (q * jnp.exp(b)).astype(BF16)
        o = _dot_nt(q_in, state.astype(BF16))

        kw = (k * jnp.exp(b_last - b)).astype(BF16)
        decay = jnp.exp(jnp.concatenate([b_last] * (GLA_WIDTH // CHUNK), axis=0))
        s_ref[...] = state * decay + jnp.where(state_mask, _dot_tn(v, kw), 0.0)

        q_sub = (q * jnp.exp(jnp.minimum(b - b_ref, 0.0))).astype(BF16)
        k_st = jnp.concatenate([k[0:SUB * j] for j in range(1, N_SUB)], axis=0) * jnp.exp(e_kst)
        k_bd = jnp.concatenate(
            [jnp.where(qk_head == hd, k_st, 0.0) for hd in range(GLA_HEADS)], axis=0).astype(BF16)
        a = _dot_nt(q_sub, k_bd)
        a = jnp.where(amask, a, 0.0).astype(BF16)
        v_st = jnp.concatenate([v[0:SUB * j] for j in range(1, N_SUB)], axis=0)
        v_bd = jnp.concatenate(
            [jnp.where(v_head == hd, v_st, jnp.zeros_like(v_st)) for hd in range(GLA_HEADS)], axis=0)
        o = o + _dot(a, v_bd)

        vf = v.astype(F32)
        prods = []
        for d in range(SUB):
            ks = k if d == 0 else pltpu.roll(k, d, 0)
            bs = b if d == 0 else pltpu.roll(b, d, 0)
            prods.append((q * ks * jnp.exp(jnp.minimum(b - bs, 0.0))).astype(BF16))
        dsum = _dot(jnp.concatenate(prods, axis=0), ind_sum)
        valid = jnp.concatenate([row_in_sub >= d for d in range(SUB)], axis=0)
        dsum = jnp.where(valid, dsum, 0.0).astype(BF16)
        dbc = _dot(dsum, ind_bcast)
        for d in range(SUB):
            vs = vf if d == 0 else pltpu.roll(vf, d, 0)
            o = o + dbc[d * CHUNK:(d + 1) * CHUNK] * vs
        raw_ref[pl.ds(r0, CHUNK), :] = o
        return carry

    lax.fori_loop(0, n_chunks, chunk_body, 0)


def _gla_finish(o, gate, gain):
    hr = lax.broadcasted_iota(jnp.int32, (GLA_WIDTH, GLA_WIDTH), 0) // GLA_DV
    hc = lax.broadcasted_iota(jnp.int32, (GLA_WIDTH, GLA_WIDTH), 1) // GLA_DV
    head_ones = jnp.where(hr == hc, 1.0, 0.0).astype(BF16)
    ms = _dot((o * o).astype(BF16), head_ones) * (1.0 / GLA_DV)
    return (o * lax.rsqrt(ms + EPS) * gain * (gate * _sigmoid(gate))).astype(BF16)


def _proj_kernel(x_ref, g_ref, w_ref, gkup_ref, gkb_ref, cw_ref, cum_ref, tri_ref, amask_ref, gn_ref,
                 gla_ref, dq_ref, dk_ref, dv_ref, conv_ref,
                 zbuf_ref, q_s, k_s, g_s, v_s, s_ref, s_next_ref, raw_ref, *, tiles_per_seq):
    tm = x_ref.shape[0]

    @pl.when(pl.program_id(0) % tiles_per_seq == 0)
    def _():
        zbuf_ref[0:F32_SUBLANES, :] = jnp.zeros((F32_SUBLANES, CONV_DIM), F32)
        s_ref[...] = jnp.zeros_like(s_ref)

    h = _rms(x_ref[...], g_ref[...]).astype(BF16)

    def proj(off, width):
        return _dot(h, w_ref[:, off:off + width])

    qk = proj(OFF_GQ, 2 * GLA_QK_LANES)
    gq = qk[:, 0:GLA_QK_LANES]
    gk = qk[:, GLA_QK_LANES:]
    vg = proj(OFF_GV, 2 * GLA_WIDTH)
    gv = vg[:, 0:GLA_WIDTH].astype(BF16)
    gate = vg[:, GLA_WIDTH:]

    glr_hi, glr_lo = _split_bf16(gq)
    up_hi = gkup_ref[0]
    up_lo = gkup_ref[1]
    z = _dot(glr_hi, up_hi) + _dot(glr_lo, up_hi) + _dot(glr_hi, up_lo) + gkb_ref[...]
    logsig = jnp.minimum(z, 0.0) - jnp.log1p(jnp.exp(-jnp.abs(z)))
    lane = lax.broadcasted_iota(jnp.int32, (tm, GLA_QK_LANES), 1)
    gg = jnp.where(lane % HEAD_SLOT < GLA_DK, logsig * (1.0 / GLA_GATE_NORM), 0.0)

    dqk = proj(OFF_DQ, 2 * DIFF_LANES)
    dq_ref[...] = (dqk[:, 0:DIFF_LANES] * (DIFF_DQK ** -0.5 * LOG2E)).astype(BF16)
    dk_ref[...] = dqk[:, DIFF_LANES:].astype(BF16)
    lane_v = lax.broadcasted_iota(jnp.int32, (tm, DIFF_LANES), 1)
    dv_ref[...] = jnp.where(lane_v % LANES == DIFF_DV, 1.0, proj(OFF_DV, DIFF_LANES)).astype(BF16)

    conv = proj(OFF_CB, 3 * CONV_DIM)
    zc = conv[:, CONV_DIM:2 * CONV_DIM] * conv[:, 2 * CONV_DIM:]
    zbuf_ref[F32_SUBLANES:F32_SUBLANES + tm, :] = zc
    cw = cw_ref[...]
    y = (cw[2:3, :] * zc + cw[1:2, :] * zbuf_ref[F32_SUBLANES - 1:F32_SUBLANES - 1 + tm, :]
         + cw[0:1, :] * zbuf_ref[F32_SUBLANES - 2:F32_SUBLANES - 2 + tm, :])
    conv_ref[...] = (conv[:, 0:CONV_DIM] * y).astype(BF16)
    zbuf_ref[0:F32_SUBLANES, :] = zbuf_ref[tm:tm + F32_SUBLANES, :]

    q_scaled = gq * (GLA_DK ** -0.5)
    raw, new_state = _gla_fast(q_scaled, gk, gg, gv, s_ref[...], tri_ref[...])
    gla_ref[...] = _gla_finish(raw, gate, gn_ref[...])
    s_next_ref[...] = new_state
    q_s[...] = q_scaled
    k_s[...] = gk
    g_s[...] = gg
    v_s[...] = gv
    raw_ref[...] = gate
    safe = jnp.min(gg) * CHUNK >= -GLA_SAFE_DECAY

    @pl.when(safe)
    def _():
        s_ref[...] = s_next_ref[...]

    @pl.when(jnp.logical_not(safe))
    def _():
        gate_kept = raw_ref[...]
        _gla_general(q_s, k_s, g_s, v_s, s_ref, raw_ref, cum_ref, amask_ref)
        gla_ref[...] = _gla_finish(raw_ref[...], gate_kept, gn_ref[...])


def _proj(x, layer, g, w, gkup, gkb, cw, gn_tiled, seq):
    t = x.shape[0]
    tm = min(PROJ_ROWS, seq)
    row = lambda width: pl.BlockSpec((tm, width), lambda i: (i, 0))
    cum_np = _gla_cum_matrix()
    cum = jnp.asarray(cum_np, BF16)
    tri = jnp.asarray(cum_np[0:CHUNK], BF16)
    amask = jnp.asarray(_gla_score_mask(), F32)
    out_shapes = (
        jax.ShapeDtypeStruct((t, GLA_WIDTH), BF16),
        jax.ShapeDtypeStruct((t, DIFF_LANES), BF16),
        jax.ShapeDtypeStruct((t, DIFF_LANES), BF16),
        jax.ShapeDtypeStruct((t, DIFF_LANES), BF16),
        jax.ShapeDtypeStruct((t, CONV_DIM), BF16),
    )
    return pl.pallas_call(
        functools.partial(_proj_kernel, tiles_per_seq=seq // tm),
        out_shape=out_shapes,
        grid=(t // tm,),
        in_specs=[
            row(D_MODEL),
            _layer_spec(g, layer),
            _layer_spec(w, layer),
            _layer_spec(gkup, layer),
            _layer_spec(gkb, layer),
            _layer_spec(cw, layer),
            _const_spec(cum.shape),
            _const_spec(tri.shape),
            _const_spec(amask.shape),
            _layer_spec(gn_tiled, layer),
        ],
        out_specs=tuple(row(s.shape[1]) for s in out_shapes),
        scratch_shapes=[pltpu.VMEM((tm + F32_SUBLANES, CONV_DIM), F32),
                        pltpu.VMEM((tm, GLA_QK_LANES), F32),
                        pltpu.VMEM((tm, GLA_QK_LANES), F32),
                        pltpu.VMEM((tm, GLA_QK_LANES), F32),
                        pltpu.VMEM((tm, GLA_WIDTH), BF16),
                        pltpu.VMEM((GLA_WIDTH, GLA_QK_LANES), F32),
                        pltpu.VMEM((GLA_WIDTH, GLA_QK_LANES), F32),
                        pltpu.VMEM((tm, GLA_WIDTH), F32)],
        compiler_params=pltpu.CompilerParams(
            dimension_semantics=("arbitrary",), vmem_limit_bytes=VMEM_LIMIT_BYTES),
        name="mixer_proj_gla",
    )(x, g, w, gkup, gkb, cw, cum, tri, amask, gn_tiled)


def _t5_bucket(rel):
    nb = NUM_BUCKETS // 2
    max_exact = nb // 2
    ret = (rel > 0).astype(jnp.int32) * nb
    n = jnp.abs(rel)
    nf = jnp.maximum(n, 1).astype(jnp.float32)
    large = max_exact + (jnp.log(nf / max_exact) / math.log(MAX_DISTANCE / max_exact)
                         * (nb - max_exact)).astype(jnp.int32)
    large = jnp.minimum(large, nb - 1)
    return ret + jnp.where(n < max_exact, n, large)


def _bias_tiles(rel_bias, tile):
    assert tile >= MAX_DISTANCE and tile % CHUNK == 0
    table = rel_bias.astype(F32)
    far = table[NUM_BUCKETS // 2 - 1]
    heads = table.shape[1]
    rel = jnp.concatenate([jnp.arange(0, tile), jnp.arange(-2 * tile, 0)])
    onehot = _t5_bucket(rel)[None, :, None] == jnp.arange(NUM_BUCKETS)[None, None, :]
    per_rel = jnp.sum(jnp.where(onehot, table.T[:, None, :], 0.0), axis=-1)
    per_rel = (per_rel - far[:, None]) * LOG2E
    span = 3 * tile
    flat = jnp.tile(per_rel, (1, 2 * tile))[:, :2 * tile * (span - 1)]
    toep = flat.reshape(heads, 2 * tile, span - 1)[:, :, :tile]
    r = jnp.arange(tile)[:, None]
    c = jnp.arange(tile)[None, :]
    diag = jnp.where(((c // CHUNK) <= (r // CHUNK))[None], toep[:, :tile], MASK_VALUE)
    left = toep[:, tile:]
    return jnp.stack([left, diag], axis=1)


def _attn_kernel(q_ref, k_ref, v_ref, bias_ref, lamv_ref, subln_ref, o_ref,
                 s_buf, p_buf, alpha_buf, m_all, acc_all, *, tile, lam_init):
    seq = q_ref.shape[0]
    nq = seq // tile

    s_buf[...] = jnp.zeros(s_buf.shape, F32)
    p_buf[...] = jnp.zeros(p_buf.shape, BF16)
    alpha_buf[...] = jnp.zeros(alpha_buf.shape, F32)
    m_all[...] = jnp.full(m_all.shape, MASK_VALUE, F32)
    acc_all[...] = jnp.zeros(acc_all.shape, F32)

    lane = lax.broadcasted_iota(jnp.int32, (tile, LANES), 1)

    def stage1(i, j, slot):
        q = q_ref[pl.ds(pl.multiple_of(i * tile, tile), tile), :]
        zero = jnp.zeros_like(q)
        qs = jnp.concatenate([jnp.where(lane < LANES // 2, q, zero),
                              jnp.where(lane >= LANES // 2, q, zero)], axis=0)
        kj = k_ref[pl.ds(pl.multiple_of(j * tile, tile), tile), :]
        s_buf[slot] = _dot_nt(qs, kj)

    def stage2(i, bias, slot):
        for half in range(2):
            rows = slice(half * tile, (half + 1) * tile)
            s = s_buf[slot, rows, :]
            if bias is not None:
                s = s + bias
            m_prev = m_all[i, rows, :]
            m_next = jnp.maximum(m_prev, jnp.max(s, axis=-1, keepdims=True))
            p = jnp.exp2(s - jnp.concatenate([m_next] * (tile // LANES), axis=1))
            p_buf[slot, rows, :] = p.astype(BF16)
            alpha_buf[slot, rows, :] = jnp.exp2(m_prev - m_next)
            m_all[i, rows, :] = m_next

    def stage3(i, j, slot):
        vj = v_ref[pl.ds(pl.multiple_of(j * tile, tile), tile), :]
        acc_all[i] = alpha_buf[slot] * acc_all[i] + _dot(p_buf[slot], vj)

    def run_pipeline(n_pairs, first_pair, advance, with_bias, unroll):
        n_steps = pl.cdiv(n_pairs + 2, unroll) * unroll

        def step(t, pairs, slot_a, slot_b):
            (i0, j0), (i1, j1), (i2, j2) = pairs
            valid1 = jnp.logical_and(t >= 1, t <= n_pairs)
            valid2 = jnp.logical_and(t >= 2, t <= n_pairs + 1)
            stage3(jnp.where(valid2, i2, nq), j2, slot_a)
            stage2(jnp.where(valid1, i1, nq), bias_ref[0, j1 - i1 + 1] if with_bias else None, slot_b)
            stage1(i0, j0, slot_a)
            return (advance(i0, j0), (i0, j0), (i1, j1))

        def body(u, pairs):
            for r in range(unroll):
                pairs = step(unroll * u + r, pairs, r % 2, (r + 1) % 2)
            return pairs

        lax.fori_loop(0, n_steps // unroll, body, (first_pair,) * 3)

    def next_far(i, j):
        wrap = j == i - 2
        done = jnp.logical_and(wrap, i == nq - 1)
        step_i = jnp.logical_and(wrap, jnp.logical_not(done))
        return (jnp.where(step_i, i + 1, i), jnp.where(done, j, jnp.where(wrap, 0, j + 1)))

    def next_near(i, j):
        wrap = j == i
        done = jnp.logical_and(wrap, i == nq - 1)
        step_i = jnp.logical_and(wrap, jnp.logical_not(done))
        return (jnp.where(step_i, i + 1, i), jnp.where(jnp.logical_or(done, wrap), j, j + 1))

    zero = jnp.int32(0)
    if nq > 2:
        run_pipeline((nq - 1) * (nq - 2) // 2, (jnp.int32(2), zero), next_far, False, ATT_UNROLL_FAR)
    run_pipeline(2 * nq - 1, (zero, zero), next_near, True, ATT_UNROLL_NEAR)

    lv = lamv_ref[...]
    lam = (jnp.exp(jnp.sum(lv[0:1] * lv[1:2], axis=-1, keepdims=True))
           - jnp.exp(jnp.sum(lv[2:3] * lv[3:4], axis=-1, keepdims=True)) + lam_init)

    def finish(i):
        acc = acc_all[i]
        a1 = acc[0:tile]
        a2 = acc[tile:2 * tile]
        ones_col = lane == DIFF_DV
        ratio = jnp.sum(jnp.where(ones_col, a1 / jnp.where(ones_col, a2, 1.0), 0.0),
                        axis=-1, keepdims=True)
        u = a1 - lam * (ratio * a2)
        sq = jnp.where(ones_col, a1 * math.sqrt(DIFF_DV * EPS), u)
        ms = jnp.sum(sq * sq, axis=-1, keepdims=True) * (1.0 / DIFF_DV)
        y = u * lax.rsqrt(ms) * subln_ref[...] * (1.0 - lam_init)
        o_ref[pl.ds(pl.multiple_of(i * tile, tile), tile), :] = y.astype(BF16)

    group = math.gcd(nq, ATT_FINISH_GROUP)

    def finish_group(u, carry):
        for r in range(group):
            finish(u * group + r)
        return carry

    lax.fori_loop(0, nq // group, finish_group, 0)


def _attn(dq, dk, dv, bias_tiles, layer, lamv, subln, batch, seq, lam_init):
    tile = min(ATT_TILE, seq)
    nq = seq // tile
    head_block = pl.BlockSpec((seq, LANES), lambda b, h: (b, h))
    return pl.pallas_call(
        functools.partial(_attn_kernel, tile=tile, lam_init=lam_init),
        out_shape=jax.ShapeDtypeStruct((batch * seq, DIFF_LANES), BF16),
        grid=(batch, DIFF_HEADS),
        in_specs=[
            head_block, head_block, head_block,
            pl.BlockSpec((1, 2, tile, tile), lambda b, h: (h, 0, 0, 0)),
            pl.BlockSpec((None, 4, LANES), lambda b, h: (layer, 0, 0)),
            pl.BlockSpec((None, 1, LANES), lambda b, h: (layer, 0, 0)),
        ],
        out_specs=head_block,
        scratch_shapes=[pltpu.VMEM((2, 2 * tile, tile), F32),
                        pltpu.VMEM((2, 2 * tile, tile), BF16),
                        pltpu.VMEM((2, 2 * tile, LANES), F32),
                        pltpu.VMEM((nq + 1, 2 * tile, LANES), F32),
                        pltpu.VMEM((nq + 1, 2 * tile, LANES), F32)],
        compiler_params=pltpu.CompilerParams(
            dimension_semantics=("parallel", "parallel"), vmem_limit_bytes=VMEM_LIMIT_BYTES),
        name="diff_attn",
    )(dq, dk, dv, bias_tiles, lamv, subln)


def _pad_last(w, width):
    return jnp.pad(w, [(0, 0)] * (w.ndim - 1) + [(0, width - w.shape[-1])])


def _slot_cols(w, heads, used, slot):
    lead = w.shape[:-1]
    w = _pad_last(w.reshape(lead + (heads, used)), slot)
    return w.reshape(lead + (heads * slot,))


def _layout_w_in(w_in):
    offs = np.cumsum([0, GLA_HEADS * GLA_DK, GLA_HEADS * GLA_DK, GLA_WIDTH, GLA_RANK, GLA_WIDTH,
                      DIFF_HEADS * 2 * DIFF_DQK, DIFF_HEADS * 2 * DIFF_DQK, DIFF_WIDTH,
                      CONV_DIM, CONV_DIM, CONV_DIM])
    w_in = w_in.astype(BF16)
    gq, gk, gv, glr, gate, dq, dk, dv, cb, cc, ch = [w_in[..., offs[n]:offs[n + 1]] for n in range(11)]
    half = LANES // 2
    gq_slots = _slot_cols(gq, GLA_HEADS, GLA_DK, HEAD_SLOT)
    gq_slots = gq_slots.at[..., GLR_LANE:GLR_LANE + GLA_RANK].set(glr)
    cols = [
        gq_slots,
        _slot_cols(gk, GLA_HEADS, GLA_DK, HEAD_SLOT),
        gv,
        gate,
        _slot_cols(dq, 2 * DIFF_HEADS, DIFF_DQK, half),
        _slot_cols(dk, 2 * DIFF_HEADS, DIFF_DQK, half),
        _slot_cols(dv, DIFF_HEADS, DIFF_DV, LANES),
        cb, cc, ch,
    ]
    w = jnp.concatenate(cols, axis=-1)
    assert w.shape[-1] == N_PROJ
    return w


def kernel(x, ffn1_norm, ffn1_gate, ffn1_up, ffn1_down, mix_norm, w_in, gla_gk_up, gla_gk_bias,
           gla_norm, diff_lambda_q1, diff_lambda_k1, diff_lambda_q2, diff_lambda_k2, diff_subln,
           rel_bias, conv_w, w_out, ffn2_norm, ffn2_gate, ffn2_up, ffn2_down, final_norm):
    batch, seq, _ = x.shape
    depth = w_in.shape[0]
    t = batch * seq
    xf = x.reshape(t, D_MODEL)
    bias_tiles = _bias_tiles(rel_bias, min(ATT_TILE, seq))
    final_g = final_norm.reshape(1, D_MODEL)
    ffn1_w = (ffn1_gate[0].astype(BF16), ffn1_up[0].astype(BF16), ffn1_down[0].astype(BF16))

    ffn1_g = ffn1_norm.reshape(depth, 1, D_MODEL)
    ffn2_g = ffn2_norm.reshape(depth, 1, D_MODEL)
    mix_g = mix_norm.reshape(depth, 1, D_MODEL)
    w_pad = _layout_w_in(w_in)
    up = _slot_cols(gla_gk_up, GLA_HEADS, GLA_DK, HEAD_SLOT)
    up = jnp.pad(up, ((0, 0), (GLR_LANE, GLA_QK_LANES - GLR_LANE - GLA_RANK), (0, 0)))
    up_hi = up.astype(BF16)
    up_lo = (up - up_hi.astype(F32)).astype(BF16)
    gkup = jnp.stack([up_hi, up_lo], axis=1)
    gkb = _slot_cols(gla_gk_bias, GLA_HEADS, GLA_DK, HEAD_SLOT)[:, None, :]
    cw = jnp.pad(conv_w, ((0, 0), (0, F32_SUBLANES - CONV_WIDTH), (0, 0)))
    gn_tiled = jnp.tile(gla_norm, (1, GLA_HEADS))[:, None, :]
    lamv = jnp.stack([diff_lambda_q1, diff_lambda_k1, diff_lambda_q2, diff_lambda_k2], axis=1)
    lamv = _pad_last(lamv.astype(F32), LANES)
    subln = _pad_last(diff_subln[:, None, :], LANES)
    w1 = w_out[:, 0:GLA_WIDTH].astype(BF16)
    w2 = w_out[:, GLA_WIDTH:GLA_WIDTH + DIFF_WIDTH].reshape(depth, DIFF_HEADS, DIFF_DV, D_MODEL)
    w2 = jnp.pad(w2, ((0, 0), (0, 0), (0, LANES - DIFF_DV), (0, 0)))
    w2 = w2.reshape(depth, DIFF_LANES, D_MODEL).astype(BF16)
    w3 = w_out[:, GLA_WIDTH + DIFF_WIDTH:].astype(BF16)

    for l in range(depth):
        lam_init = 0.8 - 0.6 * math.exp(-0.3 * l)
        xf, ffn2_w = _ffn(xf, ffn1_g, l, *ffn1_w, final_g, False,
                          cast=((ffn2_gate, ffn2_up, ffn2_down), l))
        gla_o, dq, dk, dv, conv_o = _proj(xf, l, mix_g, w_pad, gkup, gkb, cw, gn_tiled, seq)
        diff_o = _attn(dq, dk, dv, bias_tiles, l, lamv, subln, batch, seq, lam_init)
        last = l == depth - 1
        xf, ffn1_w = _ffn(xf, ffn2_g, l, *ffn2_w, final_g, last,
                          mix=(gla_o, diff_o, conv_o, w1, w2, w3),
                          cast=None if last else ((ffn1_gate, ffn1_up, ffn1_down), l + 1))

    return xf.reshape(batch, seq, D_MODEL)
```

```python
import functools
import math

import numpy as np
import jax
import jax.numpy as jnp
from jax import lax
from jax.experimental import pallas as pl
from jax.experimental.pallas import tpu as pltpu

F32 = jnp.float32
BF16 = jnp.bfloat16

D_MODEL = 1024
D_FF = 2816
EPS = 1e-6
CHUNK = 64
GLA_HEADS = 4
GLA_DK = 48
GLA_DV = 96
GLA_RANK = 16
GLA_GATE_NORM = 16.0
DIFF_HEADS = 4
DIFF_DQK = 48
DIFF_DV = 96
CONV_DIM = 256
CONV_WIDTH = 3
NUM_BUCKETS = 32
MAX_DISTANCE = 128
GLA_WIDTH = GLA_HEADS * GLA_DV
DIFF_WIDTH = DIFF_HEADS * DIFF_DV

LANES = 128
F32_SUBLANES = 8
BF16_SUBLANES = 16
MXU_DEPTH = 256
VMEM_LIMIT_BYTES = 62 * 1024 * 1024

HEAD_SLOT = 64
GLA_QK_LANES = GLA_HEADS * HEAD_SLOT
SUB = 16
N_SUB = CHUNK // SUB
KST_ROWS = SUB * (N_SUB * (N_SUB - 1) // 2)
DIFF_LANES = DIFF_HEADS * LANES
MASK_VALUE = -1e30
GLA_SAFE_DECAY = 60.0
LOG2E = math.log2(math.e)

OFF_GQ = 0
OFF_GK = OFF_GQ + GLA_QK_LANES
OFF_GV = OFF_GK + GLA_QK_LANES
OFF_GATE = OFF_GV + GLA_WIDTH
OFF_DQ = OFF_GATE + GLA_WIDTH
OFF_DK = OFF_DQ + DIFF_LANES
OFF_DV = OFF_DK + DIFF_LANES
OFF_CB = OFF_DV + DIFF_LANES
OFF_CC = OFF_CB + CONV_DIM
OFF_CH = OFF_CC + CONV_DIM
N_PROJ = OFF_CH + CONV_DIM
GLR_LANE = GLA_DK
assert GLR_LANE + GLA_RANK <= HEAD_SLOT

FFN_ROWS = 1024
FFN_F_SPLITS = (0, 6 * MXU_DEPTH, D_FF)
assert all((hi - lo) % MXU_DEPTH == 0 for lo, hi in zip(FFN_F_SPLITS[:-1], FFN_F_SPLITS[1:]))
PROJ_ROWS = 1024
ATT_TILE = 256
ATT_UNROLL_FAR = 52
ATT_UNROLL_NEAR = 22
ATT_FINISH_GROUP = 8


def _dot(a, b):
    return jnp.dot(a, b, preferred_element_type=F32)


def _dot_nt(a, b):
    return lax.dot_general(a, b, (((1,), (1,)), ((), ())), preferred_element_type=F32)


def _dot_tn(a, b):
    return lax.dot_general(a, b, (((0,), (0,)), ((), ())), preferred_element_type=F32)


def _split_bf16(x):
    hi = x.astype(BF16)
    lo = (x - hi.astype(F32)).astype(BF16)
    return hi, lo


def _rms(x, g):
    return x * lax.rsqrt(jnp.mean(x * x, axis=-1, keepdims=True) + EPS) * g


def _sigmoid(x):
    return 1.0 / (1.0 + jnp.exp(-x))


def _ffn_kernel(*refs, final, mix, cast):
    refs = list(refs)
    x_ref, g_ref, wg_ref, wu_ref, wd_ref, fg_ref = refs[:6]
    mix_refs = refs[6:10] if mix else ()
    n_in = 6 + len(mix_refs)
    src_refs = refs[n_in:n_in + 3] if cast else ()
    o_ref = refs[n_in + len(src_refs)]
    dst_refs = refs[n_in + len(src_refs) + 1:]
    for src, dst in zip(src_refs, dst_refs):
        dst[...] = src[...].astype(BF16)

    x = x_ref[...]
    if mix:
        gla_ref, diff_ref, conv_ref, wo_ref = mix_refs
        mixed = jnp.concatenate([gla_ref[...], diff_ref[...], conv_ref[...]], axis=1)
        x = x + _dot(mixed, wo_ref[...])
    h = _rms(x, g_ref[...]).astype(BF16)
    acc = None
    for lo, hi in zip(FFN_F_SPLITS[:-1], FFN_F_SPLITS[1:]):
        sl = slice(lo, hi)
        gate = _dot(h, wg_ref[:, sl])
        up = _dot(h, wu_ref[:, sl])
        a = (gate * _sigmoid(gate) * up).astype(BF16)
        part = _dot(a, wd_ref[sl, :])
        acc = part if acc is None else acc + part
    y = x + 0.5 * acc
    if final:
        y = _rms(y, fg_ref[...])
    o_ref[...] = y


def _const_spec(shape):
    nd = len(shape)
    return pl.BlockSpec(shape, lambda *_: (0,) * nd, pipeline_mode=pl.Buffered(1))


def _layer_spec(stacked, layer):
    nd = stacked.ndim - 1
    return pl.BlockSpec((None,) + tuple(stacked.shape[1:]), lambda *_: (layer,) + (0,) * nd,
                        pipeline_mode=pl.Buffered(1))


def _cast_block(n_rows, steps):
    share = 1
    while (n_rows * share) % (steps * BF16_SUBLANES) or steps % share:
        share *= 2
        assert share <= steps
    return n_rows * share // steps, share


def _ffn(x, gains, layer, wg, wu, wd, final_g, final, mix=None, cast=None):
    t = x.shape[0]
    tm = min(FFN_ROWS, t)
    steps = t // tm
    row = lambda width: pl.BlockSpec((tm, width), lambda i: (i, 0))
    operands = [x, gains, wg, wu, wd, final_g]
    in_specs = [row(D_MODEL), _layer_spec(gains, layer), _const_spec((D_MODEL, D_FF)),
                _const_spec((D_MODEL, D_FF)), _const_spec((D_FF, D_MODEL)), _const_spec((1, D_MODEL))]
    out_shapes = [jax.ShapeDtypeStruct((t, D_MODEL), F32)]
    out_specs = [row(D_MODEL)]
    if mix is not None:
        operands += list(mix)
        in_specs += [row(a.shape[1]) for a in mix[:3]] + [_layer_spec(w, layer) for w in mix[3:]]
    if cast is not None:
        stacked, layer = cast
        for w in stacked:
            rows, share = _cast_block(w.shape[1], steps)
            operands.append(w)
            in_specs.append(pl.BlockSpec((None, rows, w.shape[2]),
                                         lambda i, share=share: (layer, i // share, 0)))
            out_shapes.append(jax.ShapeDtypeStruct(w.shape[1:], BF16))
            out_specs.append(pl.BlockSpec((rows, w.shape[2]), lambda i, share=share: (i // share, 0)))
    outs = pl.pallas_call(
        functools.partial(_ffn_kernel, final=final, mix=mix is not None, cast=cast is not None),
        out_shape=tuple(out_shapes),
        grid=(steps,),
        in_specs=in_specs,
        out_specs=tuple(out_specs),
        compiler_params=pltpu.CompilerParams(
            dimension_semantics=("arbitrary",), vmem_limit_bytes=VMEM_LIMIT_BYTES),
        name="ffn",
    )(*operands)
    return outs[0], tuple(outs[1:])


def _gla_cum_matrix():
    t = np.arange(CHUNK)[:, None]
    u = np.arange(CHUNK)[None, :]
    tri = (u <= t)
    ref = (u <= (t // SUB) * SUB - 1)
    ones = np.ones((CHUNK, CHUNK), bool)
    parts = [tri, ref, ones]
    for j in range(1, N_SUB):
        s = np.arange(SUB * j)[:, None]
        parts.append((u > s) & (u <= SUB * j - 1))
    return np.concatenate(parts, axis=0).astype(np.float32)


def _gla_score_mask():
    t = np.arange(CHUNK)[:, None]
    group = np.concatenate([np.full(SUB * j, j) for j in range(1, N_SUB)])
    keep = (group[None, :] == (t // SUB))
    return np.tile(keep, (1, GLA_HEADS)).astype(np.float32)


def _gla_head_ids():
    qk_head = lax.broadcasted_iota(jnp.int32, (1, GLA_QK_LANES), 1) // HEAD_SLOT
    v_head = lax.broadcasted_iota(jnp.int32, (1, GLA_WIDTH), 1) // GLA_DV
    row_head_v = lax.broadcasted_iota(jnp.int32, (GLA_WIDTH, 1), 0) // GLA_DV
    return qk_head, v_head, row_head_v == qk_head


def _gla_fast(q, k, g, v, state, tri):
    rows = q.shape[0]
    n_chunks = rows // CHUNK
    qk_head, v_head, state_mask = _gla_head_ids()
    chunk = lambda x, c: x[c * CHUNK:(c + 1) * CHUNK]
    sel_rows = pl.cdiv(n_chunks, BF16_SUBLANES) * BF16_SUBLANES
    sel_r = lax.broadcasted_iota(jnp.int32, (sel_rows, rows), 0)
    sel_c = lax.broadcasted_iota(jnp.int32, (sel_rows, rows), 1) // CHUNK
    chunk_sel = jnp.where(sel_r == sel_c, 1.0, 0.0).astype(BF16)
    g_hi, g_lo = _split_bf16(g)
    totals = _dot(chunk_sel, g_hi) + _dot(chunk_sel, g_lo)
    b = jnp.concatenate([_dot(tri, chunk(g_hi, c)) + _dot(tri, chunk(g_lo, c))
                         for c in range(n_chunks)], axis=0)
    b_last = jnp.concatenate([jnp.broadcast_to(totals[c:c + 1], (CHUNK, GLA_QK_LANES))
                              for c in range(n_chunks)], axis=0)
    qd = (q * jnp.exp(b)).astype(BF16)
    kd = (k * jnp.exp(-b)).astype(BF16)
    kw = (k * jnp.exp(b_last - b)).astype(BF16)
    t_idx = lax.broadcasted_iota(jnp.int32, (CHUNK, GLA_QK_LANES), 0)
    s_idx = lax.broadcasted_iota(jnp.int32, (CHUNK, GLA_QK_LANES), 1) % HEAD_SLOT
    causal = s_idx <= t_idx
    intra, update = [], []
    for c in range(n_chunks):
        kd_c, v_c = chunk(kd, c), chunk(v, c)
        kd_bd = jnp.concatenate(
            [jnp.where(qk_head == hd, kd_c, jnp.zeros_like(kd_c)) for hd in range(GLA_HEADS)],
            axis=0)
        a = _dot_nt(chunk(qd, c), kd_bd)
        a = jnp.where(causal, a, 0.0).astype(BF16)
        v_bd = jnp.concatenate(
            [jnp.where(v_head == hd, v_c, jnp.zeros_like(v_c)) for hd in range(GLA_HEADS)],
            axis=0)
        intra.append(_dot(a, v_bd))
        update.append(_dot_tn(v_c, chunk(kw, c)))
    raw = []
    for c in range(n_chunks):
        raw.append(intra[c] + _dot_nt(chunk(qd, c), state.astype(BF16)))
        state = state * jnp.exp(totals[c:c + 1]) + jnp.where(state_mask, update[c], 0.0)
    return jnp.concatenate(raw, axis=0), state


def _gla_general(q_ref, k_ref, g_ref, v_ref, s_ref, raw_ref, cum_ref, amask_ref):
    n_chunks = q_ref.shape[0] // CHUNK
    qk_head, v_head, state_mask = _gla_head_ids()
    cum = cum_ref[...]
    amask = amask_ref[...] > 0.5
    ind_r = lax.broadcasted_iota(jnp.int32, (GLA_QK_LANES, LANES), 0) // HEAD_SLOT
    ind_c = lax.broadcasted_iota(jnp.int32, (GLA_QK_LANES, LANES), 1)
    ind_sum = jnp.where(ind_r == ind_c, 1.0, 0.0).astype(BF16)
    bc_r = lax.broadcasted_iota(jnp.int32, (LANES, GLA_WIDTH), 0)
    bc_c = lax.broadcasted_iota(jnp.int32, (LANES, GLA_WIDTH), 1) // GLA_DV
    ind_bcast = jnp.where(bc_r == bc_c, 1.0, 0.0).astype(BF16)
    row_in_sub = lax.broadcasted_iota(jnp.int32, (CHUNK, 1), 0) % SUB

    def chunk_body(c, carry):
        r0 = pl.multiple_of(c * CHUNK, CHUNK)
        q = q_ref[pl.ds(r0, CHUNK), :]
        k = k_ref[pl.ds(r0, CHUNK), :]
        g = g_ref[pl.ds(r0, CHUNK), :]
        v = v_ref[pl.ds(r0, CHUNK), :]
        g_hi, g_lo = _split_bf16(g)
        cums = _dot(cum, g_hi) + _dot(cum, g_lo)
        b = cums[0:CHUNK]
        b_ref = cums[CHUNK:2 * CHUNK]
        b_last = cums[2 * CHUNK:3 * CHUNK]
        e_kst = cums[3 * CHUNK:3 * CHUNK + KST_ROWS]

        state = s_ref[...]
        q_in = (q * jnp.exp(b)).astype(BF16)
        o = _dot_nt(q_in, state.astype(BF16))

        kw = (k * jnp.exp(b_last - b)).astype(BF16)
        decay = jnp.exp(jnp.concatenate([b_last] * (GLA_WIDTH // CHUNK), axis=0))
        s_ref[...] = state * decay + jnp.where(state_mask, _dot_tn(v, kw), 0.0)

        q_sub = (q * jnp.exp(jnp.minimum(b - b_ref, 0.0))).astype(BF16)
        k_st = jnp.concatenate([k[0:SUB * j] for j in range(1, N_SUB)], axis=0) * jnp.exp(e_kst)
        k_bd = jnp.concatenate(
            [jnp.where(qk_head == hd, k_st, 0.0) for hd in range(GLA_HEADS)], axis=0).astype(BF16)
        a = _dot_nt(q_sub, k_bd)
        a = jnp.where(amask, a, 0.0).astype(BF16)
        v_st = jnp.concatenate([v[0:SUB * j] for j in range(1, N_SUB)], axis=0)
        v_bd = jnp.concatenate(
            [jnp.where(v_head == hd, v_st, jnp.zeros_like(v_st)) for hd in range(GLA_HEADS)], axis=0)
        o = o + _dot(a, v_bd)

        vf = v.astype(F32)
        prods = []
        for d in range(SUB):
            ks = k if d == 0 else pltpu.roll(k, d, 0)
            bs = b if d == 0 else pltpu.roll(b, d, 0)
            prods.append((q * ks * jnp.exp(jnp.minimum(b - bs, 0.0))).astype(BF16))
        dsum = _dot(jnp.concatenate(prods, axis=0), ind_sum)
        valid = jnp.concatenate([row_in_sub >= d for d in range(SUB)], axis=0)
        dsum = jnp.where(valid, dsum, 0.0).astype(BF16)
        dbc = _dot(dsum, ind_bcast)
        for d in range(SUB):
            vs = vf if d == 0 else pltpu.roll(vf, d, 0)
            o = o + dbc[d * CHUNK:(d + 1) * CHUNK] * vs
        raw_ref[pl.ds(r0, CHUNK), :] = o
        return carry

    lax.fori_loop(0, n_chunks, chunk_body, 0)


def _gla_finish(o, gate, gain):
    hr = lax.broadcasted_iota(jnp.int32, (GLA_WIDTH, GLA_WIDTH), 0) // GLA_DV
    hc = lax.broadcasted_iota(jnp.int32, (GLA_WIDTH, GLA_WIDTH), 1) // GLA_DV
    head_ones = jnp.where(hr == hc, 1.0, 0.0).astype(BF16)
    ms = _dot((o * o).astype(BF16), head_ones) * (1.0 / GLA_DV)
    return (o * lax.rsqrt(ms + EPS) * gain * (gate * _sigmoid(gate))).astype(BF16)


def _proj_kernel(x_ref, g_ref, w_ref, gkup_ref, gkb_ref, cw_ref, cum_ref, tri_ref, amask_ref, gn_ref,
                 gla_ref, dq_ref, dk_ref, dv_ref, conv_ref,
                 zbuf_ref, q_s, k_s, g_s, v_s, s_ref, s_next_ref, raw_ref, *, tiles_per_seq):
    tm = x_ref.shape[0]

    @pl.when(pl.program_id(0) % tiles_per_seq == 0)
    def _():
        zbuf_ref[0:F32_SUBLANES, :] = jnp.zeros((F32_SUBLANES, CONV_DIM), F32)
        s_ref[...] = jnp.zeros_like(s_ref)

    h = _rms(x_ref[...], g_ref[...]).astype(BF16)

    def proj(off, width):
        return _dot(h, w_ref[:, off:off + width])

    qk = proj(OFF_GQ, 2 * GLA_QK_LANES)
    gq = qk[:, 0:GLA_QK_LANES]
    gk = qk[:, GLA_QK_LANES:]
    vg = proj(OFF_GV, 2 * GLA_WIDTH)
    gv = vg[:, 0:GLA_WIDTH].astype(BF16)
    gate = vg[:, GLA_WIDTH:]

    glr_hi, glr_lo = _split_bf16(gq)
    up_hi = gkup_ref[0]
    up_lo = gkup_ref[1]
    z = _dot(glr_hi, up_hi) + _dot(glr_lo, up_hi) + _dot(glr_hi, up_lo) + gkb_ref[...]
    logsig = jnp.minimum(z, 0.0) - jnp.log1p(jnp.exp(-jnp.abs(z)))
    lane = lax.broadcasted_iota(jnp.int32, (tm, GLA_QK_LANES), 1)
    gg = jnp.where(lane % HEAD_SLOT < GLA_DK, logsig * (1.0 / GLA_GATE_NORM), 0.0)

    dqk = proj(OFF_DQ, 2 * DIFF_LANES)
    dq_ref[...] = (dqk[:, 0:DIFF_LANES] * (DIFF_DQK ** -0.5 * LOG2E)).astype(BF16)
    dk_ref[...] = dqk[:, DIFF_LANES:].astype(BF16)
    lane_v = lax.broadcasted_iota(jnp.int32, (tm, DIFF_LANES), 1)
    dv_ref[...] = jnp.where(lane_v % LANES == DIFF_DV, 1.0, proj(OFF_DV, DIFF_LANES)).astype(BF16)

    conv = proj(OFF_CB, 3 * CONV_DIM)
    zc = conv[:, CONV_DIM:2 * CONV_DIM] * conv[:, 2 * CONV_DIM:]
    zbuf_ref[F32_SUBLANES:F32_SUBLANES + tm, :] = zc
    cw = cw_ref[...]
    y = (cw[2:3, :] * zc + cw[1:2, :] * zbuf_ref[F32_SUBLANES - 1:F32_SUBLANES - 1 + tm, :]
         + cw[0:1, :] * zbuf_ref[F32_SUBLANES - 2:F32_SUBLANES - 2 + tm, :])
    conv_ref[...] = (conv[:, 0:CONV_DIM] * y).astype(BF16)
    zbuf_ref[0:F32_SUBLANES, :] = zbuf_ref[tm:tm + F32_SUBLANES, :]

    q_scaled = gq * (GLA_DK ** -0.5)
    raw, new_state = _gla_fast(q_scaled, gk, gg, gv, s_ref[...], tri_ref[...])
    gla_ref[...] = _gla_finish(raw, gate, gn_ref[...])
    s_next_ref[...] = new_state
    q_s[...] = q_scaled
    k_s[...] = gk
    g_s[...] = gg
    v_s[...] = gv
    raw_ref[...] = gate
    safe = jnp.min(gg) * CHUNK >= -GLA_SAFE_DECAY

    @pl.when(safe)
    def _():
        s_ref[...] = s_next_ref[...]

    @pl.when(jnp.logical_not(safe))
    def _():
        gate_kept = raw_ref[...]
        _gla_general(q_s, k_s, g_s, v_s, s_ref, raw_ref, cum_ref, amask_ref)
        gla_ref[...] = _gla_finish(raw_ref[...], gate_kept, gn_ref[...])


def _proj(x, layer, g, w, gkup, gkb, cw, gn_tiled, seq):
    t = x.shape[0]
    tm = min(PROJ_ROWS, seq)
    row = lambda width: pl.BlockSpec((tm, width), lambda i: (i, 0))
    cum_np = _gla_cum_matrix()
    cum = jnp.asarray(cum_np, BF16)
    tri = jnp.asarray(cum_np[0:CHUNK], BF16)
    amask = jnp.asarray(_gla_score_mask(), F32)
    out_shapes = (
        jax.ShapeDtypeStruct((t, GLA_WIDTH), BF16),
        jax.ShapeDtypeStruct((t, DIFF_LANES), BF16),
        jax.ShapeDtypeStruct((t, DIFF_LANES), BF16),
        jax.ShapeDtypeStruct((t, DIFF_LANES), BF16),
        jax.ShapeDtypeStruct((t, CONV_DIM), BF16),
    )
    return pl.pallas_call(
        functools.partial(_proj_kernel, tiles_per_seq=seq // tm),
        out_shape=out_shapes,
        grid=(t // tm,),
        in_specs=[
            row(D_MODEL),
            _layer_spec(g, layer),
            _layer_spec(w, layer),
            _layer_spec(gkup, layer),
            _layer_spec(gkb, layer),
            _layer_spec(cw, layer),
            _const_spec(cum.shape),
            _const_spec(tri.shape),
            _const_spec(amask.shape),
            _layer_spec(gn_tiled, layer),
        ],
        out_specs=tuple(row(s.shape[1]) for s in out_shapes),
        scratch_shapes=[pltpu.VMEM((tm + F32_SUBLANES, CONV_DIM), F32),
                        pltpu.VMEM((tm, GLA_QK_LANES), F32),
                        pltpu.VMEM((tm, GLA_QK_LANES), F32),
                        pltpu.VMEM((tm, GLA_QK_LANES), F32),
                        pltpu.VMEM((tm, GLA_WIDTH), BF16),
                        pltpu.VMEM((GLA_WIDTH, GLA_QK_LANES), F32),
                        pltpu.VMEM((GLA_WIDTH, GLA_QK_LANES), F32),
                        pltpu.VMEM((tm, GLA_WIDTH), F32)],
        compiler_params=pltpu.CompilerParams(
            dimension_semantics=("arbitrary",), vmem_limit_bytes=VMEM_LIMIT_BYTES),
        name="mixer_proj_gla",
    )(x, g, w, gkup, gkb, cw, cum, tri, amask, gn_tiled)


def _t5_bucket(rel):
    nb = NUM_BUCKETS // 2
    max_exact = nb // 2
    ret = (rel > 0).astype(jnp.int32) * nb
    n = jnp.abs(rel)
    nf = jnp.maximum(n, 1).astype(jnp.float32)
    large = max_exact + (jnp.log(nf / max_exact) / math.log(MAX_DISTANCE / max_exact)
                         * (nb - max_exact)).astype(jnp.int32)
    large = jnp.minimum(large, nb - 1)
    return ret + jnp.where(n < max_exact, n, large)


def _bias_tiles(rel_bias, tile):
    assert tile >= MAX_DISTANCE and tile % CHUNK == 0
    table = rel_bias.astype(F32)
    far = table[NUM_BUCKETS // 2 - 1]
    heads = table.shape[1]
    rel = jnp.concatenate([jnp.arange(0, tile), jnp.arange(-2 * tile, 0)])
    onehot = _t5_bucket(rel)[None, :, None] == jnp.arange(NUM_BUCKETS)[None, None, :]
    per_rel = jnp.sum(jnp.where(onehot, table.T[:, None, :], 0.0), axis=-1)
    per_rel = (per_rel - far[:, None]) * LOG2E
    span = 3 * tile
    flat = jnp.tile(per_rel, (1, 2 * tile))[:, :2 * tile * (span - 1)]
    toep = flat.reshape(heads, 2 * tile, span - 1)[:, :, :tile]
    r = jnp.arange(tile)[:, None]
    c = jnp.arange(tile)[None, :]
    diag = jnp.where(((c // CHUNK) <= (r // CHUNK))[None], toep[:, :tile], MASK_VALUE)
    left = toep[:, tile:]
    return jnp.stack([left, diag], axis=1)


def _attn_kernel(q_ref, k_ref, v_ref, bias_ref, lamv_ref, subln_ref, o_ref,
                 s_buf, p_buf, alpha_buf, m_all, acc_all, *, tile, lam_init):
    seq = q_ref.shape[0]
    nq = seq // tile

    s_buf[...] = jnp.zeros(s_buf.shape, F32)
    p_buf[...] = jnp.zeros(p_buf.shape, BF16)
    alpha_buf[...] = jnp.zeros(alpha_buf.shape, F32)
    m_all[...] = jnp.full(m_all.shape, MASK_VALUE, F32)
    acc_all[...] = jnp.zeros(acc_all.shape, F32)

    lane = lax.broadcasted_iota(jnp.int32, (tile, LANES), 1)

    def stage1(i, j, slot):
        q = q_ref[pl.ds(pl.multiple_of(i * tile, tile), tile), :]
        zero = jnp.zeros_like(q)
        qs = jnp.concatenate([jnp.where(lane < LANES // 2, q, zero),
                              jnp.where(lane >= LANES // 2, q, zero)], axis=0)
        kj = k_ref[pl.ds(pl.multiple_of(j * tile, tile), tile), :]
        s_buf[slot] = _dot_nt(qs, kj)

    def stage2(i, bias, slot):
        for half in range(2):
            rows = slice(half * tile, (half + 1) * tile)
            s = s_buf[slot, rows, :]
            if bias is not None:
                s = s + bias
            m_prev = m_all[i, rows, :]
            m_next = jnp.maximum(m_prev, jnp.max(s, axis=-1, keepdims=True))
            p = jnp.exp2(s - jnp.concatenate([m_next] * (tile // LANES), axis=1))
            p_buf[slot, rows, :] = p.astype(BF16)
            alpha_buf[slot, rows, :] = jnp.exp2(m_prev - m_next)
            m_all[i, rows, :] = m_next

    def stage3(i, j, slot):
        vj = v_ref[pl.ds(pl.multiple_of(j * tile, tile), tile), :]
        acc_all[i] = alpha_buf[slot] * acc_all[i] + _dot(p_buf[slot], vj)

    def run_pipeline(n_pairs, first_pair, advance, with_bias, unroll):
        n_steps = pl.cdiv(n_pairs + 2, unroll) * unroll

        def step(t, pairs, slot_a, slot_b):
            (i0, j0), (i1, j1), (i2, j2) = pairs
            valid1 = jnp.logical_and(t >= 1, t <= n_pairs)
            valid2 = jnp.logical_and(t >= 2, t <= n_pairs + 1)
            stage3(jnp.where(valid2, i2, nq), j2, slot_a)
            stage2(jnp.where(valid1, i1, nq), bias_ref[0, j1 - i1 + 1] if with_bias else None, slot_b)
            stage1(i0, j0, slot_a)
            return (advance(i0, j0), (i0, j0), (i1, j1))

        def body(u, pairs):
            for r in range(unroll):
                pairs = step(unroll * u + r, pairs, r % 2, (r + 1) % 2)
            return pairs

        lax.fori_loop(0, n_steps // unroll, body, (first_pair,) * 3)

    def next_far(i, j):
        wrap = j == i - 2
        done = jnp.logical_and(wrap, i == nq - 1)
        step_i = jnp.logical_and(wrap, jnp.logical_not(done))
        return (jnp.where(step_i, i + 1, i), jnp.where(done, j, jnp.where(wrap, 0, j + 1)))

    def next_near(i, j):
        wrap = j == i
        done = jnp.logical_and(wrap, i == nq - 1)
        step_i = jnp.logical_and(wrap, jnp.logical_not(done))
        return (jnp.where(step_i, i + 1, i), jnp.where(jnp.logical_or(done, wrap), j, j + 1))

    zero = jnp.int32(0)
    if nq > 2:
        run_pipeline((nq - 1) * (nq - 2) // 2, (jnp.int32(2), zero), next_far, False, ATT_UNROLL_FAR)
    run_pipeline(2 * nq - 1, (zero, zero), next_near, True, ATT_UNROLL_NEAR)

    lv = lamv_ref[...]
    lam = (jnp.exp(jnp.sum(lv[0:1] * lv[1:2], axis=-1, keepdims=True))
           - jnp.exp(jnp.sum(lv[2:3] * lv[3:4], axis=-1, keepdims=True)) + lam_init)

    def finish(i):
        acc = acc_all[i]
        a1 = acc[0:tile]
        a2 = acc[tile:2 * tile]
        ones_col = lane == DIFF_DV
        ratio = jnp.sum(jnp.where(ones_col, a1 / jnp.where(ones_col, a2, 1.0), 0.0),
                        axis=-1, keepdims=True)
        u = a1 - lam * (ratio * a2)
        sq = jnp.where(ones_col, a1 * math.sqrt(DIFF_DV * EPS), u)
        ms = jnp.sum(sq * sq, axis=-1, keepdims=True) * (1.0 / DIFF_DV)
        y = u * lax.rsqrt(ms) * subln_ref[...] * (1.0 - lam_init)
        o_ref[pl.ds(pl.multiple_of(i * tile, tile), tile), :] = y.astype(BF16)

    group = math.gcd(nq, ATT_FINISH_GROUP)

    def finish_group(u, carry):
        for r in range(group):
            finish(u * group + r)
        return carry

    lax.fori_loop(0, nq // group, finish_group, 0)


def _attn(dq, dk, dv, bias_tiles, layer, lamv, subln, batch, seq, lam_init):
    tile = min(ATT_TILE, seq)
    nq = seq // tile
    head_block = pl.BlockSpec((seq, LANES), lambda b, h: (b, h))
    return pl.pallas_call(
        functools.partial(_attn_kernel, tile=tile, lam_init=lam_init),
        out_shape=jax.ShapeDtypeStruct((batch * seq, DIFF_LANES), BF16),
        grid=(batch, DIFF_HEADS),
        in_specs=[
            head_block, head_block, head_block,
            pl.BlockSpec((1, 2, tile, tile), lambda b, h: (h, 0, 0, 0)),
            pl.BlockSpec((None, 4, LANES), lambda b, h: (layer, 0, 0)),
            pl.BlockSpec((None, 1, LANES), lambda b, h: (layer, 0, 0)),
        ],
        out_specs=head_block,
        scratch_shapes=[pltpu.VMEM((2, 2 * tile, tile), F32),
                        pltpu.VMEM((2, 2 * tile, tile), BF16),
                        pltpu.VMEM((2, 2 * tile, LANES), F32),
                        pltpu.VMEM((nq + 1, 2 * tile, LANES), F32),
                        pltpu.VMEM((nq + 1, 2 * tile, LANES), F32)],
        compiler_params=pltpu.CompilerParams(
            dimension_semantics=("parallel", "parallel"), vmem_limit_bytes=VMEM_LIMIT_BYTES),
        name="diff_attn",
    )(dq, dk, dv, bias_tiles, lamv, subln)


def _pad_last(w, width):
    return jnp.pad(w, [(0, 0)] * (w.ndim - 1) + [(0, width - w.shape[-1])])


def _slot_cols(w, heads, used, slot):
    lead = w.shape[:-1]
    w = _pad_last(w.reshape(lead + (heads, used)), slot)
    return w.reshape(lead + (heads * slot,))


def _layout_w_in(w_in):
    offs = np.cumsum([0, GLA_HEADS * GLA_DK, GLA_HEADS * GLA_DK, GLA_WIDTH, GLA_RANK, GLA_WIDTH,
                      DIFF_HEADS * 2 * DIFF_DQK, DIFF_HEADS * 2 * DIFF_DQK, DIFF_WIDTH,
                      CONV_DIM, CONV_DIM, CONV_DIM])
    w_in = w_in.astype(BF16)
    gq, gk, gv, glr, gate, dq, dk, dv, cb, cc, ch = [w_in[..., offs[n]:offs[n + 1]] for n in range(11)]
    half = LANES // 2
    gq_slots = _slot_cols(gq, GLA_HEADS, GLA_DK, HEAD_SLOT)
    gq_slots = gq_slots.at[..., GLR_LANE:GLR_LANE + GLA_RANK].set(glr)
    cols = [
        gq_slots,
        _slot_cols(gk, GLA_HEADS, GLA_DK, HEAD_SLOT),
        gv,
        gate,
        _slot_cols(dq, 2 * DIFF_HEADS, DIFF_DQK, half),
        _slot_cols(dk, 2 * DIFF_HEADS, DIFF_DQK, half),
        _slot_cols(dv, DIFF_HEADS, DIFF_DV, LANES),
        cb, cc, ch,
    ]
    w = jnp.concatenate(cols, axis=-1)
    assert w.shape[-1] == N_PROJ
    return w


def kernel(x, ffn1_norm, ffn1_gate, ffn1_up, ffn1_down, mix_norm, w_in, gla_gk_up, gla_gk_bias,
           gla_norm, diff_lambda_q1, diff_lambda_k1, diff_lambda_q2, diff_lambda_k2, diff_subln,
           rel_bias, conv_w, w_out, ffn2_norm, ffn2_gate, ffn2_up, ffn2_down, final_norm):
    batch, seq, _ = x.shape
    depth = w_in.shape[0]
    t = batch * seq
    xf = x.reshape(t, D_MODEL)
    bias_tiles = _bias_tiles(rel_bias, min(ATT_TILE, seq))
    final_g = final_norm.reshape(1, D_MODEL)
    ffn1_w = (ffn1_gate[0].astype(BF16), ffn1_up[0].astype(BF16), ffn1_down[0].astype(BF16))

    ffn1_g = ffn1_norm.reshape(depth, 1, D_MODEL)
    ffn2_g = ffn2_norm.reshape(depth, 1, D_MODEL)
    mix_g = mix_norm.reshape(depth, 1, D_MODEL)
    w_pad = _layout_w_in(w_in)
    up = _slot_cols(gla_gk_up, GLA_HEADS, GLA_DK, HEAD_SLOT)
    up = jnp.pad(up, ((0, 0), (GLR_LANE, GLA_QK_LANES - GLR_LANE - GLA_RANK), (0, 0)))
    up_hi = up.astype(BF16)
    up_lo = (up - up_hi.astype(F32)).astype(BF16)
    gkup = jnp.stack([up_hi, up_lo], axis=1)
    gkb = _slot_cols(gla_gk_bias, GLA_HEADS, GLA_DK, HEAD_SLOT)[:, None, :]
    cw = jnp.pad(conv_w, ((0, 0), (0, F32_SUBLANES - CONV_WIDTH), (0, 0)))
    gn_tiled = jnp.tile(gla_norm, (1, GLA_HEADS))[:, None, :]
    lamv = jnp.stack([diff_lambda_q1, diff_lambda_k1, diff_lambda_q2, diff_lambda_k2], axis=1)
    lamv = _pad_last(lamv.astype(F32), LANES)
    subln = _pad_last(diff_subln[:, None, :], LANES)
    w1 = w_out[:, 0:GLA_WIDTH].astype(BF16)
    w2 = w_out[:, GLA_WIDTH:GLA_WIDTH + DIFF_WIDTH].reshape(depth, DIFF_HEADS, DIFF_DV, D_MODEL)
    w2 = jnp.pad(w2, ((0, 0), (0, 0), (0, LANES - DIFF_DV), (0, 0)))
    w2 = w2.reshape(depth, DIFF_LANES, D_MODEL).astype(BF16)
    w3 = w_out[:, GLA_WIDTH + DIFF_WIDTH:].astype(BF16)

    for l in range(depth):
        lam_init = 0.8 - 0.6 * math.exp(-0.3 * l)
        xf, ffn2_w = _ffn(xf, ffn1_g, l, *ffn1_w, final_g, False,
                          cast=((ffn2_gate, ffn2_up, ffn2_down), l))
        gla_o, dq, dk, dv, conv_o = _proj(xf, l, mix_g, w_pad, gkup, gkb, cw, gn_tiled, seq)
        diff_o = _attn(dq, dk, dv, bias_tiles, l, lamv, subln, batch, seq, lam_init)
        last = l == depth - 1
        xf, ffn1_w = _ffn(xf, ffn2_g, l, *ffn2_w, final_g, last,
                          mix=(gla_o, diff_o, conv_o, jnp.concatenate([w1, w2, w3], axis=1)),
                          cast=None if last else ((ffn1_gate, ffn1_up, ffn1_down), l + 1))

    return xf.reshape(batch, seq, D_MODEL)
```
